```python
import jax, jax.numpy as jnp
from jax import lax
import numpy as np

D_MODEL = 1024
BATCH = 8
SEQ = 16384
DEPTH = 4

HEAD_DIM = 64
N_TOK_HEADS = 12
N_KV_HEADS = 3
GQA_GROUP = N_TOK_HEADS // N_KV_HEADS
N_MEM_HEADS = 4
N_MEM = 256
Q_W = N_TOK_HEADS * HEAD_DIM
KV_W = N_KV_HEADS * HEAD_DIM
QM_W = N_MEM_HEADS * HEAD_DIM
IN_W = Q_W + 2 * KV_W + QM_W
MIX_WIDTH = Q_W + QM_W
D_FF = -(-8 * D_MODEL // (3 * 256)) * 256
N_MIXERS = 3
BLOCK = 128
A_RADIUS = 128
C_GROUPS = ((128, 1), (512, 4), (2048, 16))
ROPE_THETA = 500000.0
ROPE_DIMS = HEAD_DIM // 4
AXIAL_THETA = 10000.0
GRID_W = 64
EPS = 1e-6
N_A = (DEPTH + 2) // N_MIXERS
N_B = (DEPTH + 1) // N_MIXERS
ATTN_SCALE = HEAD_DIM ** -0.5

kernel_name = 'hybrid_interleaved_window_axial_dilated_encoder'


def rms_norm(x, g):
    xf = x.astype(jnp.float32)
    y = xf * lax.rsqrt(jnp.mean(xf * xf, axis=-1, keepdims=True) + EPS)
    return (y * g.astype(jnp.float32)).astype(x.dtype)


def rope_table(pos, n_dims, theta):
    inv = theta ** (-(jnp.arange(0, n_dims, 2, dtype=jnp.float32) / n_dims))
    ang = pos.astype(jnp.float32)[:, None] * inv[None, :]
    return jnp.cos(ang), jnp.sin(ang)


def apply_rotary(x, cos, sin):
    half = x.shape[-1] // 2
    xf = x.astype(jnp.float32)
    x1, x2 = xf[..., :half], xf[..., half:]
    c, s = cos[:, None, :], sin[:, None, :]
    return jnp.concatenate([x1 * c - x2 * s, x2 * c + x1 * s], axis=-1).astype(x.dtype)


def partial_rope(x, cos, sin):
    return jnp.concatenate([apply_rotary(x[..., :ROPE_DIMS], cos, sin), x[..., ROPE_DIMS:]], axis=-1)


def axial_rope(x, cos_r, sin_r, cos_c, sin_c):
    half = HEAD_DIM // 2
    return jnp.concatenate([apply_rotary(x[..., :half], cos_r, sin_r),
                            apply_rotary(x[..., half:], cos_c, sin_c)], axis=-1)


def banded_attention(q, k, v, radius, sink=None):
    B, L, KVH, G, HD = q.shape
    blk = radius
    nb = -(-L // blk)
    Lp = nb * blk
    pad = Lp - L
    qb = jnp.pad(q, [(0, 0), (0, pad), (0, 0), (0, 0), (0, 0)]).reshape(B, nb, blk, KVH, G, HD)

    def windows(t):
        tp = jnp.pad(t, [(0, 0), (blk, blk + pad), (0, 0), (0, 0)]).reshape(B, nb + 2, blk, KVH, HD)
        return jnp.concatenate([tp[:, :-2], tp[:, 1:-1], tp[:, 2:]], axis=2)

    kw, vw = windows(k), windows(v)
    qpos = jnp.arange(Lp).reshape(nb, blk)
    kpos = (jnp.arange(nb)[:, None] - 1) * blk + jnp.arange(3 * blk)[None, :]
    mask = ((jnp.abs(qpos[:, :, None] - kpos[:, None, :]) <= radius)
            & (kpos >= 0)[:, None, :] & (kpos < L)[:, None, :])
    s = jnp.einsum('bnqhgd,bnkhd->bnhgqk', qb, kw, preferred_element_type=jnp.float32) * ATTN_SCALE
    s = jnp.where(mask[None, :, None, None], s, -jnp.inf)
    m = jnp.max(s, axis=-1, keepdims=True)
    if sink is not None:
        sk = sink.astype(jnp.float32)[None, None, :, :, None, None]
        m = jnp.maximum(m, sk)
    p = jnp.exp(s - m)
    denom = jnp.sum(p, axis=-1)
    if sink is not None:
        denom = denom + jnp.exp(sk - m)[..., 0]
    o = jnp.einsum('bnhgqk,bnkhd->bnqhgd', p.astype(v.dtype), vw)
    den_t = jnp.transpose(denom, (0, 1, 4, 2, 3))
    o = (o / den_t[..., None]).astype(q.dtype).reshape(B, Lp, KVH, G, HD)[:, :L]
    lse = jnp.transpose(m[..., 0] + jnp.log(denom), (0, 1, 4, 2, 3)).reshape(B, Lp, KVH, G)[:, :L]
    return o, lse


def full_attention_blocks(q, k, v):
    B, S, KVH, G, HD = q.shape
    nb = S // BLOCK
    qb = jnp.moveaxis(q.reshape(B, nb, BLOCK, KVH, G, HD), 1, 0)

    def one_block(qblk):
        s = jnp.einsum('bqhgd,bkhd->bhgqk', qblk, k, preferred_element_type=jnp.float32) * ATTN_SCALE
        p = jax.nn.softmax(s, axis=-1)
        return jnp.einsum('bhgqk,bkhd->bqhgd', p.astype(v.dtype), v)

    o = lax.map(one_block, qb)
    return jnp.moveaxis(o, 0, 1).reshape(B, S, KVH, G, HD)


def mixer_a(q, k, v, sink, cos_p, sin_p):
    B, S = q.shape[:2]
    q = partial_rope(q, cos_p, sin_p).reshape(B, S, N_KV_HEADS, GQA_GROUP, HEAD_DIM)
    k = partial_rope(k, cos_p, sin_p)
    o, _ = banded_attention(q, k, v, A_RADIUS, sink.reshape(N_KV_HEADS, GQA_GROUP))
    return o.reshape(B, S, Q_W)


def mixer_b(q, k, v, qk_g, cos_r, sin_r, cos_c, sin_c):
    B, S = q.shape[:2]
    q = axial_rope(rms_norm(q, qk_g[0]), cos_r, sin_r, cos_c, sin_c)
    k = axial_rope(rms_norm(k, qk_g[1]), cos_r, sin_r, cos_c, sin_c)
    o = full_attention_blocks(q.reshape(B, S, N_KV_HEADS, GQA_GROUP, HEAD_DIM), k, v)
    return o.reshape(B, S, Q_W)


def dilated_group(q, k, v, dil, radius):
    B, S = q.shape[:2]
    L = S // dil

    def split(t):
        t = jnp.moveaxis(t.reshape((B, L, dil) + t.shape[2:]), 2, 1)
        return t.reshape((B * dil, L) + t.shape[3:])

    def merge(t):
        t = jnp.moveaxis(t.reshape((B, dil, L) + t.shape[2:]), 1, 2)
        return t.reshape((B, S) + t.shape[3:])

    o, lse = banded_attention(split(q)[:, :, None], split(k)[:, :, None], split(v)[:, :, None], radius)
    return merge(o[:, :, 0]), merge(lse[:, :, 0])


def mixer_c(q, k, v, cos_p, sin_p):
    B, S = q.shape[:2]
    q = partial_rope(q, cos_p, sin_p)
    k = partial_rope(k, cos_p, sin_p)
    outs, lses = [], []
    for g, (window, dil) in enumerate(C_GROUPS):
        o, l = dilated_group(q[:, :, g * GQA_GROUP:(g + 1) * GQA_GROUP], k[:, :, g], v[:, :, g],
                             dil, window // (2 * dil))
        outs.append(o)
        lses.append(l)
    alpha = jax.nn.softmax(jnp.stack(lses, axis=2), axis=2)
    o = jnp.stack(outs, axis=2) * alpha[..., None].astype(q.dtype)
    return o.reshape(B, S, Q_W)


def memory_attention(qm, km, vm):
    s = jnp.einsum('bshd,bmhd->bhsm', qm, km, preferred_element_type=jnp.float32) * ATTN_SCALE
    p = jax.nn.softmax(s, axis=-1)
    o = jnp.einsum('bhsm,bmhd->bshd', p.astype(vm.dtype), vm)
    return o.reshape(qm.shape[0], qm.shape[1], QM_W)


def _fwd_setup_inputs(seed: int = 0) -> dict:
    key = jax.random.key(seed)
    ks = jax.random.split(key, 14)
    nrm = jax.random.normal
    f32 = jnp.float32
    return {
        'x': nrm(ks[0], (BATCH, SEQ, D_MODEL), f32),
        'mem': nrm(ks[1], (BATCH, N_MEM, D_MODEL), f32),
        'mem_norm_g': 1.0 + 0.02 * nrm(ks[2], (D_MODEL,), f32),
        'w_in': nrm(ks[3], (DEPTH, D_MODEL, IN_W), f32) * D_MODEL ** -0.5,
        'w_mem_kv': nrm(ks[4], (DEPTH, D_MODEL, 2 * QM_W), f32) * D_MODEL ** -0.5,
        'w_o': nrm(ks[5], (DEPTH, MIX_WIDTH, D_MODEL), f32) * MIX_WIDTH ** -0.5,
        'g_mix_pre': 1.0 + 0.02 * nrm(ks[6], (DEPTH, D_MODEL), f32),
        'g_mix_post': 1.0 + 0.02 * nrm(ks[7], (DEPTH, D_MODEL), f32),
        'attn_sink': 0.5 * nrm(ks[8], (N_A, N_TOK_HEADS), f32),
        'qk_norm_g': 1.0 + 0.02 * nrm(ks[9], (N_B, 2, HEAD_DIM), f32),
        'w_gate_up': nrm(ks[10], (DEPTH, D_MODEL, 2 * D_FF), f32) * D_MODEL ** -0.5,
        'w_down': nrm(ks[11], (DEPTH, D_FF, D_MODEL), f32) * D_FF ** -0.5,
        'g_ffn_pre': 1.0 + 0.02 * nrm(ks[12], (DEPTH, D_MODEL), f32),
        'g_ffn_post': 1.0 + 0.02 * nrm(ks[13], (DEPTH, D_MODEL), f32),
    }


def _fwd_reference(x, mem, mem_norm_g, w_in, w_mem_kv, w_o, g_mix_pre, g_mix_post, attn_sink,
              qk_norm_g, w_gate_up, w_down, g_ffn_pre, g_ffn_post):
    B, S, _ = x.shape
    rows = S // GRID_W
    pos = jnp.arange(S, dtype=jnp.int32)
    row_ids = jnp.repeat(jnp.arange(rows, dtype=jnp.int32), GRID_W)
    col_ids = jnp.tile(jnp.arange(GRID_W, dtype=jnp.int32), rows)
    cos_p, sin_p = rope_table(pos, ROPE_DIMS, ROPE_THETA)
    cos_r, sin_r = rope_table(row_ids, HEAD_DIM // 2, AXIAL_THETA)
    cos_c, sin_c = rope_table(col_ids, HEAD_DIM // 2, AXIAL_THETA)
    mem_n = rms_norm(mem, mem_norm_g)

    for i in range(DEPTH):
        h = rms_norm(x, g_mix_pre[i])
        proj = h @ w_in[i]
        q = proj[..., :Q_W].reshape(B, S, N_TOK_HEADS, HEAD_DIM)
        k = proj[..., Q_W:Q_W + KV_W].reshape(B, S, N_KV_HEADS, HEAD_DIM)
        v = proj[..., Q_W + KV_W:Q_W + 2 * KV_W].reshape(B, S, N_KV_HEADS, HEAD_DIM)
        qm = proj[..., Q_W + 2 * KV_W:].reshape(B, S, N_MEM_HEADS, HEAD_DIM)
        kind = i % N_MIXERS
        if kind == 0:
            tok = mixer_a(q, k, v, attn_sink[i // N_MIXERS], cos_p, sin_p)
        elif kind == 1:
            tok = mixer_b(q, k, v, qk_norm_g[i // N_MIXERS], cos_r, sin_r, cos_c, sin_c)
        else:
            tok = mixer_c(q, k, v, cos_p, sin_p)
        mkv = mem_n @ w_mem_kv[i]
        km = mkv[..., :QM_W].reshape(B, N_MEM, N_MEM_HEADS, HEAD_DIM)
        vm = mkv[..., QM_W:].reshape(B, N_MEM, N_MEM_HEADS, HEAD_DIM)
        mo = memory_attention(qm, km, vm)
        o = jnp.concatenate([tok, mo], axis=-1) @ w_o[i]
        x = x + rms_norm(o, g_mix_post[i])

        h = rms_norm(x, g_ffn_pre[i])
        gu = h @ w_gate_up[i]
        f = (jax.nn.silu(gu[..., :D_FF]) * gu[..., D_FF:]) @ w_down[i]
        x = x + rms_norm(f, g_ffn_post[i])
    return x


import jax as _jax
import jax.numpy as _jnp

TWIN_FORMAT = 'train_step'
FWD_PARAMS = ['x', 'mem', 'mem_norm_g', 'w_in', 'w_mem_kv', 'w_o', 'g_mix_pre', 'g_mix_post', 'attn_sink', 'qk_norm_g', 'w_gate_up', 'w_down', 'g_ffn_pre', 'g_ffn_post']
TWIN_WEIGHTS = ['mem_norm_g', 'w_in', 'w_mem_kv', 'w_o', 'g_mix_pre', 'g_mix_post', 'attn_sink', 'qk_norm_g', 'w_gate_up', 'w_down', 'g_ffn_pre', 'g_ffn_post']
TWIN_DIFF_INPUT = 'x'
TWIN_INPUTS = ['x', 'mem', 'mem_norm_g', 'w_in', 'w_mem_kv', 'w_o', 'g_mix_pre', 'g_mix_post', 'attn_sink', 'qk_norm_g', 'w_gate_up', 'w_down', 'g_ffn_pre', 'g_ffn_post', 'loss_target', 'm_mem_norm_g', 'm_w_in', 'm_w_mem_kv', 'm_w_o', 'm_g_mix_pre', 'm_g_mix_post', 'm_attn_sink', 'm_qk_norm_g', 'm_w_gate_up', 'm_w_down', 'm_g_ffn_pre', 'm_g_ffn_post', 'v_mem_norm_g', 'v_w_in', 'v_w_mem_kv', 'v_w_o', 'v_g_mix_pre', 'v_g_mix_post', 'v_attn_sink', 'v_qk_norm_g', 'v_w_gate_up', 'v_w_down', 'v_g_ffn_pre', 'v_g_ffn_post']
TWIN_OUTPUTS = ['loss', 'grad_x', 'grad_mem_norm_g', 'grad_w_in', 'grad_w_mem_kv', 'grad_w_o', 'grad_g_mix_pre', 'grad_g_mix_post', 'grad_attn_sink', 'grad_qk_norm_g', 'grad_w_gate_up', 'grad_w_down', 'grad_g_ffn_pre', 'grad_g_ffn_post', 'delta_mem_norm_g', 'delta_w_in', 'delta_w_mem_kv', 'delta_w_o', 'delta_g_mix_pre', 'delta_g_mix_post', 'delta_attn_sink', 'delta_qk_norm_g', 'delta_w_gate_up', 'delta_w_down', 'delta_g_ffn_pre', 'delta_g_ffn_post', 'new_m_mem_norm_g', 'new_m_w_in', 'new_m_w_mem_kv', 'new_m_w_o', 'new_m_g_mix_pre', 'new_m_g_mix_post', 'new_m_attn_sink', 'new_m_qk_norm_g', 'new_m_w_gate_up', 'new_m_w_down', 'new_m_g_ffn_pre', 'new_m_g_ffn_post', 'new_v_mem_norm_g', 'new_v_w_in', 'new_v_w_mem_kv', 'new_v_w_o', 'new_v_g_mix_pre', 'new_v_g_mix_post', 'new_v_attn_sink', 'new_v_qk_norm_g', 'new_v_w_gate_up', 'new_v_w_down', 'new_v_g_ffn_pre', 'new_v_g_ffn_post']
TWIN_LEAF_KINDS = {'loss': 'loss', 'grad_x': 'grad_x', 'grad_mem_norm_g': 'grad_w', 'grad_w_in': 'grad_w', 'grad_w_mem_kv': 'grad_w', 'grad_w_o': 'grad_w', 'grad_g_mix_pre': 'grad_w', 'grad_g_mix_post': 'grad_w', 'grad_attn_sink': 'grad_w', 'grad_qk_norm_g': 'grad_w', 'grad_w_gate_up': 'grad_w', 'grad_w_down': 'grad_w', 'grad_g_ffn_pre': 'grad_w', 'grad_g_ffn_post': 'grad_w', 'delta_mem_norm_g': 'delta_w', 'delta_w_in': 'delta_w', 'delta_w_mem_kv': 'delta_w', 'delta_w_o': 'delta_w', 'delta_g_mix_pre': 'delta_w', 'delta_g_mix_post': 'delta_w', 'delta_attn_sink': 'delta_w', 'delta_qk_norm_g': 'delta_w', 'delta_w_gate_up': 'delta_w', 'delta_w_down': 'delta_w', 'delta_g_ffn_pre': 'delta_w', 'delta_g_ffn_post': 'delta_w', 'new_m_mem_norm_g': 'new_m', 'new_m_w_in': 'new_m', 'new_m_w_mem_kv': 'new_m', 'new_m_w_o': 'new_m', 'new_m_g_mix_pre': 'new_m', 'new_m_g_mix_post': 'new_m', 'new_m_attn_sink': 'new_m', 'new_m_qk_norm_g': 'new_m', 'new_m_w_gate_up': 'new_m', 'new_m_w_down': 'new_m', 'new_m_g_ffn_pre': 'new_m', 'new_m_g_ffn_post': 'new_m', 'new_v_mem_norm_g': 'new_v', 'new_v_w_in': 'new_v', 'new_v_w_mem_kv': 'new_v', 'new_v_w_o': 'new_v', 'new_v_g_mix_pre': 'new_v', 'new_v_g_mix_post': 'new_v', 'new_v_attn_sink': 'new_v', 'new_v_qk_norm_g': 'new_v', 'new_v_w_gate_up': 'new_v', 'new_v_w_down': 'new_v', 'new_v_g_ffn_pre': 'new_v', 'new_v_g_ffn_post': 'new_v'}


def _forward(args):
    return _fwd_reference(*[args[k] for k in FWD_PARAMS])


def _output_shape():
    def fwd():
        inp = _fwd_setup_inputs(0)
        return _fwd_reference(*[inp[k] for k in FWD_PARAMS])
    out = _jax.eval_shape(fwd)
    return out.shape, out.dtype

N_MICROBATCH = 1
ADAM_LR = 0.001
ADAM_B1 = 0.9
ADAM_B2 = 0.999
ADAM_EPS = 1e-08
ADAM_WD = 0.01
ADAM_STEP = 10
PER_EXAMPLE_BATCH_AXIS = {'x': 0, 'mem': 0, 'loss_target': 0}
SHARED_INPUTS = []
_WEIGHT_DTYPES = {'mem_norm_g': _jnp.float32, 'w_in': _jnp.float32, 'w_mem_kv': _jnp.float32, 'w_o': _jnp.float32, 'g_mix_pre': _jnp.float32, 'g_mix_post': _jnp.float32, 'attn_sink': _jnp.float32, 'qk_norm_g': _jnp.float32, 'w_gate_up': _jnp.float32, 'w_down': _jnp.float32, 'g_ffn_pre': _jnp.float32, 'g_ffn_post': _jnp.float32}
MOMENT_SCALE = {'mem_norm_g': 8.373996e+01, 'w_in': 5.346124e+01, 'w_mem_kv': 5.861211e+01, 'w_o': 8.137652e+01, 'g_mix_pre': 5.886181e+01, 'g_mix_post': 1.522020e+02, 'attn_sink': 4.854092e-01, 'qk_norm_g': 6.742281e+00, 'w_gate_up': 1.207709e+01, 'w_down': 2.309372e+01, 'g_ffn_pre': 3.259627e+01, 'g_ffn_post': 1.257981e+02}


def _to_microbatches(a, axis):
    t = _jnp.moveaxis(a, axis, 0)
    t = t.reshape((N_MICROBATCH, t.shape[0] // N_MICROBATCH) + t.shape[1:])
    return _jnp.moveaxis(t, 1, axis + 1)


def setup_inputs(seed: int = 0) -> dict:
    inp = _fwd_setup_inputs(seed)
    key = _jax.random.fold_in(_jax.random.key(seed), 7919)
    shape, _ = _output_shape()
    out = dict(inp)
    out["loss_target"] = _jax.random.normal(_jax.random.fold_in(key, 0), shape, _jnp.float32)
    for i, name in enumerate(TWIN_WEIGHTS):
        w = inp[name].astype(_jnp.float32)
        if MOMENT_SCALE is None:
            s = _jnp.sqrt(_jnp.mean(_jnp.square(w)) + 1e-30)
        else:
            s = MOMENT_SCALE[name]
        km, kv = _jax.random.split(_jax.random.fold_in(key, i + 1))
        out[name] = w
        out["m_" + name] = s * _jax.random.normal(km, w.shape, _jnp.float32)
        out["v_" + name] = (s * s) * _jax.random.uniform(kv, w.shape, _jnp.float32, 0.5, 1.5)
    if N_MICROBATCH > 1:
        for name, axis in PER_EXAMPLE_BATCH_AXIS.items():
            out[name] = _to_microbatches(out[name], axis)
    return {'x': out['x'], 'mem': out['mem'], 'mem_norm_g': out['mem_norm_g'], 'w_in': out['w_in'], 'w_mem_kv': out['w_mem_kv'], 'w_o': out['w_o'], 'g_mix_pre': out['g_mix_pre'], 'g_mix_post': out['g_mix_post'], 'attn_sink': out['attn_sink'], 'qk_norm_g': out['qk_norm_g'], 'w_gate_up': out['w_gate_up'], 'w_down': out['w_down'], 'g_ffn_pre': out['g_ffn_pre'], 'g_ffn_post': out['g_ffn_post'], 'loss_target': out['loss_target'], 'm_mem_norm_g': out['m_mem_norm_g'], 'm_w_in': out['m_w_in'], 'm_w_mem_kv': out['m_w_mem_kv'], 'm_w_o': out['m_w_o'], 'm_g_mix_pre': out['m_g_mix_pre'], 'm_g_mix_post': out['m_g_mix_post'], 'm_attn_sink': out['m_attn_sink'], 'm_qk_norm_g': out['m_qk_norm_g'], 'm_w_gate_up': out['m_w_gate_up'], 'm_w_down': out['m_w_down'], 'm_g_ffn_pre': out['m_g_ffn_pre'], 'm_g_ffn_post': out['m_g_ffn_post'], 'v_mem_norm_g': out['v_mem_norm_g'], 'v_w_in': out['v_w_in'], 'v_w_mem_kv': out['v_w_mem_kv'], 'v_w_o': out['v_w_o'], 'v_g_mix_pre': out['v_g_mix_pre'], 'v_g_mix_post': out['v_g_mix_post'], 'v_attn_sink': out['v_attn_sink'], 'v_qk_norm_g': out['v_qk_norm_g'], 'v_w_gate_up': out['v_w_gate_up'], 'v_w_down': out['v_w_down'], 'v_g_ffn_pre': out['v_g_ffn_pre'], 'v_g_ffn_post': out['v_g_ffn_post']}


def _loss(weights, diff, rest, loss_target):
    with _jax.named_scope("forward"):
        args = {**rest, TWIN_DIFF_INPUT: diff, **{k: w.astype(_WEIGHT_DTYPES[k]) for k, w in weights.items()}}
        y = _forward(args)
    with _jax.named_scope("loss_head"):
        err = _jnp.square(y.astype(_jnp.float32) - loss_target)
        return 0.5 * _jnp.sum(_jnp.mean(err, axis=-1)) if err.ndim else 0.5 * err


def _adamw(w, g, m, v):
    m = ADAM_B1 * m + (1.0 - ADAM_B1) * g
    v = ADAM_B2 * v + (1.0 - ADAM_B2) * _jnp.square(g)
    m_hat = m / (1.0 - ADAM_B1 ** ADAM_STEP)
    v_hat = v / (1.0 - ADAM_B2 ** ADAM_STEP)
    delta = -ADAM_LR * (m_hat / (_jnp.sqrt(v_hat) + ADAM_EPS) + ADAM_WD * w)
    return delta, m, v


def reference(x, mem, mem_norm_g, w_in, w_mem_kv, w_o, g_mix_pre, g_mix_post, attn_sink, qk_norm_g, w_gate_up, w_down, g_ffn_pre, g_ffn_post, loss_target, m_mem_norm_g, m_w_in, m_w_mem_kv, m_w_o, m_g_mix_pre, m_g_mix_post, m_attn_sink, m_qk_norm_g, m_w_gate_up, m_w_down, m_g_ffn_pre, m_g_ffn_post, v_mem_norm_g, v_w_in, v_w_mem_kv, v_w_o, v_g_mix_pre, v_g_mix_post, v_attn_sink, v_qk_norm_g, v_w_gate_up, v_w_down, v_g_ffn_pre, v_g_ffn_post):
    given = dict(x=x, mem=mem, mem_norm_g=mem_norm_g, w_in=w_in, w_mem_kv=w_mem_kv, w_o=w_o, g_mix_pre=g_mix_pre, g_mix_post=g_mix_post, attn_sink=attn_sink, qk_norm_g=qk_norm_g, w_gate_up=w_gate_up, w_down=w_down, g_ffn_pre=g_ffn_pre, g_ffn_post=g_ffn_post, loss_target=loss_target, m_mem_norm_g=m_mem_norm_g, m_w_in=m_w_in, m_w_mem_kv=m_w_mem_kv, m_w_o=m_w_o, m_g_mix_pre=m_g_mix_pre, m_g_mix_post=m_g_mix_post, m_attn_sink=m_attn_sink, m_qk_norm_g=m_qk_norm_g, m_w_gate_up=m_w_gate_up, m_w_down=m_w_down, m_g_ffn_pre=m_g_ffn_pre, m_g_ffn_post=m_g_ffn_post, v_mem_norm_g=v_mem_norm_g, v_w_in=v_w_in, v_w_mem_kv=v_w_mem_kv, v_w_o=v_w_o, v_g_mix_pre=v_g_mix_pre, v_g_mix_post=v_g_mix_post, v_attn_sink=v_attn_sink, v_qk_norm_g=v_qk_norm_g, v_w_gate_up=v_w_gate_up, v_w_down=v_w_down, v_g_ffn_pre=v_g_ffn_pre, v_g_ffn_post=v_g_ffn_post)
    weights = {n: given[n] for n in TWIN_WEIGHTS}
    shared = {n: given[n] for n in SHARED_INPUTS}
    per_example = {n: given[n] for n in ['x', 'mem']}
    grad_fn = _jax.value_and_grad(_loss, argnums=(0, 1))

    def one_microbatch(ex, loss_target):
        ex = dict(ex)
        diff = ex.pop(TWIN_DIFF_INPUT)
        return grad_fn(weights, diff, {**shared, **ex}, loss_target)

    if N_MICROBATCH == 1:
        loss, (grad_w, grad_x) = one_microbatch(per_example, given["loss_target"])
    else:
        def body(carry, xs):
            loss_sum, grad_sum = carry
            l_k, (gw_k, gx_k) = one_microbatch(xs[0], xs[1])
            with _jax.named_scope("update"):
                return (loss_sum + l_k, _jax.tree.map(_jnp.add, grad_sum, gw_k)), gx_k

        init = (_jnp.zeros((), _jnp.float32), _jax.tree.map(_jnp.zeros_like, weights))
        (loss, grad_w), grad_x = _jax.lax.scan(body, init, (per_example, given["loss_target"]))
    with _jax.named_scope("update"):
        delta_w, new_m, new_v = {}, {}, {}
        for n in TWIN_WEIGHTS:
            delta_w[n], new_m[n], new_v[n] = _adamw(weights[n], grad_w[n], given["m_" + n], given["v_" + n])
    return (loss, grad_x, *[grad_w[n] for n in TWIN_WEIGHTS], *[delta_w[n] for n in TWIN_WEIGHTS],
            *[new_m[n] for n in TWIN_WEIGHTS], *[new_v[n] for n in TWIN_WEIGHTS])
```

```python
import functools

import jax
import jax.numpy as jnp
from jax import lax
from jax.experimental import pallas as pl
from jax.experimental.pallas import tpu as pltpu

F32 = jnp.float32
BF = jnp.bfloat16

D = 1024
HD = 64
LANES = 128
N_PAIRS = 6
DFF = 2816
IN_W = 1408
PW = 14 * LANES
K_T0, V_T0, M_T0 = 6, 9, 12
EPS = 1e-6
SCALE = HD ** -0.5
NEG = -1e30
ROPE_THETA = 500000.0
AXIAL_THETA = 10000.0
GRID_W = 64
A_RADIUS = 128
C_GROUPS = ((128, 1), (512, 4), (2048, 16))
N_MEM = 256
LR, B1, B2, AEPS, WD, STEP = 0.001, 0.9, 0.999, 1e-08, 0.01, 10
VMEM_LIMIT = 56 * 1024 * 1024
MESH = pl.DeviceIdType.MESH


def _pcall(body, *, name, grid, in_specs, out_specs, out_shape, scratch=()):
    return pl.pallas_call(
        body, name=name, grid=grid, in_specs=in_specs, out_specs=out_specs, out_shape=out_shape,
        scratch_shapes=scratch,
        compiler_params=pltpu.CompilerParams(dimension_semantics=("arbitrary",) * len(grid),
                                             vmem_limit_bytes=VMEM_LIMIT))


def _dot(a, b):
    return lax.dot_general(a, b, (((1,), (0,)), ((), ())), preferred_element_type=F32)


def _dot_nt(a, b):
    return lax.dot_general(a, b, (((1,), (1,)), ((), ())), preferred_element_type=F32)


def _lo(shape):
    return lax.broadcasted_iota(jnp.int32, shape, len(shape) - 1) < HD


def _half_sum(x, lo):
    a = jnp.sum(jnp.where(lo, x, 0.0), axis=-1, keepdims=True)
    b = jnp.sum(jnp.where(lo, 0.0, x), axis=-1, keepdims=True)
    return jnp.where(lo, a, b)


def _col(tile, lane):
    idx = lax.broadcasted_iota(jnp.int32, tile.shape, 1)
    return jnp.sum(jnp.where(idx == lane, tile, 0.0), axis=-1, keepdims=True)


def _split(t, lo):
    z = jnp.zeros_like(t)
    return jnp.where(lo, t, z), jnp.where(lo, z, t)


def _rms(xf, g):
    r = lax.rsqrt(jnp.mean(xf * xf, axis=-1, keepdims=True) + EPS)
    return xf * r * g


def _rms_bwd(xf, g, dy):
    r = lax.rsqrt(jnp.mean(xf * xf, axis=-1, keepdims=True) + EPS)
    xr = xf * r
    dg = jnp.sum(dy * xr, axis=0, keepdims=True)
    t = dy * g
    return r * (t - xr * jnp.mean(xr * t, axis=-1, keepdims=True)), dg


def _rope_fwd(y, c, s1, s2, sh):
    return y * c + pltpu.roll(y, sh, 1) * s1 + pltpu.roll(y, LANES - sh, 1) * s2


def _rope_bwd(dy, c, s1, s2, sh):
    return dy * c + pltpu.roll(dy * s1, LANES - sh, 1) + pltpu.roll(dy * s2, sh, 1)


def _tile(t):
    return slice(t * LANES, (t + 1) * LANES)


def inproj_fwd(x, g, w_pad, tabs, qkg, *, axial, tm=512):
    S = x.shape[0]
    sh = 16 if axial else 8

    def body(x_ref, g_ref, w_ref, c_ref, s1_ref, s2_ref, qkg_ref, h_ref, p_ref, *raw):
        h = _rms(x_ref[...], g_ref[...]).astype(BF)
        h_ref[...] = h
        acc = _dot(h, w_ref[...])
        c, s1, s2 = c_ref[...], s1_ref[...], s2_ref[...]
        lo = _lo((tm, LANES))
        for t in range(14):
            y = acc[:, _tile(t)]
            if t < V_T0:
                if axial:
                    raw[0][:, _tile(t)] = y.astype(BF)
                    gt = qkg_ref[0:1, :] if t < K_T0 else qkg_ref[1:2, :]
                    y = y * lax.rsqrt(_half_sum(y * y, lo) * (1.0 / HD) + EPS) * gt
                y = _rope_fwd(y, c, s1, s2, sh)
            if t < K_T0 or t >= M_T0:
                y = y * SCALE
            p_ref[:, _tile(t)] = y.astype(BF)

    row = lambda w: pl.BlockSpec((tm, w), lambda i: (i, 0))
    full = lambda a: pl.BlockSpec(a.shape, lambda i: (0, 0))
    out_shape = [jax.ShapeDtypeStruct((S, D), BF), jax.ShapeDtypeStruct((S, PW), BF)]
    out_specs = [row(D), row(PW)]
    if axial:
        out_shape.append(jax.ShapeDtypeStruct((S, V_T0 * LANES), BF))
        out_specs.append(row(V_T0 * LANES))
    return _pcall(body, name="inproj_fwd_axial" if axial else "inproj_fwd", grid=(S // tm,),
                  in_specs=[row(D), full(g), full(w_pad), row(LANES), row(LANES), row(LANES), full(qkg)],
                  out_specs=out_specs, out_shape=out_shape)(x, g, w_pad, *tabs, qkg)


def norm_mm(x, g, w, *, tm):
    S, N = x.shape[0], w.shape[1]

    def body(x_ref, g_ref, w_ref, h_ref, y_ref):
        h = _rms(x_ref[...], g_ref[...]).astype(BF)
        h_ref[...] = h
        y_ref[...] = _dot(h, w_ref[...]).astype(BF)

    return _pcall(body, name="norm_mm", grid=(S // tm,),
                  in_specs=[pl.BlockSpec((tm, D), lambda i: (i, 0)), pl.BlockSpec(g.shape, lambda i: (0, 0)),
                            pl.BlockSpec(w.shape, lambda i: (0, 0))],
                  out_specs=[pl.BlockSpec((tm, D), lambda i: (i, 0)), pl.BlockSpec((tm, N), lambda i: (i, 0))],
                  out_shape=[jax.ShapeDtypeStruct((S, D), BF), jax.ShapeDtypeStruct((S, N), BF)])(x, g, w)


def ffn_up_fwd(x, g, wg, wu, *, tm=512, tn=1408):
    S = x.shape[0]

    def body(x_ref, g_ref, wg_ref, wu_ref, h_ref, gate_ref, up_ref, a_ref, h_scr):
        @pl.when(pl.program_id(1) == 0)
        def _():
            h = _rms(x_ref[...], g_ref[...]).astype(BF)
            h_scr[...] = h
            h_ref[...] = h

        h = h_scr[...]
        gate = _dot(h, wg_ref[...])
        up = _dot(h, wu_ref[...])
        gate_ref[...] = gate.astype(BF)
        up_ref[...] = up.astype(BF)
        a_ref[...] = (gate * (1.0 / (1.0 + jnp.exp(-gate))) * up).astype(BF)

    rowd = pl.BlockSpec((tm, D), lambda i, j: (i, 0))
    wsp = pl.BlockSpec((D, tn), lambda i, j: (0, j))
    osp = pl.BlockSpec((tm, tn), lambda i, j: (i, j))
    sd = jax.ShapeDtypeStruct((S, DFF), BF)
    return _pcall(body, name="ffn_up_fwd", grid=(S // tm, DFF // tn),
                  in_specs=[rowd, pl.BlockSpec(g.shape, lambda i, j: (0, 0)), wsp, wsp],
                  out_specs=[rowd, osp, osp, osp],
                  out_shape=[jax.ShapeDtypeStruct((S, D), BF), sd, sd, sd],
                  scratch=[pltpu.VMEM((tm, D), BF)])(x, g, wg, wu)


def mm_norm_res(a, w, g, res, *, tm=512):
    S, K = a.shape

    def body(a_ref, w_ref, g_ref, res_ref, y_ref, o_ref):
        y = _dot(a_ref[...], w_ref[...])
        y_ref[...] = y
        o_ref[...] = res_ref[...] + _rms(y, g_ref[...])

    rowd = pl.BlockSpec((tm, D), lambda i: (i, 0))
    sd = jax.ShapeDtypeStruct((S, D), F32)
    return _pcall(body, name="mm_norm_res", grid=(S // tm,),
                  in_specs=[pl.BlockSpec((tm, K), lambda i: (i, 0)), pl.BlockSpec(w.shape, lambda i: (0, 0)),
                            pl.BlockSpec(g.shape, lambda i: (0, 0)), rowd],
                  out_specs=[rowd, rowd], out_shape=[sd, sd])(a, w, g, res)


def loss_bwd(y, tgt, *, tm=512):
    S = y.shape[0]

    def body(y_ref, t_ref, dy_ref, l_ref):
        @pl.when(pl.program_id(0) == 0)
        def _():
            l_ref[...] = jnp.zeros_like(l_ref)

        e = y_ref[...] - t_ref[...]
        dy_ref[...] = e * (1.0 / D)
        col = jnp.sum(e * e, axis=0, keepdims=True)
        part = col[:, _tile(0)]
        for t in range(1, D // LANES):
            part = part + col[:, _tile(t)]
        l_ref[...] += part * (0.5 / D)

    rowd = pl.BlockSpec((tm, D), lambda i: (i, 0))
    return _pcall(body, name="loss_bwd", grid=(S // tm,), in_specs=[rowd, rowd],
                  out_specs=[rowd, pl.BlockSpec((1, LANES), lambda i: (0, 0))],
                  out_shape=[jax.ShapeDtypeStruct((S, D), F32), jax.ShapeDtypeStruct((1, LANES), F32)])(y, tgt)


def normbwd_mm_cat(dy, ysaved, g, w, cat, *, tm=512):
    S = dy.shape[0]

    def body(dy_ref, y_ref, g_ref, w_ref, cat_ref, d_ref, dcat_ref, delta_ref, dg_ref):
        @pl.when(pl.program_id(0) == 0)
        def _():
            dg_ref[...] = jnp.zeros_like(dg_ref)

        d, dg = _rms_bwd(y_ref[...], g_ref[...], dy_ref[...])
        dg_ref[...] += dg
        d = d.astype(BF)
        d_ref[...] = d
        z = _dot_nt(d, w_ref[...])
        dcat_ref[...] = z.astype(BF)
        lo = _lo((tm, LANES))
        for t in range(D // LANES):
            delta_ref[:, _tile(t)] = _half_sum(z[:, _tile(t)] * cat_ref[:, _tile(t)].astype(F32), lo)

    rowd = pl.BlockSpec((tm, D), lambda i: (i, 0))
    return _pcall(body, name="normbwd_mm_cat", grid=(S // tm,),
                  in_specs=[rowd, rowd, pl.BlockSpec(g.shape, lambda i: (0, 0)),
                            pl.BlockSpec(w.shape, lambda i: (0, 0)), rowd],
                  out_specs=[rowd, rowd, rowd, pl.BlockSpec((1, D), lambda i: (0, 0))],
                  out_shape=[jax.ShapeDtypeStruct((S, D), BF), jax.ShapeDtypeStruct((S, D), BF),
                             jax.ShapeDtypeStruct((S, D), F32), jax.ShapeDtypeStruct((1, D), F32)])(dy, ysaved, g, w, cat)


def normbwd_mm_swiglu(dy, fsaved, g, wd, gate, up, *, tm=512, tn=1408):
    S = dy.shape[0]

    def body(dy_ref, f_ref, g_ref, w_ref, gate_ref, up_ref, df_ref, dgate_ref, dup_ref, dg_ref, d_scr):
        i, j = pl.program_id(0), pl.program_id(1)

        @pl.when((i == 0) & (j == 0))
        def _():
            dg_ref[...] = jnp.zeros_like(dg_ref)

        @pl.when(j == 0)
        def _():
            d, dg = _rms_bwd(f_ref[...], g_ref[...], dy_ref[...])
            dg_ref[...] += dg
            d_scr[...] = d.astype(BF)
            df_ref[...] = d.astype(BF)

        da = _dot_nt(d_scr[...], w_ref[...])
        gt = gate_ref[...].astype(F32)
        sig = 1.0 / (1.0 + jnp.exp(-gt))
        dgate_ref[...] = (da * up_ref[...].astype(F32) * (sig * (1.0 + gt * (1.0 - sig)))).astype(BF)
        dup_ref[...] = (da * (gt * sig)).astype(BF)

    rowd = pl.BlockSpec((tm, D), lambda i, j: (i, 0))
    osp = pl.BlockSpec((tm, tn), lambda i, j: (i, j))
    sd = jax.ShapeDtypeStruct((S, DFF), BF)
    return _pcall(body, name="normbwd_mm_swiglu", grid=(S // tm, DFF // tn),
                  in_specs=[rowd, rowd, pl.BlockSpec(g.shape, lambda i, j: (0, 0)),
                            pl.BlockSpec((tn, D), lambda i, j: (j, 0)), osp, osp],
                  out_specs=[rowd, osp, osp, pl.BlockSpec((1, D), lambda i, j: (0, 0))],
                  out_shape=[jax.ShapeDtypeStruct((S, D), BF), sd, sd, jax.ShapeDtypeStruct((1, D), F32)],
                  scratch=[pltpu.VMEM((tm, D), BF)])(dy, fsaved, g, wd, gate, up)


def mm_nt_normbwd_res(parts, xin, g, dres, *, tm=512, nk=1):
    S = xin.shape[0]
    npart = len(parts)
    kb = parts[0][0].shape[1] // nk
    has_res = dres is not None

    def body(*refs):
        prefs = refs[:2 * npart]
        x_ref, g_ref = refs[2 * npart:2 * npart + 2]
        rest = refs[2 * npart + 2:]
        if has_res:
            dres_ref, dx_ref, dg_ref, acc = rest
        else:
            dg_ref, acc = rest
        i, k = pl.program_id(0), pl.program_id(1)

        @pl.when((i == 0) & (k == 0))
        def _():
            dg_ref[...] = jnp.zeros_like(dg_ref)

        z = _dot_nt(prefs[0][...], prefs[1][...])
        for p in range(1, npart):
            z = z + _dot_nt(prefs[2 * p][...], prefs[2 * p + 1][...])

        @pl.when(k == 0)
        def _():
            acc[...] = z

        @pl.when(k > 0)
        def _():
            acc[...] += z

        @pl.when(k == nk - 1)
        def _():
            dx, dg = _rms_bwd(x_ref[...], g_ref[...], acc[...])
            dg_ref[...] += dg
            if has_res:
                dx_ref[...] = dres_ref[...] + dx

    rowd = pl.BlockSpec((tm, D), lambda i, k: (i, 0))
    in_specs, args = [], []
    for dy, w in parts:
        in_specs += [pl.BlockSpec((tm, kb), lambda i, k: (i, k)), pl.BlockSpec((D, kb), lambda i, k: (0, k))]
        args += [dy, w]
    in_specs += [rowd, pl.BlockSpec(g.shape, lambda i, k: (0, 0))]
    args += [xin, g]
    out_specs = [pl.BlockSpec((1, D), lambda i, k: (0, 0))]
    out_shape = [jax.ShapeDtypeStruct((1, D), F32)]
    if has_res:
        in_specs.append(rowd)
        args.append(dres)
        out_specs.insert(0, rowd)
        out_shape.insert(0, jax.ShapeDtypeStruct((S, D), F32))
    return _pcall(body, name="mm_nt_normbwd_res" if has_res else "mm_nt_normbwd", grid=(S // tm, nk),
                  in_specs=in_specs, out_specs=out_specs, out_shape=out_shape,
                  scratch=[pltpu.VMEM((tm, D), F32)])(*args)


def inproj_bwd(dq, dkp, dvp, dqm, tabs, raw, qkg, w_pad, xin, g, dres, *, axial, tm=512):
    S = xin.shape[0]
    sh = 16 if axial else 8

    def body(dq_ref, dk_ref, dv_ref, dm_ref, c_ref, s1_ref, s2_ref, raw_ref, qkg_ref, w_ref, x_ref, g_ref, dres_ref,
             dx_ref, dp_ref, dg_ref, dqk_ref):
        @pl.when(pl.program_id(0) == 0)
        def _():
            dg_ref[...] = jnp.zeros_like(dg_ref)
            dqk_ref[...] = jnp.zeros_like(dqk_ref)

        c, s1, s2 = c_ref[...], s1_ref[...], s2_ref[...]
        lo = _lo((tm, LANES))
        for t in range(14):
            if t < K_T0:
                y = dq_ref[:, _tile(t)] * SCALE
            elif t < V_T0:
                y = dk_ref[:, _tile(2 * (t - K_T0))] + dk_ref[:, _tile(2 * (t - K_T0) + 1)]
            elif t < M_T0:
                y = dv_ref[:, _tile(2 * (t - V_T0))] + dv_ref[:, _tile(2 * (t - V_T0) + 1)]
            else:
                y = dm_ref[:, _tile(t - M_T0)] * SCALE
            if t < V_T0:
                y = _rope_bwd(y, c, s1, s2, sh)
                if axial:
                    row = 0 if t < K_T0 else 1
                    xr = raw_ref[:, _tile(t)].astype(F32)
                    r = lax.rsqrt(_half_sum(xr * xr, lo) * (1.0 / HD) + EPS)
                    xn = xr * r
                    dqk_ref[row:row + 1, :] += jnp.sum(y * xn, axis=0, keepdims=True)
                    tt = y * qkg_ref[row:row + 1, :]
                    y = r * (tt - xn * (_half_sum(xn * tt, lo) * (1.0 / HD)))
            dp_ref[:, _tile(t)] = y.astype(BF)
        z = _dot_nt(dp_ref[...], w_ref[...])
        dx, dg = _rms_bwd(x_ref[...], g_ref[...], z)
        dg_ref[...] += dg
        dx_ref[...] = dres_ref[...] + dx

    row = lambda w: pl.BlockSpec((tm, w), lambda i: (i, 0))
    full = lambda a: pl.BlockSpec(a.shape, lambda i: (0, 0))
    return _pcall(body, name="inproj_bwd_axial" if axial else "inproj_bwd", grid=(S // tm,),
                  in_specs=[row(768), row(768), row(768), row(256), row(LANES), row(LANES), row(LANES),
                            row(raw.shape[1] if axial else LANES), full(qkg), full(w_pad), row(D), full(g), row(D)],
                  out_specs=[row(D), row(PW), pl.BlockSpec((1, D), lambda i: (0, 0)),
                             pl.BlockSpec((8, LANES), lambda i: (0, 0))],
                  out_shape=[jax.ShapeDtypeStruct((S, D), F32), jax.ShapeDtypeStruct((S, PW), BF),
                             jax.ShapeDtypeStruct((1, D), F32), jax.ShapeDtypeStruct((8, LANES), F32)])(
        dq, dkp, dvp, dqm, *tabs, raw, qkg, w_pad, xin, g, dres)


def mm_acc(at, b, *, tk, tn, ts):
    K, S = at.shape
    N = b.shape[1]
    ts = min(ts, S)

    def body(a_ref, b_ref, o_ref):
        z = _dot(a_ref[...], b_ref[...])

        @pl.when(pl.program_id(2) == 0)
        def _():
            o_ref[...] = z

        @pl.when(pl.program_id(2) > 0)
        def _():
            o_ref[...] += z

    return _pcall(body, name="mm_acc", grid=(K // tk, N // tn, S // ts),
                  in_specs=[pl.BlockSpec((tk, ts), lambda k, n, s: (k, s)), pl.BlockSpec((ts, tn), lambda k, n, s: (s, n))],
                  out_specs=pl.BlockSpec((tk, tn), lambda k, n, s: (k, n)),
                  out_shape=jax.ShapeDtypeStruct((K, N), F32))(at, b)


def _band_specs(L, d, R, TQ, width, tile_of):
    n = TQ // R
    nt = width // LANES
    last = L // R - 1
    col = lambda p, r: r * nt + tile_of(p)
    return [pl.BlockSpec((R, LANES), lambda p, r, i: (jnp.maximum(i * n - 1, 0), col(p, r))),
            pl.BlockSpec((TQ, LANES), lambda p, r, i: (i, col(p, r))),
            pl.BlockSpec((R, LANES), lambda p, r, i: (jnp.minimum((i + 1) * n, last), col(p, r)))]


def _band_valid(i, TQ, R, L, rows_are_window):
    W = TQ + 2 * R
    if rows_are_window:
        wpos = i * TQ - R + lax.broadcasted_iota(jnp.int32, (W, TQ), 0)
        cpos = i * TQ + lax.broadcasted_iota(jnp.int32, (W, TQ), 1)
    else:
        wpos = i * TQ - R + lax.broadcasted_iota(jnp.int32, (TQ, W), 1)
        cpos = i * TQ + lax.broadcasted_iota(jnp.int32, (TQ, W), 0)
    return (jnp.abs(wpos - cpos) <= R) & (wpos >= 0) & (wpos < L)


def banded_fwd(proj, sink, *, d, R, TQ, pair0, npairs, use_sink, o_dtype):
    S = proj.shape[0]
    L = S // d
    pv = proj.reshape(L, d * PW)
    ow = npairs * LANES

    def body(sink_ref, q_ref, kp, kc, kn, vp, vc, vn, o_ref, lse_ref):
        p, i = pl.program_id(0), pl.program_id(2)
        lo = _lo((TQ, LANES))
        qa, qb = _split(q_ref[...], lo)
        kw = jnp.concatenate([kp[...], kc[...], kn[...]], axis=0)
        vw = jnp.concatenate([vp[...], vc[...], vn[...]], axis=0)
        valid = _band_valid(i, TQ, R, L, False)
        ps, dens, lses = [], [], []
        for h, qh in enumerate((qa, qb)):
            s = jnp.where(valid, _dot_nt(qh, kw), NEG)
            m = jnp.max(s, axis=-1, keepdims=True)
            if use_sink:
                sk = sink_ref[2 * (pair0 + p) + h]
                m = jnp.maximum(m, sk)
            e = jnp.exp(s - m)
            den = jnp.sum(e, axis=-1, keepdims=True)
            if use_sink:
                den = den + jnp.exp(sk - m)
            ps.append(e.astype(BF))
            dens.append(den)
            lses.append(m + jnp.log(den))
        lov = _lo(vw.shape)
        v_lo, v_hi = _split(vw, lov)
        o = _dot(jnp.concatenate(ps, axis=1), jnp.concatenate([v_lo, v_hi], axis=0))
        o_ref[...] = (o / jnp.where(lo, dens[0], dens[1])).astype(o_dtype)
        lse_ref[...] = jnp.where(lo, lses[0], lses[1])

    qspec = pl.BlockSpec((TQ, LANES), lambda p, r, i: (i, r * 14 + pair0 + p))
    kspecs = _band_specs(L, d, R, TQ, PW, lambda p: K_T0 + (pair0 + p) // 2)
    vspecs = _band_specs(L, d, R, TQ, PW, lambda p: V_T0 + (pair0 + p) // 2)
    ospec = pl.BlockSpec((TQ, LANES), lambda p, r, i: (i, r * npairs + p))
    o, lse = _pcall(body, name="banded_fwd", grid=(npairs, d, L // TQ),
                    in_specs=[pl.BlockSpec(memory_space=pltpu.SMEM), qspec] + kspecs + vspecs,
                    out_specs=[ospec, ospec],
                    out_shape=[jax.ShapeDtypeStruct((L, d * ow), o_dtype), jax.ShapeDtypeStruct((L, d * ow), F32)])(
        sink, pv, pv, pv, pv, pv, pv, pv)
    return o.reshape(S, ow), lse.reshape(S, ow)


def banded_bwd_dq(proj, do, lse, delta, sink, *, d, R, TQ, pair0, npairs, use_sink):
    S = proj.shape[0]
    L = S // d
    pv = proj.reshape(L, d * PW)
    ow = npairs * LANES
    view = lambda a: a.reshape(L, d * ow)

    def body(sink_ref, q_ref, kp, kc, kn, vp, vc, vn, do_ref, lse_ref, delta_ref, dq_ref, dsink_ref):
        p, r, i = pl.program_id(0), pl.program_id(1), pl.program_id(2)

        @pl.when((r == 0) & (i == 0))
        def _():
            dsink_ref[...] = jnp.zeros_like(dsink_ref)

        lo = _lo((TQ, LANES))
        qa, qb = _split(q_ref[...], lo)
        doa, dob = _split(do_ref[...], lo)
        kw = jnp.concatenate([kp[...], kc[...], kn[...]], axis=0)
        vw = jnp.concatenate([vp[...], vc[...], vn[...]], axis=0)
        valid = _band_valid(i, TQ, R, L, False)
        lse_t, delta_t = lse_ref[...], delta_ref[...]
        dss, dsk = [], []
        for h, (qh, doh) in enumerate(((qa, doa), (qb, dob))):
            lse_h, delta_h = _col(lse_t, h * HD), _col(delta_t, h * HD)
            pr = jnp.where(valid, jnp.exp(_dot_nt(qh, kw) - lse_h), 0.0)
            dss.append((pr * (_dot_nt(doh, vw) - delta_h)).astype(BF))
            if use_sink:
                psink = jnp.exp(sink_ref[2 * (pair0 + p) + h] - lse_h)
                dsk.append(-jnp.sum(psink * delta_h, axis=0, keepdims=True))
        k_lo, k_hi = _split(kw, _lo(kw.shape))
        dq_ref[...] = _dot(jnp.concatenate(dss, axis=1), jnp.concatenate([k_lo, k_hi], axis=0))
        if use_sink:
            dsink_ref[...] += jnp.where(_lo((8, LANES)), dsk[0], dsk[1])

    qspec = pl.BlockSpec((TQ, LANES), lambda p, r, i: (i, r * 14 + pair0 + p))
    kspecs = _band_specs(L, d, R, TQ, PW, lambda p: K_T0 + (pair0 + p) // 2)
    vspecs = _band_specs(L, d, R, TQ, PW, lambda p: V_T0 + (pair0 + p) // 2)
    ospec = pl.BlockSpec((TQ, LANES), lambda p, r, i: (i, r * npairs + p))
    dq, dsink = _pcall(body, name="banded_bwd_dq", grid=(npairs, d, L // TQ),
                       in_specs=[pl.BlockSpec(memory_space=pltpu.SMEM), qspec] + kspecs + vspecs + [ospec, ospec, ospec],
                       out_specs=[ospec, pl.BlockSpec((8, LANES), lambda p, r, i: (p, 0))],
                       out_shape=[jax.ShapeDtypeStruct((L, d * ow), F32),
                                  jax.ShapeDtypeStruct((npairs * 8, LANES), F32)])(
        sink, pv, pv, pv, pv, pv, pv, pv, view(do), view(lse), view(delta))
    return dq.reshape(S, ow), dsink


def banded_bwd_dkv(proj, do, lse, delta, *, d, R, TK, pair0, npairs):
    S = proj.shape[0]
    L = S // d
    pv = proj.reshape(L, d * PW)
    ow = npairs * LANES
    view = lambda a: a.reshape(L, d * ow)

    def body(k_ref, v_ref, qp, qc, qn, dop, doc, don, lp, lc, ln, dp_, dc_, dn_, dk_ref, dv_ref):
        j = pl.program_id(2)
        W = TK + 2 * R
        low = _lo((W, LANES))
        qa, qb = _split(jnp.concatenate([qp[...], qc[...], qn[...]], axis=0), low)
        doa, dob = _split(jnp.concatenate([dop[...], doc[...], don[...]], axis=0), low)
        lse_t = jnp.concatenate([lp[...], lc[...], ln[...]], axis=0)
        delta_t = jnp.concatenate([dp_[...], dc_[...], dn_[...]], axis=0)
        k, v = k_ref[...], v_ref[...]
        valid = _band_valid(j, TK, R, L, True)
        prs, dss = [], []
        for h, (qh, doh) in enumerate(((qa, doa), (qb, dob))):
            pr = jnp.where(valid, jnp.exp(_dot_nt(qh, k) - _col(lse_t, h * HD)), 0.0)
            dss.append(pr * (_dot_nt(doh, v) - _col(delta_t, h * HD)))
            prs.append(pr)
        dv_ref[...] = _dot(jnp.concatenate(prs, axis=0).T.astype(BF), jnp.concatenate([doa, dob], axis=0))
        dk_ref[...] = _dot(jnp.concatenate(dss, axis=0).T.astype(BF), jnp.concatenate([qa, qb], axis=0))

    kspec = pl.BlockSpec((TK, LANES), lambda p, r, j: (j, r * 14 + K_T0 + (pair0 + p) // 2))
    vspec = pl.BlockSpec((TK, LANES), lambda p, r, j: (j, r * 14 + V_T0 + (pair0 + p) // 2))
    qspecs = _band_specs(L, d, R, TK, PW, lambda p: pair0 + p)
    ospecs = _band_specs(L, d, R, TK, ow, lambda p: p)
    ospec = pl.BlockSpec((TK, LANES), lambda p, r, j: (j, r * npairs + p))
    sd = jax.ShapeDtypeStruct((L, d * ow), F32)
    dk, dv = _pcall(body, name="banded_bwd_dkv", grid=(npairs, d, L // TK),
                    in_specs=[kspec, vspec] + qspecs + ospecs + ospecs + ospecs,
                    out_specs=[ospec, ospec], out_shape=[sd, sd])(
        pv, pv, pv, pv, pv, *([view(do)] * 3), *([view(lse)] * 3), *([view(delta)] * 3))
    return dk.reshape(S, ow), dv.reshape(S, ow)


def flash_fwd(proj, *, tq=512, tk=512):
    S = proj.shape[0]

    def body(q_ref, k_ref, v_ref, o_ref, lse_ref):
        lo = _lo((tq, LANES))
        qa, qb = _split(q_ref[...], lo)
        lov = _lo((tk, LANES))

        def step(j, carry):
            ma, la, mb, lb, acc = carry
            rows = pl.ds(pl.multiple_of(j * tk, tk), tk)
            k, v = k_ref[rows, :], v_ref[rows, :]
            outs = []
            for qh, m0, l0 in ((qa, ma, la), (qb, mb, lb)):
                s = _dot_nt(qh, k)
                m1 = jnp.maximum(m0, jnp.max(s, axis=-1, keepdims=True))
                al = jnp.exp(m0 - m1)
                e = jnp.exp(s - m1)
                outs.append((m1, al * l0 + jnp.sum(e, axis=-1, keepdims=True), al, e.astype(BF)))
            v_lo, v_hi = _split(v, lov)
            pvv = _dot(jnp.concatenate([outs[0][3], outs[1][3]], axis=1), jnp.concatenate([v_lo, v_hi], axis=0))
            acc = acc * jnp.where(lo, outs[0][2], outs[1][2]) + pvv
            return outs[0][0], outs[0][1], outs[1][0], outs[1][1], acc

        m_init = jnp.full((tq, 1), NEG, F32)
        l_init = jnp.zeros((tq, 1), F32)
        ma, la, mb, lb, acc = lax.fori_loop(0, S // tk, step,
                                            (m_init, l_init, m_init, l_init, jnp.zeros((tq, LANES), F32)))
        o_ref[...] = (acc / jnp.where(lo, la, lb)).astype(BF)
        lse_ref[...] = jnp.where(lo, ma + jnp.log(la), mb + jnp.log(lb))

    ospec = pl.BlockSpec((tq, LANES), lambda p, i: (i, p))
    return _pcall(body, name="flash_fwd", grid=(N_PAIRS, S // tq),
                  in_specs=[ospec, pl.BlockSpec((S, LANES), lambda p, i: (0, K_T0 + p // 2)),
                            pl.BlockSpec((S, LANES), lambda p, i: (0, V_T0 + p // 2))],
                  out_specs=[ospec, ospec],
                  out_shape=[jax.ShapeDtypeStruct((S, 768), BF), jax.ShapeDtypeStruct((S, 768), F32)])(proj, proj, proj)


def flash_bwd_dq(proj, do, lse, delta, *, tq=512, tk=512):
    S = proj.shape[0]

    def body(q_ref, k_ref, v_ref, do_ref, lse_ref, delta_ref, dq_ref):
        lo = _lo((tq, LANES))
        qa, qb = _split(q_ref[...], lo)
        doa, dob = _split(do_ref[...], lo)
        lse_t, delta_t = lse_ref[...], delta_ref[...]
        stats = [(_col(lse_t, 0), _col(delta_t, 0)), (_col(lse_t, HD), _col(delta_t, HD))]
        lov = _lo((tk, LANES))

        def step(j, acc):
            rows = pl.ds(pl.multiple_of(j * tk, tk), tk)
            k, v = k_ref[rows, :], v_ref[rows, :]
            dss = []
            for (qh, doh), (lse_h, delta_h) in zip(((qa, doa), (qb, dob)), stats):
                pr = jnp.exp(_dot_nt(qh, k) - lse_h)
                dss.append((pr * (_dot_nt(doh, v) - delta_h)).astype(BF))
            k_lo, k_hi = _split(k, lov)
            return acc + _dot(jnp.concatenate(dss, axis=1), jnp.concatenate([k_lo, k_hi], axis=0))

        dq_ref[...] = lax.fori_loop(0, S // tk, step, jnp.zeros((tq, LANES), F32))

    ospec = pl.BlockSpec((tq, LANES), lambda p, i: (i, p))
    return _pcall(body, name="flash_bwd_dq", grid=(N_PAIRS, S // tq),
                  in_specs=[ospec, pl.BlockSpec((S, LANES), lambda p, i: (0, K_T0 + p // 2)),
                            pl.BlockSpec((S, LANES), lambda p, i: (0, V_T0 + p // 2)), ospec, ospec, ospec],
                  out_specs=ospec, out_shape=jax.ShapeDtypeStruct((S, 768), F32))(proj, proj, proj, do, lse, delta)


def flash_bwd_dkv(proj, do, lse_t, delta_t, *, tq=512, tk=512):
    S = proj.shape[0]

    def body(k_ref, v_ref, q_ref, do_ref, lse_ref, delta_ref, dk_ref, dv_ref):
        k, v = k_ref[...], v_ref[...]
        lo = _lo((tq, LANES))

        def step(i, carry):
            dk, dv = carry
            rows = pl.ds(pl.multiple_of(i * tq, tq), tq)
            qa, qb = _split(q_ref[rows, :], lo)
            doa, dob = _split(do_ref[rows, :], lo)
            lse_i, delta_i = lse_ref[i], delta_ref[i]
            prs, dss = [], []
            for h, (qh, doh) in enumerate(((qa, doa), (qb, dob))):
                pr = jnp.exp(_dot_nt(k, qh) - lse_i[h:h + 1, :])
                dss.append((pr * (_dot_nt(v, doh) - delta_i[h:h + 1, :])).astype(BF))
                prs.append(pr.astype(BF))
            dv = dv + _dot(jnp.concatenate(prs, axis=1), jnp.concatenate([doa, dob], axis=0))
            dk = dk + _dot(jnp.concatenate(dss, axis=1), jnp.concatenate([qa, qb], axis=0))
            return dk, dv

        z = jnp.zeros((tk, LANES), F32)
        dk, dv = lax.fori_loop(0, S // tq, step, (z, z))
        dk_ref[...] = dk
        dv_ref[...] = dv

    ospec = pl.BlockSpec((tk, LANES), lambda p, j: (j, p))
    stat = pl.BlockSpec((None, S // tq, 8, tq), lambda p, j: (p, 0, 0, 0))
    sd = jax.ShapeDtypeStruct((S, 768), F32)
    return _pcall(body, name="flash_bwd_dkv", grid=(N_PAIRS, S // tk),
                  in_specs=[pl.BlockSpec((tk, LANES), lambda p, j: (j, K_T0 + p // 2)),
                            pl.BlockSpec((tk, LANES), lambda p, j: (j, V_T0 + p // 2)),
                            pl.BlockSpec((S, LANES), lambda p, j: (0, p)), pl.BlockSpec((S, LANES), lambda p, j: (0, p)),
                            stat, stat],
                  out_specs=[ospec, ospec], out_shape=[sd, sd])(proj, proj, proj, do, lse_t, delta_t)


def mem_fwd(proj, mkv, *, tq=512):
    S = proj.shape[0]

    def body(q_ref, km_ref, vm_ref, o_ref, lse_ref):
        lo = _lo((tq, LANES))
        lov = _lo((N_MEM, LANES))
        for t in range(2):
            qa, qb = _split(q_ref[:, _tile(t)], lo)
            km, vm = km_ref[:, _tile(t)], vm_ref[:, _tile(t)]
            ps, dens, lses = [], [], []
            for qh in (qa, qb):
                s = _dot_nt(qh, km)
                m = jnp.max(s, axis=-1, keepdims=True)
                e = jnp.exp(s - m)
                den = jnp.sum(e, axis=-1, keepdims=True)
                ps.append(e.astype(BF))
                dens.append(den)
                lses.append(m + jnp.log(den))
            v_lo, v_hi = _split(vm, lov)
            o = _dot(jnp.concatenate(ps, axis=1), jnp.concatenate([v_lo, v_hi], axis=0))
            o_ref[:, _tile(t)] = (o / jnp.where(lo, dens[0], dens[1])).astype(BF)
            lse_ref[:, _tile(t)] = jnp.where(lo, lses[0], lses[1])

    ospec = pl.BlockSpec((tq, 256), lambda i: (i, 0))
    return _pcall(body, name="mem_fwd", grid=(S // tq,),
                  in_specs=[pl.BlockSpec((tq, 256), lambda i: (i, M_T0 // 2)),
                            pl.BlockSpec((N_MEM, 256), lambda i: (0, 0)), pl.BlockSpec((N_MEM, 256), lambda i: (0, 1))],
                  out_specs=[ospec, ospec],
                  out_shape=[jax.ShapeDtypeStruct((S, 256), BF), jax.ShapeDtypeStruct((S, 256), F32)])(proj, mkv, mkv)


def mem_bwd(proj, mkv, dcat, lse, delta, *, tq=512):
    S = proj.shape[0]

    def body(q_ref, km_ref, vm_ref, do_ref, lse_ref, delta_ref, dq_ref, dkm_ref, dvm_ref):
        @pl.when(pl.program_id(0) == 0)
        def _():
            dkm_ref[...] = jnp.zeros_like(dkm_ref)
            dvm_ref[...] = jnp.zeros_like(dvm_ref)

        lo = _lo((tq, LANES))
        lov = _lo((N_MEM, LANES))
        for t in range(2):
            qa, qb = _split(q_ref[:, _tile(t)], lo)
            doa, dob = _split(do_ref[:, _tile(t)], lo)
            km, vm = km_ref[:, _tile(t)], vm_ref[:, _tile(t)]
            lse_t, delta_t = lse_ref[:, _tile(t)], delta_ref[:, _tile(t)]
            prs, dss = [], []
            for h, (qh, doh) in enumerate(((qa, doa), (qb, dob))):
                pr = jnp.exp(_dot_nt(qh, km) - _col(lse_t, h * HD))
                dss.append(pr * (_dot_nt(doh, vm) - _col(delta_t, h * HD)))
                prs.append(pr)
            k_lo, k_hi = _split(km, lov)
            dq_ref[:, _tile(t)] = _dot(jnp.concatenate(dss, axis=1).astype(BF), jnp.concatenate([k_lo, k_hi], axis=0))
            dvm_ref[:, _tile(t)] += _dot(jnp.concatenate(prs, axis=0).T.astype(BF), jnp.concatenate([doa, dob], axis=0))
            dkm_ref[:, _tile(t)] += _dot(jnp.concatenate(dss, axis=0).T.astype(BF), jnp.concatenate([qa, qb], axis=0))

    ospec = pl.BlockSpec((tq, 256), lambda i: (i, 0))
    msp = pl.BlockSpec((N_MEM, 256), lambda i: (0, 0))
    md = jax.ShapeDtypeStruct((N_MEM, 256), F32)
    dq, dkm, dvm = _pcall(body, name="mem_bwd", grid=(S // tq,),
                          in_specs=[pl.BlockSpec((tq, 256), lambda i: (i, M_T0 // 2)), msp,
                                    pl.BlockSpec((N_MEM, 256), lambda i: (0, 1)),
                                    pl.BlockSpec((tq, 256), lambda i: (i, 3)), ospec,
                                    pl.BlockSpec((tq, 256), lambda i: (i, 3))],
                          out_specs=[ospec, msp, msp], out_shape=[jax.ShapeDtypeStruct((S, 256), F32), md, md])(
        proj, mkv, mkv, dcat, lse, delta)
    return dq, jnp.concatenate([dkm, dvm], axis=1)


def combine_fwd(os_, lses, *, tm=512):
    S = os_[0].shape[0]

    def body(o0, o1, o2, l0, l1, l2, tok_ref):
        ls = [l0[...], l1[...], l2[...]]
        m = jnp.maximum(jnp.maximum(ls[0], ls[1]), ls[2])
        es = [jnp.exp(l - m) for l in ls]
        den = es[0] + es[1] + es[2]
        for g, o in enumerate((o0, o1, o2)):
            tok_ref[:, 256 * g:256 * (g + 1)] = (o[...] * (es[g] / den)).astype(BF)

    sp = pl.BlockSpec((tm, 256), lambda i: (i, 0))
    return _pcall(body, name="combine_fwd", grid=(S // tm,), in_specs=[sp] * 6,
                  out_specs=pl.BlockSpec((tm, 768), lambda i: (i, 0)),
                  out_shape=jax.ShapeDtypeStruct((S, 768), BF))(*os_, *lses)


def combine_bwd(dcat, os_, lses, *, tm=512):
    S = dcat.shape[0]

    def body(dt_ref, o0, o1, o2, l0, l1, l2, do0, do1, do2, de0, de1, de2):
        ls = [l0[...], l1[...], l2[...]]
        m = jnp.maximum(jnp.maximum(ls[0], ls[1]), ls[2])
        es = [jnp.exp(l - m) for l in ls]
        den = es[0] + es[1] + es[2]
        alphas = [e / den for e in es]
        lo = _lo((tm, LANES))
        dts = [dt_ref[:, 256 * g:256 * (g + 1)].astype(F32) for g in range(3)]
        dal = []
        for g, o in enumerate((o0, o1, o2)):
            pr = dts[g] * o[...]
            dal.append(jnp.concatenate([_half_sum(pr[:, _tile(0)], lo), _half_sum(pr[:, _tile(1)], lo)], axis=1))
        mix = alphas[0] * dal[0] + alphas[1] * dal[1] + alphas[2] * dal[2]
        for g, (do_ref, de_ref) in enumerate(((do0, de0), (do1, de1), (do2, de2))):
            do_ref[...] = (dts[g] * alphas[g]).astype(BF)
            de_ref[...] = alphas[g] * mix

    sp = pl.BlockSpec((tm, 256), lambda i: (i, 0))
    outs = _pcall(body, name="combine_bwd", grid=(S // tm,),
                  in_specs=[pl.BlockSpec((tm, 768), lambda i: (i, 0))] + [sp] * 6, out_specs=[sp] * 6,
                  out_shape=[jax.ShapeDtypeStruct((S, 256), BF)] * 3 + [jax.ShapeDtypeStruct((S, 256), F32)] * 3)(
        dcat, *os_, *lses)
    return outs[:3], outs[3:]


def _coords():
    return lax.axis_index("x"), lax.axis_index("y"), lax.axis_index("c")


def _other_chips(x, y):
    return [(1 - x, y), (x, 1 - y), (1 - x, 1 - y)]


HBM_SPEC = pl.BlockSpec(memory_space=pltpu.HBM)


def gather_shards(shards):
    n = len(shards)

    def body(*refs):
        ins, outs = refs[:n], refs[n:2 * n]
        send, recv, lsem = refs[2 * n:]
        x, y, c = _coords()
        me = 2 * x + y
        chips = _other_chips(x, y)
        local, sends = [], []
        for a in range(n):
            cp = pltpu.make_async_copy(ins[a], outs[a].at[me], lsem.at[a])
            cp.start()
            local.append(cp)
            for j, (px, py) in enumerate(chips):
                cp = pltpu.make_async_remote_copy(src_ref=ins[a], dst_ref=outs[a].at[me], send_sem=send.at[3 * a + j],
                                                  recv_sem=recv.at[3 * a + j], device_id=(px, py, c), device_id_type=MESH)
                cp.start()
                sends.append(cp)
        for a in range(n):
            for j, (px, py) in enumerate(chips):
                pltpu.make_async_remote_copy(src_ref=ins[a], dst_ref=outs[a].at[2 * px + py], send_sem=send.at[3 * a + j],
                                             recv_sem=recv.at[3 * a + j], device_id=(px, py, c),
                                             device_id_type=MESH).wait_recv()
        for cp in sends:
            cp.wait_send()
        for cp in local:
            cp.wait()

    return pl.pallas_call(
        body, name="gather_shards", in_specs=[HBM_SPEC] * n, out_specs=[HBM_SPEC] * n,
        out_shape=[jax.ShapeDtypeStruct((4,) + s.shape, s.dtype) for s in shards],
        scratch_shapes=[pltpu.SemaphoreType.DMA((3 * n,)), pltpu.SemaphoreType.DMA((3 * n,)),
                        pltpu.SemaphoreType.DMA((n,))])(*shards)


def scatter_grads(parts):
    n = len(parts)

    def body(*refs):
        ins, outs = refs[:n], refs[n:2 * n]
        send, recv = refs[2 * n:]
        x, y, c = _coords()
        chips = _other_chips(x, y)
        sends = []
        for a in range(n):
            for j, (px, py) in enumerate(chips):
                cp = pltpu.make_async_remote_copy(src_ref=ins[a].at[2 * px + py], dst_ref=outs[a].at[j],
                                                  send_sem=send.at[3 * a + j], recv_sem=recv.at[3 * a + j],
                                                  device_id=(px, py, c), device_id_type=MESH)
                cp.start()
                sends.append(cp)
        for cp in sends:
            cp.wait_recv()
        for cp in sends:
            cp.wait_send()

    return pl.pallas_call(
        body, name="scatter_grads", in_specs=[HBM_SPEC] * n, out_specs=[HBM_SPEC] * n,
        out_shape=[jax.ShapeDtypeStruct((3,) + p.shape[1:], p.dtype) for p in parts],
        scratch_shapes=[pltpu.SemaphoreType.DMA((3 * n,)), pltpu.SemaphoreType.DMA((3 * n,))])(*parts)


def sibling_swap(arrs):
    n = len(arrs)

    def body(*refs):
        ins, outs = refs[:n], refs[n:2 * n]
        send, recv = refs[2 * n:]
        x, y, c = _coords()
        cps = []
        for a in range(n):
            cp = pltpu.make_async_remote_copy(src_ref=ins[a], dst_ref=outs[a], send_sem=send.at[a], recv_sem=recv.at[a],
                                              device_id=(x, y, 1 - c), device_id_type=MESH)
            cp.start()
            cps.append(cp)
        for cp in cps:
            cp.wait_recv()
        for cp in cps:
            cp.wait_send()

    return pl.pallas_call(
        body, name="sibling_swap", in_specs=[HBM_SPEC] * n, out_specs=[HBM_SPEC] * n,
        out_shape=[jax.ShapeDtypeStruct(a.shape, a.dtype) for a in arrs],
        scratch_shapes=[pltpu.SemaphoreType.DMA((n,)), pltpu.SemaphoreType.DMA((n,))])(*arrs)


def allsum_small(v):
    rows = v.shape[0]

    def body(v_ref, tot_ref, gath_ref, send, recv):
        x, y, c = _coords()
        me = 4 * x + 2 * y + c
        gath_ref[me] = v_ref[...]
        cps = []
        for k in range(1, 8):
            fx, fy, fc = (k >> 2) & 1, (k >> 1) & 1, k & 1
            peer = (1 - x if fx else x, 1 - y if fy else y, 1 - c if fc else c)
            cp = pltpu.make_async_remote_copy(src_ref=v_ref, dst_ref=gath_ref.at[me], send_sem=send.at[k - 1],
                                              recv_sem=recv.at[k - 1], device_id=peer, device_id_type=MESH)
            cp.start()
            cps.append(cp)
        for cp in cps:
            cp.wait_recv()
        for cp in cps:
            cp.wait_send()
        tot = gath_ref[0]
        for k in range(1, 8):
            tot = tot + gath_ref[k]
        tot_ref[...] = tot

    vm = pl.BlockSpec(memory_space=pltpu.VMEM)
    tot, _ = pl.pallas_call(
        body, name="allsum_small", in_specs=[vm], out_specs=[vm, vm],
        out_shape=[jax.ShapeDtypeStruct((rows, LANES), F32), jax.ShapeDtypeStruct((8, rows, LANES), F32)],
        scratch_shapes=[pltpu.SemaphoreType.DMA((7,)), pltpu.SemaphoreType.DMA((7,))])(v)
    return tot


def sum_parts(own, recv, *, tr=256):
    R, C = own.shape
    tr = min(tr, R)

    def body(o_ref, r_ref, out_ref):
        out_ref[...] = ((o_ref[...] + r_ref[0].astype(F32)) + r_ref[1].astype(F32)) + r_ref[2].astype(F32)

    sp = pl.BlockSpec((tr, C), lambda i: (i, 0))
    return _pcall(body, name="sum_parts", grid=(R // tr,),
                  in_specs=[sp, pl.BlockSpec((3, tr, C), lambda i: (0, i, 0))], out_specs=sp,
                  out_shape=jax.ShapeDtypeStruct((R, C), F32))(own, recv)


def adamw(w, ga, gb, m, v, *, tr=256):
    R, C = w.shape
    tr = min(tr, R)
    two = gb is not None

    def body(*refs):
        if two:
            w_ref, ga_ref, gb_ref, m_ref, v_ref, g_out, d_out, m_out, v_out = refs
            g = ga_ref[...] + gb_ref[...]
        else:
            w_ref, ga_ref, m_ref, v_ref, g_out, d_out, m_out, v_out = refs
            g = ga_ref[...]
        mn = B1 * m_ref[...] + (1.0 - B1) * g
        vn = B2 * v_ref[...] + (1.0 - B2) * (g * g)
        m_hat = mn / (1.0 - B1 ** STEP)
        v_hat = vn / (1.0 - B2 ** STEP)
        g_out[...] = g
        d_out[...] = -LR * (m_hat / (jnp.sqrt(v_hat) + AEPS) + WD * w_ref[...])
        m_out[...] = mn
        v_out[...] = vn

    sp = pl.BlockSpec((tr, C), lambda i: (i, 0))
    args = [w, ga, gb, m, v] if two else [w, ga, m, v]
    sd = jax.ShapeDtypeStruct((R, C), F32)
    return _pcall(body, name="adamw", grid=(R // tr,), in_specs=[sp] * len(args), out_specs=[sp] * 4,
                  out_shape=[sd] * 4)(*args)


def _rope_tables(S):
    def table(pos, n_dims, theta):
        inv = theta ** (-(jnp.arange(0, n_dims, 2, dtype=jnp.float32) / n_dims))
        ang = pos.astype(jnp.float32)[:, None] * inv[None, :]
        return jnp.cos(ang), jnp.sin(ang)

    pos = jnp.arange(S, dtype=jnp.int32)
    rows = S // GRID_W
    row_ids = jnp.repeat(jnp.arange(rows, dtype=jnp.int32), GRID_W)
    col_ids = jnp.tile(jnp.arange(GRID_W, dtype=jnp.int32), rows)
    cp, sp = table(pos, HD // 4, ROPE_THETA)
    cr, sr = table(row_ids, HD // 2, AXIAL_THETA)
    cc, sc = table(col_ids, HD // 2, AXIAL_THETA)
    z = lambda n: jnp.zeros((S, n), F32)
    two = lambda t: jnp.concatenate([t, t], axis=1)
    partial = (two(jnp.concatenate([cp, cp, jnp.ones((S, 48), F32)], axis=1)),
               two(jnp.concatenate([z(8), sp, z(48)], axis=1)),
               two(jnp.concatenate([-sp, z(56)], axis=1)))
    axial = (two(jnp.concatenate([cr, cr, cc, cc], axis=1)),
             two(jnp.concatenate([z(16), sr, z(16), sc], axis=1)),
             two(jnp.concatenate([-sr, z(16), -sc, z(16)], axis=1)))
    return partial, axial


def _pad_w_in(w):
    cols = [w[:, :768]]
    for base in (768, 960):
        for g in range(3):
            kg = w[:, base + g * HD:base + (g + 1) * HD]
            cols += [kg, kg]
    cols.append(w[:, 1152:])
    return jnp.concatenate(cols, axis=1)


def _unpad_dw_in(dw):
    cols = [dw[:, :768]]
    for t0 in (K_T0, V_T0):
        for g in range(3):
            b = (t0 + g) * LANES
            cols.append(dw[:, b:b + HD] + dw[:, b + HD:b + LANES])
    cols.append(dw[:, M_T0 * LANES:])
    return jnp.concatenate(cols, axis=1)


def _stats_t(a, tq):
    S = a.shape[0]
    t = a.reshape(S, N_PAIRS, 2, HD)[:, :, :, 0]
    t = jnp.transpose(t, (1, 2, 0))
    t = jnp.pad(t, ((0, 0), (0, 6), (0, 0)))
    return jnp.transpose(t.reshape(N_PAIRS, 8, S // tq, tq), (0, 2, 1, 3))


def _fold(t):
    return t[..., :HD] + t[..., HD:]


def kernel(x, mem, mem_norm_g, w_in, w_mem_kv, w_o, g_mix_pre, g_mix_post, attn_sink, qk_norm_g, w_gate_up, w_down, g_ffn_pre, g_ffn_post, loss_target, m_mem_norm_g, m_w_in, m_w_mem_kv, m_w_o, m_g_mix_pre, m_g_mix_post, m_attn_sink, m_qk_norm_g, m_w_gate_up, m_w_down, m_g_ffn_pre, m_g_ffn_post, v_mem_norm_g, v_w_in, v_w_mem_kv, v_w_o, v_g_mix_pre, v_g_mix_post, v_attn_sink, v_qk_norm_g, v_w_gate_up, v_w_down, v_g_ffn_pre, v_g_ffn_post):
    S = x.shape[1]
    depth = w_in.shape[0]
    xs, memx, tgt = x[0], mem[0], loss_target[0]
    tab_p, tab_a = _rope_tables(S)
    row = lambda a: a.reshape(1, -1)

    gi, gm, go, gg, gd = gather_shards([w_in.astype(BF), w_mem_kv.astype(BF), w_o.astype(BF), w_gate_up.astype(BF),
                                        w_down.astype(BF)])
    W_in = jnp.concatenate([gi[s] for s in range(4)], axis=2)
    W_mkv = jnp.concatenate([gm[s] for s in range(4)], axis=1)
    W_o = jnp.concatenate([go[s] for s in range(4)], axis=1)
    W_g = jnp.concatenate([gg[0], gg[1]], axis=2)
    W_u = jnp.concatenate([gg[2], gg[3]], axis=2)
    W_d = jnp.concatenate([gd[s] for s in range(4)], axis=1)
    mem_g = row(mem_norm_g)
    zero_sink = jnp.zeros((12,), F32)
    qkg = jnp.pad(jnp.concatenate([qk_norm_g[0], qk_norm_g[0]], axis=1), ((0, 6), (0, 0)))
    no_qkg = jnp.zeros((8, LANES), F32)

    saved = []
    cur = xs
    for i in range(depth):
        kind = i % 3
        wp = _pad_w_in(W_in[i])
        sv = dict(x=cur, wp=wp)
        if kind == 1:
            h1, proj, raw = inproj_fwd(cur, row(g_mix_pre[i]), wp, tab_a, qkg, axial=True)
            sv["raw"] = raw
        else:
            h1, proj = inproj_fwd(cur, row(g_mix_pre[i]), wp, tab_p, no_qkg, axial=False)
        if kind == 0:
            tok, lse = banded_fwd(proj, attn_sink[i // 3], d=1, R=A_RADIUS, TQ=2 * A_RADIUS, pair0=0, npairs=6,
                                  use_sink=True, o_dtype=BF)
        elif kind == 1:
            tok, lse = flash_fwd(proj)
        else:
            os_, lses = [], []
            for g, (window, dil) in enumerate(C_GROUPS):
                rad = window // (2 * dil)
                o_g, l_g = banded_fwd(proj, zero_sink, d=dil, R=rad, TQ=2 * rad, pair0=2 * g, npairs=2, use_sink=False,
                                      o_dtype=F32)
                os_.append(o_g)
                lses.append(l_g)
            tok = combine_fwd(os_, lses)
            sv["os"], lse = os_, lses
        mem_n, mkv = norm_mm(memx, mem_g, W_mkv[i], tm=N_MEM)
        mo, mlse = mem_fwd(proj, mkv)
        cat = jnp.concatenate([tok, mo], axis=1)
        o, x2 = mm_norm_res(cat, W_o[i], row(g_mix_post[i]), cur)
        h2, gate, up, act = ffn_up_fwd(x2, row(g_ffn_pre[i]), W_g[i], W_u[i])
        f, x3 = mm_norm_res(act, W_d[i], row(g_ffn_post[i]), x2)
        sv.update(h1=h1, proj=proj, lse=lse, mem_n=mem_n, mkv=mkv, mlse=mlse, cat=cat, o=o, x2=x2, h2=h2, gate=gate,
                  up=up, act=act, f=f)
        saved.append(sv)
        cur = x3

    dcur, loss_vec = loss_bwd(cur, tgt)

    dW_in, dW_mkv, dW_o, dW_gu, dW_d = [None] * depth, [None] * depth, [None] * depth, [None] * depth, [None] * depth
    dg_pre, dg_post, dg_fpre, dg_fpost = [None] * depth, [None] * depth, [None] * depth, [None] * depth
    dg_mem = jnp.zeros((1, D), F32)
    dsinks, dqk = {}, None
    for i in reversed(range(depth)):
        sv = saved[i]
        kind = i % 3
        proj = sv["proj"]
        df, dgate, dup, dg_fpost[i] = normbwd_mm_swiglu(dcur, sv["f"], row(g_ffn_post[i]), W_d[i], sv["gate"], sv["up"])
        dx2, dg_fpre[i] = mm_nt_normbwd_res([(dgate, W_g[i]), (dup, W_u[i])], sv["x2"], row(g_ffn_pre[i]), dcur, nk=2)
        dW_d[i] = mm_acc(sv["act"].T, df, tk=1408, tn=D, ts=1024)
        h2t = sv["h2"].T
        dW_gu[i] = jnp.concatenate([mm_acc(h2t, dgate, tk=D, tn=1408, ts=1024), mm_acc(h2t, dup, tk=D, tn=1408, ts=1024)],
                                   axis=1)
        do, dcat, delta, dg_post[i] = normbwd_mm_cat(dx2, sv["o"], row(g_mix_post[i]), W_o[i], sv["cat"])
        dW_o[i] = mm_acc(sv["cat"].T, do, tk=D, tn=D, ts=1024)
        dqm, dmkv = mem_bwd(proj, sv["mkv"], dcat, sv["mlse"], delta)
        dmkv = dmkv.astype(BF)
        (dgm,) = mm_nt_normbwd_res([(dmkv, W_mkv[i])], memx, mem_g, None, tm=N_MEM)
        dg_mem = dg_mem + dgm
        dW_mkv[i] = mm_acc(sv["mem_n"].T, dmkv, tk=D, tn=512, ts=N_MEM)
        if kind == 0:
            sink = attn_sink[i // 3]
            args = dict(d=1, R=A_RADIUS, pair0=0, npairs=6)
            dq, dsk = banded_bwd_dq(proj, dcat[:, :768], sv["lse"], delta[:, :768], sink, TQ=2 * A_RADIUS, use_sink=True,
                                    **args)
            dkp, dvp = banded_bwd_dkv(proj, dcat[:, :768], sv["lse"], delta[:, :768], TK=2 * A_RADIUS, **args)
            dsinks[i // 3] = dsk.reshape(6, 8, 2, HD)[:, 0, :, 0].reshape(12)
        elif kind == 1:
            dtok, dl = dcat[:, :768], delta[:, :768]
            dq = flash_bwd_dq(proj, dtok, sv["lse"], dl)
            dkp, dvp = flash_bwd_dkv(proj, dtok, _stats_t(sv["lse"], 512), _stats_t(dl, 512))
        else:
            dos, des = combine_bwd(dcat, sv["os"], sv["lse"])
            dqs, dks, dvs = [], [], []
            for g, (window, dil) in enumerate(C_GROUPS):
                rad = window // (2 * dil)
                args = dict(d=dil, R=rad, pair0=2 * g, npairs=2)
                dq_g, _ = banded_bwd_dq(proj, dos[g], sv["lse"][g], des[g], zero_sink, TQ=2 * rad, use_sink=False, **args)
                dk_g, dv_g = banded_bwd_dkv(proj, dos[g], sv["lse"][g], des[g], TK=2 * rad, **args)
                dqs.append(dq_g)
                dks.append(dk_g)
                dvs.append(dv_g)
            dq, dkp, dvp = (jnp.concatenate(t, axis=1) for t in (dqs, dks, dvs))
        if kind == 1:
            dcur, dproj, dg_pre[i], dqk_t = inproj_bwd(dq, dkp, dvp, dqm, tab_a, sv["raw"], qkg, sv["wp"], sv["x"],
                                                       row(g_mix_pre[i]), dx2, axial=True)
            dqk = _fold(dqk_t[:2]).reshape(1, 2, HD)
        else:
            dcur, dproj, dg_pre[i], _ = inproj_bwd(dq, dkp, dvp, dqm, tab_p, proj, no_qkg, sv["wp"],
                                                   sv["x"], row(g_mix_pre[i]), dx2, axial=False)
        dW_in[i] = _unpad_dw_in(mm_acc(sv["h1"].T, dproj, tk=D, tn=896, ts=1024))

    x_i, y_i, _ = _coords()
    me = 2 * x_i + y_i
    big = [
        (jnp.transpose(jnp.stack(dW_in).reshape(depth, D, 4, IN_W // 4), (2, 0, 1, 3)), w_in, m_w_in, v_w_in),
        (jnp.transpose(jnp.stack(dW_mkv).reshape(depth, 4, D // 4, 512), (1, 0, 2, 3)), w_mem_kv, m_w_mem_kv, v_w_mem_kv),
        (jnp.transpose(jnp.stack(dW_o).reshape(depth, 4, D // 4, D), (1, 0, 2, 3)), w_o, m_w_o, v_w_o),
        (jnp.transpose(jnp.stack(dW_gu).reshape(depth, D, 4, 2 * DFF // 4), (2, 0, 1, 3)), w_gate_up, m_w_gate_up,
         v_w_gate_up),
        (jnp.transpose(jnp.stack(dW_d).reshape(depth, 4, DFF // 4, D), (1, 0, 2, 3)), w_down, m_w_down, v_w_down),
    ]
    recvs = scatter_grads([b[0].astype(BF) for b in big])
    parts = []
    for (g4, w, _, _), rc in zip(big, recvs):
        C = w.shape[-1]
        own = lax.dynamic_index_in_dim(g4, me, 0, keepdims=False).reshape(-1, C)
        parts.append(sum_parts(own, rc.reshape(3, -1, C)))
    sibs = sibling_swap(parts)
    big_out = []
    for (_, w, m, v), pa, pb in zip(big, parts, sibs):
        C = w.shape[-1]
        outs = adamw(w.reshape(-1, C), pa, pb, m.reshape(-1, C), v.reshape(-1, C))
        big_out.append([o.reshape(w.shape) for o in outs])

    small_w = [mem_norm_g, g_mix_pre, g_mix_post, attn_sink, qk_norm_g, g_ffn_pre, g_ffn_post]
    small_m = [m_mem_norm_g, m_g_mix_pre, m_g_mix_post, m_attn_sink, m_qk_norm_g, m_g_ffn_pre, m_g_ffn_post]
    small_v = [v_mem_norm_g, v_g_mix_pre, v_g_mix_post, v_attn_sink, v_qk_norm_g, v_g_ffn_pre, v_g_ffn_post]
    small_g = [dg_mem.reshape(D), jnp.concatenate(dg_pre, axis=0), jnp.concatenate(dg_post, axis=0),
               jnp.stack([dsinks[k] for k in sorted(dsinks)]), dqk, jnp.concatenate(dg_fpre, axis=0),
               jnp.concatenate(dg_fpost, axis=0)]
    sizes = [a.size for a in small_w]
    total = sum(sizes)
    rows_s = -(-(total + LANES) // (8 * LANES)) * 8

    def pack(arrs, extra=None):
        flat = jnp.concatenate([a.reshape(-1).astype(F32) for a in arrs])
        flat = jnp.pad(flat, (0, rows_s * LANES - LANES - total))
        tail = jnp.zeros((LANES,), F32) if extra is None else extra.reshape(LANES)
        return jnp.concatenate([flat, tail]).reshape(rows_s, LANES)

    tot = allsum_small(pack(small_g, loss_vec))
    loss = jnp.sum(tot[rows_s - 1])
    s_out = adamw(pack(small_w), tot, None, pack(small_m), pack(small_v))

    def unpack(buf):
        flat = buf.reshape(-1)
        out, off = [], 0
        for a, n in zip(small_w, sizes):
            out.append(flat[off:off + n].reshape(a.shape))
            off += n
        return out

    sg, sd_, sm, sv_ = (unpack(b) for b in s_out)

    def ordered(k):
        sm_ = (sg, sd_, sm, sv_)[k]
        b = [bo[k] for bo in big_out]
        return [sm_[0], b[0], b[1], b[2], sm_[1], sm_[2], sm_[3], sm_[4], b[3], b[4], sm_[5], sm_[6]]

    dx_out = dcur.reshape(1, S, D)
    return (loss, dx_out, *ordered(0), *ordered(1), *ordered(2), *ordered(3))
```

```python
import functools

import jax
import jax.numpy as jnp
from jax import lax
from jax.experimental import pallas as pl
from jax.experimental.pallas import tpu as pltpu

F32 = jnp.float32
BF = jnp.bfloat16

D = 1024
HD = 64
LANES = 128
N_PAIRS = 6
DFF = 2816
IN_W = 1408
PW = 14 * LANES
K_T0, V_T0, M_T0 = 6, 9, 12
EPS = 1e-6
SCALE = HD ** -0.5
NEG = -1e30
LOG2E = 1.4426950408889634
LN2 = 0.6931471805599453
MAX_PLAIN_SCORE = 40.0
ROPE_THETA = 500000.0
AXIAL_THETA = 10000.0
GRID_W = 64
A_RADIUS = 128
C_GROUPS = ((128, 1), (512, 4), (2048, 16))
N_MEM = 256
LR, B1, B2, AEPS, WD, STEP = 0.001, 0.9, 0.999, 1e-08, 0.01, 10
VMEM_LIMIT = 56 * 1024 * 1024
MESH = pl.DeviceIdType.MESH


def _pcall(body, *, name, grid, in_specs, out_specs, out_shape, scratch=()):
    return pl.pallas_call(
        body, name=name, grid=grid, in_specs=in_specs, out_specs=out_specs, out_shape=out_shape,
        scratch_shapes=scratch,
        compiler_params=pltpu.CompilerParams(dimension_semantics=("arbitrary",) * len(grid),
                                             vmem_limit_bytes=VMEM_LIMIT))


def _dot(a, b):
    return lax.dot_general(a, b, (((1,), (0,)), ((), ())), preferred_element_type=F32)


def _dot_nt(a, b):
    return lax.dot_general(a, b, (((1,), (1,)), ((), ())), preferred_element_type=F32)


def _lo(shape):
    return lax.broadcasted_iota(jnp.int32, shape, len(shape) - 1) < HD


def _half_sum(x, lo):
    a = jnp.sum(jnp.where(lo, x, 0.0), axis=-1, keepdims=True)
    b = jnp.sum(jnp.where(lo, 0.0, x), axis=-1, keepdims=True)
    return jnp.where(lo, a, b)


def _col(tile, lane):
    idx = lax.broadcasted_iota(jnp.int32, tile.shape, 1)
    return jnp.sum(jnp.where(idx == lane, tile, 0.0), axis=-1, keepdims=True)


def _split(t, lo):
    z = jnp.zeros_like(t)
    return jnp.where(lo, t, z), jnp.where(lo, z, t)


def _rms(xf, g):
    r = lax.rsqrt(jnp.mean(xf * xf, axis=-1, keepdims=True) + EPS)
    return xf * r * g


def _rms_bwd(xf, g, dy):
    r = lax.rsqrt(jnp.mean(xf * xf, axis=-1, keepdims=True) + EPS)
    xr = xf * r
    dg = jnp.sum(dy * xr, axis=0, keepdims=True)
    t = dy * g
    return r * (t - xr * jnp.mean(xr * t, axis=-1, keepdims=True)), dg


def _rope_fwd(y, c, s1, s2, sh):
    return y * c + pltpu.roll(y, sh, 1) * s1 + pltpu.roll(y, LANES - sh, 1) * s2


def _rope_bwd(dy, c, s1, s2, sh):
    return dy * c + pltpu.roll(dy * s1, LANES - sh, 1) + pltpu.roll(dy * s2, sh, 1)


def _tile(t):
    return slice(t * LANES, (t + 1) * LANES)


def inproj_fwd(x, g, w_pad, tabs, qkg, *, axial, tm=512):
    S = x.shape[0]
    sh = 16 if axial else 8

    def body(x_ref, g_ref, w_ref, c_ref, s1_ref, s2_ref, qkg_ref, h_ref, p_ref, *extra):
        h = _rms(x_ref[...], g_ref[...]).astype(BF)
        h_ref[...] = h
        acc = _dot(h, w_ref[...])
        c, s1, s2 = c_ref[...], s1_ref[...], s2_ref[...]
        lo = _lo((tm, LANES))
        if axial:
            raw_ref, nrm_ref = extra

            @pl.when(pl.program_id(0) == 0)
            def _():
                nrm_ref[...] = jnp.zeros_like(nrm_ref)

        for t in range(14):
            y = acc[:, _tile(t)]
            if t < V_T0:
                if axial:
                    raw_ref[:, _tile(t)] = y.astype(BF)
                    gt = qkg_ref[0:1, :] if t < K_T0 else qkg_ref[1:2, :]
                    y = y * lax.rsqrt(_half_sum(y * y, lo) * (1.0 / HD) + EPS) * gt
                y = _rope_fwd(y, c, s1, s2, sh)
            if t < K_T0:
                y = y * (SCALE * LOG2E if axial else SCALE)
            elif t >= M_T0:
                y = y * SCALE
            yb = y.astype(BF)
            p_ref[:, _tile(t)] = yb
            if axial and t < V_T0:
                yf = yb.astype(F32)
                n2 = jnp.max(_half_sum(yf * yf, lo), axis=0, keepdims=True)
                r = 0 if t < K_T0 else 1
                nrm_ref[r:r + 1, :] = jnp.maximum(nrm_ref[r:r + 1, :], n2)

    row = lambda w: pl.BlockSpec((tm, w), lambda i: (i, 0))
    full = lambda a: pl.BlockSpec(a.shape, lambda i: (0, 0))
    out_shape = [jax.ShapeDtypeStruct((S, D), BF), jax.ShapeDtypeStruct((S, PW), BF)]
    out_specs = [row(D), row(PW)]
    if axial:
        out_shape += [jax.ShapeDtypeStruct((S, V_T0 * LANES), BF), jax.ShapeDtypeStruct((8, LANES), F32)]
        out_specs += [row(V_T0 * LANES), pl.BlockSpec((8, LANES), lambda i: (0, 0))]
    return _pcall(body, name="inproj_fwd_axial" if axial else "inproj_fwd", grid=(S // tm,),
                  in_specs=[row(D), full(g), full(w_pad), row(LANES), row(LANES), row(LANES), full(qkg)],
                  out_specs=out_specs, out_shape=out_shape)(x, g, w_pad, *tabs, qkg)


def norm_mm(x, g, w, *, tm):
    S, N = x.shape[0], w.shape[1]

    def body(x_ref, g_ref, w_ref, h_ref, y_ref):
        h = _rms(x_ref[...], g_ref[...]).astype(BF)
        h_ref[...] = h
        y_ref[...] = _dot(h, w_ref[...]).astype(BF)

    return _pcall(body, name="norm_mm", grid=(S // tm,),
                  in_specs=[pl.BlockSpec((tm, D), lambda i: (i, 0)), pl.BlockSpec(g.shape, lambda i: (0, 0)),
                            pl.BlockSpec(w.shape, lambda i: (0, 0))],
                  out_specs=[pl.BlockSpec((tm, D), lambda i: (i, 0)), pl.BlockSpec((tm, N), lambda i: (i, 0))],
                  out_shape=[jax.ShapeDtypeStruct((S, D), BF), jax.ShapeDtypeStruct((S, N), BF)])(x, g, w)


def ffn_up_fwd(x, g, wg, wu, *, tm=512, tn=1408):
    S = x.shape[0]

    def body(x_ref, g_ref, wg_ref, wu_ref, h_ref, gate_ref, up_ref, a_ref, h_scr):
        @pl.when(pl.program_id(1) == 0)
        def _():
            h = _rms(x_ref[...], g_ref[...]).astype(BF)
            h_scr[...] = h
            h_ref[...] = h

        h = h_scr[...]
        gate = _dot(h, wg_ref[...])
        up = _dot(h, wu_ref[...])
        gate_ref[...] = gate.astype(BF)
        up_ref[...] = up.astype(BF)
        a_ref[...] = (gate * (1.0 / (1.0 + jnp.exp(-gate))) * up).astype(BF)

    rowd = pl.BlockSpec((tm, D), lambda i, j: (i, 0))
    wsp = pl.BlockSpec((D, tn), lambda i, j: (0, j))
    osp = pl.BlockSpec((tm, tn), lambda i, j: (i, j))
    sd = jax.ShapeDtypeStruct((S, DFF), BF)
    return _pcall(body, name="ffn_up_fwd", grid=(S // tm, DFF // tn),
                  in_specs=[rowd, pl.BlockSpec(g.shape, lambda i, j: (0, 0)), wsp, wsp],
                  out_specs=[rowd, osp, osp, osp],
                  out_shape=[jax.ShapeDtypeStruct((S, D), BF), sd, sd, sd],
                  scratch=[pltpu.VMEM((tm, D), BF)])(x, g, wg, wu)


def mm_norm_res(a, w, g, res, *, tm=512):
    S, K = a.shape

    def body(a_ref, w_ref, g_ref, res_ref, y_ref, o_ref):
        y = _dot(a_ref[...], w_ref[...])
        y_ref[...] = y
        o_ref[...] = res_ref[...] + _rms(y, g_ref[...])

    rowd = pl.BlockSpec((tm, D), lambda i: (i, 0))
    sd = jax.ShapeDtypeStruct((S, D), F32)
    return _pcall(body, name="mm_norm_res", grid=(S // tm,),
                  in_specs=[pl.BlockSpec((tm, K), lambda i: (i, 0)), pl.BlockSpec(w.shape, lambda i: (0, 0)),
                            pl.BlockSpec(g.shape, lambda i: (0, 0)), rowd],
                  out_specs=[rowd, rowd], out_shape=[sd, sd])(a, w, g, res)


def loss_bwd(y, tgt, *, tm=512):
    S = y.shape[0]

    def body(y_ref, t_ref, dy_ref, l_ref):
        @pl.when(pl.program_id(0) == 0)
        def _():
            l_ref[...] = jnp.zeros_like(l_ref)

        e = y_ref[...] - t_ref[...]
        dy_ref[...] = e * (1.0 / D)
        col = jnp.sum(e * e, axis=0, keepdims=True)
        part = col[:, _tile(0)]
        for t in range(1, D // LANES):
            part = part + col[:, _tile(t)]
        l_ref[...] += part * (0.5 / D)

    rowd = pl.BlockSpec((tm, D), lambda i: (i, 0))
    return _pcall(body, name="loss_bwd", grid=(S // tm,), in_specs=[rowd, rowd],
                  out_specs=[rowd, pl.BlockSpec((1, LANES), lambda i: (0, 0))],
                  out_shape=[jax.ShapeDtypeStruct((S, D), F32), jax.ShapeDtypeStruct((1, LANES), F32)])(y, tgt)


def normbwd_mm_cat(dy, ysaved, g, w, cat, *, tm=512):
    S = dy.shape[0]

    def body(dy_ref, y_ref, g_ref, w_ref, cat_ref, d_ref, dcat_ref, delta_ref, dg_ref):
        @pl.when(pl.program_id(0) == 0)
        def _():
            dg_ref[...] = jnp.zeros_like(dg_ref)

        d, dg = _rms_bwd(y_ref[...], g_ref[...], dy_ref[...])
        dg_ref[...] += dg
        d = d.astype(BF)
        d_ref[...] = d
        z = _dot_nt(d, w_ref[...])
        dcat_ref[...] = z.astype(BF)
        lo = _lo((tm, LANES))
        for t in range(D // LANES):
            delta_ref[:, _tile(t)] = _half_sum(z[:, _tile(t)] * cat_ref[:, _tile(t)].astype(F32), lo)

    rowd = pl.BlockSpec((tm, D), lambda i: (i, 0))
    return _pcall(body, name="normbwd_mm_cat", grid=(S // tm,),
                  in_specs=[rowd, rowd, pl.BlockSpec(g.shape, lambda i: (0, 0)),
                            pl.BlockSpec(w.shape, lambda i: (0, 0)), rowd],
                  out_specs=[rowd, rowd, rowd, pl.BlockSpec((1, D), lambda i: (0, 0))],
                  out_shape=[jax.ShapeDtypeStruct((S, D), BF), jax.ShapeDtypeStruct((S, D), BF),
                             jax.ShapeDtypeStruct((S, D), F32), jax.ShapeDtypeStruct((1, D), F32)])(dy, ysaved, g, w, cat)


def normbwd_mm_swiglu(dy, fsaved, g, wd, gate, up, *, tm=512, tn=1408):
    S = dy.shape[0]

    def body(dy_ref, f_ref, g_ref, w_ref, gate_ref, up_ref, df_ref, dgate_ref, dup_ref, dg_ref, d_scr):
        i, j = pl.program_id(0), pl.program_id(1)

        @pl.when((i == 0) & (j == 0))
        def _():
            dg_ref[...] = jnp.zeros_like(dg_ref)

        @pl.when(j == 0)
        def _():
            d, dg = _rms_bwd(f_ref[...], g_ref[...], dy_ref[...])
            dg_ref[...] += dg
            d_scr[...] = d.astype(BF)
            df_ref[...] = d.astype(BF)

        da = _dot_nt(d_scr[...], w_ref[...])
        gt = gate_ref[...].astype(F32)
        sig = 1.0 / (1.0 + jnp.exp(-gt))
        dgate_ref[...] = (da * up_ref[...].astype(F32) * (sig * (1.0 + gt * (1.0 - sig)))).astype(BF)
        dup_ref[...] = (da * (gt * sig)).astype(BF)

    rowd = pl.BlockSpec((tm, D), lambda i, j: (i, 0))
    osp = pl.BlockSpec((tm, tn), lambda i, j: (i, j))
    sd = jax.ShapeDtypeStruct((S, DFF), BF)
    return _pcall(body, name="normbwd_mm_swiglu", grid=(S // tm, DFF // tn),
                  in_specs=[rowd, rowd, pl.BlockSpec(g.shape, lambda i, j: (0, 0)),
                            pl.BlockSpec((tn, D), lambda i, j: (j, 0)), osp, osp],
                  out_specs=[rowd, osp, osp, pl.BlockSpec((1, D), lambda i, j: (0, 0))],
                  out_shape=[jax.ShapeDtypeStruct((S, D), BF), sd, sd, jax.ShapeDtypeStruct((1, D), F32)],
                  scratch=[pltpu.VMEM((tm, D), BF)])(dy, fsaved, g, wd, gate, up)


def mm_nt_normbwd_res(parts, xin, g, dres, *, tm=512, nk=1):
    S = xin.shape[0]
    npart = len(parts)
    kb = parts[0][0].shape[1] // nk
    has_res = dres is not None

    def body(*refs):
        prefs = refs[:2 * npart]
        x_ref, g_ref = refs[2 * npart:2 * npart + 2]
        rest = refs[2 * npart + 2:]
        if has_res:
            dres_ref, dx_ref, dg_ref, acc = rest
        else:
            dg_ref, acc = rest
        i, k = pl.program_id(0), pl.program_id(1)

        @pl.when((i == 0) & (k == 0))
        def _():
            dg_ref[...] = jnp.zeros_like(dg_ref)

        z = _dot_nt(prefs[0][...], prefs[1][...])
        for p in range(1, npart):
            z = z + _dot_nt(prefs[2 * p][...], prefs[2 * p + 1][...])

        @pl.when(k == 0)
        def _():
            acc[...] = z

        @pl.when(k > 0)
        def _():
            acc[...] += z

        @pl.when(k == nk - 1)
        def _():
            dx, dg = _rms_bwd(x_ref[...], g_ref[...], acc[...])
            dg_ref[...] += dg
            if has_res:
                dx_ref[...] = dres_ref[...] + dx

    rowd = pl.BlockSpec((tm, D), lambda i, k: (i, 0))
    in_specs, args = [], []
    for dy, w in parts:
        in_specs += [pl.BlockSpec((tm, kb), lambda i, k: (i, k)), pl.BlockSpec((D, kb), lambda i, k: (0, k))]
        args += [dy, w]
    in_specs += [rowd, pl.BlockSpec(g.shape, lambda i, k: (0, 0))]
    args += [xin, g]
    out_specs = [pl.BlockSpec((1, D), lambda i, k: (0, 0))]
    out_shape = [jax.ShapeDtypeStruct((1, D), F32)]
    if has_res:
        in_specs.append(rowd)
        args.append(dres)
        out_specs.insert(0, rowd)
        out_shape.insert(0, jax.ShapeDtypeStruct((S, D), F32))
    return _pcall(body, name="mm_nt_normbwd_res" if has_res else "mm_nt_normbwd", grid=(S // tm, nk),
                  in_specs=in_specs, out_specs=out_specs, out_shape=out_shape,
                  scratch=[pltpu.VMEM((tm, D), F32)])(*args)


def inproj_bwd(dq, dkp, dvp, dqm, tabs, raw, qkg, w_pad, xin, g, dres, *, axial, tm=512):
    S = xin.shape[0]
    sh = 16 if axial else 8

    def body(dq_ref, dk_ref, dv_ref, dm_ref, c_ref, s1_ref, s2_ref, raw_ref, qkg_ref, w_ref, x_ref, g_ref, dres_ref,
             dx_ref, dp_ref, dg_ref, dqk_ref):
        @pl.when(pl.program_id(0) == 0)
        def _():
            dg_ref[...] = jnp.zeros_like(dg_ref)
            dqk_ref[...] = jnp.zeros_like(dqk_ref)

        c, s1, s2 = c_ref[...], s1_ref[...], s2_ref[...]
        lo = _lo((tm, LANES))
        for t in range(14):
            if t < K_T0:
                y = dq_ref[:, _tile(t)] * SCALE
            elif t < V_T0:
                y = dk_ref[:, _tile(2 * (t - K_T0))] + dk_ref[:, _tile(2 * (t - K_T0) + 1)]
                if axial:
                    y = y * LN2
            elif t < M_T0:
                y = dv_ref[:, _tile(2 * (t - V_T0))] + dv_ref[:, _tile(2 * (t - V_T0) + 1)]
            else:
                y = dm_ref[:, _tile(t - M_T0)] * SCALE
            if t < V_T0:
                y = _rope_bwd(y, c, s1, s2, sh)
                if axial:
                    row = 0 if t < K_T0 else 1
                    xr = raw_ref[:, _tile(t)].astype(F32)
                    r = lax.rsqrt(_half_sum(xr * xr, lo) * (1.0 / HD) + EPS)
                    xn = xr * r
                    dqk_ref[row:row + 1, :] += jnp.sum(y * xn, axis=0, keepdims=True)
                    tt = y * qkg_ref[row:row + 1, :]
                    y = r * (tt - xn * (_half_sum(xn * tt, lo) * (1.0 / HD)))
            dp_ref[:, _tile(t)] = y.astype(BF)
        z = _dot_nt(dp_ref[...], w_ref[...])
        dx, dg = _rms_bwd(x_ref[...], g_ref[...], z)
        dg_ref[...] += dg
        dx_ref[...] = dres_ref[...] + dx

    row = lambda w: pl.BlockSpec((tm, w), lambda i: (i, 0))
    full = lambda a: pl.BlockSpec(a.shape, lambda i: (0, 0))
    return _pcall(body, name="inproj_bwd_axial" if axial else "inproj_bwd", grid=(S // tm,),
                  in_specs=[row(768), row(768), row(768), row(256), row(LANES), row(LANES), row(LANES),
                            row(raw.shape[1] if axial else LANES), full(qkg), full(w_pad), row(D), full(g), row(D)],
                  out_specs=[row(D), row(PW), pl.BlockSpec((1, D), lambda i: (0, 0)),
                             pl.BlockSpec((8, LANES), lambda i: (0, 0))],
                  out_shape=[jax.ShapeDtypeStruct((S, D), F32), jax.ShapeDtypeStruct((S, PW), BF),
                             jax.ShapeDtypeStruct((1, D), F32), jax.ShapeDtypeStruct((8, LANES), F32)])(
        dq, dkp, dvp, dqm, *tabs, raw, qkg, w_pad, xin, g, dres)


def mm_acc(at, b, *, tk, tn, ts):
    K, S = at.shape
    N = b.shape[1]
    ts = min(ts, S)

    def body(a_ref, b_ref, o_ref):
        z = _dot(a_ref[...], b_ref[...])

        @pl.when(pl.program_id(2) == 0)
        def _():
            o_ref[...] = z

        @pl.when(pl.program_id(2) > 0)
        def _():
            o_ref[...] += z

    return _pcall(body, name="mm_acc", grid=(K // tk, N // tn, S // ts),
                  in_specs=[pl.BlockSpec((tk, ts), lambda k, n, s: (k, s)), pl.BlockSpec((ts, tn), lambda k, n, s: (s, n))],
                  out_specs=pl.BlockSpec((tk, tn), lambda k, n, s: (k, n)),
                  out_shape=jax.ShapeDtypeStruct((K, N), F32))(at, b)


def _band_specs(L, d, R, TQ, width, tile_of):
    n = TQ // R
    nt = width // LANES
    last = L // R - 1
    col = lambda p, r: r * nt + tile_of(p)
    return [pl.BlockSpec((R, LANES), lambda p, r, i: (jnp.maximum(i * n - 1, 0), col(p, r))),
            pl.BlockSpec((TQ, LANES), lambda p, r, i: (i, col(p, r))),
            pl.BlockSpec((R, LANES), lambda p, r, i: (jnp.minimum((i + 1) * n, last), col(p, r)))]


def _band_valid(i, TQ, R, L, rows_are_window):
    W = TQ + 2 * R
    if rows_are_window:
        wpos = i * TQ - R + lax.broadcasted_iota(jnp.int32, (W, TQ), 0)
        cpos = i * TQ + lax.broadcasted_iota(jnp.int32, (W, TQ), 1)
    else:
        wpos = i * TQ - R + lax.broadcasted_iota(jnp.int32, (TQ, W), 1)
        cpos = i * TQ + lax.broadcasted_iota(jnp.int32, (TQ, W), 0)
    return (jnp.abs(wpos - cpos) <= R) & (wpos >= 0) & (wpos < L)


def banded_fwd(proj, sink, *, d, R, TQ, pair0, npairs, use_sink, o_dtype):
    S = proj.shape[0]
    L = S // d
    pv = proj.reshape(L, d * PW)
    ow = npairs * LANES

    def body(sink_ref, q_ref, kp, kc, kn, vp, vc, vn, o_ref, lse_ref):
        p, i = pl.program_id(0), pl.program_id(2)
        lo = _lo((TQ, LANES))
        qa, qb = _split(q_ref[...], lo)
        kw = jnp.concatenate([kp[...], kc[...], kn[...]], axis=0)
        vw = jnp.concatenate([vp[...], vc[...], vn[...]], axis=0)
        valid = _band_valid(i, TQ, R, L, False)
        ps, dens, lses = [], [], []
        for h, qh in enumerate((qa, qb)):
            s = jnp.where(valid, _dot_nt(qh, kw), NEG)
            m = jnp.max(s, axis=-1, keepdims=True)
            if use_sink:
                sk = sink_ref[2 * (pair0 + p) + h]
                m = jnp.maximum(m, sk)
            e = jnp.exp(s - m)
            den = jnp.sum(e, axis=-1, keepdims=True)
            if use_sink:
                den = den + jnp.exp(sk - m)
            ps.append(e.astype(BF))
            dens.append(den)
            lses.append(m + jnp.log(den))
        lov = _lo(vw.shape)
        v_lo, v_hi = _split(vw, lov)
        o = _dot(jnp.concatenate(ps, axis=1), jnp.concatenate([v_lo, v_hi], axis=0))
        o_ref[...] = (o / jnp.where(lo, dens[0], dens[1])).astype(o_dtype)
        lse_ref[...] = jnp.where(lo, lses[0], lses[1])

    qspec = pl.BlockSpec((TQ, LANES), lambda p, r, i: (i, r * 14 + pair0 + p))
    kspecs = _band_specs(L, d, R, TQ, PW, lambda p: K_T0 + (pair0 + p) // 2)
    vspecs = _band_specs(L, d, R, TQ, PW, lambda p: V_T0 + (pair0 + p) // 2)
    ospec = pl.BlockSpec((TQ, LANES), lambda p, r, i: (i, r * npairs + p))
    o, lse = _pcall(body, name="banded_fwd", grid=(npairs, d, L // TQ),
                    in_specs=[pl.BlockSpec(memory_space=pltpu.SMEM), qspec] + kspecs + vspecs,
                    out_specs=[ospec, ospec],
                    out_shape=[jax.ShapeDtypeStruct((L, d * ow), o_dtype), jax.ShapeDtypeStruct((L, d * ow), F32)])(
        sink, pv, pv, pv, pv, pv, pv, pv)
    return o.reshape(S, ow), lse.reshape(S, ow)


def banded_bwd_dq(proj, do, lse, delta, sink, *, d, R, TQ, pair0, npairs, use_sink):
    S = proj.shape[0]
    L = S // d
    pv = proj.reshape(L, d * PW)
    ow = npairs * LANES

    def body(sink_ref, q_ref, kp, kc, kn, vp, vc, vn, do_ref, lse_ref, delta_ref, dq_ref, dsink_ref):
        p, r, i = pl.program_id(0), pl.program_id(1), pl.program_id(2)

        @pl.when((r == 0) & (i == 0))
        def _():
            dsink_ref[...] = jnp.zeros_like(dsink_ref)

        lo = _lo((TQ, LANES))
        qa, qb = _split(q_ref[...], lo)
        doa, dob = _split(do_ref[...], lo)
        kw = jnp.concatenate([kp[...], kc[...], kn[...]], axis=0)
        vw = jnp.concatenate([vp[...], vc[...], vn[...]], axis=0)
        valid = _band_valid(i, TQ, R, L, False)
        lse_t, delta_t = lse_ref[...], delta_ref[...]
        dss, dsk = [], []
        for h, (qh, doh) in enumerate(((qa, doa), (qb, dob))):
            lse_h, delta_h = _col(lse_t, h * HD), _col(delta_t, h * HD)
            pr = jnp.where(valid, jnp.exp(_dot_nt(qh, kw) - lse_h), 0.0)
            dss.append((pr * (_dot_nt(doh, vw) - delta_h)).astype(BF))
            if use_sink:
                psink = jnp.exp(sink_ref[2 * (pair0 + p) + h] - lse_h)
                dsk.append(-jnp.sum(psink * delta_h, axis=0, keepdims=True))
        k_lo, k_hi = _split(kw, _lo(kw.shape))
        dq_ref[...] = _dot(jnp.concatenate(dss, axis=1), jnp.concatenate([k_lo, k_hi], axis=0))
        if use_sink:
            dsink_ref[...] += jnp.where(_lo((8, LANES)), dsk[0], dsk[1])

    qspec = pl.BlockSpec((TQ, LANES), lambda p, r, i: (i, r * 14 + pair0 + p))
    kspecs = _band_specs(L, d, R, TQ, PW, lambda p: K_T0 + (pair0 + p) // 2)
    vspecs = _band_specs(L, d, R, TQ, PW, lambda p: V_T0 + (pair0 + p) // 2)
    ospec = pl.BlockSpec((TQ, LANES), lambda p, r, i: (i, r * npairs + p))
    view = lambda a: a.reshape(L, d * a.shape[1])
    ispec = lambda a: pl.BlockSpec((TQ, LANES), lambda p, r, i: (i, r * (a.shape[1] // LANES) + p))
    dq, dsink = _pcall(body, name="banded_bwd_dq", grid=(npairs, d, L // TQ),
                       in_specs=[pl.BlockSpec(memory_space=pltpu.SMEM), qspec] + kspecs + vspecs
                       + [ispec(do), ispec(lse), ispec(delta)],
                       out_specs=[ospec, pl.BlockSpec((8, LANES), lambda p, r, i: (p, 0))],
                       out_shape=[jax.ShapeDtypeStruct((L, d * ow), F32),
                                  jax.ShapeDtypeStruct((npairs * 8, LANES), F32)])(
        sink, pv, pv, pv, pv, pv, pv, pv, view(do), view(lse), view(delta))
    return dq.reshape(S, ow), dsink


def banded_bwd_dkv(proj, do, lse, delta, *, d, R, TK, pair0, npairs):
    S = proj.shape[0]
    L = S // d
    pv = proj.reshape(L, d * PW)
    ow = npairs * LANES

    def body(k_ref, v_ref, qp, qc, qn, dop, doc, don, lp, lc, ln, dp_, dc_, dn_, dk_ref, dv_ref):
        j = pl.program_id(2)
        W = TK + 2 * R
        low = _lo((W, LANES))
        qa, qb = _split(jnp.concatenate([qp[...], qc[...], qn[...]], axis=0), low)
        doa, dob = _split(jnp.concatenate([dop[...], doc[...], don[...]], axis=0), low)
        lse_t = jnp.concatenate([lp[...], lc[...], ln[...]], axis=0)
        delta_t = jnp.concatenate([dp_[...], dc_[...], dn_[...]], axis=0)
        k, v = k_ref[...], v_ref[...]
        valid = _band_valid(j, TK, R, L, True)
        prs, dss = [], []
        for h, (qh, doh) in enumerate(((qa, doa), (qb, dob))):
            pr = jnp.where(valid, jnp.exp(_dot_nt(qh, k) - _col(lse_t, h * HD)), 0.0)
            dss.append(pr * (_dot_nt(doh, v) - _col(delta_t, h * HD)))
            prs.append(pr)
        dv_ref[...] = _dot(jnp.concatenate(prs, axis=0).T.astype(BF), jnp.concatenate([doa, dob], axis=0))
        dk_ref[...] = _dot(jnp.concatenate(dss, axis=0).T.astype(BF), jnp.concatenate([qa, qb], axis=0))

    kspec = pl.BlockSpec((TK, LANES), lambda p, r, j: (j, r * 14 + K_T0 + (pair0 + p) // 2))
    vspec = pl.BlockSpec((TK, LANES), lambda p, r, j: (j, r * 14 + V_T0 + (pair0 + p) // 2))
    qspecs = _band_specs(L, d, R, TK, PW, lambda p: pair0 + p)
    ispecs = lambda a: _band_specs(L, d, R, TK, a.shape[1], lambda p: p)
    view = lambda a: a.reshape(L, d * a.shape[1])
    ospec = pl.BlockSpec((TK, LANES), lambda p, r, j: (j, r * npairs + p))
    sd = jax.ShapeDtypeStruct((L, d * ow), F32)
    dk, dv = _pcall(body, name="banded_bwd_dkv", grid=(npairs, d, L // TK),
                    in_specs=[kspec, vspec] + qspecs + ispecs(do) + ispecs(lse) + ispecs(delta),
                    out_specs=[ospec, ospec], out_shape=[sd, sd])(
        pv, pv, pv, pv, pv, *([view(do)] * 3), *([view(lse)] * 3), *([view(delta)] * 3))
    return dk.reshape(S, ow), dv.reshape(S, ow)


def flash_fwd(proj, bound, *, tq=1024, tk=512):
    S = proj.shape[0]

    def body_general(q_ref, k_ref, v_ref, o_ref, lse_ref):
        lo = _lo((tq, LANES))
        qa, qb = _split(q_ref[...], lo)
        lov = _lo((tk, LANES))

        def step(j, carry):
            ma, la, mb, lb, acc = carry
            rows = pl.ds(pl.multiple_of(j * tk, tk), tk)
            k, v = k_ref[rows, :], v_ref[rows, :]
            outs = []
            for qh, m0, l0 in ((qa, ma, la), (qb, mb, lb)):
                s = _dot_nt(qh, k)
                m1 = jnp.maximum(m0, jnp.max(s, axis=-1, keepdims=True))
                al = jnp.exp2(m0 - m1)
                e = jnp.exp2(s - m1)
                outs.append((m1, al * l0 + jnp.sum(e, axis=-1, keepdims=True), al, e.astype(BF)))
            v_lo, v_hi = _split(v, lov)
            pvv = _dot(jnp.concatenate([outs[0][3], outs[1][3]], axis=1), jnp.concatenate([v_lo, v_hi], axis=0))
            acc = acc * jnp.where(lo, outs[0][2], outs[1][2]) + pvv
            return outs[0][0], outs[0][1], outs[1][0], outs[1][1], acc

        m_init = jnp.full((tq, 1), NEG, F32)
        l_init = jnp.zeros((tq, 1), F32)
        ma, la, mb, lb, acc = lax.fori_loop(0, S // tk, step,
                                            (m_init, l_init, m_init, l_init, jnp.zeros((tq, LANES), F32)))
        o_ref[...] = (acc / jnp.where(lo, la, lb)).astype(BF)
        lse_ref[...] = jnp.where(lo, ma * LN2 + jnp.log(la), mb * LN2 + jnp.log(lb))

    def body_plain(q_ref, k_ref, v_ref, o_ref, lse_ref):
        lo = _lo((tq, LANES))
        qa, qb = _split(q_ref[...], lo)
        lov = _lo((tk, LANES))
        one = jnp.ones((tk, LANES), BF)

        def step(j, carry):
            acc_a, acc_b = carry
            rows = pl.ds(pl.multiple_of(j * tk, tk), tk)
            k, v = k_ref[rows, :], v_ref[rows, :]
            ea = jnp.exp2(_dot_nt(qa, k)).astype(BF)
            eb = jnp.exp2(_dot_nt(qb, k)).astype(BF)
            acc_a = acc_a + _dot(ea, jnp.where(lov, v, one))
            acc_b = acc_b + _dot(eb, jnp.where(lov, one, v))
            return acc_a, acc_b

        z = jnp.zeros((tq, LANES), F32)
        acc_a, acc_b = lax.fori_loop(0, S // tk, step, (z, z))
        den = jnp.where(lo, pltpu.roll(acc_a, HD, 1), pltpu.roll(acc_b, HD, 1))
        o_ref[...] = (jnp.where(lo, acc_a, acc_b) / den).astype(BF)
        lse_ref[...] = jnp.log(den)

    def body(bound_ref, q_ref, k_ref, v_ref, o_ref, lse_ref):
        small = bound_ref[0] <= MAX_PLAIN_SCORE

        @pl.when(small)
        def _():
            body_plain(q_ref, k_ref, v_ref, o_ref, lse_ref)

        @pl.when(jnp.logical_not(small))
        def _():
            body_general(q_ref, k_ref, v_ref, o_ref, lse_ref)

    ospec = pl.BlockSpec((tq, LANES), lambda p, i: (i, p))
    return _pcall(body, name="flash_fwd", grid=(N_PAIRS, S // tq),
                  in_specs=[pl.BlockSpec(memory_space=pltpu.SMEM), ospec,
                            pl.BlockSpec((S, LANES), lambda p, i: (0, K_T0 + p // 2)),
                            pl.BlockSpec((S, LANES), lambda p, i: (0, V_T0 + p // 2))],
                  out_specs=[ospec, ospec],
                  out_shape=[jax.ShapeDtypeStruct((S, 768), BF), jax.ShapeDtypeStruct((S, 768), F32)])(
        bound.reshape(1), proj, proj, proj)


def flash_bwd_dq(proj, do, lse, delta, *, tq=1024, tk=512):
    S = proj.shape[0]

    def body(q_ref, k_ref, v_ref, do_ref, lse_ref, delta_ref, dq_ref):
        lo = _lo((tq, LANES))
        qa, qb = _split(q_ref[...], lo)
        doa, dob = _split(do_ref[...], lo)
        lse_t, delta_t = lse_ref[...], delta_ref[...]
        stats = [(_col(lse_t, 0) * LOG2E, _col(delta_t, 0)), (_col(lse_t, HD) * LOG2E, _col(delta_t, HD))]
        lov = _lo((tk, LANES))

        def step(j, acc):
            rows = pl.ds(pl.multiple_of(j * tk, tk), tk)
            k, v = k_ref[rows, :], v_ref[rows, :]
            dss = []
            for (qh, doh), (lse_h, delta_h) in zip(((qa, doa), (qb, dob)), stats):
                pr = jnp.exp2(_dot_nt(qh, k) - lse_h)
                dss.append((pr * (_dot_nt(doh, v) - delta_h)).astype(BF))
            k_lo, k_hi = _split(k, lov)
            return acc + _dot(jnp.concatenate(dss, axis=1), jnp.concatenate([k_lo, k_hi], axis=0))

        dq_ref[...] = lax.fori_loop(0, S // tk, step, jnp.zeros((tq, LANES), F32))

    ospec = pl.BlockSpec((tq, LANES), lambda p, i: (i, p))
    return _pcall(body, name="flash_bwd_dq", grid=(N_PAIRS, S // tq),
                  in_specs=[ospec, pl.BlockSpec((S, LANES), lambda p, i: (0, K_T0 + p // 2)),
                            pl.BlockSpec((S, LANES), lambda p, i: (0, V_T0 + p // 2)), ospec, ospec, ospec],
                  out_specs=ospec, out_shape=jax.ShapeDtypeStruct((S, 768), F32))(proj, proj, proj, do, lse, delta)


def flash_bwd_dkv(proj, do, lse_t, delta_t, *, tq=512, tk=1024):
    S = proj.shape[0]

    def body(k_ref, v_ref, q_ref, do_ref, lse_ref, delta_ref, dk_ref, dv_ref):
        k, v = k_ref[...], v_ref[...]
        lo = _lo((tq, LANES))

        def step(i, carry):
            dk, dv = carry
            rows = pl.ds(pl.multiple_of(i * tq, tq), tq)
            qa, qb = _split(q_ref[rows, :], lo)
            doa, dob = _split(do_ref[rows, :], lo)
            lse_i, delta_i = lse_ref[i] * LOG2E, delta_ref[i]
            prs, dss = [], []
            for h, (qh, doh) in enumerate(((qa, doa), (qb, dob))):
                pr = jnp.exp2(_dot_nt(k, qh) - lse_i[h:h + 1, :])
                dss.append((pr * (_dot_nt(v, doh) - delta_i[h:h + 1, :])).astype(BF))
                prs.append(pr.astype(BF))
            dv = dv + _dot(jnp.concatenate(prs, axis=1), jnp.concatenate([doa, dob], axis=0))
            dk = dk + _dot(jnp.concatenate(dss, axis=1), jnp.concatenate([qa, qb], axis=0))
            return dk, dv

        z = jnp.zeros((tk, LANES), F32)
        dk, dv = lax.fori_loop(0, S // tq, step, (z, z))
        dk_ref[...] = dk
        dv_ref[...] = dv

    ospec = pl.BlockSpec((tk, LANES), lambda p, j: (j, p))
    stat = pl.BlockSpec((None, S // tq, 8, tq), lambda p, j: (p, 0, 0, 0))
    sd = jax.ShapeDtypeStruct((S, 768), F32)
    return _pcall(body, name="flash_bwd_dkv", grid=(N_PAIRS, S // tk),
                  in_specs=[pl.BlockSpec((tk, LANES), lambda p, j: (j, K_T0 + p // 2)),
                            pl.BlockSpec((tk, LANES), lambda p, j: (j, V_T0 + p // 2)),
                            pl.BlockSpec((S, LANES), lambda p, j: (0, p)), pl.BlockSpec((S, LANES), lambda p, j: (0, p)),
                            stat, stat],
                  out_specs=[ospec, ospec], out_shape=[sd, sd])(proj, proj, proj, do, lse_t, delta_t)


def mem_fwd(proj, mkv, *, tq=512):
    S = proj.shape[0]

    def body(q_ref, km_ref, vm_ref, o_ref, lse_ref):
        lo = _lo((tq, LANES))
        lov = _lo((N_MEM, LANES))
        for t in range(2):
            qa, qb = _split(q_ref[:, _tile(t)], lo)
            km, vm = km_ref[:, _tile(t)], vm_ref[:, _tile(t)]
            ps, dens, lses = [], [], []
            for qh in (qa, qb):
                s = _dot_nt(qh, km)
                m = jnp.max(s, axis=-1, keepdims=True)
                e = jnp.exp(s - m)
                den = jnp.sum(e, axis=-1, keepdims=True)
                ps.append(e.astype(BF))
                dens.append(den)
                lses.append(m + jnp.log(den))
            v_lo, v_hi = _split(vm, lov)
            o = _dot(jnp.concatenate(ps, axis=1), jnp.concatenate([v_lo, v_hi], axis=0))
            o_ref[:, _tile(t)] = (o / jnp.where(lo, dens[0], dens[1])).astype(BF)
            lse_ref[:, _tile(t)] = jnp.where(lo, lses[0], lses[1])

    ospec = pl.BlockSpec((tq, 256), lambda i: (i, 0))
    return _pcall(body, name="mem_fwd", grid=(S // tq,),
                  in_specs=[pl.BlockSpec((tq, 256), lambda i: (i, M_T0 // 2)),
                            pl.BlockSpec((N_MEM, 256), lambda i: (0, 0)), pl.BlockSpec((N_MEM, 256), lambda i: (0, 1))],
                  out_specs=[ospec, ospec],
                  out_shape=[jax.ShapeDtypeStruct((S, 256), BF), jax.ShapeDtypeStruct((S, 256), F32)])(proj, mkv, mkv)


def mem_bwd(proj, mkv, dcat, lse, delta, *, tq=512):
    S = proj.shape[0]

    def body(q_ref, km_ref, vm_ref, do_ref, lse_ref, delta_ref, dq_ref, dkm_ref, dvm_ref):
        @pl.when(pl.program_id(0) == 0)
        def _():
            dkm_ref[...] = jnp.zeros_like(dkm_ref)
            dvm_ref[...] = jnp.zeros_like(dvm_ref)

        lo = _lo((tq, LANES))
        lov = _lo((N_MEM, LANES))
        for t in range(2):
            qa, qb = _split(q_ref[:, _tile(t)], lo)
            doa, dob = _split(do_ref[:, _tile(t)], lo)
            km, vm = km_ref[:, _tile(t)], vm_ref[:, _tile(t)]
            lse_t, delta_t = lse_ref[:, _tile(t)], delta_ref[:, _tile(t)]
            prs, dss = [], []
            for h, (qh, doh) in enumerate(((qa, doa), (qb, dob))):
                pr = jnp.exp(_dot_nt(qh, km) - _col(lse_t, h * HD))
                dss.append(pr * (_dot_nt(doh, vm) - _col(delta_t, h * HD)))
                prs.append(pr)
            k_lo, k_hi = _split(km, lov)
            dq_ref[:, _tile(t)] = _dot(jnp.concatenate(dss, axis=1).astype(BF), jnp.concatenate([k_lo, k_hi], axis=0))
            dvm_ref[:, _tile(t)] += _dot(jnp.concatenate(prs, axis=0).T.astype(BF), jnp.concatenate([doa, dob], axis=0))
            dkm_ref[:, _tile(t)] += _dot(jnp.concatenate(dss, axis=0).T.astype(BF), jnp.concatenate([qa, qb], axis=0))

    ospec = pl.BlockSpec((tq, 256), lambda i: (i, 0))
    msp = pl.BlockSpec((N_MEM, 256), lambda i: (0, 0))
    md = jax.ShapeDtypeStruct((N_MEM, 256), F32)
    dq, dkm, dvm = _pcall(body, name="mem_bwd", grid=(S // tq,),
                          in_specs=[pl.BlockSpec((tq, 256), lambda i: (i, M_T0 // 2)), msp,
                                    pl.BlockSpec((N_MEM, 256), lambda i: (0, 1)),
                                    pl.BlockSpec((tq, 256), lambda i: (i, 3)), ospec,
                                    pl.BlockSpec((tq, 256), lambda i: (i, 3))],
                          out_specs=[ospec, msp, msp], out_shape=[jax.ShapeDtypeStruct((S, 256), F32), md, md])(
        proj, mkv, mkv, dcat, lse, delta)
    return dq, jnp.concatenate([dkm, dvm], axis=1)


def combine_fwd(os_, lses, *, tm=512):
    S = os_[0].shape[0]

    def body(o0, o1, o2, l0, l1, l2, tok_ref):
        ls = [l0[...], l1[...], l2[...]]
        m = jnp.maximum(jnp.maximum(ls[0], ls[1]), ls[2])
        es = [jnp.exp(l - m) for l in ls]
        den = es[0] + es[1] + es[2]
        for g, o in enumerate((o0, o1, o2)):
            tok_ref[:, 256 * g:256 * (g + 1)] = (o[...] * (es[g] / den)).astype(BF)

    sp = pl.BlockSpec((tm, 256), lambda i: (i, 0))
    return _pcall(body, name="combine_fwd", grid=(S // tm,), in_specs=[sp] * 6,
                  out_specs=pl.BlockSpec((tm, 768), lambda i: (i, 0)),
                  out_shape=jax.ShapeDtypeStruct((S, 768), BF))(*os_, *lses)


def combine_bwd(dcat, os_, lses, *, tm=512):
    S = dcat.shape[0]

    def body(dt_ref, o0, o1, o2, l0, l1, l2, do0, do1, do2, de0, de1, de2):
        ls = [l0[...], l1[...], l2[...]]
        m = jnp.maximum(jnp.maximum(ls[0], ls[1]), ls[2])
        es = [jnp.exp(l - m) for l in ls]
        den = es[0] + es[1] + es[2]
        alphas = [e / den for e in es]
        lo = _lo((tm, LANES))
        dts = [dt_ref[:, 256 * g:256 * (g + 1)].astype(F32) for g in range(3)]
        dal = []
        for g, o in enumerate((o0, o1, o2)):
            pr = dts[g] * o[...]
            dal.append(jnp.concatenate([_half_sum(pr[:, _tile(0)], lo), _half_sum(pr[:, _tile(1)], lo)], axis=1))
        mix = alphas[0] * dal[0] + alphas[1] * dal[1] + alphas[2] * dal[2]
        for g, (do_ref, de_ref) in enumerate(((do0, de0), (do1, de1), (do2, de2))):
            do_ref[...] = (dts[g] * alphas[g]).astype(BF)
            de_ref[...] = alphas[g] * mix

    sp = pl.BlockSpec((tm, 256), lambda i: (i, 0))
    outs = _pcall(body, name="combine_bwd", grid=(S // tm,),
                  in_specs=[pl.BlockSpec((tm, 768), lambda i: (i, 0))] + [sp] * 6, out_specs=[sp] * 6,
                  out_shape=[jax.ShapeDtypeStruct((S, 256), BF)] * 3 + [jax.ShapeDtypeStruct((S, 256), F32)] * 3)(
        dcat, *os_, *lses)
    return outs[:3], outs[3:]


def _coords():
    return lax.axis_index("x"), lax.axis_index("y"), lax.axis_index("c")


def _other_chips(x, y):
    return [(1 - x, y), (x, 1 - y), (1 - x, 1 - y)]


HBM_SPEC = pl.BlockSpec(memory_space=pltpu.HBM)


def gather_shards(shards):
    n = len(shards)

    def body(*refs):
        ins, outs = refs[:n], refs[n:2 * n]
        send, recv, lsem = refs[2 * n:]
        x, y, c = _coords()
        me = 2 * x + y
        chips = _other_chips(x, y)
        local, sends = [], []
        for a in range(n):
            cp = pltpu.make_async_copy(ins[a], outs[a].at[me], lsem.at[a])
            cp.start()
            local.append(cp)
            for j, (px, py) in enumerate(chips):
                cp = pltpu.make_async_remote_copy(src_ref=ins[a], dst_ref=outs[a].at[me], send_sem=send.at[3 * a + j],
                                                  recv_sem=recv.at[3 * a + j], device_id=(px, py, c), device_id_type=MESH)
                cp.start()
                sends.append(cp)
        for a in range(n):
            for j, (px, py) in enumerate(chips):
                pltpu.make_async_remote_copy(src_ref=ins[a], dst_ref=outs[a].at[2 * px + py], send_sem=send.at[3 * a + j],
                                             recv_sem=recv.at[3 * a + j], device_id=(px, py, c),
                                             device_id_type=MESH).wait_recv()
        for cp in sends:
            cp.wait_send()
        for cp in local:
            cp.wait()

    return pl.pallas_call(
        body, name="gather_shards", in_specs=[HBM_SPEC] * n, out_specs=[HBM_SPEC] * n,
        out_shape=[jax.ShapeDtypeStruct((4,) + s.shape, s.dtype) for s in shards],
        scratch_shapes=[pltpu.SemaphoreType.DMA((3 * n,)), pltpu.SemaphoreType.DMA((3 * n,)),
                        pltpu.SemaphoreType.DMA((n,))])(*shards)


def scatter_grads(parts):
    n = len(parts)

    def body(*refs):
        ins, outs = refs[:n], refs[n:2 * n]
        send, recv = refs[2 * n:]
        x, y, c = _coords()
        chips = _other_chips(x, y)
        sends = []
        for a in range(n):
            for j, (px, py) in enumerate(chips):
                cp = pltpu.make_async_remote_copy(src_ref=ins[a].at[2 * px + py], dst_ref=outs[a].at[j],
                                                  send_sem=send.at[3 * a + j], recv_sem=recv.at[3 * a + j],
                                                  device_id=(px, py, c), device_id_type=MESH)
                cp.start()
                sends.append(cp)
        for cp in sends:
            cp.wait_recv()
        for cp in sends:
            cp.wait_send()

    return pl.pallas_call(
        body, name="scatter_grads", in_specs=[HBM_SPEC] * n, out_specs=[HBM_SPEC] * n,
        out_shape=[jax.ShapeDtypeStruct((3,) + p.shape[1:], p.dtype) for p in parts],
        scratch_shapes=[pltpu.SemaphoreType.DMA((3 * n,)), pltpu.SemaphoreType.DMA((3 * n,))])(*parts)


def sibling_swap(arrs):
    n = len(arrs)

    def body(*refs):
        ins, outs = refs[:n], refs[n:2 * n]
        send, recv = refs[2 * n:]
        x, y, c = _coords()
        cps = []
        for a in range(n):
            cp = pltpu.make_async_remote_copy(src_ref=ins[a], dst_ref=outs[a], send_sem=send.at[a], recv_sem=recv.at[a],
                                              device_id=(x, y, 1 - c), device_id_type=MESH)
            cp.start()
            cps.append(cp)
        for cp in cps:
            cp.wait_recv()
        for cp in cps:
            cp.wait_send()

    return pl.pallas_call(
        body, name="sibling_swap", in_specs=[HBM_SPEC] * n, out_specs=[HBM_SPEC] * n,
        out_shape=[jax.ShapeDtypeStruct(a.shape, a.dtype) for a in arrs],
        scratch_shapes=[pltpu.SemaphoreType.DMA((n,)), pltpu.SemaphoreType.DMA((n,))])(*arrs)


def allsum_small(v):
    rows = v.shape[0]

    def body(v_ref, tot_ref, gath_ref, send, recv):
        x, y, c = _coords()
        me = 4 * x + 2 * y + c
        gath_ref[me] = v_ref[...]
        cps = []
        for k in range(1, 8):
            fx, fy, fc = (k >> 2) & 1, (k >> 1) & 1, k & 1
            peer = (1 - x if fx else x, 1 - y if fy else y, 1 - c if fc else c)
            cp = pltpu.make_async_remote_copy(src_ref=v_ref, dst_ref=gath_ref.at[me], send_sem=send.at[k - 1],
                                              recv_sem=recv.at[k - 1], device_id=peer, device_id_type=MESH)
            cp.start()
            cps.append(cp)
        for cp in cps:
            cp.wait_recv()
        for cp in cps:
            cp.wait_send()
        tot = gath_ref[0]
        for k in range(1, 8):
            tot = tot + gath_ref[k]
        tot_ref[...] = tot

    vm = pl.BlockSpec(memory_space=pltpu.VMEM)
    tot, _ = pl.pallas_call(
        body, name="allsum_small", in_specs=[vm], out_specs=[vm, vm],
        out_shape=[jax.ShapeDtypeStruct((rows, LANES), F32), jax.ShapeDtypeStruct((8, rows, LANES), F32)],
        scratch_shapes=[pltpu.SemaphoreType.DMA((7,)), pltpu.SemaphoreType.DMA((7,))])(v)
    return tot


def sum_parts(own, recv, *, tr=256):
    R, C = own.shape
    tr = min(tr, R)

    def body(o_ref, r_ref, out_ref):
        out_ref[...] = ((o_ref[...] + r_ref[0].astype(F32)) + r_ref[1].astype(F32)) + r_ref[2].astype(F32)

    sp = pl.BlockSpec((tr, C), lambda i: (i, 0))
    return _pcall(body, name="sum_parts", grid=(R // tr,),
                  in_specs=[sp, pl.BlockSpec((3, tr, C), lambda i: (0, i, 0))], out_specs=sp,
                  out_shape=jax.ShapeDtypeStruct((R, C), F32))(own, recv)


def adamw(w, ga, gb, m, v, *, tr=256):
    R, C = w.shape
    tr = min(tr, R)
    two = gb is not None

    def body(*refs):
        if two:
            w_ref, ga_ref, gb_ref, m_ref, v_ref, g_out, d_out, m_out, v_out = refs
            g = ga_ref[...] + gb_ref[...]
        else:
            w_ref, ga_ref, m_ref, v_ref, g_out, d_out, m_out, v_out = refs
            g = ga_ref[...]
        mn = B1 * m_ref[...] + (1.0 - B1) * g
        vn = B2 * v_ref[...] + (1.0 - B2) * (g * g)
        m_hat = mn / (1.0 - B1 ** STEP)
        v_hat = vn / (1.0 - B2 ** STEP)
        g_out[...] = g
        d_out[...] = -LR * (m_hat / (jnp.sqrt(v_hat) + AEPS) + WD * w_ref[...])
        m_out[...] = mn
        v_out[...] = vn

    sp = pl.BlockSpec((tr, C), lambda i: (i, 0))
    args = [w, ga, gb, m, v] if two else [w, ga, m, v]
    sd = jax.ShapeDtypeStruct((R, C), F32)
    return _pcall(body, name="adamw", grid=(R // tr,), in_specs=[sp] * len(args), out_specs=[sp] * 4,
                  out_shape=[sd] * 4)(*args)


def _rope_tables(S):
    def inv_freq(n_dims, theta):
        return theta ** (-(jnp.arange(0, n_dims, 2, dtype=jnp.float32) / n_dims))

    pos = lax.broadcasted_iota(jnp.int32, (S, LANES), 0)
    d = lax.broadcasted_iota(jnp.int32, (S, LANES), 1) % HD
    d1 = lax.iota(jnp.int32, LANES) % HD
    ang = pos.astype(F32) * inv_freq(HD // 4, ROPE_THETA)[d1 % 8][None, :]
    sin = jnp.sin(ang)
    partial = (jnp.where(d < 16, jnp.cos(ang), 1.0), jnp.where((d >= 8) & (d < 16), sin, 0.0),
               jnp.where(d < 8, -sin, 0.0))
    grid_pos = jnp.where(d < 32, pos // GRID_W, pos % GRID_W)
    ang = grid_pos.astype(F32) * inv_freq(HD // 2, AXIAL_THETA)[d1 % 16][None, :]
    sin = jnp.sin(ang)
    axial = (jnp.cos(ang), jnp.where(d % 32 >= 16, sin, 0.0), jnp.where(d % 32 < 16, -sin, 0.0))
    return partial, axial


def _pad_w_in(w):
    cols = [w[:, :768]]
    for base in (768, 960):
        for g in range(3):
            kg = w[:, base + g * HD:base + (g + 1) * HD]
            cols += [kg, kg]
    cols.append(w[:, 1152:])
    return jnp.concatenate(cols, axis=1)


def _unpad_dw_in(dw):
    cols = [dw[:, :768]]
    for t0 in (K_T0, V_T0):
        for g in range(3):
            b = (t0 + g) * LANES
            cols.append(dw[:, b:b + HD] + dw[:, b + HD:b + LANES])
    cols.append(dw[:, M_T0 * LANES:])
    return jnp.concatenate(cols, axis=1)


def _stats_t(a, tq):
    S = a.shape[0]
    t = a.reshape(S, -1, 2, HD)[:, :N_PAIRS, :, 0]
    t = jnp.transpose(t, (1, 2, 0))
    t = jnp.pad(t, ((0, 0), (0, 6), (0, 0)))
    return jnp.transpose(t.reshape(N_PAIRS, 8, S // tq, tq), (0, 2, 1, 3))


def _fold(t):
    return t[..., :HD] + t[..., HD:]


def kernel(x, mem, mem_norm_g, w_in, w_mem_kv, w_o, g_mix_pre, g_mix_post, attn_sink, qk_norm_g, w_gate_up, w_down, g_ffn_pre, g_ffn_post, loss_target, m_mem_norm_g, m_w_in, m_w_mem_kv, m_w_o, m_g_mix_pre, m_g_mix_post, m_attn_sink, m_qk_norm_g, m_w_gate_up, m_w_down, m_g_ffn_pre, m_g_ffn_post, v_mem_norm_g, v_w_in, v_w_mem_kv, v_w_o, v_g_mix_pre, v_g_mix_post, v_attn_sink, v_qk_norm_g, v_w_gate_up, v_w_down, v_g_ffn_pre, v_g_ffn_post):
    S = x.shape[1]
    depth = w_in.shape[0]
    xs, memx, tgt = x[0], mem[0], loss_target[0]
    tab_p, tab_a = _rope_tables(S)
    row = lambda a: a.reshape(1, -1)

    gi, gm, go, gg, gd = gather_shards([w_in.astype(BF), w_mem_kv.astype(BF), w_o.astype(BF), w_gate_up.astype(BF),
                                        w_down.astype(BF)])
    W_in = jnp.concatenate([gi[s] for s in range(4)], axis=2)
    W_mkv = jnp.concatenate([gm[s] for s in range(4)], axis=1)
    W_o = jnp.concatenate([go[s] for s in range(4)], axis=1)
    W_g = jnp.concatenate([gg[0], gg[1]], axis=2)
    W_u = jnp.concatenate([gg[2], gg[3]], axis=2)
    W_d = jnp.concatenate([gd[s] for s in range(4)], axis=1)
    mem_g = row(mem_norm_g)
    zero_sink = jnp.zeros((12,), F32)
    qkg = jnp.pad(jnp.concatenate([qk_norm_g[0], qk_norm_g[0]], axis=1), ((0, 6), (0, 0)))
    no_qkg = jnp.zeros((8, LANES), F32)

    saved = []
    cur = xs
    for i in range(depth):
        kind = i % 3
        wp = _pad_w_in(W_in[i])
        sv = dict(x=cur, wp=wp)
        if kind == 1:
            h1, proj, raw, nrm = inproj_fwd(cur, row(g_mix_pre[i]), wp, tab_a, qkg, axial=True)
            sv["raw"] = raw
        else:
            h1, proj = inproj_fwd(cur, row(g_mix_pre[i]), wp, tab_p, no_qkg, axial=False)
        if kind == 0:
            tok, lse = banded_fwd(proj, attn_sink[i // 3], d=1, R=A_RADIUS, TQ=2 * A_RADIUS, pair0=0, npairs=6,
                                  use_sink=True, o_dtype=BF)
        elif kind == 1:
            bound = jnp.sqrt(jnp.max(nrm[0]) * jnp.max(nrm[1])) * LN2
            tok, lse = flash_fwd(proj, bound)
        else:
            os_, lses = [], []
            for g, (window, dil) in enumerate(C_GROUPS):
                rad = window // (2 * dil)
                o_g, l_g = banded_fwd(proj, zero_sink, d=dil, R=rad, TQ=2 * rad, pair0=2 * g, npairs=2, use_sink=False,
                                      o_dtype=F32)
                os_.append(o_g)
                lses.append(l_g)
            tok = combine_fwd(os_, lses)
            sv["os"], lse = os_, lses
        mem_n, mkv = norm_mm(memx, mem_g, W_mkv[i], tm=N_MEM)
        mo, mlse = mem_fwd(proj, mkv)
        cat = jnp.concatenate([tok, mo], axis=1)
        o, x2 = mm_norm_res(cat, W_o[i], row(g_mix_post[i]), cur)
        h2, gate, up, act = ffn_up_fwd(x2, row(g_ffn_pre[i]), W_g[i], W_u[i])
        f, x3 = mm_norm_res(act, W_d[i], row(g_ffn_post[i]), x2)
        sv.update(h1=h1, proj=proj, lse=lse, mem_n=mem_n, mkv=mkv, mlse=mlse, cat=cat, o=o, x2=x2, h2=h2, gate=gate,
                  up=up, act=act, f=f)
        saved.append(sv)
        cur = x3

    dcur, loss_vec = loss_bwd(cur, tgt)

    dW_in, dW_mkv, dW_o, dW_gu, dW_d = [None] * depth, [None] * depth, [None] * depth, [None] * depth, [None] * depth
    dg_pre, dg_post, dg_fpre, dg_fpost = [None] * depth, [None] * depth, [None] * depth, [None] * depth
    dg_mem = jnp.zeros((1, D), F32)
    dsinks, dqk = {}, None
    for i in reversed(range(depth)):
        sv = saved[i]
        kind = i % 3
        proj = sv["proj"]
        df, dgate, dup, dg_fpost[i] = normbwd_mm_swiglu(dcur, sv["f"], row(g_ffn_post[i]), W_d[i], sv["gate"], sv["up"])
        dx2, dg_fpre[i] = mm_nt_normbwd_res([(dgate, W_g[i]), (dup, W_u[i])], sv["x2"], row(g_ffn_pre[i]), dcur, nk=2)
        dW_d[i] = mm_acc(sv["act"].T, df, tk=1408, tn=D, ts=1024)
        h2t = sv["h2"].T
        dW_gu[i] = jnp.concatenate([mm_acc(h2t, dgate, tk=D, tn=1408, ts=1024), mm_acc(h2t, dup, tk=D, tn=1408, ts=1024)],
                                   axis=1)
        do, dcat, delta, dg_post[i] = normbwd_mm_cat(dx2, sv["o"], row(g_mix_post[i]), W_o[i], sv["cat"])
        dW_o[i] = mm_acc(sv["cat"].T, do, tk=D, tn=D, ts=1024)
        dqm, dmkv = mem_bwd(proj, sv["mkv"], dcat, sv["mlse"], delta)
        dmkv = dmkv.astype(BF)
        (dgm,) = mm_nt_normbwd_res([(dmkv, W_mkv[i])], memx, mem_g, None, tm=N_MEM)
        dg_mem = dg_mem + dgm
        dW_mkv[i] = mm_acc(sv["mem_n"].T, dmkv, tk=D, tn=512, ts=N_MEM)
        if kind == 0:
            sink = attn_sink[i // 3]
            args = dict(d=1, R=A_RADIUS, pair0=0, npairs=6)
            dq, dsk = banded_bwd_dq(proj, dcat, sv["lse"], delta, sink, TQ=2 * A_RADIUS, use_sink=True, **args)
            dkp, dvp = banded_bwd_dkv(proj, dcat, sv["lse"], delta, TK=2 * A_RADIUS, **args)
            dsinks[i // 3] = dsk.reshape(6, 8, 2, HD)[:, 0, :, 0].reshape(12)
        elif kind == 1:
            dq = flash_bwd_dq(proj, dcat, sv["lse"], delta)
            dkp, dvp = flash_bwd_dkv(proj, dcat, _stats_t(sv["lse"], 512), _stats_t(delta, 512))
        else:
            dos, des = combine_bwd(dcat, sv["os"], sv["lse"])
            dqs, dks, dvs = [], [], []
            for g, (window, dil) in enumerate(C_GROUPS):
                rad = window // (2 * dil)
                args = dict(d=dil, R=rad, pair0=2 * g, npairs=2)
                dq_g, _ = banded_bwd_dq(proj, dos[g], sv["lse"][g], des[g], zero_sink, TQ=2 * rad, use_sink=False, **args)
                dk_g, dv_g = banded_bwd_dkv(proj, dos[g], sv["lse"][g], des[g], TK=2 * rad, **args)
                dqs.append(dq_g)
                dks.append(dk_g)
                dvs.append(dv_g)
            dq, dkp, dvp = (jnp.concatenate(t, axis=1) for t in (dqs, dks, dvs))
        if kind == 1:
            dcur, dproj, dg_pre[i], dqk_t = inproj_bwd(dq, dkp, dvp, dqm, tab_a, sv["raw"], qkg, sv["wp"], sv["x"],
                                                       row(g_mix_pre[i]), dx2, axial=True)
            dqk = _fold(dqk_t[:2]).reshape(1, 2, HD)
        else:
            dcur, dproj, dg_pre[i], _ = inproj_bwd(dq, dkp, dvp, dqm, tab_p, proj, no_qkg, sv["wp"],
                                                   sv["x"], row(g_mix_pre[i]), dx2, axial=False)
        dW_in[i] = _unpad_dw_in(mm_acc(sv["h1"].T, dproj, tk=D, tn=896, ts=1024))

    x_i, y_i, _ = _coords()
    me = 2 * x_i + y_i
    big = [
        (jnp.transpose(jnp.stack(dW_in).reshape(depth, D, 4, IN_W // 4), (2, 0, 1, 3)), w_in, m_w_in, v_w_in),
        (jnp.transpose(jnp.stack(dW_mkv).reshape(depth, 4, D // 4, 512), (1, 0, 2, 3)), w_mem_kv, m_w_mem_kv, v_w_mem_kv),
        (jnp.transpose(jnp.stack(dW_o).reshape(depth, 4, D // 4, D), (1, 0, 2, 3)), w_o, m_w_o, v_w_o),
        (jnp.transpose(jnp.stack(dW_gu).reshape(depth, D, 4, 2 * DFF // 4), (2, 0, 1, 3)), w_gate_up, m_w_gate_up,
         v_w_gate_up),
        (jnp.transpose(jnp.stack(dW_d).reshape(depth, 4, DFF // 4, D), (1, 0, 2, 3)), w_down, m_w_down, v_w_down),
    ]
    recvs = scatter_grads([b[0].astype(BF) for b in big])
    parts = []
    for (g4, w, _, _), rc in zip(big, recvs):
        C = w.shape[-1]
        own = lax.dynamic_index_in_dim(g4, me, 0, keepdims=False).reshape(-1, C)
        parts.append(sum_parts(own, rc.reshape(3, -1, C)))
    sibs = sibling_swap(parts)
    big_out = []
    for (_, w, m, v), pa, pb in zip(big, parts, sibs):
        C = w.shape[-1]
        outs = adamw(w.reshape(-1, C), pa, pb, m.reshape(-1, C), v.reshape(-1, C))
        big_out.append([o.reshape(w.shape) for o in outs])

    small_w = [mem_norm_g, g_mix_pre, g_mix_post, attn_sink, qk_norm_g, g_ffn_pre, g_ffn_post]
    small_m = [m_mem_norm_g, m_g_mix_pre, m_g_mix_post, m_attn_sink, m_qk_norm_g, m_g_ffn_pre, m_g_ffn_post]
    small_v = [v_mem_norm_g, v_g_mix_pre, v_g_mix_post, v_attn_sink, v_qk_norm_g, v_g_ffn_pre, v_g_ffn_post]
    small_g = [dg_mem.reshape(D), jnp.concatenate(dg_pre, axis=0), jnp.concatenate(dg_post, axis=0),
               jnp.stack([dsinks[k] for k in sorted(dsinks)]), dqk, jnp.concatenate(dg_fpre, axis=0),
               jnp.concatenate(dg_fpost, axis=0)]
    sizes = [a.size for a in small_w]
    total = sum(sizes)
    rows_s = -(-(total + LANES) // (8 * LANES)) * 8

    def pack(arrs, extra=None):
        flat = jnp.concatenate([a.reshape(-1).astype(F32) for a in arrs])
        flat = jnp.pad(flat, (0, rows_s * LANES - LANES - total))
        tail = jnp.zeros((LANES,), F32) if extra is None else extra.reshape(LANES)
        return jnp.concatenate([flat, tail]).reshape(rows_s, LANES)

    tot = allsum_small(pack(small_g, loss_vec))
    loss = jnp.sum(tot[rows_s - 1])
    s_out = adamw(pack(small_w), tot, None, pack(small_m), pack(small_v))

    def unpack(buf):
        flat = buf.reshape(-1)
        out, off = [], 0
        for a, n in zip(small_w, sizes):
            out.append(flat[off:off + n].reshape(a.shape))
            off += n
        return out

    sg, sd_, sm, sv_ = (unpack(b) for b in s_out)

    def ordered(k):
        sm_ = (sg, sd_, sm, sv_)[k]
        b = [bo[k] for bo in big_out]
        return [sm_[0], b[0], b[1], b[2], sm_[1], sm_[2], sm_[3], sm_[4], b[3], b[4], sm_[5], sm_[6]]

    dx_out = dcur.reshape(1, S, D)
    return (loss, dx_out, *ordered(0), *ordered(1), *ordered(2), *ordered(3))
```

```python
import functools

import jax
import jax.numpy as jnp
from jax import lax
from jax.experimental import pallas as pl
from jax.experimental.pallas import tpu as pltpu

F32 = jnp.float32
BF = jnp.bfloat16

D = 1024
HD = 64
LANES = 128
N_PAIRS = 6
DFF = 2816
IN_W = 1408
PW = 14 * LANES
K_T0, V_T0, M_T0 = 6, 9, 12
EPS = 1e-6
SCALE = HD ** -0.5
NEG = -1e30
LOG2E = 1.4426950408889634
LN2 = 0.6931471805599453
MAX_PLAIN_SCORE = 40.0
ROPE_THETA = 500000.0
AXIAL_THETA = 10000.0
GRID_W = 64
A_RADIUS = 128
C_GROUPS = ((128, 1), (512, 4), (2048, 16))
N_MEM = 256
LR, B1, B2, AEPS, WD, STEP = 0.001, 0.9, 0.999, 1e-08, 0.01, 10
VMEM_LIMIT = 56 * 1024 * 1024
MESH = pl.DeviceIdType.MESH


def _pcall(body, *, name, grid, in_specs, out_specs, out_shape, scratch=()):
    return pl.pallas_call(
        body, name=name, grid=grid, in_specs=in_specs, out_specs=out_specs, out_shape=out_shape,
        scratch_shapes=scratch,
        compiler_params=pltpu.CompilerParams(dimension_semantics=("arbitrary",) * len(grid),
                                             vmem_limit_bytes=VMEM_LIMIT))


def _dot(a, b):
    return lax.dot_general(a, b, (((1,), (0,)), ((), ())), preferred_element_type=F32)


def _dot_nt(a, b):
    return lax.dot_general(a, b, (((1,), (1,)), ((), ())), preferred_element_type=F32)


def _lo(shape):
    return lax.broadcasted_iota(jnp.int32, shape, len(shape) - 1) < HD


def _half_sum(x, lo):
    a = jnp.sum(jnp.where(lo, x, 0.0), axis=-1, keepdims=True)
    b = jnp.sum(jnp.where(lo, 0.0, x), axis=-1, keepdims=True)
    return jnp.where(lo, a, b)


def _col(tile, lane):
    idx = lax.broadcasted_iota(jnp.int32, tile.shape, 1)
    return jnp.sum(jnp.where(idx == lane, tile, 0.0), axis=-1, keepdims=True)


def _split(t, lo):
    z = jnp.zeros_like(t)
    return jnp.where(lo, t, z), jnp.where(lo, z, t)


def _rms(xf, g):
    r = lax.rsqrt(jnp.mean(xf * xf, axis=-1, keepdims=True) + EPS)
    return xf * r * g


def _rms_bwd(xf, g, dy):
    r = lax.rsqrt(jnp.mean(xf * xf, axis=-1, keepdims=True) + EPS)
    xr = xf * r
    dg = jnp.sum(dy * xr, axis=0, keepdims=True)
    t = dy * g
    return r * (t - xr * jnp.mean(xr * t, axis=-1, keepdims=True)), dg


def _rope_fwd(y, c, s1, s2, sh):
    return y * c + pltpu.roll(y, sh, 1) * s1 + pltpu.roll(y, LANES - sh, 1) * s2


def _rope_bwd(dy, c, s1, s2, sh):
    return dy * c + pltpu.roll(dy * s1, LANES - sh, 1) + pltpu.roll(dy * s2, sh, 1)


def _tile(t):
    return slice(t * LANES, (t + 1) * LANES)


def inproj_fwd(x, g, w_pad, tabs, qkg, *, axial, tm=512):
    S = x.shape[0]
    sh = 16 if axial else 8

    def body(x_ref, g_ref, w_ref, c_ref, s1_ref, s2_ref, qkg_ref, h_ref, p_ref, *extra):
        h = _rms(x_ref[...], g_ref[...]).astype(BF)
        h_ref[...] = h
        acc = _dot(h, w_ref[...])
        c, s1, s2 = c_ref[...], s1_ref[...], s2_ref[...]
        lo = _lo((tm, LANES))
        if axial:
            raw_ref, nrm_ref = extra

            @pl.when(pl.program_id(0) == 0)
            def _():
                nrm_ref[...] = jnp.zeros_like(nrm_ref)

        for t in range(14):
            y = acc[:, _tile(t)]
            if t < V_T0:
                if axial:
                    raw_ref[:, _tile(t)] = y.astype(BF)
                    gt = qkg_ref[0:1, :] if t < K_T0 else qkg_ref[1:2, :]
                    y = y * lax.rsqrt(_half_sum(y * y, lo) * (1.0 / HD) + EPS) * gt
                y = _rope_fwd(y, c, s1, s2, sh)
            if t < K_T0:
                y = y * (SCALE * LOG2E if axial else SCALE)
            elif t >= M_T0:
                y = y * SCALE
            yb = y.astype(BF)
            p_ref[:, _tile(t)] = yb
            if axial and t < V_T0:
                yf = yb.astype(F32)
                n2 = jnp.max(_half_sum(yf * yf, lo), axis=0, keepdims=True)
                r = 0 if t < K_T0 else 1
                nrm_ref[r:r + 1, :] = jnp.maximum(nrm_ref[r:r + 1, :], n2)

    row = lambda w: pl.BlockSpec((tm, w), lambda i: (i, 0))
    full = lambda a: pl.BlockSpec(a.shape, lambda i: (0, 0))
    out_shape = [jax.ShapeDtypeStruct((S, D), BF), jax.ShapeDtypeStruct((S, PW), BF)]
    out_specs = [row(D), row(PW)]
    if axial:
        out_shape += [jax.ShapeDtypeStruct((S, V_T0 * LANES), BF), jax.ShapeDtypeStruct((8, LANES), F32)]
        out_specs += [row(V_T0 * LANES), pl.BlockSpec((8, LANES), lambda i: (0, 0))]
    return _pcall(body, name="inproj_fwd_axial" if axial else "inproj_fwd", grid=(S // tm,),
                  in_specs=[row(D), full(g), full(w_pad), row(LANES), row(LANES), row(LANES), full(qkg)],
                  out_specs=out_specs, out_shape=out_shape)(x, g, w_pad, *tabs, qkg)


def norm_mm(x, g, w, *, tm):
    S, N = x.shape[0], w.shape[1]

    def body(x_ref, g_ref, w_ref, h_ref, y_ref):
        h = _rms(x_ref[...], g_ref[...]).astype(BF)
        h_ref[...] = h
        y_ref[...] = _dot(h, w_ref[...]).astype(BF)

    return _pcall(body, name="norm_mm", grid=(S // tm,),
                  in_specs=[pl.BlockSpec((tm, D), lambda i: (i, 0)), pl.BlockSpec(g.shape, lambda i: (0, 0)),
                            pl.BlockSpec(w.shape, lambda i: (0, 0))],
                  out_specs=[pl.BlockSpec((tm, D), lambda i: (i, 0)), pl.BlockSpec((tm, N), lambda i: (i, 0))],
                  out_shape=[jax.ShapeDtypeStruct((S, D), BF), jax.ShapeDtypeStruct((S, N), BF)])(x, g, w)


def ffn_up_fwd(x, g, wg, wu, *, tm=512, tn=1408):
    S = x.shape[0]

    def body(x_ref, g_ref, wg_ref, wu_ref, h_ref, gate_ref, up_ref, a_ref, h_scr):
        @pl.when(pl.program_id(1) == 0)
        def _():
            h = _rms(x_ref[...], g_ref[...]).astype(BF)
            h_scr[...] = h
            h_ref[...] = h

        h = h_scr[...]
        gate = _dot(h, wg_ref[...])
        up = _dot(h, wu_ref[...])
        gate_ref[...] = gate.astype(BF)
        up_ref[...] = up.astype(BF)
        a_ref[...] = (gate * (1.0 / (1.0 + jnp.exp(-gate))) * up).astype(BF)

    rowd = pl.BlockSpec((tm, D), lambda i, j: (i, 0))
    wsp = pl.BlockSpec((D, tn), lambda i, j: (0, j))
    osp = pl.BlockSpec((tm, tn), lambda i, j: (i, j))
    sd = jax.ShapeDtypeStruct((S, DFF), BF)
    return _pcall(body, name="ffn_up_fwd", grid=(S // tm, DFF // tn),
                  in_specs=[rowd, pl.BlockSpec(g.shape, lambda i, j: (0, 0)), wsp, wsp],
                  out_specs=[rowd, osp, osp, osp],
                  out_shape=[jax.ShapeDtypeStruct((S, D), BF), sd, sd, sd],
                  scratch=[pltpu.VMEM((tm, D), BF)])(x, g, wg, wu)


def mm_norm_res(a, w, g, res, *, tm=512):
    S, K = a.shape

    def body(a_ref, w_ref, g_ref, res_ref, y_ref, o_ref):
        y = _dot(a_ref[...], w_ref[...])
        y_ref[...] = y
        o_ref[...] = res_ref[...] + _rms(y, g_ref[...])

    rowd = pl.BlockSpec((tm, D), lambda i: (i, 0))
    sd = jax.ShapeDtypeStruct((S, D), F32)
    return _pcall(body, name="mm_norm_res", grid=(S // tm,),
                  in_specs=[pl.BlockSpec((tm, K), lambda i: (i, 0)), pl.BlockSpec(w.shape, lambda i: (0, 0)),
                            pl.BlockSpec(g.shape, lambda i: (0, 0)), rowd],
                  out_specs=[rowd, rowd], out_shape=[sd, sd])(a, w, g, res)


def loss_bwd(y, tgt, *, tm=512):
    S = y.shape[0]

    def body(y_ref, t_ref, dy_ref, l_ref):
        @pl.when(pl.program_id(0) == 0)
        def _():
            l_ref[...] = jnp.zeros_like(l_ref)

        e = y_ref[...] - t_ref[...]
        dy_ref[...] = e * (1.0 / D)
        col = jnp.sum(e * e, axis=0, keepdims=True)
        part = col[:, _tile(0)]
        for t in range(1, D // LANES):
            part = part + col[:, _tile(t)]
        l_ref[...] += part * (0.5 / D)

    rowd = pl.BlockSpec((tm, D), lambda i: (i, 0))
    return _pcall(body, name="loss_bwd", grid=(S // tm,), in_specs=[rowd, rowd],
                  out_specs=[rowd, pl.BlockSpec((1, LANES), lambda i: (0, 0))],
                  out_shape=[jax.ShapeDtypeStruct((S, D), F32), jax.ShapeDtypeStruct((1, LANES), F32)])(y, tgt)


def normbwd_mm_cat(dy, ysaved, g, w, cat, *, tm=512):
    S = dy.shape[0]

    def body(dy_ref, y_ref, g_ref, w_ref, cat_ref, d_ref, dcat_ref, delta_ref, dg_ref):
        @pl.when(pl.program_id(0) == 0)
        def _():
            dg_ref[...] = jnp.zeros_like(dg_ref)

        d, dg = _rms_bwd(y_ref[...], g_ref[...], dy_ref[...])
        dg_ref[...] += dg
        d = d.astype(BF)
        d_ref[...] = d
        z = _dot_nt(d, w_ref[...])
        dcat_ref[...] = z.astype(BF)
        lo = _lo((tm, LANES))
        for t in range(D // LANES):
            delta_ref[:, _tile(t)] = _half_sum(z[:, _tile(t)] * cat_ref[:, _tile(t)].astype(F32), lo)

    rowd = pl.BlockSpec((tm, D), lambda i: (i, 0))
    return _pcall(body, name="normbwd_mm_cat", grid=(S // tm,),
                  in_specs=[rowd, rowd, pl.BlockSpec(g.shape, lambda i: (0, 0)),
                            pl.BlockSpec(w.shape, lambda i: (0, 0)), rowd],
                  out_specs=[rowd, rowd, rowd, pl.BlockSpec((1, D), lambda i: (0, 0))],
                  out_shape=[jax.ShapeDtypeStruct((S, D), BF), jax.ShapeDtypeStruct((S, D), BF),
                             jax.ShapeDtypeStruct((S, D), F32), jax.ShapeDtypeStruct((1, D), F32)])(dy, ysaved, g, w, cat)


def normbwd_mm_swiglu(dy, fsaved, g, wd, gate, up, *, tm=512, tn=1408):
    S = dy.shape[0]

    def body(dy_ref, f_ref, g_ref, w_ref, gate_ref, up_ref, df_ref, dgate_ref, dup_ref, dg_ref, d_scr):
        i, j = pl.program_id(0), pl.program_id(1)

        @pl.when((i == 0) & (j == 0))
        def _():
            dg_ref[...] = jnp.zeros_like(dg_ref)

        @pl.when(j == 0)
        def _():
            d, dg = _rms_bwd(f_ref[...], g_ref[...], dy_ref[...])
            dg_ref[...] += dg
            d_scr[...] = d.astype(BF)
            df_ref[...] = d.astype(BF)

        da = _dot_nt(d_scr[...], w_ref[...])
        gt = gate_ref[...].astype(F32)
        sig = 1.0 / (1.0 + jnp.exp(-gt))
        dgate_ref[...] = (da * up_ref[...].astype(F32) * (sig * (1.0 + gt * (1.0 - sig)))).astype(BF)
        dup_ref[...] = (da * (gt * sig)).astype(BF)

    rowd = pl.BlockSpec((tm, D), lambda i, j: (i, 0))
    osp = pl.BlockSpec((tm, tn), lambda i, j: (i, j))
    sd = jax.ShapeDtypeStruct((S, DFF), BF)
    return _pcall(body, name="normbwd_mm_swiglu", grid=(S // tm, DFF // tn),
                  in_specs=[rowd, rowd, pl.BlockSpec(g.shape, lambda i, j: (0, 0)),
                            pl.BlockSpec((tn, D), lambda i, j: (j, 0)), osp, osp],
                  out_specs=[rowd, osp, osp, pl.BlockSpec((1, D), lambda i, j: (0, 0))],
                  out_shape=[jax.ShapeDtypeStruct((S, D), BF), sd, sd, jax.ShapeDtypeStruct((1, D), F32)],
                  scratch=[pltpu.VMEM((tm, D), BF)])(dy, fsaved, g, wd, gate, up)


def mm_nt_normbwd_res(parts, xin, g, dres, *, tm=512, nk=1):
    S = xin.shape[0]
    npart = len(parts)
    kb = parts[0][0].shape[1] // nk
    has_res = dres is not None

    def body(*refs):
        prefs = refs[:2 * npart]
        x_ref, g_ref = refs[2 * npart:2 * npart + 2]
        rest = refs[2 * npart + 2:]
        if has_res:
            dres_ref, dx_ref, dg_ref, acc = rest
        else:
            dg_ref, acc = rest
        i, k = pl.program_id(0), pl.program_id(1)

        @pl.when((i == 0) & (k == 0))
        def _():
            dg_ref[...] = jnp.zeros_like(dg_ref)

        z = _dot_nt(prefs[0][...], prefs[1][...])
        for p in range(1, npart):
            z = z + _dot_nt(prefs[2 * p][...], prefs[2 * p + 1][...])

        @pl.when(k == 0)
        def _():
            acc[...] = z

        @pl.when(k > 0)
        def _():
            acc[...] += z

        @pl.when(k == nk - 1)
        def _():
            dx, dg = _rms_bwd(x_ref[...], g_ref[...], acc[...])
            dg_ref[...] += dg
            if has_res:
                dx_ref[...] = dres_ref[...] + dx

    rowd = pl.BlockSpec((tm, D), lambda i, k: (i, 0))
    in_specs, args = [], []
    for dy, w in parts:
        in_specs += [pl.BlockSpec((tm, kb), lambda i, k: (i, k)), pl.BlockSpec((D, kb), lambda i, k: (0, k))]
        args += [dy, w]
    in_specs += [rowd, pl.BlockSpec(g.shape, lambda i, k: (0, 0))]
    args += [xin, g]
    out_specs = [pl.BlockSpec((1, D), lambda i, k: (0, 0))]
    out_shape = [jax.ShapeDtypeStruct((1, D), F32)]
    if has_res:
        in_specs.append(rowd)
        args.append(dres)
        out_specs.insert(0, rowd)
        out_shape.insert(0, jax.ShapeDtypeStruct((S, D), F32))
    return _pcall(body, name="mm_nt_normbwd_res" if has_res else "mm_nt_normbwd", grid=(S // tm, nk),
                  in_specs=in_specs, out_specs=out_specs, out_shape=out_shape,
                  scratch=[pltpu.VMEM((tm, D), F32)])(*args)


def inproj_bwd(dq, dkp, dvp, dqm, tabs, raw, qkg, w_pad, xin, g, dres, *, axial, tm=512):
    S = xin.shape[0]
    sh = 16 if axial else 8

    def body(dq_ref, dk_ref, dv_ref, dm_ref, c_ref, s1_ref, s2_ref, raw_ref, qkg_ref, w_ref, x_ref, g_ref, dres_ref,
             dx_ref, dp_ref, dg_ref, dqk_ref):
        @pl.when(pl.program_id(0) == 0)
        def _():
            dg_ref[...] = jnp.zeros_like(dg_ref)
            dqk_ref[...] = jnp.zeros_like(dqk_ref)

        c, s1, s2 = c_ref[...], s1_ref[...], s2_ref[...]
        lo = _lo((tm, LANES))
        for t in range(14):
            if t < K_T0:
                y = dq_ref[:, _tile(t)] * SCALE
            elif t < V_T0:
                y = dk_ref[:, _tile(2 * (t - K_T0))] + dk_ref[:, _tile(2 * (t - K_T0) + 1)]
                if axial:
                    y = y * LN2
            elif t < M_T0:
                y = dv_ref[:, _tile(2 * (t - V_T0))] + dv_ref[:, _tile(2 * (t - V_T0) + 1)]
            else:
                y = dm_ref[:, _tile(t - M_T0)] * SCALE
            if t < V_T0:
                y = _rope_bwd(y, c, s1, s2, sh)
                if axial:
                    row = 0 if t < K_T0 else 1
                    xr = raw_ref[:, _tile(t)].astype(F32)
                    r = lax.rsqrt(_half_sum(xr * xr, lo) * (1.0 / HD) + EPS)
                    xn = xr * r
                    dqk_ref[row:row + 1, :] += jnp.sum(y * xn, axis=0, keepdims=True)
                    tt = y * qkg_ref[row:row + 1, :]
                    y = r * (tt - xn * (_half_sum(xn * tt, lo) * (1.0 / HD)))
            dp_ref[:, _tile(t)] = y.astype(BF)
        z = _dot_nt(dp_ref[...], w_ref[...])
        dx, dg = _rms_bwd(x_ref[...], g_ref[...], z)
        dg_ref[...] += dg
        dx_ref[...] = dres_ref[...] + dx

    row = lambda w: pl.BlockSpec((tm, w), lambda i: (i, 0))
    full = lambda a: pl.BlockSpec(a.shape, lambda i: (0, 0))
    return _pcall(body, name="inproj_bwd_axial" if axial else "inproj_bwd", grid=(S // tm,),
                  in_specs=[row(768), row(768), row(768), row(256), row(LANES), row(LANES), row(LANES),
                            row(raw.shape[1] if axial else LANES), full(qkg), full(w_pad), row(D), full(g), row(D)],
                  out_specs=[row(D), row(PW), pl.BlockSpec((1, D), lambda i: (0, 0)),
                             pl.BlockSpec((8, LANES), lambda i: (0, 0))],
                  out_shape=[jax.ShapeDtypeStruct((S, D), F32), jax.ShapeDtypeStruct((S, PW), BF),
                             jax.ShapeDtypeStruct((1, D), F32), jax.ShapeDtypeStruct((8, LANES), F32)])(
        dq, dkp, dvp, dqm, *tabs, raw, qkg, w_pad, xin, g, dres)


def mm_acc(at, b, *, tk, tn, ts):
    K, S = at.shape
    N = b.shape[1]
    ts = min(ts, S)

    def body(a_ref, b_ref, o_ref):
        z = _dot(a_ref[...], b_ref[...])

        @pl.when(pl.program_id(2) == 0)
        def _():
            o_ref[...] = z

        @pl.when(pl.program_id(2) > 0)
        def _():
            o_ref[...] += z

    return _pcall(body, name="mm_acc", grid=(K // tk, N // tn, S // ts),
                  in_specs=[pl.BlockSpec((tk, ts), lambda k, n, s: (k, s)), pl.BlockSpec((ts, tn), lambda k, n, s: (s, n))],
                  out_specs=pl.BlockSpec((tk, tn), lambda k, n, s: (k, n)),
                  out_shape=jax.ShapeDtypeStruct((K, N), F32))(at, b)


def _band_specs(L, d, R, TQ, width, tile_of):
    n = TQ // R
    nt = width // LANES
    last = L // R - 1
    col = lambda p, r: r * nt + tile_of(p)
    return [pl.BlockSpec((R, LANES), lambda p, r, i: (jnp.maximum(i * n - 1, 0), col(p, r))),
            pl.BlockSpec((TQ, LANES), lambda p, r, i: (i, col(p, r))),
            pl.BlockSpec((R, LANES), lambda p, r, i: (jnp.minimum((i + 1) * n, last), col(p, r)))]


def _band_valid(i, TQ, R, L, rows_are_window):
    W = TQ + 2 * R
    if rows_are_window:
        wpos = i * TQ - R + lax.broadcasted_iota(jnp.int32, (W, TQ), 0)
        cpos = i * TQ + lax.broadcasted_iota(jnp.int32, (W, TQ), 1)
    else:
        wpos = i * TQ - R + lax.broadcasted_iota(jnp.int32, (TQ, W), 1)
        cpos = i * TQ + lax.broadcasted_iota(jnp.int32, (TQ, W), 0)
    return (jnp.abs(wpos - cpos) <= R) & (wpos >= 0) & (wpos < L)


def banded_fwd(proj, sink, *, d, R, TQ, pair0, npairs, use_sink, o_dtype):
    S = proj.shape[0]
    L = S // d
    pv = proj.reshape(L, d * PW)
    ow = npairs * LANES

    def body(sink_ref, q_ref, kp, kc, kn, vp, vc, vn, o_ref, lse_ref):
        p, i = pl.program_id(0), pl.program_id(2)
        lo = _lo((TQ, LANES))
        qa, qb = _split(q_ref[...], lo)
        kw = jnp.concatenate([kp[...], kc[...], kn[...]], axis=0)
        vw = jnp.concatenate([vp[...], vc[...], vn[...]], axis=0)
        valid = _band_valid(i, TQ, R, L, False)
        ps, dens, lses = [], [], []
        for h, qh in enumerate((qa, qb)):
            s = jnp.where(valid, _dot_nt(qh, kw), NEG)
            m = jnp.max(s, axis=-1, keepdims=True)
            if use_sink:
                sk = sink_ref[2 * (pair0 + p) + h]
                m = jnp.maximum(m, sk)
            e = jnp.exp(s - m)
            den = jnp.sum(e, axis=-1, keepdims=True)
            if use_sink:
                den = den + jnp.exp(sk - m)
            ps.append(e.astype(BF))
            dens.append(den)
            lses.append(m + jnp.log(den))
        lov = _lo(vw.shape)
        v_lo, v_hi = _split(vw, lov)
        o = _dot(jnp.concatenate(ps, axis=1), jnp.concatenate([v_lo, v_hi], axis=0))
        o_ref[...] = (o / jnp.where(lo, dens[0], dens[1])).astype(o_dtype)
        lse_ref[...] = jnp.where(lo, lses[0], lses[1])

    qspec = pl.BlockSpec((TQ, LANES), lambda p, r, i: (i, r * 14 + pair0 + p))
    kspecs = _band_specs(L, d, R, TQ, PW, lambda p: K_T0 + (pair0 + p) // 2)
    vspecs = _band_specs(L, d, R, TQ, PW, lambda p: V_T0 + (pair0 + p) // 2)
    ospec = pl.BlockSpec((TQ, LANES), lambda p, r, i: (i, r * npairs + p))
    o, lse = _pcall(body, name="banded_fwd", grid=(npairs, d, L // TQ),
                    in_specs=[pl.BlockSpec(memory_space=pltpu.SMEM), qspec] + kspecs + vspecs,
                    out_specs=[ospec, ospec],
                    out_shape=[jax.ShapeDtypeStruct((L, d * ow), o_dtype), jax.ShapeDtypeStruct((L, d * ow), F32)])(
        sink, pv, pv, pv, pv, pv, pv, pv)
    return o.reshape(S, ow), lse.reshape(S, ow)


def banded_bwd_dq(proj, do, lse, delta, sink, *, d, R, TQ, pair0, npairs, use_sink):
    S = proj.shape[0]
    L = S // d
    pv = proj.reshape(L, d * PW)
    ow = npairs * LANES

    def body(sink_ref, q_ref, kp, kc, kn, vp, vc, vn, do_ref, lse_ref, delta_ref, dq_ref, dsink_ref):
        p, r, i = pl.program_id(0), pl.program_id(1), pl.program_id(2)

        @pl.when((r == 0) & (i == 0))
        def _():
            dsink_ref[...] = jnp.zeros_like(dsink_ref)

        lo = _lo((TQ, LANES))
        qa, qb = _split(q_ref[...], lo)
        doa, dob = _split(do_ref[...], lo)
        kw = jnp.concatenate([kp[...], kc[...], kn[...]], axis=0)
        vw = jnp.concatenate([vp[...], vc[...], vn[...]], axis=0)
        valid = _band_valid(i, TQ, R, L, False)
        lse_t, delta_t = lse_ref[...], delta_ref[...]
        dss, dsk = [], []
        for h, (qh, doh) in enumerate(((qa, doa), (qb, dob))):
            lse_h, delta_h = _col(lse_t, h * HD), _col(delta_t, h * HD)
            pr = jnp.where(valid, jnp.exp(_dot_nt(qh, kw) - lse_h), 0.0)
            dss.append((pr * (_dot_nt(doh, vw) - delta_h)).astype(BF))
            if use_sink:
                psink = jnp.exp(sink_ref[2 * (pair0 + p) + h] - lse_h)
                dsk.append(-jnp.sum(psink * delta_h, axis=0, keepdims=True))
        k_lo, k_hi = _split(kw, _lo(kw.shape))
        dq_ref[...] = _dot(jnp.concatenate(dss, axis=1), jnp.concatenate([k_lo, k_hi], axis=0))
        if use_sink:
            dsink_ref[...] += jnp.where(_lo((8, LANES)), dsk[0], dsk[1])

    qspec = pl.BlockSpec((TQ, LANES), lambda p, r, i: (i, r * 14 + pair0 + p))
    kspecs = _band_specs(L, d, R, TQ, PW, lambda p: K_T0 + (pair0 + p) // 2)
    vspecs = _band_specs(L, d, R, TQ, PW, lambda p: V_T0 + (pair0 + p) // 2)
    ospec = pl.BlockSpec((TQ, LANES), lambda p, r, i: (i, r * npairs + p))
    view = lambda a: a.reshape(L, d * a.shape[1])
    ispec = lambda a: pl.BlockSpec((TQ, LANES), lambda p, r, i: (i, r * (a.shape[1] // LANES) + p))
    dq, dsink = _pcall(body, name="banded_bwd_dq", grid=(npairs, d, L // TQ),
                       in_specs=[pl.BlockSpec(memory_space=pltpu.SMEM), qspec] + kspecs + vspecs
                       + [ispec(do), ispec(lse), ispec(delta)],
                       out_specs=[ospec, pl.BlockSpec((8, LANES), lambda p, r, i: (p, 0))],
                       out_shape=[jax.ShapeDtypeStruct((L, d * ow), F32),
                                  jax.ShapeDtypeStruct((npairs * 8, LANES), F32)])(
        sink, pv, pv, pv, pv, pv, pv, pv, view(do), view(lse), view(delta))
    return dq.reshape(S, ow), dsink


def banded_bwd_dkv(proj, do, lse, delta, *, d, R, TK, pair0, npairs):
    S = proj.shape[0]
    L = S // d
    pv = proj.reshape(L, d * PW)
    ow = npairs * LANES

    def body(k_ref, v_ref, qp, qc, qn, dop, doc, don, lp, lc, ln, dp_, dc_, dn_, dk_ref, dv_ref):
        j = pl.program_id(2)
        W = TK + 2 * R
        low = _lo((W, LANES))
        qa, qb = _split(jnp.concatenate([qp[...], qc[...], qn[...]], axis=0), low)
        doa, dob = _split(jnp.concatenate([dop[...], doc[...], don[...]], axis=0), low)
        lse_t = jnp.concatenate([lp[...], lc[...], ln[...]], axis=0)
        delta_t = jnp.concatenate([dp_[...], dc_[...], dn_[...]], axis=0)
        k, v = k_ref[...], v_ref[...]
        valid = _band_valid(j, TK, R, L, True)
        prs, dss = [], []
        for h, (qh, doh) in enumerate(((qa, doa), (qb, dob))):
            pr = jnp.where(valid, jnp.exp(_dot_nt(qh, k) - _col(lse_t, h * HD)), 0.0)
            dss.append(pr * (_dot_nt(doh, v) - _col(delta_t, h * HD)))
            prs.append(pr)
        dv_ref[...] = _dot(jnp.concatenate(prs, axis=0).T.astype(BF), jnp.concatenate([doa, dob], axis=0))
        dk_ref[...] = _dot(jnp.concatenate(dss, axis=0).T.astype(BF), jnp.concatenate([qa, qb], axis=0))

    kspec = pl.BlockSpec((TK, LANES), lambda p, r, j: (j, r * 14 + K_T0 + (pair0 + p) // 2))
    vspec = pl.BlockSpec((TK, LANES), lambda p, r, j: (j, r * 14 + V_T0 + (pair0 + p) // 2))
    qspecs = _band_specs(L, d, R, TK, PW, lambda p: pair0 + p)
    ispecs = lambda a: _band_specs(L, d, R, TK, a.shape[1], lambda p: p)
    view = lambda a: a.reshape(L, d * a.shape[1])
    ospec = pl.BlockSpec((TK, LANES), lambda p, r, j: (j, r * npairs + p))
    sd = jax.ShapeDtypeStruct((L, d * ow), F32)
    dk, dv = _pcall(body, name="banded_bwd_dkv", grid=(npairs, d, L // TK),
                    in_specs=[kspec, vspec] + qspecs + ispecs(do) + ispecs(lse) + ispecs(delta),
                    out_specs=[ospec, ospec], out_shape=[sd, sd])(
        pv, pv, pv, pv, pv, *([view(do)] * 3), *([view(lse)] * 3), *([view(delta)] * 3))
    return dk.reshape(S, ow), dv.reshape(S, ow)


def flash_fwd(proj, bound, *, tq=1024, tk=512):
    S = proj.shape[0]

    def body_general(q_ref, k_ref, v_ref, o_ref, lse_ref):
        lo = _lo((tq, LANES))
        qa, qb = _split(q_ref[...], lo)
        lov = _lo((tk, LANES))

        def step(j, carry):
            ma, la, mb, lb, acc = carry
            rows = pl.ds(pl.multiple_of(j * tk, tk), tk)
            k, v = k_ref[rows, :], v_ref[rows, :]
            outs = []
            for qh, m0, l0 in ((qa, ma, la), (qb, mb, lb)):
                s = _dot_nt(qh, k)
                m1 = jnp.maximum(m0, jnp.max(s, axis=-1, keepdims=True))
                al = jnp.exp2(m0 - m1)
                e = jnp.exp2(s - m1)
                outs.append((m1, al * l0 + jnp.sum(e, axis=-1, keepdims=True), al, e.astype(BF)))
            v_lo, v_hi = _split(v, lov)
            pvv = _dot(jnp.concatenate([outs[0][3], outs[1][3]], axis=1), jnp.concatenate([v_lo, v_hi], axis=0))
            acc = acc * jnp.where(lo, outs[0][2], outs[1][2]) + pvv
            return outs[0][0], outs[0][1], outs[1][0], outs[1][1], acc

        m_init = jnp.full((tq, 1), NEG, F32)
        l_init = jnp.zeros((tq, 1), F32)
        ma, la, mb, lb, acc = lax.fori_loop(0, S // tk, step,
                                            (m_init, l_init, m_init, l_init, jnp.zeros((tq, LANES), F32)))
        o_ref[...] = (acc / jnp.where(lo, la, lb)).astype(BF)
        lse_ref[...] = jnp.where(lo, ma * LN2 + jnp.log(la), mb * LN2 + jnp.log(lb))

    def body_plain(q_ref, k_ref, v_ref, o_ref, lse_ref):
        lo = _lo((tq, LANES))
        qa, qb = _split(q_ref[...], lo)
        lov = _lo((tk, LANES))
        one = jnp.ones((tk, LANES), BF)

        def step(j, carry):
            acc_a, acc_b = carry
            rows = pl.ds(pl.multiple_of(j * tk, tk), tk)
            k, v = k_ref[rows, :], v_ref[rows, :]
            ea = jnp.exp2(_dot_nt(qa, k)).astype(BF)
            eb = jnp.exp2(_dot_nt(qb, k)).astype(BF)
            acc_a = acc_a + _dot(ea, jnp.where(lov, v, one))
            acc_b = acc_b + _dot(eb, jnp.where(lov, one, v))
            return acc_a, acc_b

        z = jnp.zeros((tq, LANES), F32)
        acc_a, acc_b = lax.fori_loop(0, S // tk, step, (z, z))
        den = jnp.where(lo, pltpu.roll(acc_a, HD, 1), pltpu.roll(acc_b, HD, 1))
        o_ref[...] = (jnp.where(lo, acc_a, acc_b) / den).astype(BF)
        lse_ref[...] = jnp.log(den)

    def body(bound_ref, q_ref, k_ref, v_ref, o_ref, lse_ref):
        small = bound_ref[0] <= MAX_PLAIN_SCORE

        @pl.when(small)
        def _():
            body_plain(q_ref, k_ref, v_ref, o_ref, lse_ref)

        @pl.when(jnp.logical_not(small))
        def _():
            body_general(q_ref, k_ref, v_ref, o_ref, lse_ref)

    ospec = pl.BlockSpec((tq, LANES), lambda p, i: (i, p))
    return _pcall(body, name="flash_fwd", grid=(N_PAIRS, S // tq),
                  in_specs=[pl.BlockSpec(memory_space=pltpu.SMEM), ospec,
                            pl.BlockSpec((S, LANES), lambda p, i: (0, K_T0 + p // 2)),
                            pl.BlockSpec((S, LANES), lambda p, i: (0, V_T0 + p // 2))],
                  out_specs=[ospec, ospec],
                  out_shape=[jax.ShapeDtypeStruct((S, 768), BF), jax.ShapeDtypeStruct((S, 768), F32)])(
        bound.reshape(1), proj, proj, proj)


def flash_bwd(proj, do, lse_t, delta_t, *, tq=512, tk=1024):
    S = proj.shape[0]
    nq = S // tq

    def body(k_ref, v_ref, q_ref, do_ref, lse_ref, delta_ref, dk_ref, dv_ref, dqt_ref):
        @pl.when(pl.program_id(1) == 0)
        def _():
            dqt_ref[...] = jnp.zeros_like(dqt_ref)

        k, v = k_ref[...], v_ref[...]
        lo = _lo((tq, LANES))
        k_lo, k_hi = _split(k.astype(F32), _lo((tk, LANES)))
        kt = jnp.concatenate([k_lo.T, k_hi.T], axis=1).astype(BF)

        def step(i, carry):
            dk, dv = carry
            rows = pl.ds(pl.multiple_of(i * tq, tq), tq)
            qa, qb = _split(q_ref[rows, :], lo)
            doa, dob = _split(do_ref[rows, :], lo)
            lse_i, delta_i = lse_ref[i] * LOG2E, delta_ref[i]
            prs, dss = [], []
            for h, (qh, doh) in enumerate(((qa, doa), (qb, dob))):
                pr = jnp.exp2(_dot_nt(k, qh) - lse_i[h:h + 1, :])
                dss.append((pr * (_dot_nt(v, doh) - delta_i[h:h + 1, :])).astype(BF))
                prs.append(pr.astype(BF))
            dv = dv + _dot(jnp.concatenate(prs, axis=1), jnp.concatenate([doa, dob], axis=0))
            dk = dk + _dot(jnp.concatenate(dss, axis=1), jnp.concatenate([qa, qb], axis=0))
            dqt_ref[i] += _dot(kt, jnp.concatenate(dss, axis=0))
            return dk, dv

        z = jnp.zeros((tk, LANES), F32)
        dk, dv = lax.fori_loop(0, nq, step, (z, z))
        dk_ref[...] = dk
        dv_ref[...] = dv

    ospec = pl.BlockSpec((tk, LANES), lambda p, j: (j, p))
    stat = pl.BlockSpec((None, nq, 8, tq), lambda p, j: (p, 0, 0, 0))
    whole = lambda: pl.BlockSpec((S, LANES), lambda p, j: (0, p), pipeline_mode=pl.Buffered(1))
    sd = jax.ShapeDtypeStruct((S, 768), F32)
    dk, dv, dqt = _pcall(body, name="flash_bwd", grid=(N_PAIRS, S // tk),
                         in_specs=[pl.BlockSpec((tk, LANES), lambda p, j: (j, K_T0 + p // 2)),
                                   pl.BlockSpec((tk, LANES), lambda p, j: (j, V_T0 + p // 2)), whole(), whole(),
                                   stat, stat],
                         out_specs=[ospec, ospec,
                                    pl.BlockSpec((None, nq, LANES, tq), lambda p, j: (p, 0, 0, 0),
                                                 pipeline_mode=pl.Buffered(1))],
                         out_shape=[sd, sd, jax.ShapeDtypeStruct((N_PAIRS, nq, LANES, tq), F32)])(
        proj, proj, proj, do, lse_t, delta_t)
    dq = jnp.transpose(dqt, (1, 3, 0, 2)).reshape(S, 768)
    return dq, dk, dv


def mem_fwd(proj, mkv, *, tq=512):
    S = proj.shape[0]

    def body(q_ref, km_ref, vm_ref, o_ref, lse_ref):
        lo = _lo((tq, LANES))
        lov = _lo((N_MEM, LANES))
        for t in range(2):
            qa, qb = _split(q_ref[:, _tile(t)], lo)
            km, vm = km_ref[:, _tile(t)], vm_ref[:, _tile(t)]
            ps, dens, lses = [], [], []
            for qh in (qa, qb):
                s = _dot_nt(qh, km)
                m = jnp.max(s, axis=-1, keepdims=True)
                e = jnp.exp(s - m)
                den = jnp.sum(e, axis=-1, keepdims=True)
                ps.append(e.astype(BF))
                dens.append(den)
                lses.append(m + jnp.log(den))
            v_lo, v_hi = _split(vm, lov)
            o = _dot(jnp.concatenate(ps, axis=1), jnp.concatenate([v_lo, v_hi], axis=0))
            o_ref[:, _tile(t)] = (o / jnp.where(lo, dens[0], dens[1])).astype(BF)
            lse_ref[:, _tile(t)] = jnp.where(lo, lses[0], lses[1])

    ospec = pl.BlockSpec((tq, 256), lambda i: (i, 0))
    return _pcall(body, name="mem_fwd", grid=(S // tq,),
                  in_specs=[pl.BlockSpec((tq, 256), lambda i: (i, M_T0 // 2)),
                            pl.BlockSpec((N_MEM, 256), lambda i: (0, 0)), pl.BlockSpec((N_MEM, 256), lambda i: (0, 1))],
                  out_specs=[ospec, ospec],
                  out_shape=[jax.ShapeDtypeStruct((S, 256), BF), jax.ShapeDtypeStruct((S, 256), F32)])(proj, mkv, mkv)


def mem_bwd(proj, mkv, dcat, lse, delta, *, tq=512):
    S = proj.shape[0]

    def body(q_ref, km_ref, vm_ref, do_ref, lse_ref, delta_ref, dq_ref, dkm_ref, dvm_ref):
        @pl.when(pl.program_id(0) == 0)
        def _():
            dkm_ref[...] = jnp.zeros_like(dkm_ref)
            dvm_ref[...] = jnp.zeros_like(dvm_ref)

        lo = _lo((tq, LANES))
        lov = _lo((N_MEM, LANES))
        for t in range(2):
            qa, qb = _split(q_ref[:, _tile(t)], lo)
            doa, dob = _split(do_ref[:, _tile(t)], lo)
            km, vm = km_ref[:, _tile(t)], vm_ref[:, _tile(t)]
            lse_t, delta_t = lse_ref[:, _tile(t)], delta_ref[:, _tile(t)]
            prs, dss = [], []
            for h, (qh, doh) in enumerate(((qa, doa), (qb, dob))):
                pr = jnp.exp(_dot_nt(qh, km) - _col(lse_t, h * HD))
                dss.append(pr * (_dot_nt(doh, vm) - _col(delta_t, h * HD)))
                prs.append(pr)
            k_lo, k_hi = _split(km, lov)
            dq_ref[:, _tile(t)] = _dot(jnp.concatenate(dss, axis=1).astype(BF), jnp.concatenate([k_lo, k_hi], axis=0))
            dvm_ref[:, _tile(t)] += _dot(jnp.concatenate(prs, axis=0).T.astype(BF), jnp.concatenate([doa, dob], axis=0))
            dkm_ref[:, _tile(t)] += _dot(jnp.concatenate(dss, axis=0).T.astype(BF), jnp.concatenate([qa, qb], axis=0))

    ospec = pl.BlockSpec((tq, 256), lambda i: (i, 0))
    msp = pl.BlockSpec((N_MEM, 256), lambda i: (0, 0))
    md = jax.ShapeDtypeStruct((N_MEM, 256), F32)
    dq, dkm, dvm = _pcall(body, name="mem_bwd", grid=(S // tq,),
                          in_specs=[pl.BlockSpec((tq, 256), lambda i: (i, M_T0 // 2)), msp,
                                    pl.BlockSpec((N_MEM, 256), lambda i: (0, 1)),
                                    pl.BlockSpec((tq, 256), lambda i: (i, 3)), ospec,
                                    pl.BlockSpec((tq, 256), lambda i: (i, 3))],
                          out_specs=[ospec, msp, msp], out_shape=[jax.ShapeDtypeStruct((S, 256), F32), md, md])(
        proj, mkv, mkv, dcat, lse, delta)
    return dq, jnp.concatenate([dkm, dvm], axis=1)


def combine_fwd(os_, lses, *, tm=512):
    S = os_[0].shape[0]

    def body(o0, o1, o2, l0, l1, l2, tok_ref):
        ls = [l0[...], l1[...], l2[...]]
        m = jnp.maximum(jnp.maximum(ls[0], ls[1]), ls[2])
        es = [jnp.exp(l - m) for l in ls]
        den = es[0] + es[1] + es[2]
        for g, o in enumerate((o0, o1, o2)):
            tok_ref[:, 256 * g:256 * (g + 1)] = (o[...] * (es[g] / den)).astype(BF)

    sp = pl.BlockSpec((tm, 256), lambda i: (i, 0))
    return _pcall(body, name="combine_fwd", grid=(S // tm,), in_specs=[sp] * 6,
                  out_specs=pl.BlockSpec((tm, 768), lambda i: (i, 0)),
                  out_shape=jax.ShapeDtypeStruct((S, 768), BF))(*os_, *lses)


def combine_bwd(dcat, os_, lses, *, tm=512):
    S = dcat.shape[0]

    def body(dt_ref, o0, o1, o2, l0, l1, l2, do0, do1, do2, de0, de1, de2):
        ls = [l0[...], l1[...], l2[...]]
        m = jnp.maximum(jnp.maximum(ls[0], ls[1]), ls[2])
        es = [jnp.exp(l - m) for l in ls]
        den = es[0] + es[1] + es[2]
        alphas = [e / den for e in es]
        lo = _lo((tm, LANES))
        dts = [dt_ref[:, 256 * g:256 * (g + 1)].astype(F32) for g in range(3)]
        dal = []
        for g, o in enumerate((o0, o1, o2)):
            pr = dts[g] * o[...]
            dal.append(jnp.concatenate([_half_sum(pr[:, _tile(0)], lo), _half_sum(pr[:, _tile(1)], lo)], axis=1))
        mix = alphas[0] * dal[0] + alphas[1] * dal[1] + alphas[2] * dal[2]
        for g, (do_ref, de_ref) in enumerate(((do0, de0), (do1, de1), (do2, de2))):
            do_ref[...] = (dts[g] * alphas[g]).astype(BF)
            de_ref[...] = alphas[g] * mix

    sp = pl.BlockSpec((tm, 256), lambda i: (i, 0))
    outs = _pcall(body, name="combine_bwd", grid=(S // tm,),
                  in_specs=[pl.BlockSpec((tm, 768), lambda i: (i, 0))] + [sp] * 6, out_specs=[sp] * 6,
                  out_shape=[jax.ShapeDtypeStruct((S, 256), BF)] * 3 + [jax.ShapeDtypeStruct((S, 256), F32)] * 3)(
        dcat, *os_, *lses)
    return outs[:3], outs[3:]


def _coords():
    return lax.axis_index("x"), lax.axis_index("y"), lax.axis_index("c")


def _other_chips(x, y):
    return [(1 - x, y), (x, 1 - y), (1 - x, 1 - y)]


HBM_SPEC = pl.BlockSpec(memory_space=pltpu.HBM)


def gather_shards(shards):
    n = len(shards)

    def body(*refs):
        ins, outs = refs[:n], refs[n:2 * n]
        send, recv, lsem = refs[2 * n:]
        x, y, c = _coords()
        me = 2 * x + y
        chips = _other_chips(x, y)
        local, sends = [], []
        for a in range(n):
            cp = pltpu.make_async_copy(ins[a], outs[a].at[me], lsem.at[a])
            cp.start()
            local.append(cp)
            for j, (px, py) in enumerate(chips):
                cp = pltpu.make_async_remote_copy(src_ref=ins[a], dst_ref=outs[a].at[me], send_sem=send.at[3 * a + j],
                                                  recv_sem=recv.at[3 * a + j], device_id=(px, py, c), device_id_type=MESH)
                cp.start()
                sends.append(cp)
        for a in range(n):
            for j, (px, py) in enumerate(chips):
                pltpu.make_async_remote_copy(src_ref=ins[a], dst_ref=outs[a].at[2 * px + py], send_sem=send.at[3 * a + j],
                                             recv_sem=recv.at[3 * a + j], device_id=(px, py, c),
                                             device_id_type=MESH).wait_recv()
        for cp in sends:
            cp.wait_send()
        for cp in local:
            cp.wait()

    return pl.pallas_call(
        body, name="gather_shards", in_specs=[HBM_SPEC] * n, out_specs=[HBM_SPEC] * n,
        out_shape=[jax.ShapeDtypeStruct((4,) + s.shape, s.dtype) for s in shards],
        scratch_shapes=[pltpu.SemaphoreType.DMA((3 * n,)), pltpu.SemaphoreType.DMA((3 * n,)),
                        pltpu.SemaphoreType.DMA((n,))])(*shards)


def scatter_grads(parts):
    n = len(parts)

    def body(*refs):
        ins, outs = refs[:n], refs[n:2 * n]
        send, recv = refs[2 * n:]
        x, y, c = _coords()
        chips = _other_chips(x, y)
        sends = []
        for a in range(n):
            for j, (px, py) in enumerate(chips):
                cp = pltpu.make_async_remote_copy(src_ref=ins[a].at[2 * px + py], dst_ref=outs[a].at[j],
                                                  send_sem=send.at[3 * a + j], recv_sem=recv.at[3 * a + j],
                                                  device_id=(px, py, c), device_id_type=MESH)
                cp.start()
                sends.append(cp)
        for cp in sends:
            cp.wait_recv()
        for cp in sends:
            cp.wait_send()

    return pl.pallas_call(
        body, name="scatter_grads", in_specs=[HBM_SPEC] * n, out_specs=[HBM_SPEC] * n,
        out_shape=[jax.ShapeDtypeStruct((3,) + p.shape[1:], p.dtype) for p in parts],
        scratch_shapes=[pltpu.SemaphoreType.DMA((3 * n,)), pltpu.SemaphoreType.DMA((3 * n,))])(*parts)


def sibling_swap(arrs):
    n = len(arrs)

    def body(*refs):
        ins, outs = refs[:n], refs[n:2 * n]
        send, recv = refs[2 * n:]
        x, y, c = _coords()
        cps = []
        for a in range(n):
            cp = pltpu.make_async_remote_copy(src_ref=ins[a], dst_ref=outs[a], send_sem=send.at[a], recv_sem=recv.at[a],
                                              device_id=(x, y, 1 - c), device_id_type=MESH)
            cp.start()
            cps.append(cp)
        for cp in cps:
            cp.wait_recv()
        for cp in cps:
            cp.wait_send()

    return pl.pallas_call(
        body, name="sibling_swap", in_specs=[HBM_SPEC] * n, out_specs=[HBM_SPEC] * n,
        out_shape=[jax.ShapeDtypeStruct(a.shape, a.dtype) for a in arrs],
        scratch_shapes=[pltpu.SemaphoreType.DMA((n,)), pltpu.SemaphoreType.DMA((n,))])(*arrs)


def allsum_small(v):
    rows = v.shape[0]

    def body(v_ref, tot_ref, gath_ref, send, recv):
        x, y, c = _coords()
        me = 4 * x + 2 * y + c
        gath_ref[me] = v_ref[...]
        cps = []
        for k in range(1, 8):
            fx, fy, fc = (k >> 2) & 1, (k >> 1) & 1, k & 1
            peer = (1 - x if fx else x, 1 - y if fy else y, 1 - c if fc else c)
            cp = pltpu.make_async_remote_copy(src_ref=v_ref, dst_ref=gath_ref.at[me], send_sem=send.at[k - 1],
                                              recv_sem=recv.at[k - 1], device_id=peer, device_id_type=MESH)
            cp.start()
            cps.append(cp)
        for cp in cps:
            cp.wait_recv()
        for cp in cps:
            cp.wait_send()
        tot = gath_ref[0]
        for k in range(1, 8):
            tot = tot + gath_ref[k]
        tot_ref[...] = tot

    vm = pl.BlockSpec(memory_space=pltpu.VMEM)
    tot, _ = pl.pallas_call(
        body, name="allsum_small", in_specs=[vm], out_specs=[vm, vm],
        out_shape=[jax.ShapeDtypeStruct((rows, LANES), F32), jax.ShapeDtypeStruct((8, rows, LANES), F32)],
        scratch_shapes=[pltpu.SemaphoreType.DMA((7,)), pltpu.SemaphoreType.DMA((7,))])(v)
    return tot


def sum_parts(own, recv, *, tr=256):
    R, C = own.shape
    tr = min(tr, R)

    def body(o_ref, r_ref, out_ref):
        out_ref[...] = ((o_ref[...] + r_ref[0].astype(F32)) + r_ref[1].astype(F32)) + r_ref[2].astype(F32)

    sp = pl.BlockSpec((tr, C), lambda i: (i, 0))
    return _pcall(body, name="sum_parts", grid=(R // tr,),
                  in_specs=[sp, pl.BlockSpec((3, tr, C), lambda i: (0, i, 0))], out_specs=sp,
                  out_shape=jax.ShapeDtypeStruct((R, C), F32))(own, recv)


def adamw(w, ga, gb, m, v, *, tr=256):
    R, C = w.shape
    tr = min(tr, R)
    two = gb is not None

    def body(*refs):
        if two:
            w_ref, ga_ref, gb_ref, m_ref, v_ref, g_out, d_out, m_out, v_out = refs
            g = ga_ref[...] + gb_ref[...]
        else:
            w_ref, ga_ref, m_ref, v_ref, g_out, d_out, m_out, v_out = refs
            g = ga_ref[...]
        mn = B1 * m_ref[...] + (1.0 - B1) * g
        vn = B2 * v_ref[...] + (1.0 - B2) * (g * g)
        m_hat = mn / (1.0 - B1 ** STEP)
        v_hat = vn / (1.0 - B2 ** STEP)
        g_out[...] = g
        d_out[...] = -LR * (m_hat / (jnp.sqrt(v_hat) + AEPS) + WD * w_ref[...])
        m_out[...] = mn
        v_out[...] = vn

    sp = pl.BlockSpec((tr, C), lambda i: (i, 0))
    args = [w, ga, gb, m, v] if two else [w, ga, m, v]
    sd = jax.ShapeDtypeStruct((R, C), F32)
    return _pcall(body, name="adamw", grid=(R // tr,), in_specs=[sp] * len(args), out_specs=[sp] * 4,
                  out_shape=[sd] * 4)(*args)


def _rope_tables(S):
    def inv_freq(n_dims, theta):
        return theta ** (-(jnp.arange(0, n_dims, 2, dtype=jnp.float32) / n_dims))

    pos = lax.broadcasted_iota(jnp.int32, (S, LANES), 0)
    d = lax.broadcasted_iota(jnp.int32, (S, LANES), 1) % HD
    d1 = lax.iota(jnp.int32, LANES) % HD
    ang = pos.astype(F32) * inv_freq(HD // 4, ROPE_THETA)[d1 % 8][None, :]
    sin = jnp.sin(ang)
    partial = (jnp.where(d < 16, jnp.cos(ang), 1.0), jnp.where((d >= 8) & (d < 16), sin, 0.0),
               jnp.where(d < 8, -sin, 0.0))
    grid_pos = jnp.where(d < 32, pos // GRID_W, pos % GRID_W)
    ang = grid_pos.astype(F32) * inv_freq(HD // 2, AXIAL_THETA)[d1 % 16][None, :]
    sin = jnp.sin(ang)
    axial = (jnp.cos(ang), jnp.where(d % 32 >= 16, sin, 0.0), jnp.where(d % 32 < 16, -sin, 0.0))
    return partial, axial


def _pad_w_in(w):
    cols = [w[:, :768]]
    for base in (768, 960):
        for g in range(3):
            kg = w[:, base + g * HD:base + (g + 1) * HD]
            cols += [kg, kg]
    cols.append(w[:, 1152:])
    return jnp.concatenate(cols, axis=1)


def _unpad_dw_in(dw):
    cols = [dw[:, :768]]
    for t0 in (K_T0, V_T0):
        for g in range(3):
            b = (t0 + g) * LANES
            cols.append(dw[:, b:b + HD] + dw[:, b + HD:b + LANES])
    cols.append(dw[:, M_T0 * LANES:])
    return jnp.concatenate(cols, axis=1)


def _stats_t(a, tq):
    S = a.shape[0]
    t = a.reshape(S, -1, 2, HD)[:, :N_PAIRS, :, 0]
    t = jnp.transpose(t, (1, 2, 0))
    t = jnp.pad(t, ((0, 0), (0, 6), (0, 0)))
    return jnp.transpose(t.reshape(N_PAIRS, 8, S // tq, tq), (0, 2, 1, 3))


def _fold(t):
    return t[..., :HD] + t[..., HD:]


def kernel(x, mem, mem_norm_g, w_in, w_mem_kv, w_o, g_mix_pre, g_mix_post, attn_sink, qk_norm_g, w_gate_up, w_down, g_ffn_pre, g_ffn_post, loss_target, m_mem_norm_g, m_w_in, m_w_mem_kv, m_w_o, m_g_mix_pre, m_g_mix_post, m_attn_sink, m_qk_norm_g, m_w_gate_up, m_w_down, m_g_ffn_pre, m_g_ffn_post, v_mem_norm_g, v_w_in, v_w_mem_kv, v_w_o, v_g_mix_pre, v_g_mix_post, v_attn_sink, v_qk_norm_g, v_w_gate_up, v_w_down, v_g_ffn_pre, v_g_ffn_post):
    S = x.shape[1]
    depth = w_in.shape[0]
    xs, memx, tgt = x[0], mem[0], loss_target[0]
    tab_p, tab_a = _rope_tables(S)
    row = lambda a: a.reshape(1, -1)

    gi, gm, go, gg, gd = gather_shards([w_in.astype(BF), w_mem_kv.astype(BF), w_o.astype(BF), w_gate_up.astype(BF),
                                        w_down.astype(BF)])
    W_in = jnp.concatenate([gi[s] for s in range(4)], axis=2)
    W_mkv = jnp.concatenate([gm[s] for s in range(4)], axis=1)
    W_o = jnp.concatenate([go[s] for s in range(4)], axis=1)
    W_g = jnp.concatenate([gg[0], gg[1]], axis=2)
    W_u = jnp.concatenate([gg[2], gg[3]], axis=2)
    W_d = jnp.concatenate([gd[s] for s in range(4)], axis=1)
    mem_g = row(mem_norm_g)
    zero_sink = jnp.zeros((12,), F32)
    qkg = jnp.pad(jnp.concatenate([qk_norm_g[0], qk_norm_g[0]], axis=1), ((0, 6), (0, 0)))
    no_qkg = jnp.zeros((8, LANES), F32)

    saved = []
    cur = xs
    for i in range(depth):
        kind = i % 3
        wp = _pad_w_in(W_in[i])
        sv = dict(x=cur, wp=wp)
        if kind == 1:
            h1, proj, raw, nrm = inproj_fwd(cur, row(g_mix_pre[i]), wp, tab_a, qkg, axial=True)
            sv["raw"] = raw
        else:
            h1, proj = inproj_fwd(cur, row(g_mix_pre[i]), wp, tab_p, no_qkg, axial=False)
        if kind == 0:
            tok, lse = banded_fwd(proj, attn_sink[i // 3], d=1, R=A_RADIUS, TQ=2 * A_RADIUS, pair0=0, npairs=6,
                                  use_sink=True, o_dtype=BF)
        elif kind == 1:
            bound = jnp.sqrt(jnp.max(nrm[0]) * jnp.max(nrm[1])) * LN2
            tok, lse = flash_fwd(proj, bound)
        else:
            os_, lses = [], []
            for g, (window, dil) in enumerate(C_GROUPS):
                rad = window // (2 * dil)
                o_g, l_g = banded_fwd(proj, zero_sink, d=dil, R=rad, TQ=2 * rad, pair0=2 * g, npairs=2, use_sink=False,
                                      o_dtype=F32)
                os_.append(o_g)
                lses.append(l_g)
            tok = combine_fwd(os_, lses)
            sv["os"], lse = os_, lses
        mem_n, mkv = norm_mm(memx, mem_g, W_mkv[i], tm=N_MEM)
        mo, mlse = mem_fwd(proj, mkv)
        cat = jnp.concatenate([tok, mo], axis=1)
        o, x2 = mm_norm_res(cat, W_o[i], row(g_mix_post[i]), cur)
        h2, gate, up, act = ffn_up_fwd(x2, row(g_ffn_pre[i]), W_g[i], W_u[i])
        f, x3 = mm_norm_res(act, W_d[i], row(g_ffn_post[i]), x2)
        sv.update(h1=h1, proj=proj, lse=lse, mem_n=mem_n, mkv=mkv, mlse=mlse, cat=cat, o=o, x2=x2, h2=h2, gate=gate,
                  up=up, act=act, f=f)
        saved.append(sv)
        cur = x3

    dcur, loss_vec = loss_bwd(cur, tgt)

    dW_in, dW_mkv, dW_o, dW_gu, dW_d = [None] * depth, [None] * depth, [None] * depth, [None] * depth, [None] * depth
    dg_pre, dg_post, dg_fpre, dg_fpost = [None] * depth, [None] * depth, [None] * depth, [None] * depth
    dg_mem = jnp.zeros((1, D), F32)
    dsinks, dqk = {}, None
    for i in reversed(range(depth)):
        sv = saved[i]
        kind = i % 3
        proj = sv["proj"]
        df, dgate, dup, dg_fpost[i] = normbwd_mm_swiglu(dcur, sv["f"], row(g_ffn_post[i]), W_d[i], sv["gate"], sv["up"])
        dx2, dg_fpre[i] = mm_nt_normbwd_res([(dgate, W_g[i]), (dup, W_u[i])], sv["x2"], row(g_ffn_pre[i]), dcur, nk=2)
        dW_d[i] = mm_acc(sv["act"].T, df, tk=1408, tn=D, ts=1024)
        h2t = sv["h2"].T
        dW_gu[i] = jnp.concatenate([mm_acc(h2t, dgate, tk=D, tn=1408, ts=1024), mm_acc(h2t, dup, tk=D, tn=1408, ts=1024)],
                                   axis=1)
        do, dcat, delta, dg_post[i] = normbwd_mm_cat(dx2, sv["o"], row(g_mix_post[i]), W_o[i], sv["cat"])
        dW_o[i] = mm_acc(sv["cat"].T, do, tk=D, tn=D, ts=1024)
        dqm, dmkv = mem_bwd(proj, sv["mkv"], dcat, sv["mlse"], delta)
        dmkv = dmkv.astype(BF)
        (dgm,) = mm_nt_normbwd_res([(dmkv, W_mkv[i])], memx, mem_g, None, tm=N_MEM)
        dg_mem = dg_mem + dgm
        dW_mkv[i] = mm_acc(sv["mem_n"].T, dmkv, tk=D, tn=512, ts=N_MEM)
        if kind == 0:
            sink = attn_sink[i // 3]
            args = dict(d=1, R=A_RADIUS, pair0=0, npairs=6)
            dq, dsk = banded_bwd_dq(proj, dcat, sv["lse"], delta, sink, TQ=2 * A_RADIUS, use_sink=True, **args)
            dkp, dvp = banded_bwd_dkv(proj, dcat, sv["lse"], delta, TK=2 * A_RADIUS, **args)
            dsinks[i // 3] = dsk.reshape(6, 8, 2, HD)[:, 0, :, 0].reshape(12)
        elif kind == 1:
            dq, dkp, dvp = flash_bwd(proj, dcat, _stats_t(sv["lse"], 512), _stats_t(delta, 512))
        else:
            dos, des = combine_bwd(dcat, sv["os"], sv["lse"])
            dqs, dks, dvs = [], [], []
            for g, (window, dil) in enumerate(C_GROUPS):
                rad = window // (2 * dil)
                args = dict(d=dil, R=rad, pair0=2 * g, npairs=2)
                dq_g, _ = banded_bwd_dq(proj, dos[g], sv["lse"][g], des[g], zero_sink, TQ=2 * rad, use_sink=False, **args)
                dk_g, dv_g = banded_bwd_dkv(proj, dos[g], sv["lse"][g], des[g], TK=2 * rad, **args)
                dqs.append(dq_g)
                dks.append(dk_g)
                dvs.append(dv_g)
            dq, dkp, dvp = (jnp.concatenate(t, axis=1) for t in (dqs, dks, dvs))
        if kind == 1:
            dcur, dproj, dg_pre[i], dqk_t = inproj_bwd(dq, dkp, dvp, dqm, tab_a, sv["raw"], qkg, sv["wp"], sv["x"],
                                                       row(g_mix_pre[i]), dx2, axial=True)
            dqk = _fold(dqk_t[:2]).reshape(1, 2, HD)
        else:
            dcur, dproj, dg_pre[i], _ = inproj_bwd(dq, dkp, dvp, dqm, tab_p, proj, no_qkg, sv["wp"],
                                                   sv["x"], row(g_mix_pre[i]), dx2, axial=False)
        dW_in[i] = _unpad_dw_in(mm_acc(sv["h1"].T, dproj, tk=D, tn=896, ts=1024))

    x_i, y_i, _ = _coords()
    me = 2 * x_i + y_i
    big = [
        (jnp.transpose(jnp.stack(dW_in).reshape(depth, D, 4, IN_W // 4), (2, 0, 1, 3)), w_in, m_w_in, v_w_in),
        (jnp.transpose(jnp.stack(dW_mkv).reshape(depth, 4, D // 4, 512), (1, 0, 2, 3)), w_mem_kv, m_w_mem_kv, v_w_mem_kv),
        (jnp.transpose(jnp.stack(dW_o).reshape(depth, 4, D // 4, D), (1, 0, 2, 3)), w_o, m_w_o, v_w_o),
        (jnp.transpose(jnp.stack(dW_gu).reshape(depth, D, 4, 2 * DFF // 4), (2, 0, 1, 3)), w_gate_up, m_w_gate_up,
         v_w_gate_up),
        (jnp.transpose(jnp.stack(dW_d).reshape(depth, 4, DFF // 4, D), (1, 0, 2, 3)), w_down, m_w_down, v_w_down),
    ]
    recvs = scatter_grads([b[0].astype(BF) for b in big])
    parts = []
    for (g4, w, _, _), rc in zip(big, recvs):
        C = w.shape[-1]
        own = lax.dynamic_index_in_dim(g4, me, 0, keepdims=False).reshape(-1, C)
        parts.append(sum_parts(own, rc.reshape(3, -1, C)))
    sibs = sibling_swap(parts)
    big_out = []
    for (_, w, m, v), pa, pb in zip(big, parts, sibs):
        C = w.shape[-1]
        outs = adamw(w.reshape(-1, C), pa, pb, m.reshape(-1, C), v.reshape(-1, C))
        big_out.append([o.reshape(w.shape) for o in outs])

    small_w = [mem_norm_g, g_mix_pre, g_mix_post, attn_sink, qk_norm_g, g_ffn_pre, g_ffn_post]
    small_m = [m_mem_norm_g, m_g_mix_pre, m_g_mix_post, m_attn_sink, m_qk_norm_g, m_g_ffn_pre, m_g_ffn_post]
    small_v = [v_mem_norm_g, v_g_mix_pre, v_g_mix_post, v_attn_sink, v_qk_norm_g, v_g_ffn_pre, v_g_ffn_post]
    small_g = [dg_mem.reshape(D), jnp.concatenate(dg_pre, axis=0), jnp.concatenate(dg_post, axis=0),
               jnp.stack([dsinks[k] for k in sorted(dsinks)]), dqk, jnp.concatenate(dg_fpre, axis=0),
               jnp.concatenate(dg_fpost, axis=0)]
    sizes = [a.size for a in small_w]
    total = sum(sizes)
    rows_s = -(-(total + LANES) // (8 * LANES)) * 8

    def pack(arrs, extra=None):
        flat = jnp.concatenate([a.reshape(-1).astype(F32) for a in arrs])
        flat = jnp.pad(flat, (0, rows_s * LANES - LANES - total))
        tail = jnp.zeros((LANES,), F32) if extra is None else extra.reshape(LANES)
        return jnp.concatenate([flat, tail]).reshape(rows_s, LANES)

    tot = allsum_small(pack(small_g, loss_vec))
    loss = jnp.sum(tot[rows_s - 1])
    s_out = adamw(pack(small_w), tot, None, pack(small_m), pack(small_v))

    def unpack(buf):
        flat = buf.reshape(-1)
        out, off = [], 0
        for a, n in zip(small_w, sizes):
            out.append(flat[off:off + n].reshape(a.shape))
            off += n
        return out

    sg, sd_, sm, sv_ = (unpack(b) for b in s_out)

    def ordered(k):
        sm_ = (sg, sd_, sm, sv_)[k]
        b = [bo[k] for bo in big_out]
        return [sm_[0], b[0], b[1], b[2], sm_[1], sm_[2], sm_[3], sm_[4], b[3], b[4], sm_[5], sm_[6]]

    dx_out = dcur.reshape(1, S, D)
    return (loss, dx_out, *ordered(0), *ordered(1), *ordered(2), *ordered(3))
```

```python
import functools

import jax
import jax.numpy as jnp
from jax import lax
from jax.experimental import pallas as pl
from jax.experimental.pallas import tpu as pltpu

F32 = jnp.float32
BF = jnp.bfloat16

D = 1024
HD = 64
LANES = 128
N_PAIRS = 6
DFF = 2816
IN_W = 1408
PW = 14 * LANES
K_T0, V_T0, M_T0 = 6, 9, 12
EPS = 1e-6
SCALE = HD ** -0.5
NEG = -1e30
LOG2E = 1.4426950408889634
LN2 = 0.6931471805599453
MAX_PLAIN_SCORE = 40.0
ROPE_THETA = 500000.0
AXIAL_THETA = 10000.0
GRID_W = 64
A_RADIUS = 128
C_GROUPS = ((128, 1), (512, 4), (2048, 16))
N_MEM = 256
LR, B1, B2, AEPS, WD, STEP = 0.001, 0.9, 0.999, 1e-08, 0.01, 10
VMEM_LIMIT = 56 * 1024 * 1024
MESH = pl.DeviceIdType.MESH


def _pcall(body, *, name, grid, in_specs, out_specs, out_shape, scratch=()):
    return pl.pallas_call(
        body, name=name, grid=grid, in_specs=in_specs, out_specs=out_specs, out_shape=out_shape,
        scratch_shapes=scratch,
        compiler_params=pltpu.CompilerParams(dimension_semantics=("arbitrary",) * len(grid),
                                             vmem_limit_bytes=VMEM_LIMIT))


def _dot(a, b):
    return lax.dot_general(a, b, (((1,), (0,)), ((), ())), preferred_element_type=F32)


def _dot_nt(a, b):
    return lax.dot_general(a, b, (((1,), (1,)), ((), ())), preferred_element_type=F32)


def _lo(shape):
    return lax.broadcasted_iota(jnp.int32, shape, len(shape) - 1) < HD


def _half_sum(x, lo):
    a = jnp.sum(jnp.where(lo, x, 0.0), axis=-1, keepdims=True)
    b = jnp.sum(jnp.where(lo, 0.0, x), axis=-1, keepdims=True)
    return jnp.where(lo, a, b)


def _col(tile, lane):
    idx = lax.broadcasted_iota(jnp.int32, tile.shape, 1)
    return jnp.sum(jnp.where(idx == lane, tile, 0.0), axis=-1, keepdims=True)


def _split(t, lo):
    z = jnp.zeros_like(t)
    return jnp.where(lo, t, z), jnp.where(lo, z, t)


def _rms(xf, g):
    r = lax.rsqrt(jnp.mean(xf * xf, axis=-1, keepdims=True) + EPS)
    return xf * r * g


def _rms_bwd(xf, g, dy):
    r = lax.rsqrt(jnp.mean(xf * xf, axis=-1, keepdims=True) + EPS)
    xr = xf * r
    dg = jnp.sum(dy * xr, axis=0, keepdims=True)
    t = dy * g
    return r * (t - xr * jnp.mean(xr * t, axis=-1, keepdims=True)), dg


def _rope_fwd(y, c, s1, s2, sh):
    return y * c + pltpu.roll(y, sh, 1) * s1 + pltpu.roll(y, LANES - sh, 1) * s2


def _rope_bwd(dy, c, s1, s2, sh):
    return dy * c + pltpu.roll(dy * s1, LANES - sh, 1) + pltpu.roll(dy * s2, sh, 1)


def _tile(t):
    return slice(t * LANES, (t + 1) * LANES)


def inproj_fwd(x, g, w_pad, tabs, qkg, *, axial, tm=512):
    S = x.shape[0]
    sh = 16 if axial else 8

    def body(x_ref, g_ref, w_ref, c_ref, s1_ref, s2_ref, qkg_ref, h_ref, p_ref, *extra):
        h = _rms(x_ref[...], g_ref[...]).astype(BF)
        h_ref[...] = h
        acc = _dot(h, w_ref[...])
        c, s1, s2 = c_ref[...], s1_ref[...], s2_ref[...]
        lo = _lo((tm, LANES))
        if axial:
            raw_ref, nrm_ref = extra

            @pl.when(pl.program_id(0) == 0)
            def _():
                nrm_ref[...] = jnp.zeros_like(nrm_ref)

        for t in range(14):
            y = acc[:, _tile(t)]
            if t < V_T0:
                if axial:
                    raw_ref[:, _tile(t)] = y.astype(BF)
                    gt = qkg_ref[0:1, :] if t < K_T0 else qkg_ref[1:2, :]
                    y = y * lax.rsqrt(_half_sum(y * y, lo) * (1.0 / HD) + EPS) * gt
                y = _rope_fwd(y, c, s1, s2, sh)
            if t < K_T0:
                y = y * (SCALE * LOG2E if axial else SCALE)
            elif t >= M_T0:
                y = y * SCALE
            yb = y.astype(BF)
            p_ref[:, _tile(t)] = yb
            if axial and t < V_T0:
                yf = yb.astype(F32)
                n2 = jnp.max(_half_sum(yf * yf, lo), axis=0, keepdims=True)
                r = 0 if t < K_T0 else 1
                nrm_ref[r:r + 1, :] = jnp.maximum(nrm_ref[r:r + 1, :], n2)

    row = lambda w: pl.BlockSpec((tm, w), lambda i: (i, 0))
    full = lambda a: pl.BlockSpec(a.shape, lambda i: (0, 0))
    out_shape = [jax.ShapeDtypeStruct((S, D), BF), jax.ShapeDtypeStruct((S, PW), BF)]
    out_specs = [row(D), row(PW)]
    if axial:
        out_shape += [jax.ShapeDtypeStruct((S, V_T0 * LANES), BF), jax.ShapeDtypeStruct((8, LANES), F32)]
        out_specs += [row(V_T0 * LANES), pl.BlockSpec((8, LANES), lambda i: (0, 0))]
    return _pcall(body, name="inproj_fwd_axial" if axial else "inproj_fwd", grid=(S // tm,),
                  in_specs=[row(D), full(g), full(w_pad), row(LANES), row(LANES), row(LANES), full(qkg)],
                  out_specs=out_specs, out_shape=out_shape)(x, g, w_pad, *tabs, qkg)


def norm_mm(x, g, w, *, tm):
    S, N = x.shape[0], w.shape[1]

    def body(x_ref, g_ref, w_ref, h_ref, y_ref):
        h = _rms(x_ref[...], g_ref[...]).astype(BF)
        h_ref[...] = h
        y_ref[...] = _dot(h, w_ref[...]).astype(BF)

    return _pcall(body, name="norm_mm", grid=(S // tm,),
                  in_specs=[pl.BlockSpec((tm, D), lambda i: (i, 0)), pl.BlockSpec(g.shape, lambda i: (0, 0)),
                            pl.BlockSpec(w.shape, lambda i: (0, 0))],
                  out_specs=[pl.BlockSpec((tm, D), lambda i: (i, 0)), pl.BlockSpec((tm, N), lambda i: (i, 0))],
                  out_shape=[jax.ShapeDtypeStruct((S, D), BF), jax.ShapeDtypeStruct((S, N), BF)])(x, g, w)


def ffn_up_fwd(x, g, wg, wu, *, tm=512, tn=1408):
    S = x.shape[0]

    def body(x_ref, g_ref, wg_ref, wu_ref, h_ref, gate_ref, up_ref, a_ref, h_scr):
        @pl.when(pl.program_id(1) == 0)
        def _():
            h = _rms(x_ref[...], g_ref[...]).astype(BF)
            h_scr[...] = h
            h_ref[...] = h

        h = h_scr[...]
        gate = _dot(h, wg_ref[...])
        up = _dot(h, wu_ref[...])
        gate_ref[...] = gate.astype(BF)
        up_ref[...] = up.astype(BF)
        a_ref[...] = (gate * (1.0 / (1.0 + jnp.exp(-gate))) * up).astype(BF)

    rowd = pl.BlockSpec((tm, D), lambda i, j: (i, 0))
    wsp = pl.BlockSpec((D, tn), lambda i, j: (0, j))
    osp = pl.BlockSpec((tm, tn), lambda i, j: (i, j))
    sd = jax.ShapeDtypeStruct((S, DFF), BF)
    return _pcall(body, name="ffn_up_fwd", grid=(S // tm, DFF // tn),
                  in_specs=[rowd, pl.BlockSpec(g.shape, lambda i, j: (0, 0)), wsp, wsp],
                  out_specs=[rowd, osp, osp, osp],
                  out_shape=[jax.ShapeDtypeStruct((S, D), BF), sd, sd, sd],
                  scratch=[pltpu.VMEM((tm, D), BF)])(x, g, wg, wu)


def mm_norm_res(a, w, g, res, *, tm=512):
    S, K = a.shape

    def body(a_ref, w_ref, g_ref, res_ref, y_ref, o_ref):
        y = _dot(a_ref[...], w_ref[...])
        y_ref[...] = y
        o_ref[...] = res_ref[...] + _rms(y, g_ref[...])

    rowd = pl.BlockSpec((tm, D), lambda i: (i, 0))
    sd = jax.ShapeDtypeStruct((S, D), F32)
    return _pcall(body, name="mm_norm_res", grid=(S // tm,),
                  in_specs=[pl.BlockSpec((tm, K), lambda i: (i, 0)), pl.BlockSpec(w.shape, lambda i: (0, 0)),
                            pl.BlockSpec(g.shape, lambda i: (0, 0)), rowd],
                  out_specs=[rowd, rowd], out_shape=[sd, sd])(a, w, g, res)


def loss_bwd(y, tgt, *, tm=512):
    S = y.shape[0]

    def body(y_ref, t_ref, dy_ref, l_ref):
        @pl.when(pl.program_id(0) == 0)
        def _():
            l_ref[...] = jnp.zeros_like(l_ref)

        e = y_ref[...] - t_ref[...]
        dy_ref[...] = e * (1.0 / D)
        col = jnp.sum(e * e, axis=0, keepdims=True)
        part = col[:, _tile(0)]
        for t in range(1, D // LANES):
            part = part + col[:, _tile(t)]
        l_ref[...] += part * (0.5 / D)

    rowd = pl.BlockSpec((tm, D), lambda i: (i, 0))
    return _pcall(body, name="loss_bwd", grid=(S // tm,), in_specs=[rowd, rowd],
                  out_specs=[rowd, pl.BlockSpec((1, LANES), lambda i: (0, 0))],
                  out_shape=[jax.ShapeDtypeStruct((S, D), F32), jax.ShapeDtypeStruct((1, LANES), F32)])(y, tgt)


def normbwd_mm_cat(dy, ysaved, g, w, cat, *, tm=512):
    S = dy.shape[0]

    def body(dy_ref, y_ref, g_ref, w_ref, cat_ref, d_ref, dcat_ref, delta_ref, dg_ref):
        @pl.when(pl.program_id(0) == 0)
        def _():
            dg_ref[...] = jnp.zeros_like(dg_ref)

        d, dg = _rms_bwd(y_ref[...], g_ref[...], dy_ref[...])
        dg_ref[...] += dg
        d = d.astype(BF)
        d_ref[...] = d
        z = _dot_nt(d, w_ref[...])
        dcat_ref[...] = z.astype(BF)
        lo = _lo((tm, LANES))
        for t in range(D // LANES):
            delta_ref[:, _tile(t)] = _half_sum(z[:, _tile(t)] * cat_ref[:, _tile(t)].astype(F32), lo)

    rowd = pl.BlockSpec((tm, D), lambda i: (i, 0))
    return _pcall(body, name="normbwd_mm_cat", grid=(S // tm,),
                  in_specs=[rowd, rowd, pl.BlockSpec(g.shape, lambda i: (0, 0)),
                            pl.BlockSpec(w.shape, lambda i: (0, 0)), rowd],
                  out_specs=[rowd, rowd, rowd, pl.BlockSpec((1, D), lambda i: (0, 0))],
                  out_shape=[jax.ShapeDtypeStruct((S, D), BF), jax.ShapeDtypeStruct((S, D), BF),
                             jax.ShapeDtypeStruct((S, D), F32), jax.ShapeDtypeStruct((1, D), F32)])(dy, ysaved, g, w, cat)


def normbwd_mm_swiglu(dy, fsaved, g, wd, gate, up, *, tm=512, tn=1408):
    S = dy.shape[0]

    def body(dy_ref, f_ref, g_ref, w_ref, gate_ref, up_ref, df_ref, dgate_ref, dup_ref, dg_ref, d_scr):
        i, j = pl.program_id(0), pl.program_id(1)

        @pl.when((i == 0) & (j == 0))
        def _():
            dg_ref[...] = jnp.zeros_like(dg_ref)

        @pl.when(j == 0)
        def _():
            d, dg = _rms_bwd(f_ref[...], g_ref[...], dy_ref[...])
            dg_ref[...] += dg
            d_scr[...] = d.astype(BF)
            df_ref[...] = d.astype(BF)

        da = _dot_nt(d_scr[...], w_ref[...])
        gt = gate_ref[...].astype(F32)
        sig = 1.0 / (1.0 + jnp.exp(-gt))
        dgate_ref[...] = (da * up_ref[...].astype(F32) * (sig * (1.0 + gt * (1.0 - sig)))).astype(BF)
        dup_ref[...] = (da * (gt * sig)).astype(BF)

    rowd = pl.BlockSpec((tm, D), lambda i, j: (i, 0))
    osp = pl.BlockSpec((tm, tn), lambda i, j: (i, j))
    sd = jax.ShapeDtypeStruct((S, DFF), BF)
    return _pcall(body, name="normbwd_mm_swiglu", grid=(S // tm, DFF // tn),
                  in_specs=[rowd, rowd, pl.BlockSpec(g.shape, lambda i, j: (0, 0)),
                            pl.BlockSpec((tn, D), lambda i, j: (j, 0)), osp, osp],
                  out_specs=[rowd, osp, osp, pl.BlockSpec((1, D), lambda i, j: (0, 0))],
                  out_shape=[jax.ShapeDtypeStruct((S, D), BF), sd, sd, jax.ShapeDtypeStruct((1, D), F32)],
                  scratch=[pltpu.VMEM((tm, D), BF)])(dy, fsaved, g, wd, gate, up)


def mm_nt_normbwd_res(parts, xin, g, dres, *, tm=512, nk=1):
    S = xin.shape[0]
    npart = len(parts)
    kb = parts[0][0].shape[1] // nk
    has_res = dres is not None

    def body(*refs):
        prefs = refs[:2 * npart]
        x_ref, g_ref = refs[2 * npart:2 * npart + 2]
        rest = refs[2 * npart + 2:]
        if has_res:
            dres_ref, dx_ref, dg_ref, acc = rest
        else:
            dg_ref, acc = rest
        i, k = pl.program_id(0), pl.program_id(1)

        @pl.when((i == 0) & (k == 0))
        def _():
            dg_ref[...] = jnp.zeros_like(dg_ref)

        z = _dot_nt(prefs[0][...], prefs[1][...])
        for p in range(1, npart):
            z = z + _dot_nt(prefs[2 * p][...], prefs[2 * p + 1][...])

        @pl.when(k == 0)
        def _():
            acc[...] = z

        @pl.when(k > 0)
        def _():
            acc[...] += z

        @pl.when(k == nk - 1)
        def _():
            dx, dg = _rms_bwd(x_ref[...], g_ref[...], acc[...])
            dg_ref[...] += dg
            if has_res:
                dx_ref[...] = dres_ref[...] + dx

    rowd = pl.BlockSpec((tm, D), lambda i, k: (i, 0))
    in_specs, args = [], []
    for dy, w in parts:
        in_specs += [pl.BlockSpec((tm, kb), lambda i, k: (i, k)), pl.BlockSpec((D, kb), lambda i, k: (0, k))]
        args += [dy, w]
    in_specs += [rowd, pl.BlockSpec(g.shape, lambda i, k: (0, 0))]
    args += [xin, g]
    out_specs = [pl.BlockSpec((1, D), lambda i, k: (0, 0))]
    out_shape = [jax.ShapeDtypeStruct((1, D), F32)]
    if has_res:
        in_specs.append(rowd)
        args.append(dres)
        out_specs.insert(0, rowd)
        out_shape.insert(0, jax.ShapeDtypeStruct((S, D), F32))
    return _pcall(body, name="mm_nt_normbwd_res" if has_res else "mm_nt_normbwd", grid=(S // tm, nk),
                  in_specs=in_specs, out_specs=out_specs, out_shape=out_shape,
                  scratch=[pltpu.VMEM((tm, D), F32)])(*args)


def inproj_bwd(dq, dkp, dvp, dqm, tabs, raw, qkg, w_pad, xin, g, dres, *, axial, tm=512):
    S = xin.shape[0]
    sh = 16 if axial else 8

    def body(dq_ref, dk_ref, dv_ref, dm_ref, c_ref, s1_ref, s2_ref, raw_ref, qkg_ref, w_ref, x_ref, g_ref, dres_ref,
             dx_ref, dp_ref, dg_ref, dqk_ref):
        @pl.when(pl.program_id(0) == 0)
        def _():
            dg_ref[...] = jnp.zeros_like(dg_ref)
            dqk_ref[...] = jnp.zeros_like(dqk_ref)

        c, s1, s2 = c_ref[...], s1_ref[...], s2_ref[...]
        lo = _lo((tm, LANES))
        for t in range(14):
            if t < K_T0:
                y = dq_ref[:, _tile(t)] * SCALE
            elif t < V_T0:
                y = dk_ref[:, _tile(2 * (t - K_T0))] + dk_ref[:, _tile(2 * (t - K_T0) + 1)]
                if axial:
                    y = y * LN2
            elif t < M_T0:
                y = dv_ref[:, _tile(2 * (t - V_T0))] + dv_ref[:, _tile(2 * (t - V_T0) + 1)]
            else:
                y = dm_ref[:, _tile(t - M_T0)] * SCALE
            if t < V_T0:
                y = _rope_bwd(y, c, s1, s2, sh)
                if axial:
                    row = 0 if t < K_T0 else 1
                    xr = raw_ref[:, _tile(t)].astype(F32)
                    r = lax.rsqrt(_half_sum(xr * xr, lo) * (1.0 / HD) + EPS)
                    xn = xr * r
                    dqk_ref[row:row + 1, :] += jnp.sum(y * xn, axis=0, keepdims=True)
                    tt = y * qkg_ref[row:row + 1, :]
                    y = r * (tt - xn * (_half_sum(xn * tt, lo) * (1.0 / HD)))
            dp_ref[:, _tile(t)] = y.astype(BF)
        z = _dot_nt(dp_ref[...], w_ref[...])
        dx, dg = _rms_bwd(x_ref[...], g_ref[...], z)
        dg_ref[...] += dg
        dx_ref[...] = dres_ref[...] + dx

    row = lambda w: pl.BlockSpec((tm, w), lambda i: (i, 0))
    full = lambda a: pl.BlockSpec(a.shape, lambda i: (0, 0))
    return _pcall(body, name="inproj_bwd_axial" if axial else "inproj_bwd", grid=(S // tm,),
                  in_specs=[row(768), row(768), row(768), row(256), row(LANES), row(LANES), row(LANES),
                            row(raw.shape[1] if axial else LANES), full(qkg), full(w_pad), row(D), full(g), row(D)],
                  out_specs=[row(D), row(PW), pl.BlockSpec((1, D), lambda i: (0, 0)),
                             pl.BlockSpec((8, LANES), lambda i: (0, 0))],
                  out_shape=[jax.ShapeDtypeStruct((S, D), F32), jax.ShapeDtypeStruct((S, PW), BF),
                             jax.ShapeDtypeStruct((1, D), F32), jax.ShapeDtypeStruct((8, LANES), F32)])(
        dq, dkp, dvp, dqm, *tabs, raw, qkg, w_pad, xin, g, dres)


def mm_acc(at, b, *, tk, tn, ts):
    K, S = at.shape
    N = b.shape[1]
    ts = min(ts, S)

    def body(a_ref, b_ref, o_ref):
        z = _dot(a_ref[...], b_ref[...])

        @pl.when(pl.program_id(2) == 0)
        def _():
            o_ref[...] = z

        @pl.when(pl.program_id(2) > 0)
        def _():
            o_ref[...] += z

    return _pcall(body, name="mm_acc", grid=(K // tk, N // tn, S // ts),
                  in_specs=[pl.BlockSpec((tk, ts), lambda k, n, s: (k, s)), pl.BlockSpec((ts, tn), lambda k, n, s: (s, n))],
                  out_specs=pl.BlockSpec((tk, tn), lambda k, n, s: (k, n)),
                  out_shape=jax.ShapeDtypeStruct((K, N), F32))(at, b)


def _band_specs(L, d, R, T, width, bw, col_of):
    n = T // R
    nb = width // bw
    last = L // R - 1
    col = lambda g, r: r * nb + col_of(g)
    return [pl.BlockSpec((R, bw), lambda g, r, i: (jnp.maximum(i * n - 1, 0), col(g, r))),
            pl.BlockSpec((T, bw), lambda g, r, i: (i, col(g, r))),
            pl.BlockSpec((R, bw), lambda g, r, i: (jnp.minimum((i + 1) * n, last), col(g, r)))]


def _band_bias(T, R):
    w = lax.broadcasted_iota(jnp.int32, (T, T + 2 * R), 1)
    c = lax.broadcasted_iota(jnp.int32, (T, T + 2 * R), 0)
    return jnp.where(jnp.abs(w - R - c) <= R, 0.0, NEG).astype(F32)


def _edge_bias(i, T, R, L):
    wpos = i * T - R + lax.broadcasted_iota(jnp.int32, (1, T + 2 * R), 1)
    return jnp.where((wpos >= 0) & (wpos < L), 0.0, NEG)


def banded_fwd(proj, sink, *, d, R, TQ, pair0, npairs, use_sink, o_dtype):
    S = proj.shape[0]
    L = S // d
    pv = proj.reshape(L, d * PW)
    ow = npairs * LANES

    def body(sink_ref, bias_ref, q_ref, kp, kc, kn, vp, vc, vn, o_ref, lse_ref):
        g, i = pl.program_id(0), pl.program_id(2)
        bias = bias_ref[...] + _edge_bias(i, TQ, R, L)
        lo = _lo((TQ, LANES))
        kw = jnp.concatenate([kp[...], kc[...], kn[...]], axis=0)
        vw = jnp.concatenate([vp[...], vc[...], vn[...]], axis=0)
        v_lo, v_hi = _split(vw, _lo(vw.shape))
        vcat = jnp.concatenate([v_lo, v_hi], axis=0)
        for t in range(2):
            qa, qb = _split(q_ref[:, _tile(t)], lo)
            ps, dens, lses = [], [], []
            for h, qh in enumerate((qa, qb)):
                s = _dot_nt(qh, kw) + bias
                m = jnp.max(s, axis=-1, keepdims=True)
                if use_sink:
                    sk = sink_ref[2 * (pair0 + 2 * g + t) + h]
                    m = jnp.maximum(m, sk)
                e = jnp.exp(s - m)
                den = jnp.sum(e, axis=-1, keepdims=True)
                if use_sink:
                    den = den + jnp.exp(sk - m)
                ps.append(e.astype(BF))
                dens.append(den)
                lses.append(m + jnp.log(den))
            o = _dot(jnp.concatenate(ps, axis=1), vcat)
            o_ref[:, _tile(t)] = (o / jnp.where(lo, dens[0], dens[1])).astype(o_dtype)
            lse_ref[:, _tile(t)] = jnp.where(lo, lses[0], lses[1])

    g0 = pair0 // 2
    qspec = pl.BlockSpec((TQ, 2 * LANES), lambda g, r, i: (i, r * 7 + g0 + g))
    kspecs = _band_specs(L, d, R, TQ, PW, LANES, lambda g: K_T0 + g0 + g)
    vspecs = _band_specs(L, d, R, TQ, PW, LANES, lambda g: V_T0 + g0 + g)
    ospec = pl.BlockSpec((TQ, 2 * LANES), lambda g, r, i: (i, r * (npairs // 2) + g))
    bias = _band_bias(TQ, R)
    o, lse = _pcall(body, name="banded_fwd", grid=(npairs // 2, d, L // TQ),
                    in_specs=[pl.BlockSpec(memory_space=pltpu.SMEM), pl.BlockSpec(bias.shape, lambda g, r, i: (0, 0)),
                              qspec] + kspecs + vspecs,
                    out_specs=[ospec, ospec],
                    out_shape=[jax.ShapeDtypeStruct((L, d * ow), o_dtype), jax.ShapeDtypeStruct((L, d * ow), F32)])(
        sink, bias, pv, pv, pv, pv, pv, pv, pv)
    return o.reshape(S, ow), lse.reshape(S, ow)


def banded_bwd_dq(proj, do, lse, delta, sink, *, d, R, TQ, pair0, npairs, use_sink):
    S = proj.shape[0]
    L = S // d
    pv = proj.reshape(L, d * PW)
    ow = npairs * LANES

    def body(sink_ref, bias_ref, q_ref, kp, kc, kn, vp, vc, vn, do_ref, lse_ref, delta_ref, dq_ref, dsink_ref):
        g, r, i = pl.program_id(0), pl.program_id(1), pl.program_id(2)

        @pl.when((r == 0) & (i == 0))
        def _():
            dsink_ref[...] = jnp.zeros_like(dsink_ref)

        bias = bias_ref[...] + _edge_bias(i, TQ, R, L)
        lo = _lo((TQ, LANES))
        kw = jnp.concatenate([kp[...], kc[...], kn[...]], axis=0)
        vw = jnp.concatenate([vp[...], vc[...], vn[...]], axis=0)
        k_lo, k_hi = _split(kw, _lo(kw.shape))
        kcat = jnp.concatenate([k_lo, k_hi], axis=0)
        for t in range(2):
            qa, qb = _split(q_ref[:, _tile(t)], lo)
            doa, dob = _split(do_ref[:, _tile(t)], lo)
            lse_t, delta_t = lse_ref[:, _tile(t)], delta_ref[:, _tile(t)]
            dss, dsk = [], []
            for h, (qh, doh) in enumerate(((qa, doa), (qb, dob))):
                lse_h, delta_h = _col(lse_t, h * HD), _col(delta_t, h * HD)
                pr = jnp.exp(_dot_nt(qh, kw) + bias - lse_h)
                dss.append((pr * (_dot_nt(doh, vw) - delta_h)).astype(BF))
                if use_sink:
                    psink = jnp.exp(sink_ref[2 * (pair0 + 2 * g + t) + h] - lse_h)
                    dsk.append(-jnp.sum(psink * delta_h, axis=0, keepdims=True))
            dq_ref[:, _tile(t)] = _dot(jnp.concatenate(dss, axis=1), kcat)
            if use_sink:
                dsink_ref[:, _tile(t)] += jnp.where(_lo((8, LANES)), dsk[0], dsk[1])

    g0 = pair0 // 2
    qspec = pl.BlockSpec((TQ, 2 * LANES), lambda g, r, i: (i, r * 7 + g0 + g))
    kspecs = _band_specs(L, d, R, TQ, PW, LANES, lambda g: K_T0 + g0 + g)
    vspecs = _band_specs(L, d, R, TQ, PW, LANES, lambda g: V_T0 + g0 + g)
    ospec = pl.BlockSpec((TQ, 2 * LANES), lambda g, r, i: (i, r * (npairs // 2) + g))
    view = lambda a: a.reshape(L, d * a.shape[1])
    ispec = lambda a: pl.BlockSpec((TQ, 2 * LANES), lambda g, r, i: (i, r * (a.shape[1] // (2 * LANES)) + g))
    bias = _band_bias(TQ, R)
    dq, dsink = _pcall(body, name="banded_bwd_dq", grid=(npairs // 2, d, L // TQ),
                       in_specs=[pl.BlockSpec(memory_space=pltpu.SMEM), pl.BlockSpec(bias.shape, lambda g, r, i: (0, 0)),
                                 qspec] + kspecs + vspecs + [ispec(do), ispec(lse), ispec(delta)],
                       out_specs=[ospec, pl.BlockSpec((8, 2 * LANES), lambda g, r, i: (g, 0))],
                       out_shape=[jax.ShapeDtypeStruct((L, d * ow), F32),
                                  jax.ShapeDtypeStruct((npairs // 2 * 8, 2 * LANES), F32)])(
        sink, bias, pv, pv, pv, pv, pv, pv, pv, view(do), view(lse), view(delta))
    return dq.reshape(S, ow), dsink


def banded_bwd_dkv(proj, do, lse, delta, *, d, R, TK, pair0, npairs):
    S = proj.shape[0]
    L = S // d
    pv = proj.reshape(L, d * PW)
    ow = npairs * LANES

    def body(bias_ref, k_ref, v_ref, qp, qc, qn, dop, doc, don, lp, lc, ln, dp_, dc_, dn_, dk_ref, dv_ref):
        j = pl.program_id(2)
        W = TK + 2 * R
        bias = bias_ref[...] + _edge_bias(j, TK, R, L)
        low = _lo((W, LANES))
        qw = jnp.concatenate([qp[...], qc[...], qn[...]], axis=0)
        dow = jnp.concatenate([dop[...], doc[...], don[...]], axis=0)
        lse_w = jnp.concatenate([lp[...], lc[...], ln[...]], axis=0)
        delta_w = jnp.concatenate([dp_[...], dc_[...], dn_[...]], axis=0)
        k, v = k_ref[...], v_ref[...]
        for t in range(2):
            qa, qb = _split(qw[:, _tile(t)], low)
            doa, dob = _split(dow[:, _tile(t)], low)
            lse_r, delta_r = lse_w[:, _tile(t)].T, delta_w[:, _tile(t)].T
            prs, dss = [], []
            for h, (qh, doh) in enumerate(((qa, doa), (qb, dob))):
                pr = jnp.exp(_dot_nt(k, qh) + bias - lse_r[h * HD:h * HD + 1, :])
                dss.append((pr * (_dot_nt(v, doh) - delta_r[h * HD:h * HD + 1, :])).astype(BF))
                prs.append(pr.astype(BF))
            dv_ref[:, _tile(t)] = _dot(jnp.concatenate(prs, axis=1), jnp.concatenate([doa, dob], axis=0))
            dk_ref[:, _tile(t)] = _dot(jnp.concatenate(dss, axis=1), jnp.concatenate([qa, qb], axis=0))

    g0 = pair0 // 2
    kspec = pl.BlockSpec((TK, LANES), lambda g, r, j: (j, r * 14 + K_T0 + g0 + g))
    vspec = pl.BlockSpec((TK, LANES), lambda g, r, j: (j, r * 14 + V_T0 + g0 + g))
    qspecs = _band_specs(L, d, R, TK, PW, 2 * LANES, lambda g: g0 + g)
    ispecs = lambda a: _band_specs(L, d, R, TK, a.shape[1], 2 * LANES, lambda g: g)
    view = lambda a: a.reshape(L, d * a.shape[1])
    ospec = pl.BlockSpec((TK, 2 * LANES), lambda g, r, j: (j, r * (npairs // 2) + g))
    sd = jax.ShapeDtypeStruct((L, d * ow), F32)
    bias = _band_bias(TK, R)
    dk, dv = _pcall(body, name="banded_bwd_dkv", grid=(npairs // 2, d, L // TK),
                    in_specs=[pl.BlockSpec(bias.shape, lambda g, r, j: (0, 0)), kspec, vspec] + qspecs + ispecs(do)
                    + ispecs(lse) + ispecs(delta),
                    out_specs=[ospec, ospec], out_shape=[sd, sd])(
        bias, pv, pv, pv, pv, pv, *([view(do)] * 3), *([view(lse)] * 3), *([view(delta)] * 3))
    return dk.reshape(S, ow), dv.reshape(S, ow)


def flash_fwd(proj, bound, *, tq=1024, tk=512):
    S = proj.shape[0]

    def body_general(q_ref, k_ref, v_ref, o_ref, lse_ref):
        lo = _lo((tq, LANES))
        qa, qb = _split(q_ref[...], lo)
        lov = _lo((tk, LANES))

        def step(j, carry):
            ma, la, mb, lb, acc = carry
            rows = pl.ds(pl.multiple_of(j * tk, tk), tk)
            k, v = k_ref[rows, :], v_ref[rows, :]
            outs = []
            for qh, m0, l0 in ((qa, ma, la), (qb, mb, lb)):
                s = _dot_nt(qh, k)
                m1 = jnp.maximum(m0, jnp.max(s, axis=-1, keepdims=True))
                al = jnp.exp2(m0 - m1)
                e = jnp.exp2(s - m1)
                outs.append((m1, al * l0 + jnp.sum(e, axis=-1, keepdims=True), al, e.astype(BF)))
            v_lo, v_hi = _split(v, lov)
            pvv = _dot(jnp.concatenate([outs[0][3], outs[1][3]], axis=1), jnp.concatenate([v_lo, v_hi], axis=0))
            acc = acc * jnp.where(lo, outs[0][2], outs[1][2]) + pvv
            return outs[0][0], outs[0][1], outs[1][0], outs[1][1], acc

        m_init = jnp.full((tq, 1), NEG, F32)
        l_init = jnp.zeros((tq, 1), F32)
        ma, la, mb, lb, acc = lax.fori_loop(0, S // tk, step,
                                            (m_init, l_init, m_init, l_init, jnp.zeros((tq, LANES), F32)))
        o_ref[...] = (acc / jnp.where(lo, la, lb)).astype(BF)
        lse_ref[...] = jnp.where(lo, ma * LN2 + jnp.log(la), mb * LN2 + jnp.log(lb))

    def body_plain(q_ref, k_ref, v_ref, o_ref, lse_ref):
        lo = _lo((tq, LANES))
        qa, qb = _split(q_ref[...], lo)
        lov = _lo((tk, LANES))
        one = jnp.ones((tk, LANES), BF)

        def step(j, carry):
            acc_a, acc_b = carry
            rows = pl.ds(pl.multiple_of(j * tk, tk), tk)
            k, v = k_ref[rows, :], v_ref[rows, :]
            ea = jnp.exp2(_dot_nt(qa, k)).astype(BF)
            eb = jnp.exp2(_dot_nt(qb, k)).astype(BF)
            acc_a = acc_a + _dot(ea, jnp.where(lov, v, one))
            acc_b = acc_b + _dot(eb, jnp.where(lov, one, v))
            return acc_a, acc_b

        z = jnp.zeros((tq, LANES), F32)
        acc_a, acc_b = lax.fori_loop(0, S // tk, step, (z, z))
        den = jnp.where(lo, pltpu.roll(acc_a, HD, 1), pltpu.roll(acc_b, HD, 1))
        o_ref[...] = (jnp.where(lo, acc_a, acc_b) / den).astype(BF)
        lse_ref[...] = jnp.log(den)

    def body(bound_ref, q_ref, k_ref, v_ref, o_ref, lse_ref):
        small = bound_ref[0] <= MAX_PLAIN_SCORE

        @pl.when(small)
        def _():
            body_plain(q_ref, k_ref, v_ref, o_ref, lse_ref)

        @pl.when(jnp.logical_not(small))
        def _():
            body_general(q_ref, k_ref, v_ref, o_ref, lse_ref)

    ospec = pl.BlockSpec((tq, LANES), lambda p, i: (i, p))
    return _pcall(body, name="flash_fwd", grid=(N_PAIRS, S // tq),
                  in_specs=[pl.BlockSpec(memory_space=pltpu.SMEM), ospec,
                            pl.BlockSpec((S, LANES), lambda p, i: (0, K_T0 + p // 2)),
                            pl.BlockSpec((S, LANES), lambda p, i: (0, V_T0 + p // 2))],
                  out_specs=[ospec, ospec],
                  out_shape=[jax.ShapeDtypeStruct((S, 768), BF), jax.ShapeDtypeStruct((S, 768), F32)])(
        bound.reshape(1), proj, proj, proj)


def flash_bwd(proj, do, lse_t, delta_t, *, tq=512, tk=1024):
    S = proj.shape[0]
    nq = S // tq

    def body(k_ref, v_ref, q_ref, do_ref, lse_ref, delta_ref, dk_ref, dv_ref, dqt_ref):
        @pl.when(pl.program_id(1) == 0)
        def _():
            dqt_ref[...] = jnp.zeros_like(dqt_ref)

        k, v = k_ref[...], v_ref[...]
        lo = _lo((tq, LANES))
        k_lo, k_hi = _split(k.astype(F32), _lo((tk, LANES)))
        kt = jnp.concatenate([k_lo.T, k_hi.T], axis=1).astype(BF)

        def step(i, carry):
            dk, dv = carry
            rows = pl.ds(pl.multiple_of(i * tq, tq), tq)
            qa, qb = _split(q_ref[rows, :], lo)
            doa, dob = _split(do_ref[rows, :], lo)
            lse_i, delta_i = lse_ref[i] * LOG2E, delta_ref[i]
            prs, dss = [], []
            for h, (qh, doh) in enumerate(((qa, doa), (qb, dob))):
                pr = jnp.exp2(_dot_nt(k, qh) - lse_i[h:h + 1, :])
                dss.append((pr * (_dot_nt(v, doh) - delta_i[h:h + 1, :])).astype(BF))
                prs.append(pr.astype(BF))
            dv = dv + _dot(jnp.concatenate(prs, axis=1), jnp.concatenate([doa, dob], axis=0))
            dk = dk + _dot(jnp.concatenate(dss, axis=1), jnp.concatenate([qa, qb], axis=0))
            dqt_ref[i] += _dot(kt, jnp.concatenate(dss, axis=0))
            return dk, dv

        z = jnp.zeros((tk, LANES), F32)
        dk, dv = lax.fori_loop(0, nq, step, (z, z))
        dk_ref[...] = dk
        dv_ref[...] = dv

    ospec = pl.BlockSpec((tk, LANES), lambda p, j: (j, p))
    stat = pl.BlockSpec((None, nq, 8, tq), lambda p, j: (p, 0, 0, 0))
    whole = lambda: pl.BlockSpec((S, LANES), lambda p, j: (0, p), pipeline_mode=pl.Buffered(1))
    sd = jax.ShapeDtypeStruct((S, 768), F32)
    dk, dv, dqt = _pcall(body, name="flash_bwd", grid=(N_PAIRS, S // tk),
                         in_specs=[pl.BlockSpec((tk, LANES), lambda p, j: (j, K_T0 + p // 2)),
                                   pl.BlockSpec((tk, LANES), lambda p, j: (j, V_T0 + p // 2)), whole(), whole(),
                                   stat, stat],
                         out_specs=[ospec, ospec,
                                    pl.BlockSpec((None, nq, LANES, tq), lambda p, j: (p, 0, 0, 0),
                                                 pipeline_mode=pl.Buffered(1))],
                         out_shape=[sd, sd, jax.ShapeDtypeStruct((N_PAIRS, nq, LANES, tq), F32)])(
        proj, proj, proj, do, lse_t, delta_t)
    dq = jnp.transpose(dqt, (1, 3, 0, 2)).reshape(S, 768)
    return dq, dk, dv


def mem_fwd(proj, mkv, *, tq=512):
    S = proj.shape[0]

    def body(q_ref, km_ref, vm_ref, o_ref, lse_ref):
        lo = _lo((tq, LANES))
        lov = _lo((N_MEM, LANES))
        for t in range(2):
            qa, qb = _split(q_ref[:, _tile(t)], lo)
            km, vm = km_ref[:, _tile(t)], vm_ref[:, _tile(t)]
            ps, dens, lses = [], [], []
            for qh in (qa, qb):
                s = _dot_nt(qh, km)
                m = jnp.max(s, axis=-1, keepdims=True)
                e = jnp.exp(s - m)
                den = jnp.sum(e, axis=-1, keepdims=True)
                ps.append(e.astype(BF))
                dens.append(den)
                lses.append(m + jnp.log(den))
            v_lo, v_hi = _split(vm, lov)
            o = _dot(jnp.concatenate(ps, axis=1), jnp.concatenate([v_lo, v_hi], axis=0))
            o_ref[:, _tile(t)] = (o / jnp.where(lo, dens[0], dens[1])).astype(BF)
            lse_ref[:, _tile(t)] = jnp.where(lo, lses[0], lses[1])

    ospec = pl.BlockSpec((tq, 256), lambda i: (i, 0))
    return _pcall(body, name="mem_fwd", grid=(S // tq,),
                  in_specs=[pl.BlockSpec((tq, 256), lambda i: (i, M_T0 // 2)),
                            pl.BlockSpec((N_MEM, 256), lambda i: (0, 0)), pl.BlockSpec((N_MEM, 256), lambda i: (0, 1))],
                  out_specs=[ospec, ospec],
                  out_shape=[jax.ShapeDtypeStruct((S, 256), BF), jax.ShapeDtypeStruct((S, 256), F32)])(proj, mkv, mkv)


def mem_bwd(proj, mkv, dcat, lse, delta, *, tq=512):
    S = proj.shape[0]

    def body(q_ref, km_ref, vm_ref, do_ref, lse_ref, delta_ref, dq_ref, dkm_ref, dvm_ref):
        @pl.when(pl.program_id(0) == 0)
        def _():
            dkm_ref[...] = jnp.zeros_like(dkm_ref)
            dvm_ref[...] = jnp.zeros_like(dvm_ref)

        lo = _lo((tq, LANES))
        lov = _lo((N_MEM, LANES))
        for t in range(2):
            qa, qb = _split(q_ref[:, _tile(t)], lo)
            doa, dob = _split(do_ref[:, _tile(t)], lo)
            km, vm = km_ref[:, _tile(t)], vm_ref[:, _tile(t)]
            lse_t, delta_t = lse_ref[:, _tile(t)], delta_ref[:, _tile(t)]
            prs, dss = [], []
            for h, (qh, doh) in enumerate(((qa, doa), (qb, dob))):
                pr = jnp.exp(_dot_nt(qh, km) - _col(lse_t, h * HD))
                dss.append(pr * (_dot_nt(doh, vm) - _col(delta_t, h * HD)))
                prs.append(pr)
            k_lo, k_hi = _split(km, lov)
            dq_ref[:, _tile(t)] = _dot(jnp.concatenate(dss, axis=1).astype(BF), jnp.concatenate([k_lo, k_hi], axis=0))
            dvm_ref[:, _tile(t)] += _dot(jnp.concatenate(prs, axis=0).T.astype(BF), jnp.concatenate([doa, dob], axis=0))
            dkm_ref[:, _tile(t)] += _dot(jnp.concatenate(dss, axis=0).T.astype(BF), jnp.concatenate([qa, qb], axis=0))

    ospec = pl.BlockSpec((tq, 256), lambda i: (i, 0))
    msp = pl.BlockSpec((N_MEM, 256), lambda i: (0, 0))
    md = jax.ShapeDtypeStruct((N_MEM, 256), F32)
    dq, dkm, dvm = _pcall(body, name="mem_bwd", grid=(S // tq,),
                          in_specs=[pl.BlockSpec((tq, 256), lambda i: (i, M_T0 // 2)), msp,
                                    pl.BlockSpec((N_MEM, 256), lambda i: (0, 1)),
                                    pl.BlockSpec((tq, 256), lambda i: (i, 3)), ospec,
                                    pl.BlockSpec((tq, 256), lambda i: (i, 3))],
                          out_specs=[ospec, msp, msp], out_shape=[jax.ShapeDtypeStruct((S, 256), F32), md, md])(
        proj, mkv, mkv, dcat, lse, delta)
    return dq, jnp.concatenate([dkm, dvm], axis=1)


def combine_fwd(os_, lses, *, tm=512):
    S = os_[0].shape[0]

    def body(o0, o1, o2, l0, l1, l2, tok_ref):
        ls = [l0[...], l1[...], l2[...]]
        m = jnp.maximum(jnp.maximum(ls[0], ls[1]), ls[2])
        es = [jnp.exp(l - m) for l in ls]
        den = es[0] + es[1] + es[2]
        for g, o in enumerate((o0, o1, o2)):
            tok_ref[:, 256 * g:256 * (g + 1)] = (o[...] * (es[g] / den)).astype(BF)

    sp = pl.BlockSpec((tm, 256), lambda i: (i, 0))
    return _pcall(body, name="combine_fwd", grid=(S // tm,), in_specs=[sp] * 6,
                  out_specs=pl.BlockSpec((tm, 768), lambda i: (i, 0)),
                  out_shape=jax.ShapeDtypeStruct((S, 768), BF))(*os_, *lses)


def combine_bwd(dcat, os_, lses, *, tm=512):
    S = dcat.shape[0]

    def body(dt_ref, o0, o1, o2, l0, l1, l2, do0, do1, do2, de0, de1, de2):
        ls = [l0[...], l1[...], l2[...]]
        m = jnp.maximum(jnp.maximum(ls[0], ls[1]), ls[2])
        es = [jnp.exp(l - m) for l in ls]
        den = es[0] + es[1] + es[2]
        alphas = [e / den for e in es]
        lo = _lo((tm, LANES))
        dts = [dt_ref[:, 256 * g:256 * (g + 1)].astype(F32) for g in range(3)]
        dal = []
        for g, o in enumerate((o0, o1, o2)):
            pr = dts[g] * o[...]
            dal.append(jnp.concatenate([_half_sum(pr[:, _tile(0)], lo), _half_sum(pr[:, _tile(1)], lo)], axis=1))
        mix = alphas[0] * dal[0] + alphas[1] * dal[1] + alphas[2] * dal[2]
        for g, (do_ref, de_ref) in enumerate(((do0, de0), (do1, de1), (do2, de2))):
            do_ref[...] = (dts[g] * alphas[g]).astype(BF)
            de_ref[...] = alphas[g] * mix

    sp = pl.BlockSpec((tm, 256), lambda i: (i, 0))
    outs = _pcall(body, name="combine_bwd", grid=(S // tm,),
                  in_specs=[pl.BlockSpec((tm, 768), lambda i: (i, 0))] + [sp] * 6, out_specs=[sp] * 6,
                  out_shape=[jax.ShapeDtypeStruct((S, 256), BF)] * 3 + [jax.ShapeDtypeStruct((S, 256), F32)] * 3)(
        dcat, *os_, *lses)
    return outs[:3], outs[3:]


def _coords():
    return lax.axis_index("x"), lax.axis_index("y"), lax.axis_index("c")


def _other_chips(x, y):
    return [(1 - x, y), (x, 1 - y), (1 - x, 1 - y)]


HBM_SPEC = pl.BlockSpec(memory_space=pltpu.HBM)


def gather_shards(shards):
    n = len(shards)

    def body(*refs):
        ins, outs = refs[:n], refs[n:2 * n]
        send, recv, lsem = refs[2 * n:]
        x, y, c = _coords()
        me = 2 * x + y
        chips = _other_chips(x, y)
        local, sends = [], []
        for a in range(n):
            cp = pltpu.make_async_copy(ins[a], outs[a].at[me], lsem.at[a])
            cp.start()
            local.append(cp)
            for j, (px, py) in enumerate(chips):
                cp = pltpu.make_async_remote_copy(src_ref=ins[a], dst_ref=outs[a].at[me], send_sem=send.at[3 * a + j],
                                                  recv_sem=recv.at[3 * a + j], device_id=(px, py, c), device_id_type=MESH)
                cp.start()
                sends.append(cp)
        for a in range(n):
            for j, (px, py) in enumerate(chips):
                pltpu.make_async_remote_copy(src_ref=ins[a], dst_ref=outs[a].at[2 * px + py], send_sem=send.at[3 * a + j],
                                             recv_sem=recv.at[3 * a + j], device_id=(px, py, c),
                                             device_id_type=MESH).wait_recv()
        for cp in sends:
            cp.wait_send()
        for cp in local:
            cp.wait()

    return pl.pallas_call(
        body, name="gather_shards", in_specs=[HBM_SPEC] * n, out_specs=[HBM_SPEC] * n,
        out_shape=[jax.ShapeDtypeStruct((4,) + s.shape, s.dtype) for s in shards],
        scratch_shapes=[pltpu.SemaphoreType.DMA((3 * n,)), pltpu.SemaphoreType.DMA((3 * n,)),
                        pltpu.SemaphoreType.DMA((n,))])(*shards)


def scatter_grads(parts):
    n = len(parts)

    def body(*refs):
        ins, outs = refs[:n], refs[n:2 * n]
        send, recv = refs[2 * n:]
        x, y, c = _coords()
        chips = _other_chips(x, y)
        sends = []
        for a in range(n):
            for j, (px, py) in enumerate(chips):
                cp = pltpu.make_async_remote_copy(src_ref=ins[a].at[2 * px + py], dst_ref=outs[a].at[j],
                                                  send_sem=send.at[3 * a + j], recv_sem=recv.at[3 * a + j],
                                                  device_id=(px, py, c), device_id_type=MESH)
                cp.start()
                sends.append(cp)
        for cp in sends:
            cp.wait_recv()
        for cp in sends:
            cp.wait_send()

    return pl.pallas_call(
        body, name="scatter_grads", in_specs=[HBM_SPEC] * n, out_specs=[HBM_SPEC] * n,
        out_shape=[jax.ShapeDtypeStruct((3,) + p.shape[1:], p.dtype) for p in parts],
        scratch_shapes=[pltpu.SemaphoreType.DMA((3 * n,)), pltpu.SemaphoreType.DMA((3 * n,))])(*parts)


def sibling_swap(arrs):
    n = len(arrs)

    def body(*refs):
        ins, outs = refs[:n], refs[n:2 * n]
        send, recv = refs[2 * n:]
        x, y, c = _coords()
        cps = []
        for a in range(n):
            cp = pltpu.make_async_remote_copy(src_ref=ins[a], dst_ref=outs[a], send_sem=send.at[a], recv_sem=recv.at[a],
                                              device_id=(x, y, 1 - c), device_id_type=MESH)
            cp.start()
            cps.append(cp)
        for cp in cps:
            cp.wait_recv()
        for cp in cps:
            cp.wait_send()

    return pl.pallas_call(
        body, name="sibling_swap", in_specs=[HBM_SPEC] * n, out_specs=[HBM_SPEC] * n,
        out_shape=[jax.ShapeDtypeStruct(a.shape, a.dtype) for a in arrs],
        scratch_shapes=[pltpu.SemaphoreType.DMA((n,)), pltpu.SemaphoreType.DMA((n,))])(*arrs)


def allsum_small(v):
    rows = v.shape[0]

    def body(v_ref, tot_ref, gath_ref, send, recv):
        x, y, c = _coords()
        me = 4 * x + 2 * y + c
        gath_ref[me] = v_ref[...]
        cps = []
        for k in range(1, 8):
            fx, fy, fc = (k >> 2) & 1, (k >> 1) & 1, k & 1
            peer = (1 - x if fx else x, 1 - y if fy else y, 1 - c if fc else c)
            cp = pltpu.make_async_remote_copy(src_ref=v_ref, dst_ref=gath_ref.at[me], send_sem=send.at[k - 1],
                                              recv_sem=recv.at[k - 1], device_id=peer, device_id_type=MESH)
            cp.start()
            cps.append(cp)
        for cp in cps:
            cp.wait_recv()
        for cp in cps:
            cp.wait_send()
        tot = gath_ref[0]
        for k in range(1, 8):
            tot = tot + gath_ref[k]
        tot_ref[...] = tot

    vm = pl.BlockSpec(memory_space=pltpu.VMEM)
    tot, _ = pl.pallas_call(
        body, name="allsum_small", in_specs=[vm], out_specs=[vm, vm],
        out_shape=[jax.ShapeDtypeStruct((rows, LANES), F32), jax.ShapeDtypeStruct((8, rows, LANES), F32)],
        scratch_shapes=[pltpu.SemaphoreType.DMA((7,)), pltpu.SemaphoreType.DMA((7,))])(v)
    return tot


def sum_parts(own, recv, *, tr=256):
    R, C = own.shape
    tr = min(tr, R)

    def body(o_ref, r_ref, out_ref):
        out_ref[...] = ((o_ref[...] + r_ref[0].astype(F32)) + r_ref[1].astype(F32)) + r_ref[2].astype(F32)

    sp = pl.BlockSpec((tr, C), lambda i: (i, 0))
    return _pcall(body, name="sum_parts", grid=(R // tr,),
                  in_specs=[sp, pl.BlockSpec((3, tr, C), lambda i: (0, i, 0))], out_specs=sp,
                  out_shape=jax.ShapeDtypeStruct((R, C), F32))(own, recv)


def adamw(w, ga, gb, m, v, *, tr=256):
    R, C = w.shape
    tr = min(tr, R)
    two = gb is not None

    def body(*refs):
        if two:
            w_ref, ga_ref, gb_ref, m_ref, v_ref, g_out, d_out, m_out, v_out = refs
            g = ga_ref[...] + gb_ref[...]
        else:
            w_ref, ga_ref, m_ref, v_ref, g_out, d_out, m_out, v_out = refs
            g = ga_ref[...]
        mn = B1 * m_ref[...] + (1.0 - B1) * g
        vn = B2 * v_ref[...] + (1.0 - B2) * (g * g)
        m_hat = mn / (1.0 - B1 ** STEP)
        v_hat = vn / (1.0 - B2 ** STEP)
        g_out[...] = g
        d_out[...] = -LR * (m_hat / (jnp.sqrt(v_hat) + AEPS) + WD * w_ref[...])
        m_out[...] = mn
        v_out[...] = vn

    sp = pl.BlockSpec((tr, C), lambda i: (i, 0))
    args = [w, ga, gb, m, v] if two else [w, ga, m, v]
    sd = jax.ShapeDtypeStruct((R, C), F32)
    return _pcall(body, name="adamw", grid=(R // tr,), in_specs=[sp] * len(args), out_specs=[sp] * 4,
                  out_shape=[sd] * 4)(*args)


def _rope_tables(S):
    def inv_freq(n_dims, theta):
        return theta ** (-(jnp.arange(0, n_dims, 2, dtype=jnp.float32) / n_dims))

    pos = lax.broadcasted_iota(jnp.int32, (S, LANES), 0)
    d = lax.broadcasted_iota(jnp.int32, (S, LANES), 1) % HD
    d1 = lax.iota(jnp.int32, LANES) % HD
    ang = pos.astype(F32) * inv_freq(HD // 4, ROPE_THETA)[d1 % 8][None, :]
    sin = jnp.sin(ang)
    partial = (jnp.where(d < 16, jnp.cos(ang), 1.0), jnp.where((d >= 8) & (d < 16), sin, 0.0),
               jnp.where(d < 8, -sin, 0.0))
    grid_pos = jnp.where(d < 32, pos // GRID_W, pos % GRID_W)
    ang = grid_pos.astype(F32) * inv_freq(HD // 2, AXIAL_THETA)[d1 % 16][None, :]
    sin = jnp.sin(ang)
    axial = (jnp.cos(ang), jnp.where(d % 32 >= 16, sin, 0.0), jnp.where(d % 32 < 16, -sin, 0.0))
    return partial, axial


def _pad_w_in(w):
    cols = [w[:, :768]]
    for base in (768, 960):
        for g in range(3):
            kg = w[:, base + g * HD:base + (g + 1) * HD]
            cols += [kg, kg]
    cols.append(w[:, 1152:])
    return jnp.concatenate(cols, axis=1)


def _unpad_dw_in(dw):
    cols = [dw[:, :768]]
    for t0 in (K_T0, V_T0):
        for g in range(3):
            b = (t0 + g) * LANES
            cols.append(dw[:, b:b + HD] + dw[:, b + HD:b + LANES])
    cols.append(dw[:, M_T0 * LANES:])
    return jnp.concatenate(cols, axis=1)


def _stats_t(a, tq):
    S = a.shape[0]
    t = a.reshape(S, -1, 2, HD)[:, :N_PAIRS, :, 0]
    t = jnp.transpose(t, (1, 2, 0))
    t = jnp.pad(t, ((0, 0), (0, 6), (0, 0)))
    return jnp.transpose(t.reshape(N_PAIRS, 8, S // tq, tq), (0, 2, 1, 3))


def _fold(t):
    return t[..., :HD] + t[..., HD:]


def kernel(x, mem, mem_norm_g, w_in, w_mem_kv, w_o, g_mix_pre, g_mix_post, attn_sink, qk_norm_g, w_gate_up, w_down, g_ffn_pre, g_ffn_post, loss_target, m_mem_norm_g, m_w_in, m_w_mem_kv, m_w_o, m_g_mix_pre, m_g_mix_post, m_attn_sink, m_qk_norm_g, m_w_gate_up, m_w_down, m_g_ffn_pre, m_g_ffn_post, v_mem_norm_g, v_w_in, v_w_mem_kv, v_w_o, v_g_mix_pre, v_g_mix_post, v_attn_sink, v_qk_norm_g, v_w_gate_up, v_w_down, v_g_ffn_pre, v_g_ffn_post):
    S = x.shape[1]
    depth = w_in.shape[0]
    xs, memx, tgt = x[0], mem[0], loss_target[0]
    tab_p, tab_a = _rope_tables(S)
    row = lambda a: a.reshape(1, -1)

    gi, gm, go, gg, gd = gather_shards([w_in.astype(BF), w_mem_kv.astype(BF), w_o.astype(BF), w_gate_up.astype(BF),
                                        w_down.astype(BF)])
    W_in = jnp.concatenate([gi[s] for s in range(4)], axis=2)
    W_mkv = jnp.concatenate([gm[s] for s in range(4)], axis=1)
    W_o = jnp.concatenate([go[s] for s in range(4)], axis=1)
    W_g = jnp.concatenate([gg[0], gg[1]], axis=2)
    W_u = jnp.concatenate([gg[2], gg[3]], axis=2)
    W_d = jnp.concatenate([gd[s] for s in range(4)], axis=1)
    mem_g = row(mem_norm_g)
    zero_sink = jnp.zeros((12,), F32)
    qkg = jnp.pad(jnp.concatenate([qk_norm_g[0], qk_norm_g[0]], axis=1), ((0, 6), (0, 0)))
    no_qkg = jnp.zeros((8, LANES), F32)

    saved = []
    cur = xs
    for i in range(depth):
        kind = i % 3
        wp = _pad_w_in(W_in[i])
        sv = dict(x=cur, wp=wp)
        if kind == 1:
            h1, proj, raw, nrm = inproj_fwd(cur, row(g_mix_pre[i]), wp, tab_a, qkg, axial=True)
            sv["raw"] = raw
        else:
            h1, proj = inproj_fwd(cur, row(g_mix_pre[i]), wp, tab_p, no_qkg, axial=False)
        if kind == 0:
            tok, lse = banded_fwd(proj, attn_sink[i // 3], d=1, R=A_RADIUS, TQ=2 * A_RADIUS, pair0=0, npairs=6,
                                  use_sink=True, o_dtype=BF)
        elif kind == 1:
            bound = jnp.sqrt(jnp.max(nrm[0]) * jnp.max(nrm[1])) * LN2
            tok, lse = flash_fwd(proj, bound)
        else:
            os_, lses = [], []
            for g, (window, dil) in enumerate(C_GROUPS):
                rad = window // (2 * dil)
                o_g, l_g = banded_fwd(proj, zero_sink, d=dil, R=rad, TQ=2 * rad, pair0=2 * g, npairs=2, use_sink=False,
                                      o_dtype=F32)
                os_.append(o_g)
                lses.append(l_g)
            tok = combine_fwd(os_, lses)
            sv["os"], lse = os_, lses
        mem_n, mkv = norm_mm(memx, mem_g, W_mkv[i], tm=N_MEM)
        mo, mlse = mem_fwd(proj, mkv)
        cat = jnp.concatenate([tok, mo], axis=1)
        o, x2 = mm_norm_res(cat, W_o[i], row(g_mix_post[i]), cur)
        h2, gate, up, act = ffn_up_fwd(x2, row(g_ffn_pre[i]), W_g[i], W_u[i])
        f, x3 = mm_norm_res(act, W_d[i], row(g_ffn_post[i]), x2)
        sv.update(h1=h1, proj=proj, lse=lse, mem_n=mem_n, mkv=mkv, mlse=mlse, cat=cat, o=o, x2=x2, h2=h2, gate=gate,
                  up=up, act=act, f=f)
        saved.append(sv)
        cur = x3

    dcur, loss_vec = loss_bwd(cur, tgt)

    dW_in, dW_mkv, dW_o, dW_gu, dW_d = [None] * depth, [None] * depth, [None] * depth, [None] * depth, [None] * depth
    dg_pre, dg_post, dg_fpre, dg_fpost = [None] * depth, [None] * depth, [None] * depth, [None] * depth
    dg_mem = jnp.zeros((1, D), F32)
    dsinks, dqk = {}, None
    for i in reversed(range(depth)):
        sv = saved[i]
        kind = i % 3
        proj = sv["proj"]
        df, dgate, dup, dg_fpost[i] = normbwd_mm_swiglu(dcur, sv["f"], row(g_ffn_post[i]), W_d[i], sv["gate"], sv["up"])
        dx2, dg_fpre[i] = mm_nt_normbwd_res([(dgate, W_g[i]), (dup, W_u[i])], sv["x2"], row(g_ffn_pre[i]), dcur, nk=2)
        dW_d[i] = mm_acc(sv["act"].T, df, tk=1408, tn=D, ts=1024)
        h2t = sv["h2"].T
        dW_gu[i] = jnp.concatenate([mm_acc(h2t, dgate, tk=D, tn=1408, ts=1024), mm_acc(h2t, dup, tk=D, tn=1408, ts=1024)],
                                   axis=1)
        do, dcat, delta, dg_post[i] = normbwd_mm_cat(dx2, sv["o"], row(g_mix_post[i]), W_o[i], sv["cat"])
        dW_o[i] = mm_acc(sv["cat"].T, do, tk=D, tn=D, ts=1024)
        dqm, dmkv = mem_bwd(proj, sv["mkv"], dcat, sv["mlse"], delta)
        dmkv = dmkv.astype(BF)
        (dgm,) = mm_nt_normbwd_res([(dmkv, W_mkv[i])], memx, mem_g, None, tm=N_MEM)
        dg_mem = dg_mem + dgm
        dW_mkv[i] = mm_acc(sv["mem_n"].T, dmkv, tk=D, tn=512, ts=N_MEM)
        if kind == 0:
            sink = attn_sink[i // 3]
            args = dict(d=1, R=A_RADIUS, pair0=0, npairs=6)
            dq, dsk = banded_bwd_dq(proj, dcat, sv["lse"], delta, sink, TQ=2 * A_RADIUS, use_sink=True, **args)
            dkp, dvp = banded_bwd_dkv(proj, dcat, sv["lse"], delta, TK=2 * A_RADIUS, **args)
            dsinks[i // 3] = dsk.reshape(3, 8, 2, 2, HD)[:, 0, :, :, 0].reshape(12)
        elif kind == 1:
            dq, dkp, dvp = flash_bwd(proj, dcat, _stats_t(sv["lse"], 512), _stats_t(delta, 512))
        else:
            dos, des = combine_bwd(dcat, sv["os"], sv["lse"])
            dqs, dks, dvs = [], [], []
            for g, (window, dil) in enumerate(C_GROUPS):
                rad = window // (2 * dil)
                args = dict(d=dil, R=rad, pair0=2 * g, npairs=2)
                dq_g, _ = banded_bwd_dq(proj, dos[g], sv["lse"][g], des[g], zero_sink, TQ=2 * rad, use_sink=False, **args)
                dk_g, dv_g = banded_bwd_dkv(proj, dos[g], sv["lse"][g], des[g], TK=2 * rad, **args)
                dqs.append(dq_g)
                dks.append(dk_g)
                dvs.append(dv_g)
            dq, dkp, dvp = (jnp.concatenate(t, axis=1) for t in (dqs, dks, dvs))
        if kind == 1:
            dcur, dproj, dg_pre[i], dqk_t = inproj_bwd(dq, dkp, dvp, dqm, tab_a, sv["raw"], qkg, sv["wp"], sv["x"],
                                                       row(g_mix_pre[i]), dx2, axial=True)
            dqk = _fold(dqk_t[:2]).reshape(1, 2, HD)
        else:
            dcur, dproj, dg_pre[i], _ = inproj_bwd(dq, dkp, dvp, dqm, tab_p, proj, no_qkg, sv["wp"],
                                                   sv["x"], row(g_mix_pre[i]), dx2, axial=False)
        dW_in[i] = _unpad_dw_in(mm_acc(sv["h1"].T, dproj, tk=D, tn=896, ts=1024))

    x_i, y_i, _ = _coords()
    me = 2 * x_i + y_i
    big = [
        (jnp.transpose(jnp.stack(dW_in).reshape(depth, D, 4, IN_W // 4), (2, 0, 1, 3)), w_in, m_w_in, v_w_in),
        (jnp.transpose(jnp.stack(dW_mkv).reshape(depth, 4, D // 4, 512), (1, 0, 2, 3)), w_mem_kv, m_w_mem_kv, v_w_mem_kv),
        (jnp.transpose(jnp.stack(dW_o).reshape(depth, 4, D // 4, D), (1, 0, 2, 3)), w_o, m_w_o, v_w_o),
        (jnp.transpose(jnp.stack(dW_gu).reshape(depth, D, 4, 2 * DFF // 4), (2, 0, 1, 3)), w_gate_up, m_w_gate_up,
         v_w_gate_up),
        (jnp.transpose(jnp.stack(dW_d).reshape(depth, 4, DFF // 4, D), (1, 0, 2, 3)), w_down, m_w_down, v_w_down),
    ]
    recvs = scatter_grads([b[0].astype(BF) for b in big])
    parts = []
    for (g4, w, _, _), rc in zip(big, recvs):
        C = w.shape[-1]
        own = lax.dynamic_index_in_dim(g4, me, 0, keepdims=False).reshape(-1, C)
        parts.append(sum_parts(own, rc.reshape(3, -1, C)))
    sibs = sibling_swap(parts)
    big_out = []
    for (_, w, m, v), pa, pb in zip(big, parts, sibs):
        C = w.shape[-1]
        outs = adamw(w.reshape(-1, C), pa, pb, m.reshape(-1, C), v.reshape(-1, C))
        big_out.append([o.reshape(w.shape) for o in outs])

    small_w = [mem_norm_g, g_mix_pre, g_mix_post, attn_sink, qk_norm_g, g_ffn_pre, g_ffn_post]
    small_m = [m_mem_norm_g, m_g_mix_pre, m_g_mix_post, m_attn_sink, m_qk_norm_g, m_g_ffn_pre, m_g_ffn_post]
    small_v = [v_mem_norm_g, v_g_mix_pre, v_g_mix_post, v_attn_sink, v_qk_norm_g, v_g_ffn_pre, v_g_ffn_post]
    small_g = [dg_mem.reshape(D), jnp.concatenate(dg_pre, axis=0), jnp.concatenate(dg_post, axis=0),
               jnp.stack([dsinks[k] for k in sorted(dsinks)]), dqk, jnp.concatenate(dg_fpre, axis=0),
               jnp.concatenate(dg_fpost, axis=0)]
    sizes = [a.size for a in small_w]
    total = sum(sizes)
    rows_s = -(-(total + LANES) // (8 * LANES)) * 8

    def pack(arrs, extra=None):
        flat = jnp.concatenate([a.reshape(-1).astype(F32) for a in arrs])
        flat = jnp.pad(flat, (0, rows_s * LANES - LANES - total))
        tail = jnp.zeros((LANES,), F32) if extra is None else extra.reshape(LANES)
        return jnp.concatenate([flat, tail]).reshape(rows_s, LANES)

    tot = allsum_small(pack(small_g, loss_vec))
    loss = jnp.sum(tot[rows_s - 1])
    s_out = adamw(pack(small_w), tot, None, pack(small_m), pack(small_v))

    def unpack(buf):
        flat = buf.reshape(-1)
        out, off = [], 0
        for a, n in zip(small_w, sizes):
            out.append(flat[off:off + n].reshape(a.shape))
            off += n
        return out

    sg, sd_, sm, sv_ = (unpack(b) for b in s_out)

    def ordered(k):
        sm_ = (sg, sd_, sm, sv_)[k]
        b = [bo[k] for bo in big_out]
        return [sm_[0], b[0], b[1], b[2], sm_[1], sm_[2], sm_[3], sm_[4], b[3], b[4], sm_[5], sm_[6]]

    dx_out = dcur.reshape(1, S, D)
    return (loss, dx_out, *ordered(0), *ordered(1), *ordered(2), *ordered(3))
```

```python
import functools

import jax
import jax.numpy as jnp
from jax import lax
from jax.experimental import pallas as pl
from jax.experimental.pallas import tpu as pltpu

F32 = jnp.float32
BF = jnp.bfloat16

D = 1024
HD = 64
LANES = 128
N_PAIRS = 6
DFF = 2816
IN_W = 1408
PW = 14 * LANES
K_T0, V_T0, M_T0 = 6, 9, 12
EPS = 1e-6
SCALE = HD ** -0.5
NEG = -1e30
LOG2E = 1.4426950408889634
LN2 = 0.6931471805599453
MAX_PLAIN_SCORE = 40.0
ROPE_THETA = 500000.0
AXIAL_THETA = 10000.0
GRID_W = 64
A_RADIUS = 128
C_GROUPS = ((128, 1), (512, 4), (2048, 16))
N_MEM = 256
LR, B1, B2, AEPS, WD, STEP = 0.001, 0.9, 0.999, 1e-08, 0.01, 10
VMEM_LIMIT = 56 * 1024 * 1024
MESH = pl.DeviceIdType.MESH


def _pcall(body, *, name, grid, in_specs, out_specs, out_shape, scratch=()):
    return pl.pallas_call(
        body, name=name, grid=grid, in_specs=in_specs, out_specs=out_specs, out_shape=out_shape,
        scratch_shapes=scratch,
        compiler_params=pltpu.CompilerParams(dimension_semantics=("arbitrary",) * len(grid),
                                             vmem_limit_bytes=VMEM_LIMIT))


def _dot(a, b):
    return lax.dot_general(a, b, (((1,), (0,)), ((), ())), preferred_element_type=F32)


def _dot_nt(a, b):
    return lax.dot_general(a, b, (((1,), (1,)), ((), ())), preferred_element_type=F32)


def _lo(shape):
    return lax.broadcasted_iota(jnp.int32, shape, len(shape) - 1) < HD


def _half_sum(x, lo):
    a = jnp.sum(jnp.where(lo, x, 0.0), axis=-1, keepdims=True)
    b = jnp.sum(jnp.where(lo, 0.0, x), axis=-1, keepdims=True)
    return jnp.where(lo, a, b)


def _col(tile, lane):
    idx = lax.broadcasted_iota(jnp.int32, tile.shape, 1)
    return jnp.sum(jnp.where(idx == lane, tile, 0.0), axis=-1, keepdims=True)


def _split(t, lo):
    z = jnp.zeros_like(t)
    return jnp.where(lo, t, z), jnp.where(lo, z, t)


def _rms(xf, g):
    r = lax.rsqrt(jnp.mean(xf * xf, axis=-1, keepdims=True) + EPS)
    return xf * r * g


def _rms_bwd(xf, g, dy):
    r = lax.rsqrt(jnp.mean(xf * xf, axis=-1, keepdims=True) + EPS)
    xr = xf * r
    dg = jnp.sum(dy * xr, axis=0, keepdims=True)
    t = dy * g
    return r * (t - xr * jnp.mean(xr * t, axis=-1, keepdims=True)), dg


def _rope_fwd(y, c, s1, s2, sh):
    return y * c + pltpu.roll(y, sh, 1) * s1 + pltpu.roll(y, LANES - sh, 1) * s2


def _rope_bwd(dy, c, s1, s2, sh):
    return dy * c + pltpu.roll(dy * s1, LANES - sh, 1) + pltpu.roll(dy * s2, sh, 1)


def _tile(t):
    return slice(t * LANES, (t + 1) * LANES)


def inproj_fwd(x, g, w_pad, tabs, qkg, *, axial, tm=512):
    S = x.shape[0]
    sh = 16 if axial else 8

    def body(x_ref, g_ref, w_ref, c_ref, s1_ref, s2_ref, qkg_ref, h_ref, p_ref, *extra):
        h = _rms(x_ref[...], g_ref[...]).astype(BF)
        h_ref[...] = h
        acc = _dot(h, w_ref[...])
        c, s1, s2 = c_ref[...], s1_ref[...], s2_ref[...]
        lo = _lo((tm, LANES))
        if axial:
            raw_ref, nrm_ref = extra

            @pl.when(pl.program_id(0) == 0)
            def _():
                nrm_ref[...] = jnp.zeros_like(nrm_ref)

        for t in range(14):
            y = acc[:, _tile(t)]
            if t < V_T0:
                if axial:
                    raw_ref[:, _tile(t)] = y.astype(BF)
                    gt = qkg_ref[0:1, :] if t < K_T0 else qkg_ref[1:2, :]
                    y = y * lax.rsqrt(_half_sum(y * y, lo) * (1.0 / HD) + EPS) * gt
                y = _rope_fwd(y, c, s1, s2, sh)
            if t < K_T0:
                y = y * (SCALE * LOG2E if axial else SCALE)
            elif t >= M_T0:
                y = y * SCALE
            yb = y.astype(BF)
            p_ref[:, _tile(t)] = yb
            if axial and t < V_T0:
                yf = yb.astype(F32)
                n2 = jnp.max(_half_sum(yf * yf, lo), axis=0, keepdims=True)
                r = 0 if t < K_T0 else 1
                nrm_ref[r:r + 1, :] = jnp.maximum(nrm_ref[r:r + 1, :], n2)

    row = lambda w: pl.BlockSpec((tm, w), lambda i: (i, 0))
    full = lambda a: pl.BlockSpec(a.shape, lambda i: (0, 0))
    out_shape = [jax.ShapeDtypeStruct((S, D), BF), jax.ShapeDtypeStruct((S, PW), BF)]
    out_specs = [row(D), row(PW)]
    if axial:
        out_shape += [jax.ShapeDtypeStruct((S, V_T0 * LANES), BF), jax.ShapeDtypeStruct((8, LANES), F32)]
        out_specs += [row(V_T0 * LANES), pl.BlockSpec((8, LANES), lambda i: (0, 0))]
    return _pcall(body, name="inproj_fwd_axial" if axial else "inproj_fwd", grid=(S // tm,),
                  in_specs=[row(D), full(g), full(w_pad), row(LANES), row(LANES), row(LANES), full(qkg)],
                  out_specs=out_specs, out_shape=out_shape)(x, g, w_pad, *tabs, qkg)


def norm_mm(x, g, w, *, tm):
    S, N = x.shape[0], w.shape[1]

    def body(x_ref, g_ref, w_ref, h_ref, y_ref):
        h = _rms(x_ref[...], g_ref[...]).astype(BF)
        h_ref[...] = h
        y_ref[...] = _dot(h, w_ref[...]).astype(BF)

    return _pcall(body, name="norm_mm", grid=(S // tm,),
                  in_specs=[pl.BlockSpec((tm, D), lambda i: (i, 0)), pl.BlockSpec(g.shape, lambda i: (0, 0)),
                            pl.BlockSpec(w.shape, lambda i: (0, 0))],
                  out_specs=[pl.BlockSpec((tm, D), lambda i: (i, 0)), pl.BlockSpec((tm, N), lambda i: (i, 0))],
                  out_shape=[jax.ShapeDtypeStruct((S, D), BF), jax.ShapeDtypeStruct((S, N), BF)])(x, g, w)


def ffn_up_fwd(x, g, wg, wu, next_shards=(), *, tm=512, tn=1408):
    S = x.shape[0]
    n = len(next_shards)
    last = (S // tm - 1, DFF // tn - 1)

    def body(x_ref, g_ref, wg_ref, wu_ref, *refs):
        comm_in, (h_ref, gate_ref, up_ref, a_ref), comm_out = refs[:n], refs[n:n + 4], refs[n + 4:2 * n + 4]
        h_scr, sems = refs[2 * n + 4], refs[2 * n + 5:]
        if n:
            @pl.when((pl.program_id(0) == 0) & (pl.program_id(1) == 0))
            def _():
                _gather_comm(comm_in, comm_out, *sems, start=True)

        @pl.when(pl.program_id(1) == 0)
        def _():
            h = _rms(x_ref[...], g_ref[...]).astype(BF)
            h_scr[...] = h
            h_ref[...] = h

        h = h_scr[...]
        gate = _dot(h, wg_ref[...])
        up = _dot(h, wu_ref[...])
        gate_ref[...] = gate.astype(BF)
        up_ref[...] = up.astype(BF)
        a_ref[...] = (gate * pl.reciprocal(1.0 + jnp.exp(-gate), approx=True) * up).astype(BF)
        if n:
            @pl.when((pl.program_id(0) == last[0]) & (pl.program_id(1) == last[1]))
            def _():
                _gather_comm(comm_in, comm_out, *sems, start=False)

    rowd = pl.BlockSpec((tm, D), lambda i, j: (i, 0))
    wsp = pl.BlockSpec((D, tn), lambda i, j: (0, j))
    osp = pl.BlockSpec((tm, tn), lambda i, j: (i, j))
    sd = jax.ShapeDtypeStruct((S, DFF), BF)
    outs = _pcall(body, name="ffn_up_fwd_gather" if n else "ffn_up_fwd", grid=(S // tm, DFF // tn),
                  in_specs=[rowd, pl.BlockSpec(g.shape, lambda i, j: (0, 0)), wsp, wsp] + [HBM_SPEC] * n,
                  out_specs=[rowd, osp, osp, osp] + [HBM_SPEC] * n,
                  out_shape=[jax.ShapeDtypeStruct((S, D), BF), sd, sd, sd] + _gather_out_shape(next_shards),
                  scratch=[pltpu.VMEM((tm, D), BF)] + (_gather_sems(n) if n else []))(x, g, wg, wu, *next_shards)
    return outs[:4], outs[4:]


def mm_norm_res(a, w, g, res, *, tm=512):
    S, K = a.shape

    def body(a_ref, w_ref, g_ref, res_ref, y_ref, o_ref):
        y = _dot(a_ref[...], w_ref[...])
        y_ref[...] = y
        o_ref[...] = res_ref[...] + _rms(y, g_ref[...])

    rowd = pl.BlockSpec((tm, D), lambda i: (i, 0))
    sd = jax.ShapeDtypeStruct((S, D), F32)
    return _pcall(body, name="mm_norm_res", grid=(S // tm,),
                  in_specs=[pl.BlockSpec((tm, K), lambda i: (i, 0)), pl.BlockSpec(w.shape, lambda i: (0, 0)),
                            pl.BlockSpec(g.shape, lambda i: (0, 0)), rowd],
                  out_specs=[rowd, rowd], out_shape=[sd, sd])(a, w, g, res)


def loss_bwd(y, tgt, *, tm=512):
    S = y.shape[0]

    def body(y_ref, t_ref, dy_ref, l_ref):
        @pl.when(pl.program_id(0) == 0)
        def _():
            l_ref[...] = jnp.zeros_like(l_ref)

        e = y_ref[...] - t_ref[...]
        dy_ref[...] = e * (1.0 / D)
        col = jnp.sum(e * e, axis=0, keepdims=True)
        part = col[:, _tile(0)]
        for t in range(1, D // LANES):
            part = part + col[:, _tile(t)]
        l_ref[...] += part * (0.5 / D)

    rowd = pl.BlockSpec((tm, D), lambda i: (i, 0))
    return _pcall(body, name="loss_bwd", grid=(S // tm,), in_specs=[rowd, rowd],
                  out_specs=[rowd, pl.BlockSpec((1, LANES), lambda i: (0, 0))],
                  out_shape=[jax.ShapeDtypeStruct((S, D), F32), jax.ShapeDtypeStruct((1, LANES), F32)])(y, tgt)


def normbwd_mm_cat(dy, ysaved, g, w, cat, *, tm=512):
    S = dy.shape[0]

    def body(dy_ref, y_ref, g_ref, w_ref, cat_ref, d_ref, dcat_ref, delta_ref, dg_ref):
        @pl.when(pl.program_id(0) == 0)
        def _():
            dg_ref[...] = jnp.zeros_like(dg_ref)

        d, dg = _rms_bwd(y_ref[...], g_ref[...], dy_ref[...])
        dg_ref[...] += dg
        d = d.astype(BF)
        d_ref[...] = d
        z = _dot_nt(d, w_ref[...])
        dcat_ref[...] = z.astype(BF)
        lo = _lo((tm, LANES))
        for t in range(D // LANES):
            delta_ref[:, _tile(t)] = _half_sum(z[:, _tile(t)] * cat_ref[:, _tile(t)].astype(F32), lo)

    rowd = pl.BlockSpec((tm, D), lambda i: (i, 0))
    return _pcall(body, name="normbwd_mm_cat", grid=(S // tm,),
                  in_specs=[rowd, rowd, pl.BlockSpec(g.shape, lambda i: (0, 0)),
                            pl.BlockSpec(w.shape, lambda i: (0, 0)), rowd],
                  out_specs=[rowd, rowd, rowd, pl.BlockSpec((1, D), lambda i: (0, 0))],
                  out_shape=[jax.ShapeDtypeStruct((S, D), BF), jax.ShapeDtypeStruct((S, D), BF),
                             jax.ShapeDtypeStruct((S, D), F32), jax.ShapeDtypeStruct((1, D), F32)])(dy, ysaved, g, w, cat)


def normbwd_mm_swiglu(dy, fsaved, g, wd, gate, up, grad_parts=(), *, tm=512, tn=1408):
    S = dy.shape[0]
    n = len(grad_parts)
    last = (S // tm - 1, DFF // tn - 1)

    def body(dy_ref, f_ref, g_ref, w_ref, gate_ref, up_ref, *refs):
        comm_in, (df_ref, dgate_ref, dup_ref, dg_ref), comm_out = refs[:n], refs[n:n + 4], refs[n + 4:2 * n + 4]
        d_scr, sems = refs[2 * n + 4], refs[2 * n + 5:]
        i, j = pl.program_id(0), pl.program_id(1)

        @pl.when((i == 0) & (j == 0))
        def _():
            dg_ref[...] = jnp.zeros_like(dg_ref)
            if n:
                _scatter_comm(comm_in, comm_out, *sems, start=True)

        @pl.when(j == 0)
        def _():
            d, dg = _rms_bwd(f_ref[...], g_ref[...], dy_ref[...])
            dg_ref[...] += dg
            d_scr[...] = d.astype(BF)
            df_ref[...] = d.astype(BF)

        da = _dot_nt(d_scr[...], w_ref[...])
        gt = gate_ref[...].astype(F32)
        sig = pl.reciprocal(1.0 + jnp.exp(-gt), approx=True)
        dgate_ref[...] = (da * up_ref[...].astype(F32) * (sig * (1.0 + gt * (1.0 - sig)))).astype(BF)
        dup_ref[...] = (da * (gt * sig)).astype(BF)
        if n:
            @pl.when((i == last[0]) & (j == last[1]))
            def _():
                _scatter_comm(comm_in, comm_out, *sems, start=False)

    rowd = pl.BlockSpec((tm, D), lambda i, j: (i, 0))
    osp = pl.BlockSpec((tm, tn), lambda i, j: (i, j))
    sd = jax.ShapeDtypeStruct((S, DFF), BF)
    outs = _pcall(body, name="normbwd_mm_swiglu_scatter" if n else "normbwd_mm_swiglu", grid=(S // tm, DFF // tn),
                  in_specs=[rowd, rowd, pl.BlockSpec(g.shape, lambda i, j: (0, 0)),
                            pl.BlockSpec((tn, D), lambda i, j: (j, 0)), osp, osp] + [HBM_SPEC] * n,
                  out_specs=[rowd, osp, osp, pl.BlockSpec((1, D), lambda i, j: (0, 0))] + [HBM_SPEC] * n,
                  out_shape=[jax.ShapeDtypeStruct((S, D), BF), sd, sd, jax.ShapeDtypeStruct((1, D), F32)]
                  + _scatter_out_shape(grad_parts),
                  scratch=[pltpu.VMEM((tm, D), BF)] + (_scatter_sems(n) if n else []))(
        dy, fsaved, g, wd, gate, up, *grad_parts)
    return outs[:4], outs[4:]


def mm_nt_normbwd_res(parts, xin, g, dres, *, tm=512, nk=1):
    S = xin.shape[0]
    npart = len(parts)
    kb = parts[0][0].shape[1] // nk
    has_res = dres is not None

    def body(*refs):
        prefs = refs[:2 * npart]
        x_ref, g_ref = refs[2 * npart:2 * npart + 2]
        rest = refs[2 * npart + 2:]
        if has_res:
            dres_ref, dx_ref, dg_ref, acc = rest
        else:
            dg_ref, acc = rest
        i, k = pl.program_id(0), pl.program_id(1)

        @pl.when((i == 0) & (k == 0))
        def _():
            dg_ref[...] = jnp.zeros_like(dg_ref)

        z = _dot_nt(prefs[0][...], prefs[1][...])
        for p in range(1, npart):
            z = z + _dot_nt(prefs[2 * p][...], prefs[2 * p + 1][...])

        @pl.when(k == 0)
        def _():
            acc[...] = z

        @pl.when(k > 0)
        def _():
            acc[...] += z

        @pl.when(k == nk - 1)
        def _():
            dx, dg = _rms_bwd(x_ref[...], g_ref[...], acc[...])
            dg_ref[...] += dg
            if has_res:
                dx_ref[...] = dres_ref[...] + dx

    rowd = pl.BlockSpec((tm, D), lambda i, k: (i, 0))
    in_specs, args = [], []
    for dy, w in parts:
        in_specs += [pl.BlockSpec((tm, kb), lambda i, k: (i, k)), pl.BlockSpec((D, kb), lambda i, k: (0, k))]
        args += [dy, w]
    in_specs += [rowd, pl.BlockSpec(g.shape, lambda i, k: (0, 0))]
    args += [xin, g]
    out_specs = [pl.BlockSpec((1, D), lambda i, k: (0, 0))]
    out_shape = [jax.ShapeDtypeStruct((1, D), F32)]
    if has_res:
        in_specs.append(rowd)
        args.append(dres)
        out_specs.insert(0, rowd)
        out_shape.insert(0, jax.ShapeDtypeStruct((S, D), F32))
    return _pcall(body, name="mm_nt_normbwd_res" if has_res else "mm_nt_normbwd", grid=(S // tm, nk),
                  in_specs=in_specs, out_specs=out_specs, out_shape=out_shape,
                  scratch=[pltpu.VMEM((tm, D), F32)])(*args)


def inproj_bwd(dq, dkp, dvp, dqm, tabs, raw, qkg, w_pad, xin, g, dres, *, axial, tm=512):
    S = xin.shape[0]
    sh = 16 if axial else 8

    def body(dq_ref, dk_ref, dv_ref, dm_ref, c_ref, s1_ref, s2_ref, raw_ref, qkg_ref, w_ref, x_ref, g_ref, dres_ref,
             dx_ref, dp_ref, dg_ref, dqk_ref):
        @pl.when(pl.program_id(0) == 0)
        def _():
            dg_ref[...] = jnp.zeros_like(dg_ref)
            dqk_ref[...] = jnp.zeros_like(dqk_ref)

        c, s1, s2 = c_ref[...], s1_ref[...], s2_ref[...]
        lo = _lo((tm, LANES))
        for t in range(14):
            if t < K_T0:
                y = dq_ref[:, _tile(t)] * SCALE
            elif t < V_T0:
                y = dk_ref[:, _tile(2 * (t - K_T0))] + dk_ref[:, _tile(2 * (t - K_T0) + 1)]
                if axial:
                    y = y * LN2
            elif t < M_T0:
                y = dv_ref[:, _tile(2 * (t - V_T0))] + dv_ref[:, _tile(2 * (t - V_T0) + 1)]
            else:
                y = dm_ref[:, _tile(t - M_T0)] * SCALE
            if t < V_T0:
                y = _rope_bwd(y, c, s1, s2, sh)
                if axial:
                    row = 0 if t < K_T0 else 1
                    xr = raw_ref[:, _tile(t)].astype(F32)
                    r = lax.rsqrt(_half_sum(xr * xr, lo) * (1.0 / HD) + EPS)
                    xn = xr * r
                    dqk_ref[row:row + 1, :] += jnp.sum(y * xn, axis=0, keepdims=True)
                    tt = y * qkg_ref[row:row + 1, :]
                    y = r * (tt - xn * (_half_sum(xn * tt, lo) * (1.0 / HD)))
            dp_ref[:, _tile(t)] = y.astype(BF)
        z = _dot_nt(dp_ref[...], w_ref[...])
        dx, dg = _rms_bwd(x_ref[...], g_ref[...], z)
        dg_ref[...] += dg
        dx_ref[...] = dres_ref[...] + dx

    row = lambda w: pl.BlockSpec((tm, w), lambda i: (i, 0))
    full = lambda a: pl.BlockSpec(a.shape, lambda i: (0, 0))
    return _pcall(body, name="inproj_bwd_axial" if axial else "inproj_bwd", grid=(S // tm,),
                  in_specs=[row(768), row(768), row(768), row(256), row(LANES), row(LANES), row(LANES),
                            row(raw.shape[1] if axial else LANES), full(qkg), full(w_pad), row(D), full(g), row(D)],
                  out_specs=[row(D), row(PW), pl.BlockSpec((1, D), lambda i: (0, 0)),
                             pl.BlockSpec((8, LANES), lambda i: (0, 0))],
                  out_shape=[jax.ShapeDtypeStruct((S, D), F32), jax.ShapeDtypeStruct((S, PW), BF),
                             jax.ShapeDtypeStruct((1, D), F32), jax.ShapeDtypeStruct((8, LANES), F32)])(
        dq, dkp, dvp, dqm, *tabs, raw, qkg, w_pad, xin, g, dres)


def mm_acc(a, b, *, tk, tn, ts):
    S, K = a.shape
    N = b.shape[1]
    ts = min(ts, S)

    def body(a_ref, b_ref, o_ref):
        z = lax.dot_general(a_ref[...], b_ref[...], (((0,), (0,)), ((), ())), preferred_element_type=F32)

        @pl.when(pl.program_id(2) == 0)
        def _():
            o_ref[...] = z

        @pl.when(pl.program_id(2) > 0)
        def _():
            o_ref[...] += z

    return _pcall(body, name="mm_acc", grid=(K // tk, N // tn, S // ts),
                  in_specs=[pl.BlockSpec((ts, tk), lambda k, n, s: (s, k)), pl.BlockSpec((ts, tn), lambda k, n, s: (s, n))],
                  out_specs=pl.BlockSpec((tk, tn), lambda k, n, s: (k, n)),
                  out_shape=jax.ShapeDtypeStruct((K, N), F32))(a, b)


def _band_specs(L, d, R, T, width, bw, col_of):
    n = T // R
    nb = width // bw
    last = L // R - 1
    col = lambda g, r: r * nb + col_of(g)
    return [pl.BlockSpec((R, bw), lambda g, r, i: (jnp.maximum(i * n - 1, 0), col(g, r))),
            pl.BlockSpec((T, bw), lambda g, r, i: (i, col(g, r))),
            pl.BlockSpec((R, bw), lambda g, r, i: (jnp.minimum((i + 1) * n, last), col(g, r)))]


def _band_bias(T, R):
    w = lax.broadcasted_iota(jnp.int32, (T, T + 2 * R), 1)
    c = lax.broadcasted_iota(jnp.int32, (T, T + 2 * R), 0)
    return jnp.where(jnp.abs(w - R - c) <= R, 0.0, NEG).astype(F32)


def _edge_bias(i, T, R, L):
    wpos = i * T - R + lax.broadcasted_iota(jnp.int32, (1, T + 2 * R), 1)
    return jnp.where((wpos >= 0) & (wpos < L), 0.0, NEG)


def banded_fwd(proj, sink, *, d, R, TQ, pair0, npairs, use_sink, o_dtype):
    S = proj.shape[0]
    L = S // d
    pv = proj.reshape(L, d * PW)
    ow = npairs * LANES

    def body(sink_ref, bias_ref, q_ref, kp, kc, kn, vp, vc, vn, o_ref, lse_ref):
        g, i = pl.program_id(0), pl.program_id(2)
        bias = bias_ref[...] + _edge_bias(i, TQ, R, L)
        lo = _lo((TQ, LANES))
        kw = jnp.concatenate([kp[...], kc[...], kn[...]], axis=0)
        vw = jnp.concatenate([vp[...], vc[...], vn[...]], axis=0)
        v_lo, v_hi = _split(vw, _lo(vw.shape))
        vcat = jnp.concatenate([v_lo, v_hi], axis=0)
        for t in range(2):
            qa, qb = _split(q_ref[:, _tile(t)], lo)
            ps, dens, lses = [], [], []
            for h, qh in enumerate((qa, qb)):
                s = _dot_nt(qh, kw) + bias
                m = jnp.max(s, axis=-1, keepdims=True)
                if use_sink:
                    sk = sink_ref[2 * (pair0 + 2 * g + t) + h]
                    m = jnp.maximum(m, sk)
                e = jnp.exp(s - m)
                den = jnp.sum(e, axis=-1, keepdims=True)
                if use_sink:
                    den = den + jnp.exp(sk - m)
                ps.append(e.astype(BF))
                dens.append(den)
                lses.append(m + jnp.log(den))
            o = _dot(jnp.concatenate(ps, axis=1), vcat)
            o_ref[:, _tile(t)] = (o / jnp.where(lo, dens[0], dens[1])).astype(o_dtype)
            lse_ref[:, _tile(t)] = jnp.where(lo, lses[0], lses[1])

    g0 = pair0 // 2
    qspec = pl.BlockSpec((TQ, 2 * LANES), lambda g, r, i: (i, r * 7 + g0 + g))
    kspecs = _band_specs(L, d, R, TQ, PW, LANES, lambda g: K_T0 + g0 + g)
    vspecs = _band_specs(L, d, R, TQ, PW, LANES, lambda g: V_T0 + g0 + g)
    ospec = pl.BlockSpec((TQ, 2 * LANES), lambda g, r, i: (i, r * (npairs // 2) + g))
    bias = _band_bias(TQ, R)
    o, lse = _pcall(body, name="banded_fwd", grid=(npairs // 2, d, L // TQ),
                    in_specs=[pl.BlockSpec(memory_space=pltpu.SMEM), pl.BlockSpec(bias.shape, lambda g, r, i: (0, 0)),
                              qspec] + kspecs + vspecs,
                    out_specs=[ospec, ospec],
                    out_shape=[jax.ShapeDtypeStruct((L, d * ow), o_dtype), jax.ShapeDtypeStruct((L, d * ow), F32)])(
        sink, bias, pv, pv, pv, pv, pv, pv, pv)
    return o.reshape(S, ow), lse.reshape(S, ow)


def banded_bwd_dq(proj, do, lse, delta, sink, *, d, R, TQ, pair0, npairs, use_sink):
    S = proj.shape[0]
    L = S // d
    pv = proj.reshape(L, d * PW)
    ow = npairs * LANES

    def body(sink_ref, bias_ref, q_ref, kp, kc, kn, vp, vc, vn, do_ref, lse_ref, delta_ref, dq_ref, dsink_ref):
        g, r, i = pl.program_id(0), pl.program_id(1), pl.program_id(2)

        @pl.when((r == 0) & (i == 0))
        def _():
            dsink_ref[...] = jnp.zeros_like(dsink_ref)

        bias = bias_ref[...] + _edge_bias(i, TQ, R, L)
        lo = _lo((TQ, LANES))
        kw = jnp.concatenate([kp[...], kc[...], kn[...]], axis=0)
        vw = jnp.concatenate([vp[...], vc[...], vn[...]], axis=0)
        k_lo, k_hi = _split(kw, _lo(kw.shape))
        kcat = jnp.concatenate([k_lo, k_hi], axis=0)
        for t in range(2):
            qa, qb = _split(q_ref[:, _tile(t)], lo)
            doa, dob = _split(do_ref[:, _tile(t)], lo)
            lse_t, delta_t = lse_ref[:, _tile(t)], delta_ref[:, _tile(t)]
            dss, dsk = [], []
            for h, (qh, doh) in enumerate(((qa, doa), (qb, dob))):
                lse_h, delta_h = _col(lse_t, h * HD), _col(delta_t, h * HD)
                pr = jnp.exp(_dot_nt(qh, kw) + bias - lse_h)
                dss.append((pr * (_dot_nt(doh, vw) - delta_h)).astype(BF))
                if use_sink:
                    psink = jnp.exp(sink_ref[2 * (pair0 + 2 * g + t) + h] - lse_h)
                    dsk.append(-jnp.sum(psink * delta_h, axis=0, keepdims=True))
            dq_ref[:, _tile(t)] = _dot(jnp.concatenate(dss, axis=1), kcat)
            if use_sink:
                dsink_ref[:, _tile(t)] += jnp.where(_lo((8, LANES)), dsk[0], dsk[1])

    g0 = pair0 // 2
    qspec = pl.BlockSpec((TQ, 2 * LANES), lambda g, r, i: (i, r * 7 + g0 + g))
    kspecs = _band_specs(L, d, R, TQ, PW, LANES, lambda g: K_T0 + g0 + g)
    vspecs = _band_specs(L, d, R, TQ, PW, LANES, lambda g: V_T0 + g0 + g)
    ospec = pl.BlockSpec((TQ, 2 * LANES), lambda g, r, i: (i, r * (npairs // 2) + g))
    view = lambda a: a.reshape(L, d * a.shape[1])
    ispec = lambda a: pl.BlockSpec((TQ, 2 * LANES), lambda g, r, i: (i, r * (a.shape[1] // (2 * LANES)) + g))
    bias = _band_bias(TQ, R)
    dq, dsink = _pcall(body, name="banded_bwd_dq", grid=(npairs // 2, d, L // TQ),
                       in_specs=[pl.BlockSpec(memory_space=pltpu.SMEM), pl.BlockSpec(bias.shape, lambda g, r, i: (0, 0)),
                                 qspec] + kspecs + vspecs + [ispec(do), ispec(lse), ispec(delta)],
                       out_specs=[ospec, pl.BlockSpec((8, 2 * LANES), lambda g, r, i: (g, 0))],
                       out_shape=[jax.ShapeDtypeStruct((L, d * ow), F32),
                                  jax.ShapeDtypeStruct((npairs // 2 * 8, 2 * LANES), F32)])(
        sink, bias, pv, pv, pv, pv, pv, pv, pv, view(do), view(lse), view(delta))
    return dq.reshape(S, ow), dsink


def banded_bwd_dkv(proj, do, lse, delta, *, d, R, TK, pair0, npairs):
    S = proj.shape[0]
    L = S // d
    pv = proj.reshape(L, d * PW)
    ow = npairs * LANES

    def body(bias_ref, k_ref, v_ref, qp, qc, qn, dop, doc, don, lp, lc, ln, dp_, dc_, dn_, dk_ref, dv_ref):
        j = pl.program_id(2)
        W = TK + 2 * R
        bias = bias_ref[...] + _edge_bias(j, TK, R, L)
        low = _lo((W, LANES))
        qw = jnp.concatenate([qp[...], qc[...], qn[...]], axis=0)
        dow = jnp.concatenate([dop[...], doc[...], don[...]], axis=0)
        lse_w = jnp.concatenate([lp[...], lc[...], ln[...]], axis=0)
        delta_w = jnp.concatenate([dp_[...], dc_[...], dn_[...]], axis=0)
        k, v = k_ref[...], v_ref[...]
        for t in range(2):
            qa, qb = _split(qw[:, _tile(t)], low)
            doa, dob = _split(dow[:, _tile(t)], low)
            lse_r, delta_r = lse_w[:, _tile(t)].T, delta_w[:, _tile(t)].T
            prs, dss = [], []
            for h, (qh, doh) in enumerate(((qa, doa), (qb, dob))):
                pr = jnp.exp(_dot_nt(k, qh) + bias - lse_r[h * HD:h * HD + 1, :])
                dss.append((pr * (_dot_nt(v, doh) - delta_r[h * HD:h * HD + 1, :])).astype(BF))
                prs.append(pr.astype(BF))
            dv_ref[:, _tile(t)] = _dot(jnp.concatenate(prs, axis=1), jnp.concatenate([doa, dob], axis=0))
            dk_ref[:, _tile(t)] = _dot(jnp.concatenate(dss, axis=1), jnp.concatenate([qa, qb], axis=0))

    g0 = pair0 // 2
    kspec = pl.BlockSpec((TK, LANES), lambda g, r, j: (j, r * 14 + K_T0 + g0 + g))
    vspec = pl.BlockSpec((TK, LANES), lambda g, r, j: (j, r * 14 + V_T0 + g0 + g))
    qspecs = _band_specs(L, d, R, TK, PW, 2 * LANES, lambda g: g0 + g)
    ispecs = lambda a: _band_specs(L, d, R, TK, a.shape[1], 2 * LANES, lambda g: g)
    view = lambda a: a.reshape(L, d * a.shape[1])
    ospec = pl.BlockSpec((TK, 2 * LANES), lambda g, r, j: (j, r * (npairs // 2) + g))
    sd = jax.ShapeDtypeStruct((L, d * ow), F32)
    bias = _band_bias(TK, R)
    dk, dv = _pcall(body, name="banded_bwd_dkv", grid=(npairs // 2, d, L // TK),
                    in_specs=[pl.BlockSpec(bias.shape, lambda g, r, j: (0, 0)), kspec, vspec] + qspecs + ispecs(do)
                    + ispecs(lse) + ispecs(delta),
                    out_specs=[ospec, ospec], out_shape=[sd, sd])(
        bias, pv, pv, pv, pv, pv, *([view(do)] * 3), *([view(lse)] * 3), *([view(delta)] * 3))
    return dk.reshape(S, ow), dv.reshape(S, ow)


def flash_fwd(proj, bound, *, tq=1024, tk=512):
    S = proj.shape[0]

    def body_general(q_ref, k_ref, v_ref, o_ref, lse_ref):
        lo = _lo((tq, LANES))
        qa, qb = _split(q_ref[...], lo)
        lov = _lo((tk, LANES))

        def step(j, carry):
            ma, la, mb, lb, acc = carry
            rows = pl.ds(pl.multiple_of(j * tk, tk), tk)
            k, v = k_ref[rows, :], v_ref[rows, :]
            outs = []
            for qh, m0, l0 in ((qa, ma, la), (qb, mb, lb)):
                s = _dot_nt(qh, k)
                m1 = jnp.maximum(m0, jnp.max(s, axis=-1, keepdims=True))
                al = jnp.exp2(m0 - m1)
                e = jnp.exp2(s - m1)
                outs.append((m1, al * l0 + jnp.sum(e, axis=-1, keepdims=True), al, e.astype(BF)))
            v_lo, v_hi = _split(v, lov)
            pvv = _dot(jnp.concatenate([outs[0][3], outs[1][3]], axis=1), jnp.concatenate([v_lo, v_hi], axis=0))
            acc = acc * jnp.where(lo, outs[0][2], outs[1][2]) + pvv
            return outs[0][0], outs[0][1], outs[1][0], outs[1][1], acc

        m_init = jnp.full((tq, 1), NEG, F32)
        l_init = jnp.zeros((tq, 1), F32)
        ma, la, mb, lb, acc = lax.fori_loop(0, S // tk, step,
                                            (m_init, l_init, m_init, l_init, jnp.zeros((tq, LANES), F32)))
        o_ref[...] = (acc / jnp.where(lo, la, lb)).astype(BF)
        lse_ref[...] = jnp.where(lo, ma * LN2 + jnp.log(la), mb * LN2 + jnp.log(lb))

    def body_plain(q_ref, k_ref, v_ref, o_ref, lse_ref):
        lo = _lo((tq, LANES))
        qa, qb = _split(q_ref[...], lo)
        lov = _lo((tk, LANES))
        one = jnp.ones((tk, LANES), BF)

        def step(j, carry):
            acc_a, acc_b = carry
            rows = pl.ds(pl.multiple_of(j * tk, tk), tk)
            k, v = k_ref[rows, :], v_ref[rows, :]
            ea = jnp.exp2(_dot_nt(qa, k)).astype(BF)
            eb = jnp.exp2(_dot_nt(qb, k)).astype(BF)
            acc_a = acc_a + _dot(ea, jnp.where(lov, v, one))
            acc_b = acc_b + _dot(eb, jnp.where(lov, one, v))
            return acc_a, acc_b

        z = jnp.zeros((tq, LANES), F32)
        acc_a, acc_b = lax.fori_loop(0, S // tk, step, (z, z))
        den = jnp.where(lo, pltpu.roll(acc_a, HD, 1), pltpu.roll(acc_b, HD, 1))
        o_ref[...] = (jnp.where(lo, acc_a, acc_b) / den).astype(BF)
        lse_ref[...] = jnp.log(den)

    def body(bound_ref, q_ref, k_ref, v_ref, o_ref, lse_ref):
        small = bound_ref[0] <= MAX_PLAIN_SCORE

        @pl.when(small)
        def _():
            body_plain(q_ref, k_ref, v_ref, o_ref, lse_ref)

        @pl.when(jnp.logical_not(small))
        def _():
            body_general(q_ref, k_ref, v_ref, o_ref, lse_ref)

    ospec = pl.BlockSpec((tq, LANES), lambda p, i: (i, p))
    return _pcall(body, name="flash_fwd", grid=(N_PAIRS, S // tq),
                  in_specs=[pl.BlockSpec(memory_space=pltpu.SMEM), ospec,
                            pl.BlockSpec((S, LANES), lambda p, i: (0, K_T0 + p // 2)),
                            pl.BlockSpec((S, LANES), lambda p, i: (0, V_T0 + p // 2))],
                  out_specs=[ospec, ospec],
                  out_shape=[jax.ShapeDtypeStruct((S, 768), BF), jax.ShapeDtypeStruct((S, 768), F32)])(
        bound.reshape(1), proj, proj, proj)


def flash_bwd(proj, do, lse_t, delta_t, *, tq=512, tk=1024):
    S = proj.shape[0]
    nq = S // tq

    def body(k_ref, v_ref, q_ref, do_ref, lse_ref, delta_ref, dk_ref, dv_ref, dqt_ref):
        @pl.when(pl.program_id(1) == 0)
        def _():
            dqt_ref[...] = jnp.zeros_like(dqt_ref)

        k, v = k_ref[...], v_ref[...]
        lo = _lo((tq, LANES))
        k_lo, k_hi = _split(k.astype(F32), _lo((tk, LANES)))
        kt = jnp.concatenate([k_lo.T, k_hi.T], axis=1).astype(BF)

        def step(i, carry):
            dk, dv = carry
            rows = pl.ds(pl.multiple_of(i * tq, tq), tq)
            qa, qb = _split(q_ref[rows, :], lo)
            doa, dob = _split(do_ref[rows, :], lo)
            lse_i, delta_i = lse_ref[i] * LOG2E, delta_ref[i]
            prs, dss = [], []
            for h, (qh, doh) in enumerate(((qa, doa), (qb, dob))):
                pr = jnp.exp2(_dot_nt(k, qh) - lse_i[h:h + 1, :])
                dss.append((pr * (_dot_nt(v, doh) - delta_i[h:h + 1, :])).astype(BF))
                prs.append(pr.astype(BF))
            dv = dv + _dot(jnp.concatenate(prs, axis=1), jnp.concatenate([doa, dob], axis=0))
            dk = dk + _dot(jnp.concatenate(dss, axis=1), jnp.concatenate([qa, qb], axis=0))
            dqt_ref[i] += _dot(kt, jnp.concatenate(dss, axis=0))
            return dk, dv

        z = jnp.zeros((tk, LANES), F32)
        dk, dv = lax.fori_loop(0, nq, step, (z, z))
        dk_ref[...] = dk
        dv_ref[...] = dv

    ospec = pl.BlockSpec((tk, LANES), lambda p, j: (j, p))
    stat = pl.BlockSpec((None, nq, 8, tq), lambda p, j: (p, 0, 0, 0))
    whole = lambda: pl.BlockSpec((S, LANES), lambda p, j: (0, p), pipeline_mode=pl.Buffered(1))
    sd = jax.ShapeDtypeStruct((S, 768), F32)
    dk, dv, dqt = _pcall(body, name="flash_bwd", grid=(N_PAIRS, S // tk),
                         in_specs=[pl.BlockSpec((tk, LANES), lambda p, j: (j, K_T0 + p // 2)),
                                   pl.BlockSpec((tk, LANES), lambda p, j: (j, V_T0 + p // 2)), whole(), whole(),
                                   stat, stat],
                         out_specs=[ospec, ospec,
                                    pl.BlockSpec((None, nq, LANES, tq), lambda p, j: (p, 0, 0, 0),
                                                 pipeline_mode=pl.Buffered(1))],
                         out_shape=[sd, sd, jax.ShapeDtypeStruct((N_PAIRS, nq, LANES, tq), F32)])(
        proj, proj, proj, do, lse_t, delta_t)
    dq = jnp.transpose(dqt, (1, 3, 0, 2)).reshape(S, 768)
    return dq, dk, dv


def mem_fwd(proj, mkv, *, tq=512):
    S = proj.shape[0]

    def body(q_ref, km_ref, vm_ref, o_ref, lse_ref):
        lo = _lo((tq, LANES))
        lov = _lo((N_MEM, LANES))
        for t in range(2):
            qa, qb = _split(q_ref[:, _tile(t)], lo)
            km, vm = km_ref[:, _tile(t)], vm_ref[:, _tile(t)]
            ps, dens, lses = [], [], []
            for qh in (qa, qb):
                s = _dot_nt(qh, km)
                m = jnp.max(s, axis=-1, keepdims=True)
                e = jnp.exp(s - m)
                den = jnp.sum(e, axis=-1, keepdims=True)
                ps.append(e.astype(BF))
                dens.append(den)
                lses.append(m + jnp.log(den))
            v_lo, v_hi = _split(vm, lov)
            o = _dot(jnp.concatenate(ps, axis=1), jnp.concatenate([v_lo, v_hi], axis=0))
            o_ref[:, _tile(t)] = (o / jnp.where(lo, dens[0], dens[1])).astype(BF)
            lse_ref[:, _tile(t)] = jnp.where(lo, lses[0], lses[1])

    ospec = pl.BlockSpec((tq, 256), lambda i: (i, 0))
    return _pcall(body, name="mem_fwd", grid=(S // tq,),
                  in_specs=[pl.BlockSpec((tq, 256), lambda i: (i, M_T0 // 2)),
                            pl.BlockSpec((N_MEM, 256), lambda i: (0, 0)), pl.BlockSpec((N_MEM, 256), lambda i: (0, 1))],
                  out_specs=[ospec, ospec],
                  out_shape=[jax.ShapeDtypeStruct((S, 256), BF), jax.ShapeDtypeStruct((S, 256), F32)])(proj, mkv, mkv)


def mem_bwd(proj, mkv, dcat, lse, delta, *, tq=512):
    S = proj.shape[0]

    def body(q_ref, km_ref, vm_ref, do_ref, lse_ref, delta_ref, dq_ref, dkm_ref, dvm_ref):
        @pl.when(pl.program_id(0) == 0)
        def _():
            dkm_ref[...] = jnp.zeros_like(dkm_ref)
            dvm_ref[...] = jnp.zeros_like(dvm_ref)

        lo = _lo((tq, LANES))
        lov = _lo((N_MEM, LANES))
        for t in range(2):
            qa, qb = _split(q_ref[:, _tile(t)], lo)
            doa, dob = _split(do_ref[:, _tile(t)], lo)
            km, vm = km_ref[:, _tile(t)], vm_ref[:, _tile(t)]
            lse_t, delta_t = lse_ref[:, _tile(t)], delta_ref[:, _tile(t)]
            prs, dss = [], []
            for h, (qh, doh) in enumerate(((qa, doa), (qb, dob))):
                pr = jnp.exp(_dot_nt(qh, km) - _col(lse_t, h * HD))
                dss.append(pr * (_dot_nt(doh, vm) - _col(delta_t, h * HD)))
                prs.append(pr)
            k_lo, k_hi = _split(km, lov)
            dq_ref[:, _tile(t)] = _dot(jnp.concatenate(dss, axis=1).astype(BF), jnp.concatenate([k_lo, k_hi], axis=0))
            dvm_ref[:, _tile(t)] += _dot(jnp.concatenate(prs, axis=0).T.astype(BF), jnp.concatenate([doa, dob], axis=0))
            dkm_ref[:, _tile(t)] += _dot(jnp.concatenate(dss, axis=0).T.astype(BF), jnp.concatenate([qa, qb], axis=0))

    ospec = pl.BlockSpec((tq, 256), lambda i: (i, 0))
    msp = pl.BlockSpec((N_MEM, 256), lambda i: (0, 0))
    md = jax.ShapeDtypeStruct((N_MEM, 256), F32)
    dq, dkm, dvm = _pcall(body, name="mem_bwd", grid=(S // tq,),
                          in_specs=[pl.BlockSpec((tq, 256), lambda i: (i, M_T0 // 2)), msp,
                                    pl.BlockSpec((N_MEM, 256), lambda i: (0, 1)),
                                    pl.BlockSpec((tq, 256), lambda i: (i, 3)), ospec,
                                    pl.BlockSpec((tq, 256), lambda i: (i, 3))],
                          out_specs=[ospec, msp, msp], out_shape=[jax.ShapeDtypeStruct((S, 256), F32), md, md])(
        proj, mkv, mkv, dcat, lse, delta)
    return dq, jnp.concatenate([dkm, dvm], axis=1)


def combine_fwd(os_, lses, *, tm=512):
    S = os_[0].shape[0]

    def body(o0, o1, o2, l0, l1, l2, tok_ref):
        ls = [l0[...], l1[...], l2[...]]
        m = jnp.maximum(jnp.maximum(ls[0], ls[1]), ls[2])
        es = [jnp.exp(l - m) for l in ls]
        den = es[0] + es[1] + es[2]
        for g, o in enumerate((o0, o1, o2)):
            tok_ref[:, 256 * g:256 * (g + 1)] = (o[...] * (es[g] / den)).astype(BF)

    sp = pl.BlockSpec((tm, 256), lambda i: (i, 0))
    return _pcall(body, name="combine_fwd", grid=(S // tm,), in_specs=[sp] * 6,
                  out_specs=pl.BlockSpec((tm, 768), lambda i: (i, 0)),
                  out_shape=jax.ShapeDtypeStruct((S, 768), BF))(*os_, *lses)


def combine_bwd(dcat, os_, lses, *, tm=512):
    S = dcat.shape[0]

    def body(dt_ref, o0, o1, o2, l0, l1, l2, do0, do1, do2, de0, de1, de2):
        ls = [l0[...], l1[...], l2[...]]
        m = jnp.maximum(jnp.maximum(ls[0], ls[1]), ls[2])
        es = [jnp.exp(l - m) for l in ls]
        den = es[0] + es[1] + es[2]
        alphas = [e / den for e in es]
        lo = _lo((tm, LANES))
        dts = [dt_ref[:, 256 * g:256 * (g + 1)].astype(F32) for g in range(3)]
        dal = []
        for g, o in enumerate((o0, o1, o2)):
            pr = dts[g] * o[...]
            dal.append(jnp.concatenate([_half_sum(pr[:, _tile(0)], lo), _half_sum(pr[:, _tile(1)], lo)], axis=1))
        mix = alphas[0] * dal[0] + alphas[1] * dal[1] + alphas[2] * dal[2]
        for g, (do_ref, de_ref) in enumerate(((do0, de0), (do1, de1), (do2, de2))):
            do_ref[...] = (dts[g] * alphas[g]).astype(BF)
            de_ref[...] = alphas[g] * mix

    sp = pl.BlockSpec((tm, 256), lambda i: (i, 0))
    outs = _pcall(body, name="combine_bwd", grid=(S // tm,),
                  in_specs=[pl.BlockSpec((tm, 768), lambda i: (i, 0))] + [sp] * 6, out_specs=[sp] * 6,
                  out_shape=[jax.ShapeDtypeStruct((S, 256), BF)] * 3 + [jax.ShapeDtypeStruct((S, 256), F32)] * 3)(
        dcat, *os_, *lses)
    return outs[:3], outs[3:]


def _coords():
    return lax.axis_index("x"), lax.axis_index("y"), lax.axis_index("c")


def _other_chips(x, y):
    return [(1 - x, y), (x, 1 - y), (1 - x, 1 - y)]


HBM_SPEC = pl.BlockSpec(memory_space=pltpu.HBM)


def _gather_comm(ins, outs, send, recv, lsem, start):
    x, y, c = _coords()
    me = 2 * x + y
    for a in range(len(ins)):
        local = pltpu.make_async_copy(ins[a], outs[a].at[me], lsem.at[a])
        if start:
            local.start()
        for j, (px, py) in enumerate(_other_chips(x, y)):
            sems = dict(send_sem=send.at[3 * a + j], recv_sem=recv.at[3 * a + j], device_id=(px, py, c),
                        device_id_type=MESH)
            cp = pltpu.make_async_remote_copy(src_ref=ins[a], dst_ref=outs[a].at[me], **sems)
            if start:
                cp.start()
            else:
                pltpu.make_async_remote_copy(src_ref=ins[a], dst_ref=outs[a].at[2 * px + py], **sems).wait_recv()
                cp.wait_send()
        if not start:
            local.wait()


def _gather_out_shape(shards):
    return [jax.ShapeDtypeStruct((4,) + s.shape, s.dtype) for s in shards]


def _gather_sems(n):
    return [pltpu.SemaphoreType.DMA((3 * n,)), pltpu.SemaphoreType.DMA((3 * n,)), pltpu.SemaphoreType.DMA((n,))]


def gather_shards(shards):
    n = len(shards)

    def body(*refs):
        _gather_comm(refs[:n], refs[n:2 * n], *refs[2 * n:], start=True)
        _gather_comm(refs[:n], refs[n:2 * n], *refs[2 * n:], start=False)

    return pl.pallas_call(body, name="gather_shards", in_specs=[HBM_SPEC] * n, out_specs=[HBM_SPEC] * n,
                          out_shape=_gather_out_shape(shards), scratch_shapes=_gather_sems(n))(*shards)


def _scatter_comm(ins, outs, send, recv, start):
    x, y, c = _coords()
    for a in range(len(ins)):
        for j, (px, py) in enumerate(_other_chips(x, y)):
            cp = pltpu.make_async_remote_copy(src_ref=ins[a].at[2 * px + py], dst_ref=outs[a].at[j],
                                              send_sem=send.at[3 * a + j], recv_sem=recv.at[3 * a + j],
                                              device_id=(px, py, c), device_id_type=MESH)
            if start:
                cp.start()
            else:
                cp.wait_recv()
                cp.wait_send()


def _scatter_out_shape(parts):
    return [jax.ShapeDtypeStruct((3,) + p.shape[1:], p.dtype) for p in parts]


def _scatter_sems(n):
    return [pltpu.SemaphoreType.DMA((3 * n,)), pltpu.SemaphoreType.DMA((3 * n,))]


def scatter_grads(parts):
    n = len(parts)

    def body(*refs):
        _scatter_comm(refs[:n], refs[n:2 * n], *refs[2 * n:], start=True)
        _scatter_comm(refs[:n], refs[n:2 * n], *refs[2 * n:], start=False)

    return pl.pallas_call(body, name="scatter_grads", in_specs=[HBM_SPEC] * n, out_specs=[HBM_SPEC] * n,
                          out_shape=_scatter_out_shape(parts), scratch_shapes=_scatter_sems(n))(*parts)


def sibling_swap(arrs):
    n = len(arrs)

    def body(*refs):
        ins, outs = refs[:n], refs[n:2 * n]
        send, recv = refs[2 * n:]
        x, y, c = _coords()
        cps = []
        for a in range(n):
            cp = pltpu.make_async_remote_copy(src_ref=ins[a], dst_ref=outs[a], send_sem=send.at[a], recv_sem=recv.at[a],
                                              device_id=(x, y, 1 - c), device_id_type=MESH)
            cp.start()
            cps.append(cp)
        for cp in cps:
            cp.wait_recv()
        for cp in cps:
            cp.wait_send()

    return pl.pallas_call(
        body, name="sibling_swap", in_specs=[HBM_SPEC] * n, out_specs=[HBM_SPEC] * n,
        out_shape=[jax.ShapeDtypeStruct(a.shape, a.dtype) for a in arrs],
        scratch_shapes=[pltpu.SemaphoreType.DMA((n,)), pltpu.SemaphoreType.DMA((n,))])(*arrs)


def allsum_small(v):
    rows = v.shape[0]

    def body(v_ref, tot_ref, gath_ref, send, recv):
        x, y, c = _coords()
        me = 4 * x + 2 * y + c
        gath_ref[me] = v_ref[...]
        cps = []
        for k in range(1, 8):
            fx, fy, fc = (k >> 2) & 1, (k >> 1) & 1, k & 1
            peer = (1 - x if fx else x, 1 - y if fy else y, 1 - c if fc else c)
            cp = pltpu.make_async_remote_copy(src_ref=v_ref, dst_ref=gath_ref.at[me], send_sem=send.at[k - 1],
                                              recv_sem=recv.at[k - 1], device_id=peer, device_id_type=MESH)
            cp.start()
            cps.append(cp)
        for cp in cps:
            cp.wait_recv()
        for cp in cps:
            cp.wait_send()
        tot = gath_ref[0]
        for k in range(1, 8):
            tot = tot + gath_ref[k]
        tot_ref[...] = tot

    vm = pl.BlockSpec(memory_space=pltpu.VMEM)
    tot, _ = pl.pallas_call(
        body, name="allsum_small", in_specs=[vm], out_specs=[vm, vm],
        out_shape=[jax.ShapeDtypeStruct((rows, LANES), F32), jax.ShapeDtypeStruct((8, rows, LANES), F32)],
        scratch_shapes=[pltpu.SemaphoreType.DMA((7,)), pltpu.SemaphoreType.DMA((7,))])(v)
    return tot


def sum_parts(own, recv, *, tr=256):
    R, C = own.shape
    tr = min(tr, R)

    def body(o_ref, r_ref, out_ref):
        out_ref[...] = ((o_ref[...] + r_ref[0].astype(F32)) + r_ref[1].astype(F32)) + r_ref[2].astype(F32)

    sp = pl.BlockSpec((tr, C), lambda i: (i, 0))
    return _pcall(body, name="sum_parts", grid=(R // tr,),
                  in_specs=[sp, pl.BlockSpec((3, tr, C), lambda i: (0, i, 0))], out_specs=sp,
                  out_shape=jax.ShapeDtypeStruct((R, C), F32))(own, recv)


def adamw(w, ga, gb, m, v, *, tr=256):
    R, C = w.shape
    tr = min(tr, R)
    two = gb is not None

    def body(*refs):
        if two:
            w_ref, ga_ref, gb_ref, m_ref, v_ref, g_out, d_out, m_out, v_out = refs
            g = ga_ref[...] + gb_ref[...]
        else:
            w_ref, ga_ref, m_ref, v_ref, g_out, d_out, m_out, v_out = refs
            g = ga_ref[...]
        mn = B1 * m_ref[...] + (1.0 - B1) * g
        vn = B2 * v_ref[...] + (1.0 - B2) * (g * g)
        m_hat = mn / (1.0 - B1 ** STEP)
        v_hat = vn / (1.0 - B2 ** STEP)
        g_out[...] = g
        d_out[...] = -LR * (m_hat / (jnp.sqrt(v_hat) + AEPS) + WD * w_ref[...])
        m_out[...] = mn
        v_out[...] = vn

    sp = pl.BlockSpec((tr, C), lambda i: (i, 0))
    args = [w, ga, gb, m, v] if two else [w, ga, m, v]
    sd = jax.ShapeDtypeStruct((R, C), F32)
    return _pcall(body, name="adamw", grid=(R // tr,), in_specs=[sp] * len(args), out_specs=[sp] * 4,
                  out_shape=[sd] * 4)(*args)


def _rope_tables(S):
    def inv_freq(n_dims, theta):
        return theta ** (-(jnp.arange(0, n_dims, 2, dtype=jnp.float32) / n_dims))

    pos = lax.broadcasted_iota(jnp.int32, (S, LANES), 0)
    d = lax.broadcasted_iota(jnp.int32, (S, LANES), 1) % HD
    d1 = lax.iota(jnp.int32, LANES) % HD
    ang = pos.astype(F32) * inv_freq(HD // 4, ROPE_THETA)[d1 % 8][None, :]
    sin = jnp.sin(ang)
    partial = (jnp.where(d < 16, jnp.cos(ang), 1.0), jnp.where((d >= 8) & (d < 16), sin, 0.0),
               jnp.where(d < 8, -sin, 0.0))
    grid_pos = jnp.where(d < 32, pos // GRID_W, pos % GRID_W)
    ang = grid_pos.astype(F32) * inv_freq(HD // 2, AXIAL_THETA)[d1 % 16][None, :]
    sin = jnp.sin(ang)
    axial = (jnp.cos(ang), jnp.where(d % 32 >= 16, sin, 0.0), jnp.where(d % 32 < 16, -sin, 0.0))
    return partial, axial


def _pad_w_in(w):
    cols = [w[:, :768]]
    for base in (768, 960):
        for g in range(3):
            kg = w[:, base + g * HD:base + (g + 1) * HD]
            cols += [kg, kg]
    cols.append(w[:, 1152:])
    return jnp.concatenate(cols, axis=1)


def _unpad_dw_in(dw):
    cols = [dw[:, :768]]
    for t0 in (K_T0, V_T0):
        for g in range(3):
            b = (t0 + g) * LANES
            cols.append(dw[:, b:b + HD] + dw[:, b + HD:b + LANES])
    cols.append(dw[:, M_T0 * LANES:])
    return jnp.concatenate(cols, axis=1)


def _stats_t(a, tq):
    S = a.shape[0]
    t = a.reshape(S, -1, 2, HD)[:, :N_PAIRS, :, 0]
    t = jnp.transpose(t, (1, 2, 0))
    t = jnp.pad(t, ((0, 0), (0, 6), (0, 0)))
    return jnp.transpose(t.reshape(N_PAIRS, 8, S // tq, tq), (0, 2, 1, 3))


def _fold(t):
    return t[..., :HD] + t[..., HD:]


def kernel(x, mem, mem_norm_g, w_in, w_mem_kv, w_o, g_mix_pre, g_mix_post, attn_sink, qk_norm_g, w_gate_up, w_down, g_ffn_pre, g_ffn_post, loss_target, m_mem_norm_g, m_w_in, m_w_mem_kv, m_w_o, m_g_mix_pre, m_g_mix_post, m_attn_sink, m_qk_norm_g, m_w_gate_up, m_w_down, m_g_ffn_pre, m_g_ffn_post, v_mem_norm_g, v_w_in, v_w_mem_kv, v_w_o, v_g_mix_pre, v_g_mix_post, v_attn_sink, v_qk_norm_g, v_w_gate_up, v_w_down, v_g_ffn_pre, v_g_ffn_post):
    S = x.shape[1]
    depth = w_in.shape[0]
    xs, memx, tgt = x[0], mem[0], loss_target[0]
    tab_p, tab_a = _rope_tables(S)
    row = lambda a: a.reshape(1, -1)

    shards_bf = [w.astype(BF) for w in (w_in, w_mem_kv, w_o, w_gate_up, w_down)]
    layer_shards = lambda i: [s[i] for s in shards_bf]
    W_in, W_mkv, W_o, W_g, W_u, W_d = ([None] * depth for _ in range(6))

    def set_weights(i, gathered):
        gi, gm, go, gg, gd = gathered
        W_in[i] = jnp.concatenate([gi[s] for s in range(4)], axis=1)
        W_mkv[i] = jnp.concatenate([gm[s] for s in range(4)], axis=0)
        W_o[i] = jnp.concatenate([go[s] for s in range(4)], axis=0)
        W_g[i] = jnp.concatenate([gg[0], gg[1]], axis=1)
        W_u[i] = jnp.concatenate([gg[2], gg[3]], axis=1)
        W_d[i] = jnp.concatenate([gd[s] for s in range(4)], axis=0)

    set_weights(0, gather_shards(layer_shards(0)))
    mem_g = row(mem_norm_g)
    zero_sink = jnp.zeros((12,), F32)
    qkg = jnp.pad(jnp.concatenate([qk_norm_g[0], qk_norm_g[0]], axis=1), ((0, 6), (0, 0)))
    no_qkg = jnp.zeros((8, LANES), F32)

    saved = []
    cur = xs
    for i in range(depth):
        kind = i % 3
        wp = _pad_w_in(W_in[i])
        sv = dict(x=cur, wp=wp)
        if kind == 1:
            h1, proj, raw, nrm = inproj_fwd(cur, row(g_mix_pre[i]), wp, tab_a, qkg, axial=True)
            sv["raw"] = raw
        else:
            h1, proj = inproj_fwd(cur, row(g_mix_pre[i]), wp, tab_p, no_qkg, axial=False)
        if kind == 0:
            tok, lse = banded_fwd(proj, attn_sink[i // 3], d=1, R=A_RADIUS, TQ=2 * A_RADIUS, pair0=0, npairs=6,
                                  use_sink=True, o_dtype=BF)
        elif kind == 1:
            bound = jnp.sqrt(jnp.max(nrm[0]) * jnp.max(nrm[1])) * LN2
            tok, lse = flash_fwd(proj, bound)
        else:
            os_, lses = [], []
            for g, (window, dil) in enumerate(C_GROUPS):
                rad = window // (2 * dil)
                o_g, l_g = banded_fwd(proj, zero_sink, d=dil, R=rad, TQ=2 * rad, pair0=2 * g, npairs=2, use_sink=False,
                                      o_dtype=F32)
                os_.append(o_g)
                lses.append(l_g)
            tok = combine_fwd(os_, lses)
            sv["os"], lse = os_, lses
        mem_n, mkv = norm_mm(memx, mem_g, W_mkv[i], tm=N_MEM)
        mo, mlse = mem_fwd(proj, mkv)
        cat = jnp.concatenate([tok, mo], axis=1)
        o, x2 = mm_norm_res(cat, W_o[i], row(g_mix_post[i]), cur)
        (h2, gate, up, act), gathered = ffn_up_fwd(x2, row(g_ffn_pre[i]), W_g[i], W_u[i],
                                                   layer_shards(i + 1) if i + 1 < depth else ())
        if i + 1 < depth:
            set_weights(i + 1, gathered)
        f, x3 = mm_norm_res(act, W_d[i], row(g_ffn_post[i]), x2)
        sv.update(h1=h1, proj=proj, lse=lse, mem_n=mem_n, mkv=mkv, mlse=mlse, cat=cat, o=o, x2=x2, h2=h2, gate=gate,
                  up=up, act=act, f=f)
        saved.append(sv)
        cur = x3

    dcur, loss_vec = loss_bwd(cur, tgt)

    grad_parts, grad_recv = [None] * depth, [None] * depth
    dg_pre, dg_post, dg_fpre, dg_fpost = [None] * depth, [None] * depth, [None] * depth, [None] * depth
    dg_mem = jnp.zeros((1, D), F32)
    dsinks, dqk = {}, None
    for i in reversed(range(depth)):
        sv = saved[i]
        kind = i % 3
        proj = sv["proj"]
        pending = [p.astype(BF) for p in grad_parts[i + 1]] if i + 1 < depth else ()
        (df, dgate, dup, dg_fpost[i]), got = normbwd_mm_swiglu(dcur, sv["f"], row(g_ffn_post[i]), W_d[i], sv["gate"],
                                                               sv["up"], pending)
        if i + 1 < depth:
            grad_recv[i + 1] = got
        dx2, dg_fpre[i] = mm_nt_normbwd_res([(dgate, W_g[i]), (dup, W_u[i])], sv["x2"], row(g_ffn_pre[i]), dcur, nk=2)
        dW_d = mm_acc(sv["act"], df, tk=1408, tn=D, ts=1024)
        dW_gu = jnp.concatenate([mm_acc(sv["h2"], dgate, tk=D, tn=1408, ts=1024),
                                 mm_acc(sv["h2"], dup, tk=D, tn=1408, ts=1024)], axis=1)
        do, dcat, delta, dg_post[i] = normbwd_mm_cat(dx2, sv["o"], row(g_mix_post[i]), W_o[i], sv["cat"])
        dW_o = mm_acc(sv["cat"], do, tk=D, tn=D, ts=1024)
        dqm, dmkv = mem_bwd(proj, sv["mkv"], dcat, sv["mlse"], delta)
        dmkv = dmkv.astype(BF)
        (dgm,) = mm_nt_normbwd_res([(dmkv, W_mkv[i])], memx, mem_g, None, tm=N_MEM)
        dg_mem = dg_mem + dgm
        dW_mkv = mm_acc(sv["mem_n"], dmkv, tk=D, tn=512, ts=N_MEM)
        if kind == 0:
            sink = attn_sink[i // 3]
            args = dict(d=1, R=A_RADIUS, pair0=0, npairs=6)
            dq, dsk = banded_bwd_dq(proj, dcat, sv["lse"], delta, sink, TQ=2 * A_RADIUS, use_sink=True, **args)
            dkp, dvp = banded_bwd_dkv(proj, dcat, sv["lse"], delta, TK=2 * A_RADIUS, **args)
            dsinks[i // 3] = dsk.reshape(3, 8, 2, 2, HD)[:, 0, :, :, 0].reshape(12)
        elif kind == 1:
            dq, dkp, dvp = flash_bwd(proj, dcat, _stats_t(sv["lse"], 512), _stats_t(delta, 512))
        else:
            dos, des = combine_bwd(dcat, sv["os"], sv["lse"])
            dqs, dks, dvs = [], [], []
            for g, (window, dil) in enumerate(C_GROUPS):
                rad = window // (2 * dil)
                args = dict(d=dil, R=rad, pair0=2 * g, npairs=2)
                dq_g, _ = banded_bwd_dq(proj, dos[g], sv["lse"][g], des[g], zero_sink, TQ=2 * rad, use_sink=False, **args)
                dk_g, dv_g = banded_bwd_dkv(proj, dos[g], sv["lse"][g], des[g], TK=2 * rad, **args)
                dqs.append(dq_g)
                dks.append(dk_g)
                dvs.append(dv_g)
            dq, dkp, dvp = (jnp.concatenate(t, axis=1) for t in (dqs, dks, dvs))
        if kind == 1:
            dcur, dproj, dg_pre[i], dqk_t = inproj_bwd(dq, dkp, dvp, dqm, tab_a, sv["raw"], qkg, sv["wp"], sv["x"],
                                                       row(g_mix_pre[i]), dx2, axial=True)
            dqk = _fold(dqk_t[:2]).reshape(1, 2, HD)
        else:
            dcur, dproj, dg_pre[i], _ = inproj_bwd(dq, dkp, dvp, dqm, tab_p, proj, no_qkg, sv["wp"],
                                                   sv["x"], row(g_mix_pre[i]), dx2, axial=False)
        dW_in = _unpad_dw_in(mm_acc(sv["h1"], dproj, tk=D, tn=896, ts=1024))
        grad_parts[i] = [jnp.transpose(dW_in.reshape(D, 4, IN_W // 4), (1, 0, 2)), dW_mkv.reshape(4, D // 4, 512),
                         dW_o.reshape(4, D // 4, D), jnp.transpose(dW_gu.reshape(D, 4, 2 * DFF // 4), (1, 0, 2)),
                         dW_d.reshape(4, DFF // 4, D)]
    grad_recv[0] = scatter_grads([p.astype(BF) for p in grad_parts[0]])

    x_i, y_i, _ = _coords()
    me = 2 * x_i + y_i
    big = [(w_in, m_w_in, v_w_in), (w_mem_kv, m_w_mem_kv, v_w_mem_kv), (w_o, m_w_o, v_w_o),
           (w_gate_up, m_w_gate_up, v_w_gate_up), (w_down, m_w_down, v_w_down)]
    parts = []
    for a, (w, _, _) in enumerate(big):
        C = w.shape[-1]
        own = jnp.stack([lax.dynamic_index_in_dim(grad_parts[l][a], me, 0, keepdims=False) for l in range(depth)])
        rc = jnp.stack([grad_recv[l][a] for l in range(depth)], axis=1)
        parts.append(sum_parts(own.reshape(-1, C), rc.reshape(3, -1, C)))
    sibs = sibling_swap(parts)
    big_out = []
    for (w, m, v), pa, pb in zip(big, parts, sibs):
        C = w.shape[-1]
        outs = adamw(w.reshape(-1, C), pa, pb, m.reshape(-1, C), v.reshape(-1, C))
        big_out.append([o.reshape(w.shape) for o in outs])

    small_w = [mem_norm_g, g_mix_pre, g_mix_post, attn_sink, qk_norm_g, g_ffn_pre, g_ffn_post]
    small_m = [m_mem_norm_g, m_g_mix_pre, m_g_mix_post, m_attn_sink, m_qk_norm_g, m_g_ffn_pre, m_g_ffn_post]
    small_v = [v_mem_norm_g, v_g_mix_pre, v_g_mix_post, v_attn_sink, v_qk_norm_g, v_g_ffn_pre, v_g_ffn_post]
    small_g = [dg_mem.reshape(D), jnp.concatenate(dg_pre, axis=0), jnp.concatenate(dg_post, axis=0),
               jnp.stack([dsinks[k] for k in sorted(dsinks)]), dqk, jnp.concatenate(dg_fpre, axis=0),
               jnp.concatenate(dg_fpost, axis=0)]
    sizes = [a.size for a in small_w]
    total = sum(sizes)
    rows_s = -(-(total + LANES) // (8 * LANES)) * 8

    def pack(arrs, extra=None):
        flat = jnp.concatenate([a.reshape(-1).astype(F32) for a in arrs])
        flat = jnp.pad(flat, (0, rows_s * LANES - LANES - total))
        tail = jnp.zeros((LANES,), F32) if extra is None else extra.reshape(LANES)
        return jnp.concatenate([flat, tail]).reshape(rows_s, LANES)

    tot = allsum_small(pack(small_g, loss_vec))
    loss = jnp.sum(tot[rows_s - 1])
    s_out = adamw(pack(small_w), tot, None, pack(small_m), pack(small_v))

    def unpack(buf):
        flat = buf.reshape(-1)
        out, off = [], 0
        for a, n in zip(small_w, sizes):
            out.append(flat[off:off + n].reshape(a.shape))
            off += n
        return out

    sg, sd_, sm, sv_ = (unpack(b) for b in s_out)

    def ordered(k):
        sm_ = (sg, sd_, sm, sv_)[k]
        b = [bo[k] for bo in big_out]
        return [sm_[0], b[0], b[1], b[2], sm_[1], sm_[2], sm_[3], sm_[4], b[3], b[4], sm_[5], sm_[6]]

    dx_out = dcur.reshape(1, S, D)
    return (loss, dx_out, *ordered(0), *ordered(1), *ordered(2), *ordered(3))
```

```python
import functools

import jax
import jax.numpy as jnp
from jax import lax
from jax.experimental import pallas as pl
from jax.experimental.pallas import tpu as pltpu

F32 = jnp.float32
BF = jnp.bfloat16

D = 1024
HD = 64
LANES = 128
N_PAIRS = 6
DFF = 2816
IN_W = 1408
PW = 14 * LANES
K_T0, V_T0, M_T0 = 6, 9, 12
EPS = 1e-6
SCALE = HD ** -0.5
NEG = -1e30
LOG2E = 1.4426950408889634
LN2 = 0.6931471805599453
MAX_PLAIN_SCORE = 40.0
ROPE_THETA = 500000.0
AXIAL_THETA = 10000.0
GRID_W = 64
A_RADIUS = 128
C_GROUPS = ((128, 1), (512, 4), (2048, 16))
N_MEM = 256
LR, B1, B2, AEPS, WD, STEP = 0.001, 0.9, 0.999, 1e-08, 0.01, 10
VMEM_LIMIT = 56 * 1024 * 1024
MESH = pl.DeviceIdType.MESH


def _pcall(body, *, name, grid, in_specs, out_specs, out_shape, scratch=()):
    return pl.pallas_call(
        body, name=name, grid=grid, in_specs=in_specs, out_specs=out_specs, out_shape=out_shape,
        scratch_shapes=scratch,
        compiler_params=pltpu.CompilerParams(dimension_semantics=("arbitrary",) * len(grid),
                                             vmem_limit_bytes=VMEM_LIMIT))


def _dot(a, b):
    return lax.dot_general(a, b, (((1,), (0,)), ((), ())), preferred_element_type=F32)


def _dot_nt(a, b):
    return lax.dot_general(a, b, (((1,), (1,)), ((), ())), preferred_element_type=F32)


def _lo(shape):
    return lax.broadcasted_iota(jnp.int32, shape, len(shape) - 1) < HD


def _half_sum(x, lo):
    a = jnp.sum(jnp.where(lo, x, 0.0), axis=-1, keepdims=True)
    b = jnp.sum(jnp.where(lo, 0.0, x), axis=-1, keepdims=True)
    return jnp.where(lo, a, b)


def _col(tile, lane):
    idx = lax.broadcasted_iota(jnp.int32, tile.shape, 1)
    return jnp.sum(jnp.where(idx == lane, tile, 0.0), axis=-1, keepdims=True)


def _split(t, lo):
    z = jnp.zeros_like(t)
    return jnp.where(lo, t, z), jnp.where(lo, z, t)


def _rms(xf, g):
    r = lax.rsqrt(jnp.mean(xf * xf, axis=-1, keepdims=True) + EPS)
    return xf * r * g


def _rms_bwd(xf, g, dy):
    r = lax.rsqrt(jnp.mean(xf * xf, axis=-1, keepdims=True) + EPS)
    xr = xf * r
    dg = jnp.sum(dy * xr, axis=0, keepdims=True)
    t = dy * g
    return r * (t - xr * jnp.mean(xr * t, axis=-1, keepdims=True)), dg


def _rope_fwd(y, c, s1, s2, sh):
    return y * c + pltpu.roll(y, sh, 1) * s1 + pltpu.roll(y, LANES - sh, 1) * s2


def _rope_bwd(dy, c, s1, s2, sh):
    return dy * c + pltpu.roll(dy * s1, LANES - sh, 1) + pltpu.roll(dy * s2, sh, 1)


def _tile(t):
    return slice(t * LANES, (t + 1) * LANES)


def inproj_fwd(x, g, w_pad, tabs, qkg, *, axial, tm=512):
    S = x.shape[0]
    sh = 16 if axial else 8

    def body(x_ref, g_ref, w_ref, c_ref, s1_ref, s2_ref, qkg_ref, h_ref, p_ref, *extra):
        h = _rms(x_ref[...], g_ref[...]).astype(BF)
        h_ref[...] = h
        acc = _dot(h, w_ref[...])
        c, s1, s2 = c_ref[...], s1_ref[...], s2_ref[...]
        lo = _lo((tm, LANES))
        if axial:
            raw_ref, nrm_ref = extra

            @pl.when(pl.program_id(0) == 0)
            def _():
                nrm_ref[...] = jnp.zeros_like(nrm_ref)

        for t in range(14):
            y = acc[:, _tile(t)]
            if t < V_T0:
                if axial:
                    raw_ref[:, _tile(t)] = y.astype(BF)
                    gt = qkg_ref[0:1, :] if t < K_T0 else qkg_ref[1:2, :]
                    y = y * lax.rsqrt(_half_sum(y * y, lo) * (1.0 / HD) + EPS) * gt
                y = _rope_fwd(y, c, s1, s2, sh)
            if t < K_T0:
                y = y * (SCALE * LOG2E if axial else SCALE)
            elif t >= M_T0:
                y = y * SCALE
            yb = y.astype(BF)
            p_ref[:, _tile(t)] = yb
            if axial and t < V_T0:
                yf = yb.astype(F32)
                n2 = jnp.max(_half_sum(yf * yf, lo), axis=0, keepdims=True)
                r = 0 if t < K_T0 else 1
                nrm_ref[r:r + 1, :] = jnp.maximum(nrm_ref[r:r + 1, :], n2)

    row = lambda w: pl.BlockSpec((tm, w), lambda i: (i, 0))
    full = lambda a: pl.BlockSpec(a.shape, lambda i: (0, 0))
    out_shape = [jax.ShapeDtypeStruct((S, D), BF), jax.ShapeDtypeStruct((S, PW), BF)]
    out_specs = [row(D), row(PW)]
    if axial:
        out_shape += [jax.ShapeDtypeStruct((S, V_T0 * LANES), BF), jax.ShapeDtypeStruct((8, LANES), F32)]
        out_specs += [row(V_T0 * LANES), pl.BlockSpec((8, LANES), lambda i: (0, 0))]
    return _pcall(body, name="inproj_fwd_axial" if axial else "inproj_fwd", grid=(S // tm,),
                  in_specs=[row(D), full(g), full(w_pad), row(LANES), row(LANES), row(LANES), full(qkg)],
                  out_specs=out_specs, out_shape=out_shape)(x, g, w_pad, *tabs, qkg)


def norm_mm(x, g, w, *, tm):
    S, N = x.shape[0], w.shape[1]

    def body(x_ref, g_ref, w_ref, h_ref, y_ref):
        h = _rms(x_ref[...], g_ref[...]).astype(BF)
        h_ref[...] = h
        y_ref[...] = _dot(h, w_ref[...]).astype(BF)

    return _pcall(body, name="norm_mm", grid=(S // tm,),
                  in_specs=[pl.BlockSpec((tm, D), lambda i: (i, 0)), pl.BlockSpec(g.shape, lambda i: (0, 0)),
                            pl.BlockSpec(w.shape, lambda i: (0, 0))],
                  out_specs=[pl.BlockSpec((tm, D), lambda i: (i, 0)), pl.BlockSpec((tm, N), lambda i: (i, 0))],
                  out_shape=[jax.ShapeDtypeStruct((S, D), BF), jax.ShapeDtypeStruct((S, N), BF)])(x, g, w)


def ffn_up_fwd(x, g, wg, wu, next_shards=(), *, tm=512, tn=1408):
    S = x.shape[0]
    n = len(next_shards)
    last = (S // tm - 1, DFF // tn - 1)

    def body(x_ref, g_ref, wg_ref, wu_ref, *refs):
        comm_in, (h_ref, gate_ref, up_ref, a_ref), comm_out = refs[:n], refs[n:n + 4], refs[n + 4:2 * n + 4]
        h_scr, sems = refs[2 * n + 4], refs[2 * n + 5:]
        if n:
            @pl.when((pl.program_id(0) == 0) & (pl.program_id(1) == 0))
            def _():
                _gather_comm(comm_in, comm_out, *sems, start=True)

        @pl.when(pl.program_id(1) == 0)
        def _():
            h = _rms(x_ref[...], g_ref[...]).astype(BF)
            h_scr[...] = h
            h_ref[...] = h

        h = h_scr[...]
        gate = _dot(h, wg_ref[...])
        up = _dot(h, wu_ref[...])
        gate_ref[...] = gate.astype(BF)
        up_ref[...] = up.astype(BF)
        a_ref[...] = (gate * pl.reciprocal(1.0 + jnp.exp(-gate), approx=True) * up).astype(BF)
        if n:
            @pl.when((pl.program_id(0) == last[0]) & (pl.program_id(1) == last[1]))
            def _():
                _gather_comm(comm_in, comm_out, *sems, start=False)

    rowd = pl.BlockSpec((tm, D), lambda i, j: (i, 0))
    wsp = pl.BlockSpec((D, tn), lambda i, j: (0, j))
    osp = pl.BlockSpec((tm, tn), lambda i, j: (i, j))
    sd = jax.ShapeDtypeStruct((S, DFF), BF)
    outs = _pcall(body, name="ffn_up_fwd_gather" if n else "ffn_up_fwd", grid=(S // tm, DFF // tn),
                  in_specs=[rowd, pl.BlockSpec(g.shape, lambda i, j: (0, 0)), wsp, wsp] + [HBM_SPEC] * n,
                  out_specs=[rowd, osp, osp, osp] + [HBM_SPEC] * n,
                  out_shape=[jax.ShapeDtypeStruct((S, D), BF), sd, sd, sd] + _gather_out_shape(next_shards),
                  scratch=[pltpu.VMEM((tm, D), BF)] + (_gather_sems(n) if n else []))(x, g, wg, wu, *next_shards)
    return outs[:4], outs[4:]


def mm_norm_res(a, w, g, res, *, tm=512):
    S, K = a.shape

    def body(a_ref, w_ref, g_ref, res_ref, y_ref, o_ref):
        y = _dot(a_ref[...], w_ref[...])
        y_ref[...] = y
        o_ref[...] = res_ref[...] + _rms(y, g_ref[...])

    rowd = pl.BlockSpec((tm, D), lambda i: (i, 0))
    sd = jax.ShapeDtypeStruct((S, D), F32)
    return _pcall(body, name="mm_norm_res", grid=(S // tm,),
                  in_specs=[pl.BlockSpec((tm, K), lambda i: (i, 0)), pl.BlockSpec(w.shape, lambda i: (0, 0)),
                            pl.BlockSpec(g.shape, lambda i: (0, 0)), rowd],
                  out_specs=[rowd, rowd], out_shape=[sd, sd])(a, w, g, res)


def loss_bwd(y, tgt, *, tm=512):
    S = y.shape[0]

    def body(y_ref, t_ref, dy_ref, l_ref):
        @pl.when(pl.program_id(0) == 0)
        def _():
            l_ref[...] = jnp.zeros_like(l_ref)

        e = y_ref[...] - t_ref[...]
        dy_ref[...] = e * (1.0 / D)
        col = jnp.sum(e * e, axis=0, keepdims=True)
        part = col[:, _tile(0)]
        for t in range(1, D // LANES):
            part = part + col[:, _tile(t)]
        l_ref[...] += part * (0.5 / D)

    rowd = pl.BlockSpec((tm, D), lambda i: (i, 0))
    return _pcall(body, name="loss_bwd", grid=(S // tm,), in_specs=[rowd, rowd],
                  out_specs=[rowd, pl.BlockSpec((1, LANES), lambda i: (0, 0))],
                  out_shape=[jax.ShapeDtypeStruct((S, D), F32), jax.ShapeDtypeStruct((1, LANES), F32)])(y, tgt)


def normbwd_mm_cat(dy, ysaved, g, w, cat, *, tm=512):
    S = dy.shape[0]

    def body(dy_ref, y_ref, g_ref, w_ref, cat_ref, d_ref, dcat_ref, delta_ref, dg_ref):
        @pl.when(pl.program_id(0) == 0)
        def _():
            dg_ref[...] = jnp.zeros_like(dg_ref)

        d, dg = _rms_bwd(y_ref[...], g_ref[...], dy_ref[...])
        dg_ref[...] += dg
        d = d.astype(BF)
        d_ref[...] = d
        z = _dot_nt(d, w_ref[...])
        dcat_ref[...] = z.astype(BF)
        lo = _lo((tm, LANES))
        for t in range(D // LANES):
            delta_ref[:, _tile(t)] = _half_sum(z[:, _tile(t)] * cat_ref[:, _tile(t)].astype(F32), lo)

    rowd = pl.BlockSpec((tm, D), lambda i: (i, 0))
    return _pcall(body, name="normbwd_mm_cat", grid=(S // tm,),
                  in_specs=[rowd, rowd, pl.BlockSpec(g.shape, lambda i: (0, 0)),
                            pl.BlockSpec(w.shape, lambda i: (0, 0)), rowd],
                  out_specs=[rowd, rowd, rowd, pl.BlockSpec((1, D), lambda i: (0, 0))],
                  out_shape=[jax.ShapeDtypeStruct((S, D), BF), jax.ShapeDtypeStruct((S, D), BF),
                             jax.ShapeDtypeStruct((S, D), F32), jax.ShapeDtypeStruct((1, D), F32)])(dy, ysaved, g, w, cat)


def normbwd_mm_swiglu(dy, fsaved, g, wd, gate, up, grad_parts=(), *, tm=512, tn=1408):
    S = dy.shape[0]
    n = len(grad_parts)
    last = (S // tm - 1, DFF // tn - 1)

    def body(dy_ref, f_ref, g_ref, w_ref, gate_ref, up_ref, *refs):
        comm_in, (df_ref, dgate_ref, dup_ref, dg_ref), comm_out = refs[:n], refs[n:n + 4], refs[n + 4:2 * n + 4]
        d_scr, sems = refs[2 * n + 4], refs[2 * n + 5:]
        i, j = pl.program_id(0), pl.program_id(1)

        @pl.when((i == 0) & (j == 0))
        def _():
            dg_ref[...] = jnp.zeros_like(dg_ref)
            if n:
                _scatter_comm(comm_in, comm_out, *sems, start=True)

        @pl.when(j == 0)
        def _():
            d, dg = _rms_bwd(f_ref[...], g_ref[...], dy_ref[...])
            dg_ref[...] += dg
            d_scr[...] = d.astype(BF)
            df_ref[...] = d.astype(BF)

        da = _dot_nt(d_scr[...], w_ref[...])
        gt = gate_ref[...].astype(F32)
        sig = pl.reciprocal(1.0 + jnp.exp(-gt), approx=True)
        dgate_ref[...] = (da * up_ref[...].astype(F32) * (sig * (1.0 + gt * (1.0 - sig)))).astype(BF)
        dup_ref[...] = (da * (gt * sig)).astype(BF)
        if n:
            @pl.when((i == last[0]) & (j == last[1]))
            def _():
                _scatter_comm(comm_in, comm_out, *sems, start=False)

    rowd = pl.BlockSpec((tm, D), lambda i, j: (i, 0))
    osp = pl.BlockSpec((tm, tn), lambda i, j: (i, j))
    sd = jax.ShapeDtypeStruct((S, DFF), BF)
    outs = _pcall(body, name="normbwd_mm_swiglu_scatter" if n else "normbwd_mm_swiglu", grid=(S // tm, DFF // tn),
                  in_specs=[rowd, rowd, pl.BlockSpec(g.shape, lambda i, j: (0, 0)),
                            pl.BlockSpec((tn, D), lambda i, j: (j, 0)), osp, osp] + [HBM_SPEC] * n,
                  out_specs=[rowd, osp, osp, pl.BlockSpec((1, D), lambda i, j: (0, 0))] + [HBM_SPEC] * n,
                  out_shape=[jax.ShapeDtypeStruct((S, D), BF), sd, sd, jax.ShapeDtypeStruct((1, D), F32)]
                  + _scatter_out_shape(grad_parts),
                  scratch=[pltpu.VMEM((tm, D), BF)] + (_scatter_sems(n) if n else []))(
        dy, fsaved, g, wd, gate, up, *grad_parts)
    return outs[:4], outs[4:]


def mm_nt_normbwd_res(parts, xin, g, dres, *, tm=512):
    S = xin.shape[0]
    npart = len(parts)
    has_res = dres is not None

    def body(*refs):
        prefs = refs[:2 * npart]
        x_ref, g_ref = refs[2 * npart:2 * npart + 2]
        rest = refs[2 * npart + 2:]
        if has_res:
            dres_ref, dx_ref, dg_ref = rest
        else:
            (dg_ref,) = rest

        @pl.when(pl.program_id(0) == 0)
        def _():
            dg_ref[...] = jnp.zeros_like(dg_ref)

        z = _dot_nt(prefs[0][...], prefs[1][...])
        for p in range(1, npart):
            z = z + _dot_nt(prefs[2 * p][...], prefs[2 * p + 1][...])
        dx, dg = _rms_bwd(x_ref[...], g_ref[...], z)
        dg_ref[...] += dg
        if has_res:
            dx_ref[...] = dres_ref[...] + dx

    rowd = pl.BlockSpec((tm, D), lambda i: (i, 0))
    in_specs, args = [], []
    for dy, w in parts:
        in_specs += [pl.BlockSpec((tm, dy.shape[1]), lambda i: (i, 0)),
                     pl.BlockSpec(w.shape, lambda i: (0, 0), pipeline_mode=pl.Buffered(1))]
        args += [dy, w]
    in_specs += [rowd, pl.BlockSpec(g.shape, lambda i: (0, 0))]
    args += [xin, g]
    out_specs = [pl.BlockSpec((1, D), lambda i: (0, 0))]
    out_shape = [jax.ShapeDtypeStruct((1, D), F32)]
    if has_res:
        in_specs.append(rowd)
        args.append(dres)
        out_specs.insert(0, rowd)
        out_shape.insert(0, jax.ShapeDtypeStruct((S, D), F32))
    return _pcall(body, name="mm_nt_normbwd_res" if has_res else "mm_nt_normbwd", grid=(S // tm,),
                  in_specs=in_specs, out_specs=out_specs, out_shape=out_shape)(*args)


def inproj_bwd(dq, dkp, dvp, dqm, tabs, raw, qkg, w_pad, xin, g, dres, *, axial, tm=512):
    S = xin.shape[0]
    sh = 16 if axial else 8

    def body(dq_ref, dk_ref, dv_ref, dm_ref, c_ref, s1_ref, s2_ref, raw_ref, qkg_ref, w_ref, x_ref, g_ref, dres_ref,
             dx_ref, dp_ref, dg_ref, dqk_ref):
        @pl.when(pl.program_id(0) == 0)
        def _():
            dg_ref[...] = jnp.zeros_like(dg_ref)
            dqk_ref[...] = jnp.zeros_like(dqk_ref)

        c, s1, s2 = c_ref[...], s1_ref[...], s2_ref[...]
        lo = _lo((tm, LANES))
        for t in range(14):
            if t < K_T0:
                y = dq_ref[:, _tile(t)] * SCALE
            elif t < V_T0:
                y = dk_ref[:, _tile(2 * (t - K_T0))] + dk_ref[:, _tile(2 * (t - K_T0) + 1)]
                if axial:
                    y = y * LN2
            elif t < M_T0:
                y = dv_ref[:, _tile(2 * (t - V_T0))] + dv_ref[:, _tile(2 * (t - V_T0) + 1)]
            else:
                y = dm_ref[:, _tile(t - M_T0)] * SCALE
            if t < V_T0:
                y = _rope_bwd(y, c, s1, s2, sh)
                if axial:
                    row = 0 if t < K_T0 else 1
                    xr = raw_ref[:, _tile(t)].astype(F32)
                    r = lax.rsqrt(_half_sum(xr * xr, lo) * (1.0 / HD) + EPS)
                    xn = xr * r
                    dqk_ref[row:row + 1, :] += jnp.sum(y * xn, axis=0, keepdims=True)
                    tt = y * qkg_ref[row:row + 1, :]
                    y = r * (tt - xn * (_half_sum(xn * tt, lo) * (1.0 / HD)))
            dp_ref[:, _tile(t)] = y.astype(BF)
        z = _dot_nt(dp_ref[...], w_ref[...])
        dx, dg = _rms_bwd(x_ref[...], g_ref[...], z)
        dg_ref[...] += dg
        dx_ref[...] = dres_ref[...] + dx

    row = lambda w: pl.BlockSpec((tm, w), lambda i: (i, 0))
    full = lambda a: pl.BlockSpec(a.shape, lambda i: (0, 0))
    return _pcall(body, name="inproj_bwd_axial" if axial else "inproj_bwd", grid=(S // tm,),
                  in_specs=[row(768), row(768), row(768), row(256), row(LANES), row(LANES), row(LANES),
                            row(raw.shape[1] if axial else LANES), full(qkg), full(w_pad), row(D), full(g), row(D)],
                  out_specs=[row(D), row(PW), pl.BlockSpec((1, D), lambda i: (0, 0)),
                             pl.BlockSpec((8, LANES), lambda i: (0, 0))],
                  out_shape=[jax.ShapeDtypeStruct((S, D), F32), jax.ShapeDtypeStruct((S, PW), BF),
                             jax.ShapeDtypeStruct((1, D), F32), jax.ShapeDtypeStruct((8, LANES), F32)])(
        dq, dkp, dvp, dqm, *tabs, raw, qkg, w_pad, xin, g, dres)


def mm_acc(a, b, *, tk, tn, ts):
    S, K = a.shape
    N = b.shape[1]
    ts = min(ts, S)

    def body(a_ref, b_ref, o_ref):
        z = lax.dot_general(a_ref[...], b_ref[...], (((0,), (0,)), ((), ())), preferred_element_type=F32)

        @pl.when(pl.program_id(2) == 0)
        def _():
            o_ref[...] = z

        @pl.when(pl.program_id(2) > 0)
        def _():
            o_ref[...] += z

    return _pcall(body, name="mm_acc", grid=(K // tk, N // tn, S // ts),
                  in_specs=[pl.BlockSpec((ts, tk), lambda k, n, s: (s, k)), pl.BlockSpec((ts, tn), lambda k, n, s: (s, n))],
                  out_specs=pl.BlockSpec((tk, tn), lambda k, n, s: (k, n)),
                  out_shape=jax.ShapeDtypeStruct((K, N), F32))(a, b)


def _band_specs(L, d, R, T, width, bw, col_of):
    n = T // R
    nb = width // bw
    last = L // R - 1
    col = lambda g, r: r * nb + col_of(g)
    return [pl.BlockSpec((R, bw), lambda g, r, i: (jnp.maximum(i * n - 1, 0), col(g, r))),
            pl.BlockSpec((T, bw), lambda g, r, i: (i, col(g, r))),
            pl.BlockSpec((R, bw), lambda g, r, i: (jnp.minimum((i + 1) * n, last), col(g, r)))]


def _band_tile(R, L):
    return min(max(2 * R, 256), L)


def _band_bias(T, R):
    w = lax.broadcasted_iota(jnp.int32, (T, T + 2 * R), 1)
    c = lax.broadcasted_iota(jnp.int32, (T, T + 2 * R), 0)
    return jnp.where(jnp.abs(w - R - c) <= R, 0.0, NEG).astype(F32)


def _edge_bias(i, T, R, L):
    wpos = i * T - R + lax.broadcasted_iota(jnp.int32, (1, T + 2 * R), 1)
    return jnp.where((wpos >= 0) & (wpos < L), 0.0, NEG)


def banded_fwd(proj, sink, *, d, R, TQ, pair0, npairs, use_sink, o_dtype):
    S = proj.shape[0]
    L = S // d
    pv = proj.reshape(L, d * PW)
    ow = npairs * LANES

    def body(sink_ref, bias_ref, q_ref, kp, kc, kn, vp, vc, vn, o_ref, lse_ref):
        g, i = pl.program_id(0), pl.program_id(2)
        bias = bias_ref[...] + _edge_bias(i, TQ, R, L)
        lo = _lo((TQ, LANES))
        kw = jnp.concatenate([kp[...], kc[...], kn[...]], axis=0)
        vw = jnp.concatenate([vp[...], vc[...], vn[...]], axis=0)
        v_lo, v_hi = _split(vw, _lo(vw.shape))
        vcat = jnp.concatenate([v_lo, v_hi], axis=0)
        for t in range(2):
            qa, qb = _split(q_ref[:, _tile(t)], lo)
            ps, dens, lses = [], [], []
            for h, qh in enumerate((qa, qb)):
                s = _dot_nt(qh, kw) + bias
                m = jnp.max(s, axis=-1, keepdims=True)
                if use_sink:
                    sk = sink_ref[2 * (pair0 + 2 * g + t) + h]
                    m = jnp.maximum(m, sk)
                e = jnp.exp(s - m)
                den = jnp.sum(e, axis=-1, keepdims=True)
                if use_sink:
                    den = den + jnp.exp(sk - m)
                ps.append(e.astype(BF))
                dens.append(den)
                lses.append(m + jnp.log(den))
            o = _dot(jnp.concatenate(ps, axis=1), vcat)
            o_ref[:, _tile(t)] = (o / jnp.where(lo, dens[0], dens[1])).astype(o_dtype)
            lse_ref[:, _tile(t)] = jnp.where(lo, lses[0], lses[1])

    g0 = pair0 // 2
    qspec = pl.BlockSpec((TQ, 2 * LANES), lambda g, r, i: (i, r * 7 + g0 + g))
    kspecs = _band_specs(L, d, R, TQ, PW, LANES, lambda g: K_T0 + g0 + g)
    vspecs = _band_specs(L, d, R, TQ, PW, LANES, lambda g: V_T0 + g0 + g)
    ospec = pl.BlockSpec((TQ, 2 * LANES), lambda g, r, i: (i, r * (npairs // 2) + g))
    bias = _band_bias(TQ, R)
    o, lse = _pcall(body, name="banded_fwd", grid=(npairs // 2, d, L // TQ),
                    in_specs=[pl.BlockSpec(memory_space=pltpu.SMEM), pl.BlockSpec(bias.shape, lambda g, r, i: (0, 0)),
                              qspec] + kspecs + vspecs,
                    out_specs=[ospec, ospec],
                    out_shape=[jax.ShapeDtypeStruct((L, d * ow), o_dtype), jax.ShapeDtypeStruct((L, d * ow), F32)])(
        sink, bias, pv, pv, pv, pv, pv, pv, pv)
    return o.reshape(S, ow), lse.reshape(S, ow)


def banded_bwd_dq(proj, do, lse, delta, sink, *, d, R, TQ, pair0, npairs, use_sink):
    S = proj.shape[0]
    L = S // d
    pv = proj.reshape(L, d * PW)
    ow = npairs * LANES

    def body(sink_ref, bias_ref, q_ref, kp, kc, kn, vp, vc, vn, do_ref, lse_ref, delta_ref, dq_ref, dsink_ref):
        g, r, i = pl.program_id(0), pl.program_id(1), pl.program_id(2)

        @pl.when((r == 0) & (i == 0))
        def _():
            dsink_ref[...] = jnp.zeros_like(dsink_ref)

        bias = bias_ref[...] + _edge_bias(i, TQ, R, L)
        lo = _lo((TQ, LANES))
        kw = jnp.concatenate([kp[...], kc[...], kn[...]], axis=0)
        vw = jnp.concatenate([vp[...], vc[...], vn[...]], axis=0)
        k_lo, k_hi = _split(kw, _lo(kw.shape))
        kcat = jnp.concatenate([k_lo, k_hi], axis=0)
        for t in range(2):
            qa, qb = _split(q_ref[:, _tile(t)], lo)
            doa, dob = _split(do_ref[:, _tile(t)], lo)
            lse_t, delta_t = lse_ref[:, _tile(t)], delta_ref[:, _tile(t)]
            dss, dsk = [], []
            for h, (qh, doh) in enumerate(((qa, doa), (qb, dob))):
                lse_h, delta_h = _col(lse_t, h * HD), _col(delta_t, h * HD)
                pr = jnp.exp(_dot_nt(qh, kw) + bias - lse_h)
                dss.append((pr * (_dot_nt(doh, vw) - delta_h)).astype(BF))
                if use_sink:
                    psink = jnp.exp(sink_ref[2 * (pair0 + 2 * g + t) + h] - lse_h)
                    dsk.append(-jnp.sum(psink * delta_h, axis=0, keepdims=True))
            dq_ref[:, _tile(t)] = _dot(jnp.concatenate(dss, axis=1), kcat)
            if use_sink:
                dsink_ref[:, _tile(t)] += jnp.where(_lo((8, LANES)), dsk[0], dsk[1])

    g0 = pair0 // 2
    qspec = pl.BlockSpec((TQ, 2 * LANES), lambda g, r, i: (i, r * 7 + g0 + g))
    kspecs = _band_specs(L, d, R, TQ, PW, LANES, lambda g: K_T0 + g0 + g)
    vspecs = _band_specs(L, d, R, TQ, PW, LANES, lambda g: V_T0 + g0 + g)
    ospec = pl.BlockSpec((TQ, 2 * LANES), lambda g, r, i: (i, r * (npairs // 2) + g))
    view = lambda a: a.reshape(L, d * a.shape[1])
    ispec = lambda a: pl.BlockSpec((TQ, 2 * LANES), lambda g, r, i: (i, r * (a.shape[1] // (2 * LANES)) + g))
    bias = _band_bias(TQ, R)
    dq, dsink = _pcall(body, name="banded_bwd_dq", grid=(npairs // 2, d, L // TQ),
                       in_specs=[pl.BlockSpec(memory_space=pltpu.SMEM), pl.BlockSpec(bias.shape, lambda g, r, i: (0, 0)),
                                 qspec] + kspecs + vspecs + [ispec(do), ispec(lse), ispec(delta)],
                       out_specs=[ospec, pl.BlockSpec((8, 2 * LANES), lambda g, r, i: (g, 0))],
                       out_shape=[jax.ShapeDtypeStruct((L, d * ow), F32),
                                  jax.ShapeDtypeStruct((npairs // 2 * 8, 2 * LANES), F32)])(
        sink, bias, pv, pv, pv, pv, pv, pv, pv, view(do), view(lse), view(delta))
    return dq.reshape(S, ow), dsink


def banded_bwd_dkv(proj, do, lse, delta, *, d, R, TK, pair0, npairs):
    S = proj.shape[0]
    L = S // d
    pv = proj.reshape(L, d * PW)
    ow = npairs * LANES

    def body(bias_ref, k_ref, v_ref, qp, qc, qn, dop, doc, don, lp, lc, ln, dp_, dc_, dn_, dk_ref, dv_ref):
        j = pl.program_id(2)
        W = TK + 2 * R
        bias = bias_ref[...] + _edge_bias(j, TK, R, L)
        low = _lo((W, LANES))
        qw = jnp.concatenate([qp[...], qc[...], qn[...]], axis=0)
        dow = jnp.concatenate([dop[...], doc[...], don[...]], axis=0)
        lse_w = jnp.concatenate([lp[...], lc[...], ln[...]], axis=0)
        delta_w = jnp.concatenate([dp_[...], dc_[...], dn_[...]], axis=0)
        k, v = k_ref[...], v_ref[...]
        for t in range(2):
            qa, qb = _split(qw[:, _tile(t)], low)
            doa, dob = _split(dow[:, _tile(t)], low)
            lse_r, delta_r = lse_w[:, _tile(t)].T, delta_w[:, _tile(t)].T
            prs, dss = [], []
            for h, (qh, doh) in enumerate(((qa, doa), (qb, dob))):
                pr = jnp.exp(_dot_nt(k, qh) + bias - lse_r[h * HD:h * HD + 1, :])
                dss.append((pr * (_dot_nt(v, doh) - delta_r[h * HD:h * HD + 1, :])).astype(BF))
                prs.append(pr.astype(BF))
            dv_ref[:, _tile(t)] = _dot(jnp.concatenate(prs, axis=1), jnp.concatenate([doa, dob], axis=0))
            dk_ref[:, _tile(t)] = _dot(jnp.concatenate(dss, axis=1), jnp.concatenate([qa, qb], axis=0))

    g0 = pair0 // 2
    kspec = pl.BlockSpec((TK, LANES), lambda g, r, j: (j, r * 14 + K_T0 + g0 + g))
    vspec = pl.BlockSpec((TK, LANES), lambda g, r, j: (j, r * 14 + V_T0 + g0 + g))
    qspecs = _band_specs(L, d, R, TK, PW, 2 * LANES, lambda g: g0 + g)
    ispecs = lambda a: _band_specs(L, d, R, TK, a.shape[1], 2 * LANES, lambda g: g)
    view = lambda a: a.reshape(L, d * a.shape[1])
    ospec = pl.BlockSpec((TK, 2 * LANES), lambda g, r, j: (j, r * (npairs // 2) + g))
    sd = jax.ShapeDtypeStruct((L, d * ow), F32)
    bias = _band_bias(TK, R)
    dk, dv = _pcall(body, name="banded_bwd_dkv", grid=(npairs // 2, d, L // TK),
                    in_specs=[pl.BlockSpec(bias.shape, lambda g, r, j: (0, 0)), kspec, vspec] + qspecs + ispecs(do)
                    + ispecs(lse) + ispecs(delta),
                    out_specs=[ospec, ospec], out_shape=[sd, sd])(
        bias, pv, pv, pv, pv, pv, *([view(do)] * 3), *([view(lse)] * 3), *([view(delta)] * 3))
    return dk.reshape(S, ow), dv.reshape(S, ow)


def flash_fwd(proj, bound, *, tq=1024, tk=1024):
    S = proj.shape[0]

    def body_general(q_ref, k_ref, v_ref, o_ref, lse_ref):
        lo = _lo((tq, LANES))
        qa, qb = _split(q_ref[...], lo)
        lov = _lo((tk, LANES))

        def step(j, carry):
            ma, la, mb, lb, acc = carry
            rows = pl.ds(pl.multiple_of(j * tk, tk), tk)
            k, v = k_ref[rows, :], v_ref[rows, :]
            outs = []
            for qh, m0, l0 in ((qa, ma, la), (qb, mb, lb)):
                s = _dot_nt(qh, k)
                m1 = jnp.maximum(m0, jnp.max(s, axis=-1, keepdims=True))
                al = jnp.exp2(m0 - m1)
                e = jnp.exp2(s - m1)
                outs.append((m1, al * l0 + jnp.sum(e, axis=-1, keepdims=True), al, e.astype(BF)))
            v_lo, v_hi = _split(v, lov)
            pvv = _dot(jnp.concatenate([outs[0][3], outs[1][3]], axis=1), jnp.concatenate([v_lo, v_hi], axis=0))
            acc = acc * jnp.where(lo, outs[0][2], outs[1][2]) + pvv
            return outs[0][0], outs[0][1], outs[1][0], outs[1][1], acc

        m_init = jnp.full((tq, 1), NEG, F32)
        l_init = jnp.zeros((tq, 1), F32)
        ma, la, mb, lb, acc = lax.fori_loop(0, S // tk, step,
                                            (m_init, l_init, m_init, l_init, jnp.zeros((tq, LANES), F32)))
        o_ref[...] = (acc / jnp.where(lo, la, lb)).astype(BF)
        lse_ref[...] = jnp.where(lo, ma * LN2 + jnp.log(la), mb * LN2 + jnp.log(lb))

    def body_plain(q_ref, k_ref, v_ref, o_ref, lse_ref):
        lo = _lo((tq, LANES))
        qa, qb = _split(q_ref[...], lo)
        lov = _lo((tk, LANES))
        one = jnp.ones((tk, LANES), BF)

        def step(j, carry):
            acc_a, acc_b = carry
            rows = pl.ds(pl.multiple_of(j * tk, tk), tk)
            k, v = k_ref[rows, :], v_ref[rows, :]
            ea = jnp.exp2(_dot_nt(qa, k)).astype(BF)
            eb = jnp.exp2(_dot_nt(qb, k)).astype(BF)
            acc_a = acc_a + _dot(ea, jnp.where(lov, v, one))
            acc_b = acc_b + _dot(eb, jnp.where(lov, one, v))
            return acc_a, acc_b

        z = jnp.zeros((tq, LANES), F32)
        acc_a, acc_b = lax.fori_loop(0, S // tk, step, (z, z))
        den = jnp.where(lo, pltpu.roll(acc_a, HD, 1), pltpu.roll(acc_b, HD, 1))
        o_ref[...] = (jnp.where(lo, acc_a, acc_b) / den).astype(BF)
        lse_ref[...] = jnp.log(den)

    def body(bound_ref, q_ref, k_ref, v_ref, o_ref, lse_ref):
        small = bound_ref[0] <= MAX_PLAIN_SCORE

        @pl.when(small)
        def _():
            body_plain(q_ref, k_ref, v_ref, o_ref, lse_ref)

        @pl.when(jnp.logical_not(small))
        def _():
            body_general(q_ref, k_ref, v_ref, o_ref, lse_ref)

    ospec = pl.BlockSpec((tq, LANES), lambda p, i: (i, p))
    return _pcall(body, name="flash_fwd", grid=(N_PAIRS, S // tq),
                  in_specs=[pl.BlockSpec(memory_space=pltpu.SMEM), ospec,
                            pl.BlockSpec((S, LANES), lambda p, i: (0, K_T0 + p // 2)),
                            pl.BlockSpec((S, LANES), lambda p, i: (0, V_T0 + p // 2))],
                  out_specs=[ospec, ospec],
                  out_shape=[jax.ShapeDtypeStruct((S, 768), BF), jax.ShapeDtypeStruct((S, 768), F32)])(
        bound.reshape(1), proj, proj, proj)


def flash_bwd(proj, do, lse, delta, *, tq=1024, tk=1024):
    S = proj.shape[0]
    nq = S // tq
    lse_t, delta_t = _stats_t(lse, tq), _stats_t(delta, tq)

    def body(k_ref, v_ref, q_ref, do_ref, lse_ref, delta_ref, dk_ref, dv_ref, dqt_ref):
        @pl.when(pl.program_id(1) == 0)
        def _():
            dqt_ref[...] = jnp.zeros_like(dqt_ref)

        k, v = k_ref[...], v_ref[...]
        lo = _lo((tq, LANES))
        k_lo, k_hi = _split(k.astype(F32), _lo((tk, LANES)))
        kt = jnp.concatenate([k_lo.T, k_hi.T], axis=1).astype(BF)

        def step(i, carry):
            dk, dv = carry
            rows = pl.ds(pl.multiple_of(i * tq, tq), tq)
            qa, qb = _split(q_ref[rows, :], lo)
            doa, dob = _split(do_ref[rows, :], lo)
            lse_i, delta_i = lse_ref[i] * LOG2E, delta_ref[i]
            prs, dss = [], []
            for h, (qh, doh) in enumerate(((qa, doa), (qb, dob))):
                pr = jnp.exp2(_dot_nt(k, qh) - lse_i[h:h + 1, :])
                dss.append((pr * (_dot_nt(v, doh) - delta_i[h:h + 1, :])).astype(BF))
                prs.append(pr.astype(BF))
            dv = dv + _dot(jnp.concatenate(prs, axis=1), jnp.concatenate([doa, dob], axis=0))
            dk = dk + _dot(jnp.concatenate(dss, axis=1), jnp.concatenate([qa, qb], axis=0))
            dqt_ref[i] += _dot(kt, jnp.concatenate(dss, axis=0))
            return dk, dv

        z = jnp.zeros((tk, LANES), F32)
        dk, dv = lax.fori_loop(0, nq, step, (z, z))
        dk_ref[...] = dk
        dv_ref[...] = dv

    ospec = pl.BlockSpec((tk, LANES), lambda p, j: (j, p))
    stat = pl.BlockSpec((None, nq, 8, tq), lambda p, j: (p, 0, 0, 0))
    whole = lambda: pl.BlockSpec((S, LANES), lambda p, j: (0, p), pipeline_mode=pl.Buffered(1))
    sd = jax.ShapeDtypeStruct((S, 768), F32)
    dk, dv, dqt = _pcall(body, name="flash_bwd", grid=(N_PAIRS, S // tk),
                         in_specs=[pl.BlockSpec((tk, LANES), lambda p, j: (j, K_T0 + p // 2)),
                                   pl.BlockSpec((tk, LANES), lambda p, j: (j, V_T0 + p // 2)), whole(), whole(),
                                   stat, stat],
                         out_specs=[ospec, ospec,
                                    pl.BlockSpec((None, nq, LANES, tq), lambda p, j: (p, 0, 0, 0),
                                                 pipeline_mode=pl.Buffered(1))],
                         out_shape=[sd, sd, jax.ShapeDtypeStruct((N_PAIRS, nq, LANES, tq), F32)])(
        proj, proj, proj, do, lse_t, delta_t)
    dq = jnp.transpose(dqt, (1, 3, 0, 2)).reshape(S, 768)
    return dq, dk, dv


def mem_fwd(proj, mkv, *, tq=512):
    S = proj.shape[0]

    def body(q_ref, km_ref, vm_ref, o_ref, lse_ref):
        lo = _lo((tq, LANES))
        lov = _lo((N_MEM, LANES))
        for t in range(2):
            qa, qb = _split(q_ref[:, _tile(t)], lo)
            km, vm = km_ref[:, _tile(t)], vm_ref[:, _tile(t)]
            ps, dens, lses = [], [], []
            for qh in (qa, qb):
                s = _dot_nt(qh, km)
                m = jnp.max(s, axis=-1, keepdims=True)
                e = jnp.exp(s - m)
                den = jnp.sum(e, axis=-1, keepdims=True)
                ps.append(e.astype(BF))
                dens.append(den)
                lses.append(m + jnp.log(den))
            v_lo, v_hi = _split(vm, lov)
            o = _dot(jnp.concatenate(ps, axis=1), jnp.concatenate([v_lo, v_hi], axis=0))
            o_ref[:, _tile(t)] = (o / jnp.where(lo, dens[0], dens[1])).astype(BF)
            lse_ref[:, _tile(t)] = jnp.where(lo, lses[0], lses[1])

    ospec = pl.BlockSpec((tq, 256), lambda i: (i, 0))
    return _pcall(body, name="mem_fwd", grid=(S // tq,),
                  in_specs=[pl.BlockSpec((tq, 256), lambda i: (i, M_T0 // 2)),
                            pl.BlockSpec((N_MEM, 256), lambda i: (0, 0)), pl.BlockSpec((N_MEM, 256), lambda i: (0, 1))],
                  out_specs=[ospec, ospec],
                  out_shape=[jax.ShapeDtypeStruct((S, 256), BF), jax.ShapeDtypeStruct((S, 256), F32)])(proj, mkv, mkv)


def mem_bwd(proj, mkv, dcat, lse, delta, *, tq=512):
    S = proj.shape[0]

    def body(q_ref, km_ref, vm_ref, do_ref, lse_ref, delta_ref, dq_ref, dkm_ref, dvm_ref):
        @pl.when(pl.program_id(0) == 0)
        def _():
            dkm_ref[...] = jnp.zeros_like(dkm_ref)
            dvm_ref[...] = jnp.zeros_like(dvm_ref)

        lo = _lo((tq, LANES))
        lov = _lo((N_MEM, LANES))
        for t in range(2):
            qa, qb = _split(q_ref[:, _tile(t)], lo)
            doa, dob = _split(do_ref[:, _tile(t)], lo)
            km, vm = km_ref[:, _tile(t)], vm_ref[:, _tile(t)]
            lse_t, delta_t = lse_ref[:, _tile(t)], delta_ref[:, _tile(t)]
            prs, dss = [], []
            for h, (qh, doh) in enumerate(((qa, doa), (qb, dob))):
                pr = jnp.exp(_dot_nt(qh, km) - _col(lse_t, h * HD))
                dss.append(pr * (_dot_nt(doh, vm) - _col(delta_t, h * HD)))
                prs.append(pr)
            k_lo, k_hi = _split(km, lov)
            dq_ref[:, _tile(t)] = _dot(jnp.concatenate(dss, axis=1).astype(BF), jnp.concatenate([k_lo, k_hi], axis=0))
            dvm_ref[:, _tile(t)] += _dot(jnp.concatenate(prs, axis=0).T.astype(BF), jnp.concatenate([doa, dob], axis=0))
            dkm_ref[:, _tile(t)] += _dot(jnp.concatenate(dss, axis=0).T.astype(BF), jnp.concatenate([qa, qb], axis=0))

    ospec = pl.BlockSpec((tq, 256), lambda i: (i, 0))
    msp = pl.BlockSpec((N_MEM, 256), lambda i: (0, 0))
    md = jax.ShapeDtypeStruct((N_MEM, 256), F32)
    dq, dkm, dvm = _pcall(body, name="mem_bwd", grid=(S // tq,),
                          in_specs=[pl.BlockSpec((tq, 256), lambda i: (i, M_T0 // 2)), msp,
                                    pl.BlockSpec((N_MEM, 256), lambda i: (0, 1)),
                                    pl.BlockSpec((tq, 256), lambda i: (i, 3)), ospec,
                                    pl.BlockSpec((tq, 256), lambda i: (i, 3))],
                          out_specs=[ospec, msp, msp], out_shape=[jax.ShapeDtypeStruct((S, 256), F32), md, md])(
        proj, mkv, mkv, dcat, lse, delta)
    return dq, jnp.concatenate([dkm, dvm], axis=1)


def combine_fwd(os_, lses, *, tm=512):
    S = os_[0].shape[0]

    def body(o0, o1, o2, l0, l1, l2, tok_ref):
        ls = [l0[...], l1[...], l2[...]]
        m = jnp.maximum(jnp.maximum(ls[0], ls[1]), ls[2])
        es = [jnp.exp(l - m) for l in ls]
        den = es[0] + es[1] + es[2]
        for g, o in enumerate((o0, o1, o2)):
            tok_ref[:, 256 * g:256 * (g + 1)] = (o[...] * (es[g] / den)).astype(BF)

    sp = pl.BlockSpec((tm, 256), lambda i: (i, 0))
    return _pcall(body, name="combine_fwd", grid=(S // tm,), in_specs=[sp] * 6,
                  out_specs=pl.BlockSpec((tm, 768), lambda i: (i, 0)),
                  out_shape=jax.ShapeDtypeStruct((S, 768), BF))(*os_, *lses)


def combine_bwd(dcat, os_, lses, *, tm=512):
    S = dcat.shape[0]

    def body(dt_ref, o0, o1, o2, l0, l1, l2, do0, do1, do2, de0, de1, de2):
        ls = [l0[...], l1[...], l2[...]]
        m = jnp.maximum(jnp.maximum(ls[0], ls[1]), ls[2])
        es = [jnp.exp(l - m) for l in ls]
        den = es[0] + es[1] + es[2]
        alphas = [e / den for e in es]
        lo = _lo((tm, LANES))
        dts = [dt_ref[:, 256 * g:256 * (g + 1)].astype(F32) for g in range(3)]
        dal = []
        for g, o in enumerate((o0, o1, o2)):
            pr = dts[g] * o[...]
            dal.append(jnp.concatenate([_half_sum(pr[:, _tile(0)], lo), _half_sum(pr[:, _tile(1)], lo)], axis=1))
        mix = alphas[0] * dal[0] + alphas[1] * dal[1] + alphas[2] * dal[2]
        for g, (do_ref, de_ref) in enumerate(((do0, de0), (do1, de1), (do2, de2))):
            do_ref[...] = (dts[g] * alphas[g]).astype(BF)
            de_ref[...] = alphas[g] * mix

    sp = pl.BlockSpec((tm, 256), lambda i: (i, 0))
    outs = _pcall(body, name="combine_bwd", grid=(S // tm,),
                  in_specs=[pl.BlockSpec((tm, 768), lambda i: (i, 0))] + [sp] * 6, out_specs=[sp] * 6,
                  out_shape=[jax.ShapeDtypeStruct((S, 256), BF)] * 3 + [jax.ShapeDtypeStruct((S, 256), F32)] * 3)(
        dcat, *os_, *lses)
    return outs[:3], outs[3:]


def _coords():
    return lax.axis_index("x"), lax.axis_index("y"), lax.axis_index("c")


def _other_chips(x, y):
    return [(1 - x, y), (x, 1 - y), (1 - x, 1 - y)]


HBM_SPEC = pl.BlockSpec(memory_space=pltpu.HBM)


def _gather_comm(ins, outs, send, recv, lsem, start):
    x, y, c = _coords()
    me = 2 * x + y
    for a in range(len(ins)):
        local = pltpu.make_async_copy(ins[a], outs[a].at[me], lsem.at[a])
        if start:
            local.start()
        for j, (px, py) in enumerate(_other_chips(x, y)):
            sems = dict(send_sem=send.at[3 * a + j], recv_sem=recv.at[3 * a + j], device_id=(px, py, c),
                        device_id_type=MESH)
            cp = pltpu.make_async_remote_copy(src_ref=ins[a], dst_ref=outs[a].at[me], **sems)
            if start:
                cp.start()
            else:
                pltpu.make_async_remote_copy(src_ref=ins[a], dst_ref=outs[a].at[2 * px + py], **sems).wait_recv()
                cp.wait_send()
        if not start:
            local.wait()


def _gather_out_shape(shards):
    return [jax.ShapeDtypeStruct((4,) + s.shape, s.dtype) for s in shards]


def _gather_sems(n):
    return [pltpu.SemaphoreType.DMA((3 * n,)), pltpu.SemaphoreType.DMA((3 * n,)), pltpu.SemaphoreType.DMA((n,))]


def gather_shards(shards):
    n = len(shards)

    def body(*refs):
        _gather_comm(refs[:n], refs[n:2 * n], *refs[2 * n:], start=True)
        _gather_comm(refs[:n], refs[n:2 * n], *refs[2 * n:], start=False)

    return pl.pallas_call(body, name="gather_shards", in_specs=[HBM_SPEC] * n, out_specs=[HBM_SPEC] * n,
                          out_shape=_gather_out_shape(shards), scratch_shapes=_gather_sems(n))(*shards)


def _scatter_comm(ins, outs, send, recv, start):
    x, y, c = _coords()
    for a in range(len(ins)):
        for j, (px, py) in enumerate(_other_chips(x, y)):
            cp = pltpu.make_async_remote_copy(src_ref=ins[a].at[2 * px + py], dst_ref=outs[a].at[j],
                                              send_sem=send.at[3 * a + j], recv_sem=recv.at[3 * a + j],
                                              device_id=(px, py, c), device_id_type=MESH)
            if start:
                cp.start()
            else:
                cp.wait_recv()
                cp.wait_send()


def _scatter_out_shape(parts):
    return [jax.ShapeDtypeStruct((3,) + p.shape[1:], p.dtype) for p in parts]


def _scatter_sems(n):
    return [pltpu.SemaphoreType.DMA((3 * n,)), pltpu.SemaphoreType.DMA((3 * n,))]


def scatter_grads(parts):
    n = len(parts)

    def body(*refs):
        _scatter_comm(refs[:n], refs[n:2 * n], *refs[2 * n:], start=True)
        _scatter_comm(refs[:n], refs[n:2 * n], *refs[2 * n:], start=False)

    return pl.pallas_call(body, name="scatter_grads", in_specs=[HBM_SPEC] * n, out_specs=[HBM_SPEC] * n,
                          out_shape=_scatter_out_shape(parts), scratch_shapes=_scatter_sems(n))(*parts)


def sibling_swap(arrs):
    n = len(arrs)

    def body(*refs):
        ins, outs = refs[:n], refs[n:2 * n]
        send, recv = refs[2 * n:]
        x, y, c = _coords()
        cps = []
        for a in range(n):
            cp = pltpu.make_async_remote_copy(src_ref=ins[a], dst_ref=outs[a], send_sem=send.at[a], recv_sem=recv.at[a],
                                              device_id=(x, y, 1 - c), device_id_type=MESH)
            cp.start()
            cps.append(cp)
        for cp in cps:
            cp.wait_recv()
        for cp in cps:
            cp.wait_send()

    return pl.pallas_call(
        body, name="sibling_swap", in_specs=[HBM_SPEC] * n, out_specs=[HBM_SPEC] * n,
        out_shape=[jax.ShapeDtypeStruct(a.shape, a.dtype) for a in arrs],
        scratch_shapes=[pltpu.SemaphoreType.DMA((n,)), pltpu.SemaphoreType.DMA((n,))])(*arrs)


def allsum_small(v):
    rows = v.shape[0]

    def body(v_ref, tot_ref, gath_ref, send, recv):
        x, y, c = _coords()
        me = 4 * x + 2 * y + c
        gath_ref[me] = v_ref[...]
        cps = []
        for k in range(1, 8):
            fx, fy, fc = (k >> 2) & 1, (k >> 1) & 1, k & 1
            peer = (1 - x if fx else x, 1 - y if fy else y, 1 - c if fc else c)
            cp = pltpu.make_async_remote_copy(src_ref=v_ref, dst_ref=gath_ref.at[me], send_sem=send.at[k - 1],
                                              recv_sem=recv.at[k - 1], device_id=peer, device_id_type=MESH)
            cp.start()
            cps.append(cp)
        for cp in cps:
            cp.wait_recv()
        for cp in cps:
            cp.wait_send()
        tot = gath_ref[0]
        for k in range(1, 8):
            tot = tot + gath_ref[k]
        tot_ref[...] = tot

    vm = pl.BlockSpec(memory_space=pltpu.VMEM)
    tot, _ = pl.pallas_call(
        body, name="allsum_small", in_specs=[vm], out_specs=[vm, vm],
        out_shape=[jax.ShapeDtypeStruct((rows, LANES), F32), jax.ShapeDtypeStruct((8, rows, LANES), F32)],
        scratch_shapes=[pltpu.SemaphoreType.DMA((7,)), pltpu.SemaphoreType.DMA((7,))])(v)
    return tot


def sum_parts(own, recv, *, tr=256):
    R, C = own.shape
    tr = min(tr, R)

    def body(o_ref, r_ref, out_ref):
        out_ref[...] = ((o_ref[...] + r_ref[0].astype(F32)) + r_ref[1].astype(F32)) + r_ref[2].astype(F32)

    sp = pl.BlockSpec((tr, C), lambda i: (i, 0))
    return _pcall(body, name="sum_parts", grid=(R // tr,),
                  in_specs=[sp, pl.BlockSpec((3, tr, C), lambda i: (0, i, 0))], out_specs=sp,
                  out_shape=jax.ShapeDtypeStruct((R, C), F32))(own, recv)


def adamw(w, ga, gb, m, v, *, tr=256):
    R, C = w.shape
    tr = min(tr, R)
    two = gb is not None

    def body(*refs):
        if two:
            w_ref, ga_ref, gb_ref, m_ref, v_ref, g_out, d_out, m_out, v_out = refs
            g = ga_ref[...] + gb_ref[...]
        else:
            w_ref, ga_ref, m_ref, v_ref, g_out, d_out, m_out, v_out = refs
            g = ga_ref[...]
        mn = B1 * m_ref[...] + (1.0 - B1) * g
        vn = B2 * v_ref[...] + (1.0 - B2) * (g * g)
        m_hat = mn / (1.0 - B1 ** STEP)
        v_hat = vn / (1.0 - B2 ** STEP)
        g_out[...] = g
        d_out[...] = -LR * (m_hat / (jnp.sqrt(v_hat) + AEPS) + WD * w_ref[...])
        m_out[...] = mn
        v_out[...] = vn

    sp = pl.BlockSpec((tr, C), lambda i: (i, 0))
    args = [w, ga, gb, m, v] if two else [w, ga, m, v]
    sd = jax.ShapeDtypeStruct((R, C), F32)
    return _pcall(body, name="adamw", grid=(R // tr,), in_specs=[sp] * len(args), out_specs=[sp] * 4,
                  out_shape=[sd] * 4)(*args)


def _rope_tables(S):
    def inv_freq(n_dims, theta):
        return theta ** (-(jnp.arange(0, n_dims, 2, dtype=jnp.float32) / n_dims))

    pos = lax.broadcasted_iota(jnp.int32, (S, LANES), 0)
    d = lax.broadcasted_iota(jnp.int32, (S, LANES), 1) % HD
    d1 = lax.iota(jnp.int32, LANES) % HD
    ang = pos.astype(F32) * inv_freq(HD // 4, ROPE_THETA)[d1 % 8][None, :]
    sin = jnp.sin(ang)
    partial = (jnp.where(d < 16, jnp.cos(ang), 1.0), jnp.where((d >= 8) & (d < 16), sin, 0.0),
               jnp.where(d < 8, -sin, 0.0))
    grid_pos = jnp.where(d < 32, pos // GRID_W, pos % GRID_W)
    ang = grid_pos.astype(F32) * inv_freq(HD // 2, AXIAL_THETA)[d1 % 16][None, :]
    sin = jnp.sin(ang)
    axial = (jnp.cos(ang), jnp.where(d % 32 >= 16, sin, 0.0), jnp.where(d % 32 < 16, -sin, 0.0))
    return partial, axial


def _pad_w_in(w):
    cols = [w[:, :768]]
    for base in (768, 960):
        for g in range(3):
            kg = w[:, base + g * HD:base + (g + 1) * HD]
            cols += [kg, kg]
    cols.append(w[:, 1152:])
    return jnp.concatenate(cols, axis=1)


def _unpad_dw_in(dw):
    cols = [dw[:, :768]]
    for t0 in (K_T0, V_T0):
        for g in range(3):
            b = (t0 + g) * LANES
            cols.append(dw[:, b:b + HD] + dw[:, b + HD:b + LANES])
    cols.append(dw[:, M_T0 * LANES:])
    return jnp.concatenate(cols, axis=1)


def _stats_t(a, tq):
    S = a.shape[0]
    t = a.reshape(S, -1, 2, HD)[:, :N_PAIRS, :, 0]
    t = jnp.transpose(t, (1, 2, 0))
    t = jnp.pad(t, ((0, 0), (0, 6), (0, 0)))
    return jnp.transpose(t.reshape(N_PAIRS, 8, S // tq, tq), (0, 2, 1, 3))


def _fold(t):
    return t[..., :HD] + t[..., HD:]


def kernel(x, mem, mem_norm_g, w_in, w_mem_kv, w_o, g_mix_pre, g_mix_post, attn_sink, qk_norm_g, w_gate_up, w_down, g_ffn_pre, g_ffn_post, loss_target, m_mem_norm_g, m_w_in, m_w_mem_kv, m_w_o, m_g_mix_pre, m_g_mix_post, m_attn_sink, m_qk_norm_g, m_w_gate_up, m_w_down, m_g_ffn_pre, m_g_ffn_post, v_mem_norm_g, v_w_in, v_w_mem_kv, v_w_o, v_g_mix_pre, v_g_mix_post, v_attn_sink, v_qk_norm_g, v_w_gate_up, v_w_down, v_g_ffn_pre, v_g_ffn_post):
    S = x.shape[1]
    depth = w_in.shape[0]
    xs, memx, tgt = x[0], mem[0], loss_target[0]
    tab_p, tab_a = _rope_tables(S)
    row = lambda a: a.reshape(1, -1)

    shards_bf = [w.astype(BF) for w in (w_in, w_mem_kv, w_o, w_gate_up, w_down)]
    layer_shards = lambda i: [s[i] for s in shards_bf]
    W_in, W_mkv, W_o, W_g, W_u, W_d = ([None] * depth for _ in range(6))

    def set_weights(i, gathered):
        gi, gm, go, gg, gd = gathered
        W_in[i] = jnp.concatenate([gi[s] for s in range(4)], axis=1)
        W_mkv[i] = jnp.concatenate([gm[s] for s in range(4)], axis=0)
        W_o[i] = jnp.concatenate([go[s] for s in range(4)], axis=0)
        W_g[i] = jnp.concatenate([gg[0], gg[1]], axis=1)
        W_u[i] = jnp.concatenate([gg[2], gg[3]], axis=1)
        W_d[i] = jnp.concatenate([gd[s] for s in range(4)], axis=0)

    set_weights(0, gather_shards(layer_shards(0)))
    mem_g = row(mem_norm_g)
    zero_sink = jnp.zeros((12,), F32)
    qkg = jnp.pad(jnp.concatenate([qk_norm_g[0], qk_norm_g[0]], axis=1), ((0, 6), (0, 0)))
    no_qkg = jnp.zeros((8, LANES), F32)

    saved = []
    cur = xs
    for i in range(depth):
        kind = i % 3
        wp = _pad_w_in(W_in[i])
        sv = dict(x=cur, wp=wp)
        if kind == 1:
            h1, proj, raw, nrm = inproj_fwd(cur, row(g_mix_pre[i]), wp, tab_a, qkg, axial=True)
            sv["raw"] = raw
        else:
            h1, proj = inproj_fwd(cur, row(g_mix_pre[i]), wp, tab_p, no_qkg, axial=False)
        if kind == 0:
            tok, lse = banded_fwd(proj, attn_sink[i // 3], d=1, R=A_RADIUS, TQ=_band_tile(A_RADIUS, S), pair0=0, npairs=6,
                                  use_sink=True, o_dtype=BF)
        elif kind == 1:
            bound = jnp.sqrt(jnp.max(nrm[0]) * jnp.max(nrm[1])) * LN2
            tok, lse = flash_fwd(proj, bound)
        else:
            os_, lses = [], []
            for g, (window, dil) in enumerate(C_GROUPS):
                rad = window // (2 * dil)
                o_g, l_g = banded_fwd(proj, zero_sink, d=dil, R=rad, TQ=_band_tile(rad, S // dil), pair0=2 * g, npairs=2,
                                      use_sink=False, o_dtype=F32)
                os_.append(o_g)
                lses.append(l_g)
            tok = combine_fwd(os_, lses)
            sv["os"], lse = os_, lses
        mem_n, mkv = norm_mm(memx, mem_g, W_mkv[i], tm=N_MEM)
        mo, mlse = mem_fwd(proj, mkv)
        cat = jnp.concatenate([tok, mo], axis=1)
        o, x2 = mm_norm_res(cat, W_o[i], row(g_mix_post[i]), cur)
        (h2, gate, up, act), gathered = ffn_up_fwd(x2, row(g_ffn_pre[i]), W_g[i], W_u[i],
                                                   layer_shards(i + 1) if i + 1 < depth else ())
        if i + 1 < depth:
            set_weights(i + 1, gathered)
        f, x3 = mm_norm_res(act, W_d[i], row(g_ffn_post[i]), x2)
        sv.update(h1=h1, proj=proj, lse=lse, mem_n=mem_n, mkv=mkv, mlse=mlse, cat=cat, o=o, x2=x2, h2=h2, gate=gate,
                  up=up, act=act, f=f)
        saved.append(sv)
        cur = x3

    dcur, loss_vec = loss_bwd(cur, tgt)

    grad_parts, grad_recv = [None] * depth, [None] * depth
    dg_pre, dg_post, dg_fpre, dg_fpost = [None] * depth, [None] * depth, [None] * depth, [None] * depth
    dg_mem = jnp.zeros((1, D), F32)
    dsinks, dqk = {}, None
    for i in reversed(range(depth)):
        sv = saved[i]
        kind = i % 3
        proj = sv["proj"]
        pending = [p.astype(BF) for p in grad_parts[i + 1]] if i + 1 < depth else ()
        (df, dgate, dup, dg_fpost[i]), got = normbwd_mm_swiglu(dcur, sv["f"], row(g_ffn_post[i]), W_d[i], sv["gate"],
                                                               sv["up"], pending)
        if i + 1 < depth:
            grad_recv[i + 1] = got
        dx2, dg_fpre[i] = mm_nt_normbwd_res([(dgate, W_g[i]), (dup, W_u[i])], sv["x2"], row(g_ffn_pre[i]), dcur)
        dW_d = mm_acc(sv["act"], df, tk=1408, tn=D, ts=2048)
        dW_gu = jnp.concatenate([mm_acc(sv["h2"], dgate, tk=D, tn=1408, ts=2048),
                                 mm_acc(sv["h2"], dup, tk=D, tn=1408, ts=2048)], axis=1)
        do, dcat, delta, dg_post[i] = normbwd_mm_cat(dx2, sv["o"], row(g_mix_post[i]), W_o[i], sv["cat"])
        dW_o = mm_acc(sv["cat"], do, tk=D, tn=D, ts=2048)
        dqm, dmkv = mem_bwd(proj, sv["mkv"], dcat, sv["mlse"], delta)
        dmkv = dmkv.astype(BF)
        (dgm,) = mm_nt_normbwd_res([(dmkv, W_mkv[i])], memx, mem_g, None, tm=N_MEM)
        dg_mem = dg_mem + dgm
        dW_mkv = mm_acc(sv["mem_n"], dmkv, tk=D, tn=512, ts=N_MEM)
        if kind == 0:
            sink = attn_sink[i // 3]
            args = dict(d=1, R=A_RADIUS, pair0=0, npairs=6)
            tile = _band_tile(A_RADIUS, S)
            dq, dsk = banded_bwd_dq(proj, dcat, sv["lse"], delta, sink, TQ=tile, use_sink=True, **args)
            dkp, dvp = banded_bwd_dkv(proj, dcat, sv["lse"], delta, TK=tile, **args)
            dsinks[i // 3] = dsk.reshape(3, 8, 2, 2, HD)[:, 0, :, :, 0].reshape(12)
        elif kind == 1:
            dq, dkp, dvp = flash_bwd(proj, dcat, sv["lse"], delta)
        else:
            dos, des = combine_bwd(dcat, sv["os"], sv["lse"])
            dqs, dks, dvs = [], [], []
            for g, (window, dil) in enumerate(C_GROUPS):
                rad = window // (2 * dil)
                args = dict(d=dil, R=rad, pair0=2 * g, npairs=2)
                tile = _band_tile(rad, S // dil)
                dq_g, _ = banded_bwd_dq(proj, dos[g], sv["lse"][g], des[g], zero_sink, TQ=tile, use_sink=False, **args)
                dk_g, dv_g = banded_bwd_dkv(proj, dos[g], sv["lse"][g], des[g], TK=tile, **args)
                dqs.append(dq_g)
                dks.append(dk_g)
                dvs.append(dv_g)
            dq, dkp, dvp = (jnp.concatenate(t, axis=1) for t in (dqs, dks, dvs))
        if kind == 1:
            dcur, dproj, dg_pre[i], dqk_t = inproj_bwd(dq, dkp, dvp, dqm, tab_a, sv["raw"], qkg, sv["wp"], sv["x"],
                                                       row(g_mix_pre[i]), dx2, axial=True)
            dqk = _fold(dqk_t[:2]).reshape(1, 2, HD)
        else:
            dcur, dproj, dg_pre[i], _ = inproj_bwd(dq, dkp, dvp, dqm, tab_p, proj, no_qkg, sv["wp"],
                                                   sv["x"], row(g_mix_pre[i]), dx2, axial=False)
        dW_in = _unpad_dw_in(mm_acc(sv["h1"], dproj, tk=D, tn=896, ts=2048))
        grad_parts[i] = [jnp.transpose(dW_in.reshape(D, 4, IN_W // 4), (1, 0, 2)), dW_mkv.reshape(4, D // 4, 512),
                         dW_o.reshape(4, D // 4, D), jnp.transpose(dW_gu.reshape(D, 4, 2 * DFF // 4), (1, 0, 2)),
                         dW_d.reshape(4, DFF // 4, D)]
    grad_recv[0] = scatter_grads([p.astype(BF) for p in grad_parts[0]])

    x_i, y_i, _ = _coords()
    me = 2 * x_i + y_i
    big = [(w_in, m_w_in, v_w_in), (w_mem_kv, m_w_mem_kv, v_w_mem_kv), (w_o, m_w_o, v_w_o),
           (w_gate_up, m_w_gate_up, v_w_gate_up), (w_down, m_w_down, v_w_down)]
    parts = []
    for a, (w, _, _) in enumerate(big):
        C = w.shape[-1]
        own = jnp.stack([lax.dynamic_index_in_dim(grad_parts[l][a], me, 0, keepdims=False) for l in range(depth)])
        rc = jnp.stack([grad_recv[l][a] for l in range(depth)], axis=1)
        parts.append(sum_parts(own.reshape(-1, C), rc.reshape(3, -1, C)))
    sibs = sibling_swap(parts)
    big_out = []
    for (w, m, v), pa, pb in zip(big, parts, sibs):
        C = w.shape[-1]
        outs = adamw(w.reshape(-1, C), pa, pb, m.reshape(-1, C), v.reshape(-1, C))
        big_out.append([o.reshape(w.shape) for o in outs])

    small_w = [mem_norm_g, g_mix_pre, g_mix_post, attn_sink, qk_norm_g, g_ffn_pre, g_ffn_post]
    small_m = [m_mem_norm_g, m_g_mix_pre, m_g_mix_post, m_attn_sink, m_qk_norm_g, m_g_ffn_pre, m_g_ffn_post]
    small_v = [v_mem_norm_g, v_g_mix_pre, v_g_mix_post, v_attn_sink, v_qk_norm_g, v_g_ffn_pre, v_g_ffn_post]
    small_g = [dg_mem.reshape(D), jnp.concatenate(dg_pre, axis=0), jnp.concatenate(dg_post, axis=0),
               jnp.stack([dsinks[k] for k in sorted(dsinks)]), dqk, jnp.concatenate(dg_fpre, axis=0),
               jnp.concatenate(dg_fpost, axis=0)]
    sizes = [a.size for a in small_w]
    total = sum(sizes)
    rows_s = -(-(total + LANES) // (8 * LANES)) * 8

    def pack(arrs, extra=None):
        flat = jnp.concatenate([a.reshape(-1).astype(F32) for a in arrs])
        flat = jnp.pad(flat, (0, rows_s * LANES - LANES - total))
        tail = jnp.zeros((LANES,), F32) if extra is None else extra.reshape(LANES)
        return jnp.concatenate([flat, tail]).reshape(rows_s, LANES)

    tot = allsum_small(pack(small_g, loss_vec))
    loss = jnp.sum(tot[rows_s - 1])
    s_out = adamw(pack(small_w), tot, None, pack(small_m), pack(small_v))

    def unpack(buf):
        flat = buf.reshape(-1)
        out, off = [], 0
        for a, n in zip(small_w, sizes):
            out.append(flat[off:off + n].reshape(a.shape))
            off += n
        return out

    sg, sd_, sm, sv_ = (unpack(b) for b in s_out)

    def ordered(k):
        sm_ = (sg, sd_, sm, sv_)[k]
        b = [bo[k] for bo in big_out]
        return [sm_[0], b[0], b[1], b[2], sm_[1], sm_[2], sm_[3], sm_[4], b[3], b[4], sm_[5], sm_[6]]

    dx_out = dcur.reshape(1, S, D)
    return (loss, dx_out, *ordered(0), *ordered(1), *ordered(2), *ordered(3))
```

```python
import functools

import jax
import jax.numpy as jnp
from jax import lax
from jax.experimental import pallas as pl
from jax.experimental.pallas import tpu as pltpu

F32 = jnp.float32
BF = jnp.bfloat16

D = 1024
HD = 64
LANES = 128
N_PAIRS = 6
DFF = 2816
IN_W = 1408
PW = 14 * LANES
K_T0, V_T0, M_T0 = 6, 9, 12
EPS = 1e-6
SCALE = HD ** -0.5
NEG = -1e30
LOG2E = 1.4426950408889634
LN2 = 0.6931471805599453
MAX_PLAIN_SCORE = 40.0
ROPE_THETA = 500000.0
AXIAL_THETA = 10000.0
GRID_W = 64
A_RADIUS = 128
C_GROUPS = ((128, 1), (512, 4), (2048, 16))
N_MEM = 256
LR, B1, B2, AEPS, WD, STEP = 0.001, 0.9, 0.999, 1e-08, 0.01, 10
VMEM_LIMIT = 56 * 1024 * 1024
MESH = pl.DeviceIdType.MESH


def _pcall(body, *, name, grid, in_specs, out_specs, out_shape, scratch=()):
    return pl.pallas_call(
        body, name=name, grid=grid, in_specs=in_specs, out_specs=out_specs, out_shape=out_shape,
        scratch_shapes=scratch,
        compiler_params=pltpu.CompilerParams(dimension_semantics=("arbitrary",) * len(grid),
                                             vmem_limit_bytes=VMEM_LIMIT))


def _pcall_behind(body, comm, arrays, comm_out_shape, comm_sems, *, name, grid, in_specs, out_specs, out_shape,
                  scratch=()):
    n_in, n_out, n_scr, n = len(in_specs), len(out_specs), len(scratch), len(arrays)
    last = tuple(g - 1 for g in grid)

    def at(step):
        cond = pl.program_id(0) == step[0]
        for a in range(1, len(grid)):
            cond = cond & (pl.program_id(a) == step[a])
        return cond

    def wrapped(*refs):
        ins, cin = refs[:n_in], refs[n_in:n_in + n]
        outs, cout = refs[n_in + n:n_in + n + n_out], refs[n_in + n + n_out:n_in + 2 * n + n_out]
        scr, sems = refs[n_in + 2 * n + n_out:n_in + 2 * n + n_out + n_scr], refs[n_in + 2 * n + n_out + n_scr:]

        @pl.when(at((0,) * len(grid)))
        def _():
            comm(cin, cout, *sems, start=True)

        body(*ins, *outs, *scr)

        @pl.when(at(last))
        def _():
            comm(cin, cout, *sems, start=False)

    call = _pcall(wrapped, name=name, grid=grid, in_specs=list(in_specs) + [HBM_SPEC] * n,
                  out_specs=list(out_specs) + [HBM_SPEC] * n, out_shape=list(out_shape) + list(comm_out_shape),
                  scratch=list(scratch) + list(comm_sems))

    def run(*args):
        res = call(*args, *arrays)
        return res[:n_out], res[n_out:]

    return run


def _dot(a, b):
    return lax.dot_general(a, b, (((1,), (0,)), ((), ())), preferred_element_type=F32)


def _dot_nt(a, b):
    return lax.dot_general(a, b, (((1,), (1,)), ((), ())), preferred_element_type=F32)


def _lo(shape):
    return lax.broadcasted_iota(jnp.int32, shape, len(shape) - 1) < HD


def _half_sum(x, lo):
    a = jnp.sum(jnp.where(lo, x, 0.0), axis=-1, keepdims=True)
    b = jnp.sum(jnp.where(lo, 0.0, x), axis=-1, keepdims=True)
    return jnp.where(lo, a, b)


def _col(tile, lane):
    idx = lax.broadcasted_iota(jnp.int32, tile.shape, 1)
    return jnp.sum(jnp.where(idx == lane, tile, 0.0), axis=-1, keepdims=True)


def _split(t, lo):
    z = jnp.zeros_like(t)
    return jnp.where(lo, t, z), jnp.where(lo, z, t)


def _rms(xf, g):
    r = lax.rsqrt(jnp.mean(xf * xf, axis=-1, keepdims=True) + EPS)
    return xf * r * g


def _rms_bwd(xf, g, dy):
    r = lax.rsqrt(jnp.mean(xf * xf, axis=-1, keepdims=True) + EPS)
    xr = xf * r
    dg = jnp.sum(dy * xr, axis=0, keepdims=True)
    t = dy * g
    return r * (t - xr * jnp.mean(xr * t, axis=-1, keepdims=True)), dg


def _rope_fwd(y, c, s1, s2, sh):
    return y * c + pltpu.roll(y, sh, 1) * s1 + pltpu.roll(y, LANES - sh, 1) * s2


def _rope_bwd(dy, c, s1, s2, sh):
    return dy * c + pltpu.roll(dy * s1, LANES - sh, 1) + pltpu.roll(dy * s2, sh, 1)


def _tile(t):
    return slice(t * LANES, (t + 1) * LANES)


def inproj_fwd(x, g, w_pad, tabs, qkg, *, axial, tm=512):
    S = x.shape[0]
    sh = 16 if axial else 8

    def body(x_ref, g_ref, w_ref, c_ref, s1_ref, s2_ref, qkg_ref, h_ref, p_ref, *extra):
        h = _rms(x_ref[...], g_ref[...]).astype(BF)
        h_ref[...] = h
        acc = _dot(h, w_ref[...])
        c, s1, s2 = c_ref[...], s1_ref[...], s2_ref[...]
        lo = _lo((tm, LANES))
        if axial:
            raw_ref, nrm_ref = extra

            @pl.when(pl.program_id(0) == 0)
            def _():
                nrm_ref[...] = jnp.zeros_like(nrm_ref)

        for t in range(14):
            y = acc[:, _tile(t)]
            if t < V_T0:
                if axial:
                    raw_ref[:, _tile(t)] = y.astype(BF)
                    gt = qkg_ref[0:1, :] if t < K_T0 else qkg_ref[1:2, :]
                    y = y * lax.rsqrt(_half_sum(y * y, lo) * (1.0 / HD) + EPS) * gt
                y = _rope_fwd(y, c, s1, s2, sh)
            if t < K_T0:
                y = y * (SCALE * LOG2E if axial else SCALE)
            elif t >= M_T0:
                y = y * SCALE
            yb = y.astype(BF)
            p_ref[:, _tile(t)] = yb
            if axial and t < V_T0:
                yf = yb.astype(F32)
                n2 = jnp.max(_half_sum(yf * yf, lo), axis=0, keepdims=True)
                r = 0 if t < K_T0 else 1
                nrm_ref[r:r + 1, :] = jnp.maximum(nrm_ref[r:r + 1, :], n2)

    row = lambda w: pl.BlockSpec((tm, w), lambda i: (i, 0))
    full = lambda a: pl.BlockSpec(a.shape, lambda i: (0, 0))
    out_shape = [jax.ShapeDtypeStruct((S, D), BF), jax.ShapeDtypeStruct((S, PW), BF)]
    out_specs = [row(D), row(PW)]
    if axial:
        out_shape += [jax.ShapeDtypeStruct((S, V_T0 * LANES), BF), jax.ShapeDtypeStruct((8, LANES), F32)]
        out_specs += [row(V_T0 * LANES), pl.BlockSpec((8, LANES), lambda i: (0, 0))]
    return _pcall(body, name="inproj_fwd_axial" if axial else "inproj_fwd", grid=(S // tm,),
                  in_specs=[row(D), full(g), full(w_pad), row(LANES), row(LANES), row(LANES), full(qkg)],
                  out_specs=out_specs, out_shape=out_shape)(x, g, w_pad, *tabs, qkg)


def norm_mm(x, g, w, *, tm):
    S, N = x.shape[0], w.shape[1]

    def body(x_ref, g_ref, w_ref, h_ref, y_ref):
        h = _rms(x_ref[...], g_ref[...]).astype(BF)
        h_ref[...] = h
        y_ref[...] = _dot(h, w_ref[...]).astype(BF)

    return _pcall(body, name="norm_mm", grid=(S // tm,),
                  in_specs=[pl.BlockSpec((tm, D), lambda i: (i, 0)), pl.BlockSpec(g.shape, lambda i: (0, 0)),
                            pl.BlockSpec(w.shape, lambda i: (0, 0))],
                  out_specs=[pl.BlockSpec((tm, D), lambda i: (i, 0)), pl.BlockSpec((tm, N), lambda i: (i, 0))],
                  out_shape=[jax.ShapeDtypeStruct((S, D), BF), jax.ShapeDtypeStruct((S, N), BF)])(x, g, w)


def ffn_up_fwd(x, g, wg, wu, next_shards=(), *, tm=512, tn=1408):
    S = x.shape[0]
    n = len(next_shards)
    last = (S // tm - 1, DFF // tn - 1)

    def body(x_ref, g_ref, wg_ref, wu_ref, *refs):
        comm_in, (h_ref, gate_ref, up_ref, a_ref), comm_out = refs[:n], refs[n:n + 4], refs[n + 4:2 * n + 4]
        h_scr, sems = refs[2 * n + 4], refs[2 * n + 5:]
        if n:
            @pl.when((pl.program_id(0) == 0) & (pl.program_id(1) == 0))
            def _():
                _gather_comm(comm_in, comm_out, *sems, start=True)

        @pl.when(pl.program_id(1) == 0)
        def _():
            h = _rms(x_ref[...], g_ref[...]).astype(BF)
            h_scr[...] = h
            h_ref[...] = h

        h = h_scr[...]
        gate = _dot(h, wg_ref[...])
        up = _dot(h, wu_ref[...])
        sig = pl.reciprocal(1.0 + jnp.exp(-gate), approx=True)
        silu = gate * sig
        gate_ref[...] = (up * (sig * (1.0 + gate * (1.0 - sig)))).astype(BF)
        up_ref[...] = silu.astype(BF)
        a_ref[...] = (silu * up).astype(BF)
        if n:
            @pl.when((pl.program_id(0) == last[0]) & (pl.program_id(1) == last[1]))
            def _():
                _gather_comm(comm_in, comm_out, *sems, start=False)

    rowd = pl.BlockSpec((tm, D), lambda i, j: (i, 0))
    wsp = pl.BlockSpec((D, tn), lambda i, j: (0, j))
    osp = pl.BlockSpec((tm, tn), lambda i, j: (i, j))
    sd = jax.ShapeDtypeStruct((S, DFF), BF)
    outs = _pcall(body, name="ffn_up_fwd_gather" if n else "ffn_up_fwd", grid=(S // tm, DFF // tn),
                  in_specs=[rowd, pl.BlockSpec(g.shape, lambda i, j: (0, 0)), wsp, wsp] + [HBM_SPEC] * n,
                  out_specs=[rowd, osp, osp, osp] + [HBM_SPEC] * n,
                  out_shape=[jax.ShapeDtypeStruct((S, D), BF), sd, sd, sd] + _gather_out_shape(next_shards),
                  scratch=[pltpu.VMEM((tm, D), BF)] + (_gather_sems(n) if n else []))(x, g, wg, wu, *next_shards)
    return outs[:4], outs[4:]


def mm_norm_res(a, w, g, res, *, tm=512):
    S, K = a.shape

    def body(a_ref, w_ref, g_ref, res_ref, y_ref, o_ref):
        y = _dot(a_ref[...], w_ref[...])
        y_ref[...] = y
        o_ref[...] = res_ref[...] + _rms(y, g_ref[...])

    rowd = pl.BlockSpec((tm, D), lambda i: (i, 0))
    sd = jax.ShapeDtypeStruct((S, D), F32)
    return _pcall(body, name="mm_norm_res", grid=(S // tm,),
                  in_specs=[pl.BlockSpec((tm, K), lambda i: (i, 0)), pl.BlockSpec(w.shape, lambda i: (0, 0)),
                            pl.BlockSpec(g.shape, lambda i: (0, 0)), rowd],
                  out_specs=[rowd, rowd], out_shape=[sd, sd])(a, w, g, res)


def loss_bwd(y, tgt, *, tm=512):
    S = y.shape[0]

    def body(y_ref, t_ref, dy_ref, l_ref):
        @pl.when(pl.program_id(0) == 0)
        def _():
            l_ref[...] = jnp.zeros_like(l_ref)

        e = y_ref[...] - t_ref[...]
        dy_ref[...] = e * (1.0 / D)
        col = jnp.sum(e * e, axis=0, keepdims=True)
        part = col[:, _tile(0)]
        for t in range(1, D // LANES):
            part = part + col[:, _tile(t)]
        l_ref[...] += part * (0.5 / D)

    rowd = pl.BlockSpec((tm, D), lambda i: (i, 0))
    return _pcall(body, name="loss_bwd", grid=(S // tm,), in_specs=[rowd, rowd],
                  out_specs=[rowd, pl.BlockSpec((1, LANES), lambda i: (0, 0))],
                  out_shape=[jax.ShapeDtypeStruct((S, D), F32), jax.ShapeDtypeStruct((1, LANES), F32)])(y, tgt)


def normbwd_mm_cat(dy, ysaved, g, w, cat, *, tm=512):
    S = dy.shape[0]

    def body(dy_ref, y_ref, g_ref, w_ref, cat_ref, d_ref, dcat_ref, delta_ref, dg_ref):
        @pl.when(pl.program_id(0) == 0)
        def _():
            dg_ref[...] = jnp.zeros_like(dg_ref)

        d, dg = _rms_bwd(y_ref[...], g_ref[...], dy_ref[...])
        dg_ref[...] += dg
        d = d.astype(BF)
        d_ref[...] = d
        z = _dot_nt(d, w_ref[...])
        dcat_ref[...] = z.astype(BF)
        lo = _lo((tm, LANES))
        for t in range(D // LANES):
            delta_ref[:, _tile(t)] = _half_sum(z[:, _tile(t)] * cat_ref[:, _tile(t)].astype(F32), lo)

    rowd = pl.BlockSpec((tm, D), lambda i: (i, 0))
    return _pcall(body, name="normbwd_mm_cat", grid=(S // tm,),
                  in_specs=[rowd, rowd, pl.BlockSpec(g.shape, lambda i: (0, 0)),
                            pl.BlockSpec(w.shape, lambda i: (0, 0)), rowd],
                  out_specs=[rowd, rowd, rowd, pl.BlockSpec((1, D), lambda i: (0, 0))],
                  out_shape=[jax.ShapeDtypeStruct((S, D), BF), jax.ShapeDtypeStruct((S, D), BF),
                             jax.ShapeDtypeStruct((S, D), F32), jax.ShapeDtypeStruct((1, D), F32)])(dy, ysaved, g, w, cat)


def normbwd_mm_swiglu(dy, fsaved, g, wd, gate, up, grad_parts=(), *, tm=512, tn=1408):
    S = dy.shape[0]
    n = len(grad_parts)
    last = (S // tm - 1, DFF // tn - 1)

    def body(dy_ref, f_ref, g_ref, w_ref, gate_ref, up_ref, *refs):
        comm_in, (df_ref, dgate_ref, dup_ref, dg_ref), comm_out = refs[:n], refs[n:n + 4], refs[n + 4:2 * n + 4]
        d_scr, sems = refs[2 * n + 4], refs[2 * n + 5:]
        i, j = pl.program_id(0), pl.program_id(1)

        @pl.when((i == 0) & (j == 0))
        def _():
            dg_ref[...] = jnp.zeros_like(dg_ref)
            if n:
                _scatter_comm(comm_in, comm_out, *sems, start=True)

        @pl.when(j == 0)
        def _():
            d, dg = _rms_bwd(f_ref[...], g_ref[...], dy_ref[...])
            dg_ref[...] += dg
            d_scr[...] = d.astype(BF)
            df_ref[...] = d.astype(BF)

        da = _dot_nt(d_scr[...], w_ref[...])
        dgate_ref[...] = (da * gate_ref[...].astype(F32)).astype(BF)
        dup_ref[...] = (da * up_ref[...].astype(F32)).astype(BF)
        if n:
            @pl.when((i == last[0]) & (j == last[1]))
            def _():
                _scatter_comm(comm_in, comm_out, *sems, start=False)

    rowd = pl.BlockSpec((tm, D), lambda i, j: (i, 0))
    osp = pl.BlockSpec((tm, tn), lambda i, j: (i, j))
    sd = jax.ShapeDtypeStruct((S, DFF), BF)
    outs = _pcall(body, name="normbwd_mm_swiglu_scatter" if n else "normbwd_mm_swiglu", grid=(S // tm, DFF // tn),
                  in_specs=[rowd, rowd, pl.BlockSpec(g.shape, lambda i, j: (0, 0)),
                            pl.BlockSpec((tn, D), lambda i, j: (j, 0)), osp, osp] + [HBM_SPEC] * n,
                  out_specs=[rowd, osp, osp, pl.BlockSpec((1, D), lambda i, j: (0, 0))] + [HBM_SPEC] * n,
                  out_shape=[jax.ShapeDtypeStruct((S, D), BF), sd, sd, jax.ShapeDtypeStruct((1, D), F32)]
                  + _scatter_out_shape(grad_parts),
                  scratch=[pltpu.VMEM((tm, D), BF)] + (_scatter_sems(n) if n else []))(
        dy, fsaved, g, wd, gate, up, *grad_parts)
    return outs[:4], outs[4:]


def mm_nt_normbwd_res(parts, xin, g, dres, *, tm=512):
    S = xin.shape[0]
    npart = len(parts)
    has_res = dres is not None

    def body(*refs):
        prefs = refs[:2 * npart]
        x_ref, g_ref = refs[2 * npart:2 * npart + 2]
        rest = refs[2 * npart + 2:]
        if has_res:
            dres_ref, dx_ref, dg_ref = rest
        else:
            (dg_ref,) = rest

        @pl.when(pl.program_id(0) == 0)
        def _():
            dg_ref[...] = jnp.zeros_like(dg_ref)

        z = _dot_nt(prefs[0][...], prefs[1][...])
        for p in range(1, npart):
            z = z + _dot_nt(prefs[2 * p][...], prefs[2 * p + 1][...])
        dx, dg = _rms_bwd(x_ref[...], g_ref[...], z)
        dg_ref[...] += dg
        if has_res:
            dx_ref[...] = dres_ref[...] + dx

    rowd = pl.BlockSpec((tm, D), lambda i: (i, 0))
    in_specs, args = [], []
    for dy, w in parts:
        in_specs += [pl.BlockSpec((tm, dy.shape[1]), lambda i: (i, 0)),
                     pl.BlockSpec(w.shape, lambda i: (0, 0), pipeline_mode=pl.Buffered(1))]
        args += [dy, w]
    in_specs += [rowd, pl.BlockSpec(g.shape, lambda i: (0, 0))]
    args += [xin, g]
    out_specs = [pl.BlockSpec((1, D), lambda i: (0, 0))]
    out_shape = [jax.ShapeDtypeStruct((1, D), F32)]
    if has_res:
        in_specs.append(rowd)
        args.append(dres)
        out_specs.insert(0, rowd)
        out_shape.insert(0, jax.ShapeDtypeStruct((S, D), F32))
    return _pcall(body, name="mm_nt_normbwd_res" if has_res else "mm_nt_normbwd", grid=(S // tm,),
                  in_specs=in_specs, out_specs=out_specs, out_shape=out_shape)(*args)


def inproj_bwd(dq, dkp, dvp, dqm, tabs, raw, qkg, w_pad, xin, g, dres, *, axial, tm=512):
    S = xin.shape[0]
    sh = 16 if axial else 8

    def body(dq_ref, dk_ref, dv_ref, dm_ref, c_ref, s1_ref, s2_ref, raw_ref, qkg_ref, w_ref, x_ref, g_ref, dres_ref,
             dx_ref, dp_ref, dg_ref, dqk_ref):
        @pl.when(pl.program_id(0) == 0)
        def _():
            dg_ref[...] = jnp.zeros_like(dg_ref)
            dqk_ref[...] = jnp.zeros_like(dqk_ref)

        c, s1, s2 = c_ref[...], s1_ref[...], s2_ref[...]
        lo = _lo((tm, LANES))
        for t in range(14):
            if t < K_T0:
                y = dq_ref[:, _tile(t)] * SCALE
            elif t < V_T0:
                y = dk_ref[:, _tile(2 * (t - K_T0))] + dk_ref[:, _tile(2 * (t - K_T0) + 1)]
                if axial:
                    y = y * LN2
            elif t < M_T0:
                y = dv_ref[:, _tile(2 * (t - V_T0))] + dv_ref[:, _tile(2 * (t - V_T0) + 1)]
            else:
                y = dm_ref[:, _tile(t - M_T0)] * SCALE
            if t < V_T0:
                y = _rope_bwd(y, c, s1, s2, sh)
                if axial:
                    row = 0 if t < K_T0 else 1
                    xr = raw_ref[:, _tile(t)].astype(F32)
                    r = lax.rsqrt(_half_sum(xr * xr, lo) * (1.0 / HD) + EPS)
                    xn = xr * r
                    dqk_ref[row:row + 1, :] += jnp.sum(y * xn, axis=0, keepdims=True)
                    tt = y * qkg_ref[row:row + 1, :]
                    y = r * (tt - xn * (_half_sum(xn * tt, lo) * (1.0 / HD)))
            dp_ref[:, _tile(t)] = y.astype(BF)
        z = _dot_nt(dp_ref[...], w_ref[...])
        dx, dg = _rms_bwd(x_ref[...], g_ref[...], z)
        dg_ref[...] += dg
        dx_ref[...] = dres_ref[...] + dx

    row = lambda w: pl.BlockSpec((tm, w), lambda i: (i, 0))
    full = lambda a: pl.BlockSpec(a.shape, lambda i: (0, 0))
    return _pcall(body, name="inproj_bwd_axial" if axial else "inproj_bwd", grid=(S // tm,),
                  in_specs=[row(768), row(768), row(768), row(256), row(LANES), row(LANES), row(LANES),
                            row(raw.shape[1] if axial else LANES), full(qkg), full(w_pad), row(D), full(g), row(D)],
                  out_specs=[row(D), row(PW), pl.BlockSpec((1, D), lambda i: (0, 0)),
                             pl.BlockSpec((8, LANES), lambda i: (0, 0))],
                  out_shape=[jax.ShapeDtypeStruct((S, D), F32), jax.ShapeDtypeStruct((S, PW), BF),
                             jax.ShapeDtypeStruct((1, D), F32), jax.ShapeDtypeStruct((8, LANES), F32)])(
        dq, dkp, dvp, dqm, *tabs, raw, qkg, w_pad, xin, g, dres)


def mm_acc(a, b, *, tk, tn, ts):
    S, K = a.shape
    N = b.shape[1]
    ts = min(ts, S)

    def body(a_ref, b_ref, o_ref):
        z = lax.dot_general(a_ref[...], b_ref[...], (((0,), (0,)), ((), ())), preferred_element_type=F32)

        @pl.when(pl.program_id(2) == 0)
        def _():
            o_ref[...] = z

        @pl.when(pl.program_id(2) > 0)
        def _():
            o_ref[...] += z

    return _pcall(body, name="mm_acc", grid=(K // tk, N // tn, S // ts),
                  in_specs=[pl.BlockSpec((ts, tk), lambda k, n, s: (s, k)), pl.BlockSpec((ts, tn), lambda k, n, s: (s, n))],
                  out_specs=pl.BlockSpec((tk, tn), lambda k, n, s: (k, n)),
                  out_shape=jax.ShapeDtypeStruct((K, N), F32))(a, b)


def _band_specs(L, d, R, T, width, bw, col_of):
    n = T // R
    nb = width // bw
    last = L // R - 1
    col = lambda g, r: r * nb + col_of(g)
    return [pl.BlockSpec((R, bw), lambda g, r, i: (jnp.maximum(i * n - 1, 0), col(g, r))),
            pl.BlockSpec((T, bw), lambda g, r, i: (i, col(g, r))),
            pl.BlockSpec((R, bw), lambda g, r, i: (jnp.minimum((i + 1) * n, last), col(g, r)))]


def _band_tile(R, L):
    return min(max(2 * R, 256), L)


def _band_bias(T, R):
    w = lax.broadcasted_iota(jnp.int32, (T, T + 2 * R), 1)
    c = lax.broadcasted_iota(jnp.int32, (T, T + 2 * R), 0)
    return jnp.where(jnp.abs(w - R - c) <= R, 0.0, NEG).astype(F32)


def _edge_bias(i, T, R, L):
    wpos = i * T - R + lax.broadcasted_iota(jnp.int32, (1, T + 2 * R), 1)
    return jnp.where((wpos >= 0) & (wpos < L), 0.0, NEG)


def banded_fwd(proj, sink, *, d, R, TQ, pair0, npairs, use_sink, o_dtype, gather=()):
    S = proj.shape[0]
    L = S // d
    pv = proj.reshape(L, d * PW)
    ow = npairs * LANES

    def body(sink_ref, bias_ref, q_ref, kp, kc, kn, vp, vc, vn, o_ref, lse_ref):
        g, i = pl.program_id(0), pl.program_id(2)
        bias = bias_ref[...] + _edge_bias(i, TQ, R, L)
        lo = _lo((TQ, LANES))
        kw = jnp.concatenate([kp[...], kc[...], kn[...]], axis=0)
        vw = jnp.concatenate([vp[...], vc[...], vn[...]], axis=0)
        v_lo, v_hi = _split(vw, _lo(vw.shape))
        vcat = jnp.concatenate([v_lo, v_hi], axis=0)
        for t in range(2):
            qa, qb = _split(q_ref[:, _tile(t)], lo)
            ps, dens, lses = [], [], []
            for h, qh in enumerate((qa, qb)):
                s = _dot_nt(qh, kw) + bias
                m = jnp.max(s, axis=-1, keepdims=True)
                if use_sink:
                    sk = sink_ref[2 * (pair0 + 2 * g + t) + h]
                    m = jnp.maximum(m, sk)
                e = jnp.exp(s - m)
                den = jnp.sum(e, axis=-1, keepdims=True)
                if use_sink:
                    den = den + jnp.exp(sk - m)
                ps.append(e.astype(BF))
                dens.append(den)
                lses.append(m + jnp.log(den))
            o = _dot(jnp.concatenate(ps, axis=1), vcat)
            o_ref[:, _tile(t)] = (o / jnp.where(lo, dens[0], dens[1])).astype(o_dtype)
            lse_ref[:, _tile(t)] = jnp.where(lo, lses[0], lses[1])

    g0 = pair0 // 2
    qspec = pl.BlockSpec((TQ, 2 * LANES), lambda g, r, i: (i, r * 7 + g0 + g))
    kspecs = _band_specs(L, d, R, TQ, PW, LANES, lambda g: K_T0 + g0 + g)
    vspecs = _band_specs(L, d, R, TQ, PW, LANES, lambda g: V_T0 + g0 + g)
    ospec = pl.BlockSpec((TQ, 2 * LANES), lambda g, r, i: (i, r * (npairs // 2) + g))
    bias = _band_bias(TQ, R)
    kw = dict(grid=(npairs // 2, d, L // TQ),
              in_specs=[pl.BlockSpec(memory_space=pltpu.SMEM), pl.BlockSpec(bias.shape, lambda g, r, i: (0, 0)),
                        qspec] + kspecs + vspecs,
              out_specs=[ospec, ospec],
              out_shape=[jax.ShapeDtypeStruct((L, d * ow), o_dtype), jax.ShapeDtypeStruct((L, d * ow), F32)])
    args = (sink, bias, pv, pv, pv, pv, pv, pv, pv)
    if gather:
        (o, lse), gathered = _pcall_behind(body, _gather_comm, gather, _gather_out_shape(gather),
                                           _gather_sems(len(gather)), name="banded_fwd_gather", **kw)(*args)
        return o.reshape(S, ow), lse.reshape(S, ow), gathered
    o, lse = _pcall(body, name="banded_fwd", **kw)(*args)
    return o.reshape(S, ow), lse.reshape(S, ow)


def banded_bwd_dq(proj, do, lse, delta, sink, *, d, R, TQ, pair0, npairs, use_sink, scatter=()):
    S = proj.shape[0]
    L = S // d
    pv = proj.reshape(L, d * PW)
    ow = npairs * LANES

    def body(sink_ref, bias_ref, q_ref, kp, kc, kn, vp, vc, vn, do_ref, lse_ref, delta_ref, dq_ref, dsink_ref):
        g, r, i = pl.program_id(0), pl.program_id(1), pl.program_id(2)

        @pl.when((r == 0) & (i == 0))
        def _():
            dsink_ref[...] = jnp.zeros_like(dsink_ref)

        bias = bias_ref[...] + _edge_bias(i, TQ, R, L)
        lo = _lo((TQ, LANES))
        kw = jnp.concatenate([kp[...], kc[...], kn[...]], axis=0)
        vw = jnp.concatenate([vp[...], vc[...], vn[...]], axis=0)
        k_lo, k_hi = _split(kw, _lo(kw.shape))
        kcat = jnp.concatenate([k_lo, k_hi], axis=0)
        for t in range(2):
            qa, qb = _split(q_ref[:, _tile(t)], lo)
            doa, dob = _split(do_ref[:, _tile(t)], lo)
            lse_t, delta_t = lse_ref[:, _tile(t)], delta_ref[:, _tile(t)]
            dss, dsk = [], []
            for h, (qh, doh) in enumerate(((qa, doa), (qb, dob))):
                lse_h, delta_h = _col(lse_t, h * HD), _col(delta_t, h * HD)
                pr = jnp.exp(_dot_nt(qh, kw) + bias - lse_h)
                dss.append((pr * (_dot_nt(doh, vw) - delta_h)).astype(BF))
                if use_sink:
                    psink = jnp.exp(sink_ref[2 * (pair0 + 2 * g + t) + h] - lse_h)
                    dsk.append(-jnp.sum(psink * delta_h, axis=0, keepdims=True))
            dq_ref[:, _tile(t)] = _dot(jnp.concatenate(dss, axis=1), kcat)
            if use_sink:
                dsink_ref[:, _tile(t)] += jnp.where(_lo((8, LANES)), dsk[0], dsk[1])

    g0 = pair0 // 2
    qspec = pl.BlockSpec((TQ, 2 * LANES), lambda g, r, i: (i, r * 7 + g0 + g))
    kspecs = _band_specs(L, d, R, TQ, PW, LANES, lambda g: K_T0 + g0 + g)
    vspecs = _band_specs(L, d, R, TQ, PW, LANES, lambda g: V_T0 + g0 + g)
    ospec = pl.BlockSpec((TQ, 2 * LANES), lambda g, r, i: (i, r * (npairs // 2) + g))
    view = lambda a: a.reshape(L, d * a.shape[1])
    ispec = lambda a: pl.BlockSpec((TQ, 2 * LANES), lambda g, r, i: (i, r * (a.shape[1] // (2 * LANES)) + g))
    bias = _band_bias(TQ, R)
    kw = dict(grid=(npairs // 2, d, L // TQ),
              in_specs=[pl.BlockSpec(memory_space=pltpu.SMEM), pl.BlockSpec(bias.shape, lambda g, r, i: (0, 0)),
                        qspec] + kspecs + vspecs + [ispec(do), ispec(lse), ispec(delta)],
              out_specs=[ospec, pl.BlockSpec((8, 2 * LANES), lambda g, r, i: (g, 0))],
              out_shape=[jax.ShapeDtypeStruct((L, d * ow), F32),
                         jax.ShapeDtypeStruct((npairs // 2 * 8, 2 * LANES), F32)])
    args = (sink, bias, pv, pv, pv, pv, pv, pv, pv, view(do), view(lse), view(delta))
    if scatter:
        (dq, dsink), got = _pcall_behind(body, _scatter_comm, scatter, _scatter_out_shape(scatter),
                                         _scatter_sems(len(scatter)), name="banded_bwd_dq_scatter", **kw)(*args)
        return dq.reshape(S, ow), dsink, got
    dq, dsink = _pcall(body, name="banded_bwd_dq", **kw)(*args)
    return dq.reshape(S, ow), dsink


def banded_bwd_dkv(proj, do, lse, delta, *, d, R, TK, pair0, npairs):
    S = proj.shape[0]
    L = S // d
    pv = proj.reshape(L, d * PW)
    ow = npairs * LANES

    def body(bias_ref, k_ref, v_ref, qp, qc, qn, dop, doc, don, lp, lc, ln, dp_, dc_, dn_, dk_ref, dv_ref):
        j = pl.program_id(2)
        W = TK + 2 * R
        bias = bias_ref[...] + _edge_bias(j, TK, R, L)
        low = _lo((W, LANES))
        qw = jnp.concatenate([qp[...], qc[...], qn[...]], axis=0)
        dow = jnp.concatenate([dop[...], doc[...], don[...]], axis=0)
        lse_w = jnp.concatenate([lp[...], lc[...], ln[...]], axis=0)
        delta_w = jnp.concatenate([dp_[...], dc_[...], dn_[...]], axis=0)
        k, v = k_ref[...], v_ref[...]
        for t in range(2):
            qa, qb = _split(qw[:, _tile(t)], low)
            doa, dob = _split(dow[:, _tile(t)], low)
            lse_r, delta_r = lse_w[:, _tile(t)].T, delta_w[:, _tile(t)].T
            prs, dss = [], []
            for h, (qh, doh) in enumerate(((qa, doa), (qb, dob))):
                pr = jnp.exp(_dot_nt(k, qh) + bias - lse_r[h * HD:h * HD + 1, :])
                dss.append((pr * (_dot_nt(v, doh) - delta_r[h * HD:h * HD + 1, :])).astype(BF))
                prs.append(pr.astype(BF))
            dv_ref[:, _tile(t)] = _dot(jnp.concatenate(prs, axis=1), jnp.concatenate([doa, dob], axis=0))
            dk_ref[:, _tile(t)] = _dot(jnp.concatenate(dss, axis=1), jnp.concatenate([qa, qb], axis=0))

    g0 = pair0 // 2
    kspec = pl.BlockSpec((TK, LANES), lambda g, r, j: (j, r * 14 + K_T0 + g0 + g))
    vspec = pl.BlockSpec((TK, LANES), lambda g, r, j: (j, r * 14 + V_T0 + g0 + g))
    qspecs = _band_specs(L, d, R, TK, PW, 2 * LANES, lambda g: g0 + g)
    ispecs = lambda a: _band_specs(L, d, R, TK, a.shape[1], 2 * LANES, lambda g: g)
    view = lambda a: a.reshape(L, d * a.shape[1])
    ospec = pl.BlockSpec((TK, 2 * LANES), lambda g, r, j: (j, r * (npairs // 2) + g))
    sd = jax.ShapeDtypeStruct((L, d * ow), F32)
    bias = _band_bias(TK, R)
    dk, dv = _pcall(body, name="banded_bwd_dkv", grid=(npairs // 2, d, L // TK),
                    in_specs=[pl.BlockSpec(bias.shape, lambda g, r, j: (0, 0)), kspec, vspec] + qspecs + ispecs(do)
                    + ispecs(lse) + ispecs(delta),
                    out_specs=[ospec, ospec], out_shape=[sd, sd])(
        bias, pv, pv, pv, pv, pv, *([view(do)] * 3), *([view(lse)] * 3), *([view(delta)] * 3))
    return dk.reshape(S, ow), dv.reshape(S, ow)


def flash_fwd(proj, bound, *, tq=1024, tk=1024):
    S = proj.shape[0]

    def body_general(q_ref, k_ref, v_ref, o_ref, lse_ref):
        lo = _lo((tq, LANES))
        qa, qb = _split(q_ref[...], lo)
        lov = _lo((tk, LANES))

        def step(j, carry):
            ma, la, mb, lb, acc = carry
            rows = pl.ds(pl.multiple_of(j * tk, tk), tk)
            k, v = k_ref[rows, :], v_ref[rows, :]
            outs = []
            for qh, m0, l0 in ((qa, ma, la), (qb, mb, lb)):
                s = _dot_nt(qh, k)
                m1 = jnp.maximum(m0, jnp.max(s, axis=-1, keepdims=True))
                al = jnp.exp2(m0 - m1)
                e = jnp.exp2(s - m1)
                outs.append((m1, al * l0 + jnp.sum(e, axis=-1, keepdims=True), al, e.astype(BF)))
            v_lo, v_hi = _split(v, lov)
            pvv = _dot(jnp.concatenate([outs[0][3], outs[1][3]], axis=1), jnp.concatenate([v_lo, v_hi], axis=0))
            acc = acc * jnp.where(lo, outs[0][2], outs[1][2]) + pvv
            return outs[0][0], outs[0][1], outs[1][0], outs[1][1], acc

        m_init = jnp.full((tq, 1), NEG, F32)
        l_init = jnp.zeros((tq, 1), F32)
        ma, la, mb, lb, acc = lax.fori_loop(0, S // tk, step,
                                            (m_init, l_init, m_init, l_init, jnp.zeros((tq, LANES), F32)))
        o_ref[...] = (acc / jnp.where(lo, la, lb)).astype(BF)
        lse_ref[...] = jnp.where(lo, ma * LN2 + jnp.log(la), mb * LN2 + jnp.log(lb))

    def body_plain(q_ref, k_ref, v_ref, o_ref, lse_ref):
        lo = _lo((tq, LANES))
        qa, qb = _split(q_ref[...], lo)
        lov = _lo((tk, LANES))
        one = jnp.ones((tk, LANES), BF)

        def step(j, carry):
            acc_a, acc_b = carry
            rows = pl.ds(pl.multiple_of(j * tk, tk), tk)
            k, v = k_ref[rows, :], v_ref[rows, :]
            ea = jnp.exp2(_dot_nt(qa, k)).astype(BF)
            eb = jnp.exp2(_dot_nt(qb, k)).astype(BF)
            acc_a = acc_a + _dot(ea, jnp.where(lov, v, one))
            acc_b = acc_b + _dot(eb, jnp.where(lov, one, v))
            return acc_a, acc_b

        z = jnp.zeros((tq, LANES), F32)
        acc_a, acc_b = lax.fori_loop(0, S // tk, step, (z, z))
        den = jnp.where(lo, pltpu.roll(acc_a, HD, 1), pltpu.roll(acc_b, HD, 1))
        o_ref[...] = (jnp.where(lo, acc_a, acc_b) / den).astype(BF)
        lse_ref[...] = jnp.log(den)

    def body(bound_ref, q_ref, k_ref, v_ref, o_ref, lse_ref):
        small = bound_ref[0] <= MAX_PLAIN_SCORE

        @pl.when(small)
        def _():
            body_plain(q_ref, k_ref, v_ref, o_ref, lse_ref)

        @pl.when(jnp.logical_not(small))
        def _():
            body_general(q_ref, k_ref, v_ref, o_ref, lse_ref)

    ospec = pl.BlockSpec((tq, LANES), lambda p, i: (i, p))
    return _pcall(body, name="flash_fwd", grid=(N_PAIRS, S // tq),
                  in_specs=[pl.BlockSpec(memory_space=pltpu.SMEM), ospec,
                            pl.BlockSpec((S, LANES), lambda p, i: (0, K_T0 + p // 2)),
                            pl.BlockSpec((S, LANES), lambda p, i: (0, V_T0 + p // 2))],
                  out_specs=[ospec, ospec],
                  out_shape=[jax.ShapeDtypeStruct((S, 768), BF), jax.ShapeDtypeStruct((S, 768), F32)])(
        bound.reshape(1), proj, proj, proj)


def flash_bwd(proj, do, lse, delta, *, tq=1024, tk=1024):
    S = proj.shape[0]
    nq = S // tq
    lse_t, delta_t = _stats_t(lse, tq), _stats_t(delta, tq)

    def body(k_ref, v_ref, q_ref, do_ref, lse_ref, delta_ref, dk_ref, dv_ref, dqt_ref):
        @pl.when(pl.program_id(1) == 0)
        def _():
            dqt_ref[...] = jnp.zeros_like(dqt_ref)

        k, v = k_ref[...], v_ref[...]
        lo = _lo((tq, LANES))
        k_lo, k_hi = _split(k.astype(F32), _lo((tk, LANES)))
        kt = jnp.concatenate([k_lo.T, k_hi.T], axis=1).astype(BF)

        def step(i, carry):
            dk, dv = carry
            rows = pl.ds(pl.multiple_of(i * tq, tq), tq)
            qa, qb = _split(q_ref[rows, :], lo)
            doa, dob = _split(do_ref[rows, :], lo)
            lse_i, delta_i = lse_ref[i] * LOG2E, delta_ref[i]
            prs, dss = [], []
            for h, (qh, doh) in enumerate(((qa, doa), (qb, dob))):
                pr = jnp.exp2(_dot_nt(k, qh) - lse_i[h:h + 1, :])
                dss.append((pr * (_dot_nt(v, doh) - delta_i[h:h + 1, :])).astype(BF))
                prs.append(pr.astype(BF))
            dv = dv + _dot(jnp.concatenate(prs, axis=1), jnp.concatenate([doa, dob], axis=0))
            dk = dk + _dot(jnp.concatenate(dss, axis=1), jnp.concatenate([qa, qb], axis=0))
            dqt_ref[i] += _dot(kt, jnp.concatenate(dss, axis=0))
            return dk, dv

        z = jnp.zeros((tk, LANES), F32)
        dk, dv = lax.fori_loop(0, nq, step, (z, z))
        dk_ref[...] = dk
        dv_ref[...] = dv

    ospec = pl.BlockSpec((tk, LANES), lambda p, j: (j, p))
    stat = pl.BlockSpec((None, nq, 8, tq), lambda p, j: (p, 0, 0, 0))
    whole = lambda: pl.BlockSpec((S, LANES), lambda p, j: (0, p), pipeline_mode=pl.Buffered(1))
    sd = jax.ShapeDtypeStruct((S, 768), F32)
    dk, dv, dqt = _pcall(body, name="flash_bwd", grid=(N_PAIRS, S // tk),
                         in_specs=[pl.BlockSpec((tk, LANES), lambda p, j: (j, K_T0 + p // 2)),
                                   pl.BlockSpec((tk, LANES), lambda p, j: (j, V_T0 + p // 2)), whole(), whole(),
                                   stat, stat],
                         out_specs=[ospec, ospec,
                                    pl.BlockSpec((None, nq, LANES, tq), lambda p, j: (p, 0, 0, 0),
                                                 pipeline_mode=pl.Buffered(1))],
                         out_shape=[sd, sd, jax.ShapeDtypeStruct((N_PAIRS, nq, LANES, tq), F32)])(
        proj, proj, proj, do, lse_t, delta_t)
    dq = jnp.transpose(dqt, (1, 3, 0, 2)).reshape(S, 768)
    return dq, dk, dv


def mem_fwd(proj, mkv, *, tq=512):
    S = proj.shape[0]

    def body(q_ref, km_ref, vm_ref, o_ref, lse_ref):
        lo = _lo((tq, LANES))
        lov = _lo((N_MEM, LANES))
        for t in range(2):
            qa, qb = _split(q_ref[:, _tile(t)], lo)
            km, vm = km_ref[:, _tile(t)], vm_ref[:, _tile(t)]
            ps, dens, lses = [], [], []
            for qh in (qa, qb):
                s = _dot_nt(qh, km)
                m = jnp.max(s, axis=-1, keepdims=True)
                e = jnp.exp(s - m)
                den = jnp.sum(e, axis=-1, keepdims=True)
                ps.append(e.astype(BF))
                dens.append(den)
                lses.append(m + jnp.log(den))
            v_lo, v_hi = _split(vm, lov)
            o = _dot(jnp.concatenate(ps, axis=1), jnp.concatenate([v_lo, v_hi], axis=0))
            o_ref[:, _tile(t)] = (o / jnp.where(lo, dens[0], dens[1])).astype(BF)
            lse_ref[:, _tile(t)] = jnp.where(lo, lses[0], lses[1])

    ospec = pl.BlockSpec((tq, 256), lambda i: (i, 0))
    return _pcall(body, name="mem_fwd", grid=(S // tq,),
                  in_specs=[pl.BlockSpec((tq, 256), lambda i: (i, M_T0 // 2)),
                            pl.BlockSpec((N_MEM, 256), lambda i: (0, 0)), pl.BlockSpec((N_MEM, 256), lambda i: (0, 1))],
                  out_specs=[ospec, ospec],
                  out_shape=[jax.ShapeDtypeStruct((S, 256), BF), jax.ShapeDtypeStruct((S, 256), F32)])(proj, mkv, mkv)


def mem_bwd(proj, mkv, dcat, lse, delta, *, tq=512):
    S = proj.shape[0]

    def body(q_ref, km_ref, vm_ref, do_ref, lse_ref, delta_ref, dq_ref, dkm_ref, dvm_ref):
        @pl.when(pl.program_id(0) == 0)
        def _():
            dkm_ref[...] = jnp.zeros_like(dkm_ref)
            dvm_ref[...] = jnp.zeros_like(dvm_ref)

        lo = _lo((tq, LANES))
        lov = _lo((N_MEM, LANES))
        for t in range(2):
            qa, qb = _split(q_ref[:, _tile(t)], lo)
            doa, dob = _split(do_ref[:, _tile(t)], lo)
            km, vm = km_ref[:, _tile(t)], vm_ref[:, _tile(t)]
            lse_t, delta_t = lse_ref[:, _tile(t)], delta_ref[:, _tile(t)]
            prs, dss = [], []
            for h, (qh, doh) in enumerate(((qa, doa), (qb, dob))):
                pr = jnp.exp(_dot_nt(qh, km) - _col(lse_t, h * HD))
                dss.append(pr * (_dot_nt(doh, vm) - _col(delta_t, h * HD)))
                prs.append(pr)
            k_lo, k_hi = _split(km, lov)
            dq_ref[:, _tile(t)] = _dot(jnp.concatenate(dss, axis=1).astype(BF), jnp.concatenate([k_lo, k_hi], axis=0))
            dvm_ref[:, _tile(t)] += _dot(jnp.concatenate(prs, axis=0).T.astype(BF), jnp.concatenate([doa, dob], axis=0))
            dkm_ref[:, _tile(t)] += _dot(jnp.concatenate(dss, axis=0).T.astype(BF), jnp.concatenate([qa, qb], axis=0))

    ospec = pl.BlockSpec((tq, 256), lambda i: (i, 0))
    msp = pl.BlockSpec((N_MEM, 256), lambda i: (0, 0))
    md = jax.ShapeDtypeStruct((N_MEM, 256), F32)
    dq, dkm, dvm = _pcall(body, name="mem_bwd", grid=(S // tq,),
                          in_specs=[pl.BlockSpec((tq, 256), lambda i: (i, M_T0 // 2)), msp,
                                    pl.BlockSpec((N_MEM, 256), lambda i: (0, 1)),
                                    pl.BlockSpec((tq, 256), lambda i: (i, 3)), ospec,
                                    pl.BlockSpec((tq, 256), lambda i: (i, 3))],
                          out_specs=[ospec, msp, msp], out_shape=[jax.ShapeDtypeStruct((S, 256), F32), md, md])(
        proj, mkv, mkv, dcat, lse, delta)
    return dq, jnp.concatenate([dkm, dvm], axis=1)


def combine_fwd(os_, lses, *, tm=512):
    S = os_[0].shape[0]

    def body(o0, o1, o2, l0, l1, l2, tok_ref):
        ls = [l0[...], l1[...], l2[...]]
        m = jnp.maximum(jnp.maximum(ls[0], ls[1]), ls[2])
        es = [jnp.exp(l - m) for l in ls]
        den = es[0] + es[1] + es[2]
        for g, o in enumerate((o0, o1, o2)):
            tok_ref[:, 256 * g:256 * (g + 1)] = (o[...] * (es[g] / den)).astype(BF)

    sp = pl.BlockSpec((tm, 256), lambda i: (i, 0))
    return _pcall(body, name="combine_fwd", grid=(S // tm,), in_specs=[sp] * 6,
                  out_specs=pl.BlockSpec((tm, 768), lambda i: (i, 0)),
                  out_shape=jax.ShapeDtypeStruct((S, 768), BF))(*os_, *lses)


def combine_bwd(dcat, os_, lses, *, tm=512):
    S = dcat.shape[0]

    def body(dt_ref, o0, o1, o2, l0, l1, l2, do0, do1, do2, de0, de1, de2):
        ls = [l0[...], l1[...], l2[...]]
        m = jnp.maximum(jnp.maximum(ls[0], ls[1]), ls[2])
        es = [jnp.exp(l - m) for l in ls]
        den = es[0] + es[1] + es[2]
        alphas = [e / den for e in es]
        lo = _lo((tm, LANES))
        dts = [dt_ref[:, 256 * g:256 * (g + 1)].astype(F32) for g in range(3)]
        dal = []
        for g, o in enumerate((o0, o1, o2)):
            pr = dts[g] * o[...]
            dal.append(jnp.concatenate([_half_sum(pr[:, _tile(0)], lo), _half_sum(pr[:, _tile(1)], lo)], axis=1))
        mix = alphas[0] * dal[0] + alphas[1] * dal[1] + alphas[2] * dal[2]
        for g, (do_ref, de_ref) in enumerate(((do0, de0), (do1, de1), (do2, de2))):
            do_ref[...] = (dts[g] * alphas[g]).astype(BF)
            de_ref[...] = alphas[g] * mix

    sp = pl.BlockSpec((tm, 256), lambda i: (i, 0))
    outs = _pcall(body, name="combine_bwd", grid=(S // tm,),
                  in_specs=[pl.BlockSpec((tm, 768), lambda i: (i, 0))] + [sp] * 6, out_specs=[sp] * 6,
                  out_shape=[jax.ShapeDtypeStruct((S, 256), BF)] * 3 + [jax.ShapeDtypeStruct((S, 256), F32)] * 3)(
        dcat, *os_, *lses)
    return outs[:3], outs[3:]


def _coords():
    return lax.axis_index("x"), lax.axis_index("y"), lax.axis_index("c")


def _other_chips(x, y):
    return [(1 - x, y), (x, 1 - y), (1 - x, 1 - y)]


HBM_SPEC = pl.BlockSpec(memory_space=pltpu.HBM)


def _gather_comm(ins, outs, send, recv, lsem, start):
    x, y, c = _coords()
    me = 2 * x + y
    for a in range(len(ins)):
        local = pltpu.make_async_copy(ins[a], outs[a].at[me], lsem.at[a])
        if start:
            local.start()
        for j, (px, py) in enumerate(_other_chips(x, y)):
            sems = dict(send_sem=send.at[3 * a + j], recv_sem=recv.at[3 * a + j], device_id=(px, py, c),
                        device_id_type=MESH)
            cp = pltpu.make_async_remote_copy(src_ref=ins[a], dst_ref=outs[a].at[me], **sems)
            if start:
                cp.start()
            else:
                pltpu.make_async_remote_copy(src_ref=ins[a], dst_ref=outs[a].at[2 * px + py], **sems).wait_recv()
                cp.wait_send()
        if not start:
            local.wait()


def _gather_out_shape(shards):
    return [jax.ShapeDtypeStruct((4,) + s.shape, s.dtype) for s in shards]


def _gather_sems(n):
    return [pltpu.SemaphoreType.DMA((3 * n,)), pltpu.SemaphoreType.DMA((3 * n,)), pltpu.SemaphoreType.DMA((n,))]


def gather_shards(shards):
    n = len(shards)

    def body(*refs):
        _gather_comm(refs[:n], refs[n:2 * n], *refs[2 * n:], start=True)
        _gather_comm(refs[:n], refs[n:2 * n], *refs[2 * n:], start=False)

    return pl.pallas_call(body, name="gather_shards", in_specs=[HBM_SPEC] * n, out_specs=[HBM_SPEC] * n,
                          out_shape=_gather_out_shape(shards), scratch_shapes=_gather_sems(n))(*shards)


def _scatter_comm(ins, outs, send, recv, start):
    x, y, c = _coords()
    for a in range(len(ins)):
        for j, (px, py) in enumerate(_other_chips(x, y)):
            cp = pltpu.make_async_remote_copy(src_ref=ins[a].at[2 * px + py], dst_ref=outs[a].at[j],
                                              send_sem=send.at[3 * a + j], recv_sem=recv.at[3 * a + j],
                                              device_id=(px, py, c), device_id_type=MESH)
            if start:
                cp.start()
            else:
                cp.wait_recv()
                cp.wait_send()


def _scatter_out_shape(parts):
    return [jax.ShapeDtypeStruct((3,) + p.shape[1:], p.dtype) for p in parts]


def _scatter_sems(n):
    return [pltpu.SemaphoreType.DMA((3 * n,)), pltpu.SemaphoreType.DMA((3 * n,))]


def scatter_grads(parts):
    n = len(parts)

    def body(*refs):
        _scatter_comm(refs[:n], refs[n:2 * n], *refs[2 * n:], start=True)
        _scatter_comm(refs[:n], refs[n:2 * n], *refs[2 * n:], start=False)

    return pl.pallas_call(body, name="scatter_grads", in_specs=[HBM_SPEC] * n, out_specs=[HBM_SPEC] * n,
                          out_shape=_scatter_out_shape(parts), scratch_shapes=_scatter_sems(n))(*parts)


def sibling_swap(arrs):
    n = len(arrs)

    def body(*refs):
        ins, outs = refs[:n], refs[n:2 * n]
        send, recv = refs[2 * n:]
        x, y, c = _coords()
        cps = []
        for a in range(n):
            cp = pltpu.make_async_remote_copy(src_ref=ins[a], dst_ref=outs[a], send_sem=send.at[a], recv_sem=recv.at[a],
                                              device_id=(x, y, 1 - c), device_id_type=MESH)
            cp.start()
            cps.append(cp)
        for cp in cps:
            cp.wait_recv()
        for cp in cps:
            cp.wait_send()

    return pl.pallas_call(
        body, name="sibling_swap", in_specs=[HBM_SPEC] * n, out_specs=[HBM_SPEC] * n,
        out_shape=[jax.ShapeDtypeStruct(a.shape, a.dtype) for a in arrs],
        scratch_shapes=[pltpu.SemaphoreType.DMA((n,)), pltpu.SemaphoreType.DMA((n,))])(*arrs)


def allsum_small(v):
    rows = v.shape[0]

    def body(v_ref, tot_ref, gath_ref, send, recv):
        x, y, c = _coords()
        me = 4 * x + 2 * y + c
        gath_ref[me] = v_ref[...]
        cps = []
        for k in range(1, 8):
            fx, fy, fc = (k >> 2) & 1, (k >> 1) & 1, k & 1
            peer = (1 - x if fx else x, 1 - y if fy else y, 1 - c if fc else c)
            cp = pltpu.make_async_remote_copy(src_ref=v_ref, dst_ref=gath_ref.at[me], send_sem=send.at[k - 1],
                                              recv_sem=recv.at[k - 1], device_id=peer, device_id_type=MESH)
            cp.start()
            cps.append(cp)
        for cp in cps:
            cp.wait_recv()
        for cp in cps:
            cp.wait_send()
        tot = gath_ref[0]
        for k in range(1, 8):
            tot = tot + gath_ref[k]
        tot_ref[...] = tot

    vm = pl.BlockSpec(memory_space=pltpu.VMEM)
    tot, _ = pl.pallas_call(
        body, name="allsum_small", in_specs=[vm], out_specs=[vm, vm],
        out_shape=[jax.ShapeDtypeStruct((rows, LANES), F32), jax.ShapeDtypeStruct((8, rows, LANES), F32)],
        scratch_shapes=[pltpu.SemaphoreType.DMA((7,)), pltpu.SemaphoreType.DMA((7,))])(v)
    return tot


def sum_parts(own, recv, *, tr=256):
    R, C = own.shape
    tr = min(tr, R)

    def body(o_ref, r_ref, out_ref):
        out_ref[...] = ((o_ref[...] + r_ref[0].astype(F32)) + r_ref[1].astype(F32)) + r_ref[2].astype(F32)

    sp = pl.BlockSpec((tr, C), lambda i: (i, 0))
    return _pcall(body, name="sum_parts", grid=(R // tr,),
                  in_specs=[sp, pl.BlockSpec((3, tr, C), lambda i: (0, i, 0))], out_specs=sp,
                  out_shape=jax.ShapeDtypeStruct((R, C), F32))(own, recv)


def adamw(w, ga, gb, m, v, *, tr=256):
    R, C = w.shape
    tr = min(tr, R)
    two = gb is not None

    def body(*refs):
        if two:
            w_ref, ga_ref, gb_ref, m_ref, v_ref, g_out, d_out, m_out, v_out = refs
            g = ga_ref[...] + gb_ref[...]
        else:
            w_ref, ga_ref, m_ref, v_ref, g_out, d_out, m_out, v_out = refs
            g = ga_ref[...]
        mn = B1 * m_ref[...] + (1.0 - B1) * g
        vn = B2 * v_ref[...] + (1.0 - B2) * (g * g)
        m_hat = mn / (1.0 - B1 ** STEP)
        v_hat = vn / (1.0 - B2 ** STEP)
        g_out[...] = g
        d_out[...] = -LR * (m_hat / (jnp.sqrt(v_hat) + AEPS) + WD * w_ref[...])
        m_out[...] = mn
        v_out[...] = vn

    sp = pl.BlockSpec((tr, C), lambda i: (i, 0))
    args = [w, ga, gb, m, v] if two else [w, ga, m, v]
    sd = jax.ShapeDtypeStruct((R, C), F32)
    return _pcall(body, name="adamw", grid=(R // tr,), in_specs=[sp] * len(args), out_specs=[sp] * 4,
                  out_shape=[sd] * 4)(*args)


def _rope_tables(S):
    def inv_freq(n_dims, theta):
        return theta ** (-(jnp.arange(0, n_dims, 2, dtype=jnp.float32) / n_dims))

    pos = lax.broadcasted_iota(jnp.int32, (S, LANES), 0)
    d = lax.broadcasted_iota(jnp.int32, (S, LANES), 1) % HD
    d1 = lax.iota(jnp.int32, LANES) % HD
    ang = pos.astype(F32) * inv_freq(HD // 4, ROPE_THETA)[d1 % 8][None, :]
    sin = jnp.sin(ang)
    partial = (jnp.where(d < 16, jnp.cos(ang), 1.0), jnp.where((d >= 8) & (d < 16), sin, 0.0),
               jnp.where(d < 8, -sin, 0.0))
    grid_pos = jnp.where(d < 32, pos // GRID_W, pos % GRID_W)
    ang = grid_pos.astype(F32) * inv_freq(HD // 2, AXIAL_THETA)[d1 % 16][None, :]
    sin = jnp.sin(ang)
    axial = (jnp.cos(ang), jnp.where(d % 32 >= 16, sin, 0.0), jnp.where(d % 32 < 16, -sin, 0.0))
    return partial, axial


def _pad_w_in(w):
    cols = [w[:, :768]]
    for base in (768, 960):
        for g in range(3):
            kg = w[:, base + g * HD:base + (g + 1) * HD]
            cols += [kg, kg]
    cols.append(w[:, 1152:])
    return jnp.concatenate(cols, axis=1)


def _unpad_dw_in(dw):
    cols = [dw[:, :768]]
    for t0 in (K_T0, V_T0):
        for g in range(3):
            b = (t0 + g) * LANES
            cols.append(dw[:, b:b + HD] + dw[:, b + HD:b + LANES])
    cols.append(dw[:, M_T0 * LANES:])
    return jnp.concatenate(cols, axis=1)


def _stats_t(a, tq):
    S = a.shape[0]
    t = a.reshape(S, -1, 2, HD)[:, :N_PAIRS, :, 0]
    t = jnp.transpose(t, (1, 2, 0))
    t = jnp.pad(t, ((0, 0), (0, 6), (0, 0)))
    return jnp.transpose(t.reshape(N_PAIRS, 8, S // tq, tq), (0, 2, 1, 3))


def _grad_slices(dW_in, dW_mkv, dW_o, dW_gu, dW_d):
    return [None if dW_in is None else jnp.transpose(dW_in.reshape(D, 4, IN_W // 4), (1, 0, 2)),
            dW_mkv.reshape(4, D // 4, 512), dW_o.reshape(4, D // 4, D),
            jnp.transpose(dW_gu.reshape(D, 4, 2 * DFF // 4), (1, 0, 2)), dW_d.reshape(4, DFF // 4, D)]


def _fold(t):
    return t[..., :HD] + t[..., HD:]


def kernel(x, mem, mem_norm_g, w_in, w_mem_kv, w_o, g_mix_pre, g_mix_post, attn_sink, qk_norm_g, w_gate_up, w_down, g_ffn_pre, g_ffn_post, loss_target, m_mem_norm_g, m_w_in, m_w_mem_kv, m_w_o, m_g_mix_pre, m_g_mix_post, m_attn_sink, m_qk_norm_g, m_w_gate_up, m_w_down, m_g_ffn_pre, m_g_ffn_post, v_mem_norm_g, v_w_in, v_w_mem_kv, v_w_o, v_g_mix_pre, v_g_mix_post, v_attn_sink, v_qk_norm_g, v_w_gate_up, v_w_down, v_g_ffn_pre, v_g_ffn_post):
    S = x.shape[1]
    depth = w_in.shape[0]
    xs, memx, tgt = x[0], mem[0], loss_target[0]
    tab_p, tab_a = _rope_tables(S)
    row = lambda a: a.reshape(1, -1)

    shards_bf = [w.astype(BF) for w in (w_in, w_mem_kv, w_o, w_gate_up, w_down)]
    layer_shards = lambda i: [s[i] for s in shards_bf]
    W_in, W_mkv, W_o, W_g, W_u, W_d = ([None] * depth for _ in range(6))

    def set_weights(i, gathered):
        if len(gathered) == 5:
            W_in[i] = jnp.concatenate([gathered[0][s] for s in range(4)], axis=1)
        gm, go, gg, gd = gathered[-4:]
        W_mkv[i] = jnp.concatenate([gm[s] for s in range(4)], axis=0)
        W_o[i] = jnp.concatenate([go[s] for s in range(4)], axis=0)
        W_g[i] = jnp.concatenate([gg[0], gg[1]], axis=1)
        W_u[i] = jnp.concatenate([gg[2], gg[3]], axis=1)
        W_d[i] = jnp.concatenate([gd[s] for s in range(4)], axis=0)

    (g_in0,) = gather_shards(layer_shards(0)[:1])
    W_in[0] = jnp.concatenate([g_in0[s] for s in range(4)], axis=1)
    mem_g = row(mem_norm_g)
    zero_sink = jnp.zeros((12,), F32)
    qkg = jnp.pad(jnp.concatenate([qk_norm_g[0], qk_norm_g[0]], axis=1), ((0, 6), (0, 0)))
    no_qkg = jnp.zeros((8, LANES), F32)

    saved = []
    cur = xs
    for i in range(depth):
        kind = i % 3
        wp = _pad_w_in(W_in[i])
        sv = dict(x=cur, wp=wp)
        if kind == 1:
            h1, proj, raw, nrm = inproj_fwd(cur, row(g_mix_pre[i]), wp, tab_a, qkg, axial=True)
            sv["raw"] = raw
        else:
            h1, proj = inproj_fwd(cur, row(g_mix_pre[i]), wp, tab_p, no_qkg, axial=False)
        if kind == 0:
            res = banded_fwd(proj, attn_sink[i // 3], d=1, R=A_RADIUS, TQ=_band_tile(A_RADIUS, S), pair0=0, npairs=6,
                             use_sink=True, o_dtype=BF, gather=layer_shards(0)[1:] if i == 0 else ())
            tok, lse = res[:2]
            if i == 0:
                set_weights(0, res[2])
        elif kind == 1:
            bound = jnp.sqrt(jnp.max(nrm[0]) * jnp.max(nrm[1])) * LN2
            tok, lse = flash_fwd(proj, bound)
        else:
            os_, lses = [], []
            for g, (window, dil) in enumerate(C_GROUPS):
                rad = window // (2 * dil)
                o_g, l_g = banded_fwd(proj, zero_sink, d=dil, R=rad, TQ=_band_tile(rad, S // dil), pair0=2 * g, npairs=2,
                                      use_sink=False, o_dtype=F32)
                os_.append(o_g)
                lses.append(l_g)
            tok = combine_fwd(os_, lses)
            sv["os"], lse = os_, lses
        mem_n, mkv = norm_mm(memx, mem_g, W_mkv[i], tm=N_MEM)
        mo, mlse = mem_fwd(proj, mkv)
        cat = jnp.concatenate([tok, mo], axis=1)
        o, x2 = mm_norm_res(cat, W_o[i], row(g_mix_post[i]), cur)
        (h2, gate, up, act), gathered = ffn_up_fwd(x2, row(g_ffn_pre[i]), W_g[i], W_u[i],
                                                   layer_shards(i + 1) if i + 1 < depth else ())
        if i + 1 < depth:
            set_weights(i + 1, gathered)
        f, x3 = mm_norm_res(act, W_d[i], row(g_ffn_post[i]), x2)
        sv.update(h1=h1, proj=proj, lse=lse, mem_n=mem_n, mkv=mkv, mlse=mlse, cat=cat, o=o, x2=x2, h2=h2, gate=gate,
                  up=up, act=act, f=f)
        saved.append(sv)
        cur = x3

    dcur, loss_vec = loss_bwd(cur, tgt)

    grad_parts, grad_recv = [None] * depth, [None] * depth
    dg_pre, dg_post, dg_fpre, dg_fpost = [None] * depth, [None] * depth, [None] * depth, [None] * depth
    dg_mem = jnp.zeros((1, D), F32)
    dsinks, dqk = {}, None
    for i in reversed(range(depth)):
        sv = saved[i]
        kind = i % 3
        proj = sv["proj"]
        pending = [p.astype(BF) for p in grad_parts[i + 1]] if i + 1 < depth else ()
        (df, dgate, dup, dg_fpost[i]), got = normbwd_mm_swiglu(dcur, sv["f"], row(g_ffn_post[i]), W_d[i], sv["gate"],
                                                               sv["up"], pending)
        if i + 1 < depth:
            grad_recv[i + 1] = got
        dx2, dg_fpre[i] = mm_nt_normbwd_res([(dgate, W_g[i]), (dup, W_u[i])], sv["x2"], row(g_ffn_pre[i]), dcur)
        dW_d = mm_acc(sv["act"], df, tk=1408, tn=D, ts=2048)
        dW_gu = jnp.concatenate([mm_acc(sv["h2"], dgate, tk=D, tn=1408, ts=2048),
                                 mm_acc(sv["h2"], dup, tk=D, tn=1408, ts=2048)], axis=1)
        do, dcat, delta, dg_post[i] = normbwd_mm_cat(dx2, sv["o"], row(g_mix_post[i]), W_o[i], sv["cat"])
        dW_o = mm_acc(sv["cat"], do, tk=D, tn=D, ts=2048)
        dqm, dmkv = mem_bwd(proj, sv["mkv"], dcat, sv["mlse"], delta)
        dmkv = dmkv.astype(BF)
        (dgm,) = mm_nt_normbwd_res([(dmkv, W_mkv[i])], memx, mem_g, None, tm=N_MEM)
        dg_mem = dg_mem + dgm
        dW_mkv = mm_acc(sv["mem_n"], dmkv, tk=D, tn=512, ts=N_MEM)
        if kind == 0:
            sink = attn_sink[i // 3]
            args = dict(d=1, R=A_RADIUS, pair0=0, npairs=6)
            tile = _band_tile(A_RADIUS, S)
            early = [p.astype(BF) for p in _grad_slices(None, dW_mkv, dW_o, dW_gu, dW_d)[1:]] if i == 0 else ()
            res = banded_bwd_dq(proj, dcat, sv["lse"], delta, sink, TQ=tile, use_sink=True, scatter=early, **args)
            dq, dsk = res[:2]
            if i == 0:
                early_recv = res[2]
            dkp, dvp = banded_bwd_dkv(proj, dcat, sv["lse"], delta, TK=tile, **args)
            dsinks[i // 3] = dsk.reshape(3, 8, 2, 2, HD)[:, 0, :, :, 0].reshape(12)
        elif kind == 1:
            dq, dkp, dvp = flash_bwd(proj, dcat, sv["lse"], delta)
        else:
            dos, des = combine_bwd(dcat, sv["os"], sv["lse"])
            dqs, dks, dvs = [], [], []
            for g, (window, dil) in enumerate(C_GROUPS):
                rad = window // (2 * dil)
                args = dict(d=dil, R=rad, pair0=2 * g, npairs=2)
                tile = _band_tile(rad, S // dil)
                dq_g, _ = banded_bwd_dq(proj, dos[g], sv["lse"][g], des[g], zero_sink, TQ=tile, use_sink=False, **args)
                dk_g, dv_g = banded_bwd_dkv(proj, dos[g], sv["lse"][g], des[g], TK=tile, **args)
                dqs.append(dq_g)
                dks.append(dk_g)
                dvs.append(dv_g)
            dq, dkp, dvp = (jnp.concatenate(t, axis=1) for t in (dqs, dks, dvs))
        if kind == 1:
            dcur, dproj, dg_pre[i], dqk_t = inproj_bwd(dq, dkp, dvp, dqm, tab_a, sv["raw"], qkg, sv["wp"], sv["x"],
                                                       row(g_mix_pre[i]), dx2, axial=True)
            dqk = _fold(dqk_t[:2]).reshape(1, 2, HD)
        else:
            dcur, dproj, dg_pre[i], _ = inproj_bwd(dq, dkp, dvp, dqm, tab_p, proj, no_qkg, sv["wp"],
                                                   sv["x"], row(g_mix_pre[i]), dx2, axial=False)
        dW_in = _unpad_dw_in(mm_acc(sv["h1"], dproj, tk=D, tn=896, ts=2048))
        grad_parts[i] = _grad_slices(dW_in, dW_mkv, dW_o, dW_gu, dW_d)
    grad_recv[0] = list(scatter_grads([grad_parts[0][0].astype(BF)])) + list(early_recv)

    x_i, y_i, _ = _coords()
    me = 2 * x_i + y_i
    big = [(w_in, m_w_in, v_w_in), (w_mem_kv, m_w_mem_kv, v_w_mem_kv), (w_o, m_w_o, v_w_o),
           (w_gate_up, m_w_gate_up, v_w_gate_up), (w_down, m_w_down, v_w_down)]
    parts = []
    for a, (w, _, _) in enumerate(big):
        C = w.shape[-1]
        own = jnp.stack([lax.dynamic_index_in_dim(grad_parts[l][a], me, 0, keepdims=False) for l in range(depth)])
        rc = jnp.stack([grad_recv[l][a] for l in range(depth)], axis=1)
        parts.append(sum_parts(own.reshape(-1, C), rc.reshape(3, -1, C)))
    sibs = sibling_swap(parts)
    big_out = []
    for (w, m, v), pa, pb in zip(big, parts, sibs):
        C = w.shape[-1]
        outs = adamw(w.reshape(-1, C), pa, pb, m.reshape(-1, C), v.reshape(-1, C))
        big_out.append([o.reshape(w.shape) for o in outs])

    small_w = [mem_norm_g, g_mix_pre, g_mix_post, attn_sink, qk_norm_g, g_ffn_pre, g_ffn_post]
    small_m = [m_mem_norm_g, m_g_mix_pre, m_g_mix_post, m_attn_sink, m_qk_norm_g, m_g_ffn_pre, m_g_ffn_post]
    small_v = [v_mem_norm_g, v_g_mix_pre, v_g_mix_post, v_attn_sink, v_qk_norm_g, v_g_ffn_pre, v_g_ffn_post]
    small_g = [dg_mem.reshape(D), jnp.concatenate(dg_pre, axis=0), jnp.concatenate(dg_post, axis=0),
               jnp.stack([dsinks[k] for k in sorted(dsinks)]), dqk, jnp.concatenate(dg_fpre, axis=0),
               jnp.concatenate(dg_fpost, axis=0)]
    sizes = [a.size for a in small_w]
    total = sum(sizes)
    rows_s = -(-(total + LANES) // (8 * LANES)) * 8

    def pack(arrs, extra=None):
        flat = jnp.concatenate([a.reshape(-1).astype(F32) for a in arrs])
        flat = jnp.pad(flat, (0, rows_s * LANES - LANES - total))
        tail = jnp.zeros((LANES,), F32) if extra is None else extra.reshape(LANES)
        return jnp.concatenate([flat, tail]).reshape(rows_s, LANES)

    tot = allsum_small(pack(small_g, loss_vec))
    loss = jnp.sum(tot[rows_s - 1])
    s_out = adamw(pack(small_w), tot, None, pack(small_m), pack(small_v))

    def unpack(buf):
        flat = buf.reshape(-1)
        out, off = [], 0
        for a, n in zip(small_w, sizes):
            out.append(flat[off:off + n].reshape(a.shape))
            off += n
        return out

    sg, sd_, sm, sv_ = (unpack(b) for b in s_out)

    def ordered(k):
        sm_ = (sg, sd_, sm, sv_)[k]
        b = [bo[k] for bo in big_out]
        return [sm_[0], b[0], b[1], b[2], sm_[1], sm_[2], sm_[3], sm_[4], b[3], b[4], sm_[5], sm_[6]]

    dx_out = dcur.reshape(1, S, D)
    return (loss, dx_out, *ordered(0), *ordered(1), *ordered(2), *ordered(3))
```

```python
import functools

import jax
import jax.numpy as jnp
from jax import lax
from jax.experimental import pallas as pl
from jax.experimental.pallas import tpu as pltpu

F32 = jnp.float32
BF = jnp.bfloat16

D = 1024
HD = 64
LANES = 128
N_PAIRS = 6
DFF = 2816
IN_W = 1408
PW = 14 * LANES
K_T0, V_T0, M_T0 = 6, 9, 12
EPS = 1e-6
SCALE = HD ** -0.5
NEG = -1e30
LOG2E = 1.4426950408889634
LN2 = 0.6931471805599453
MAX_PLAIN_SCORE = 40.0
ROPE_THETA = 500000.0
AXIAL_THETA = 10000.0
GRID_W = 64
A_RADIUS = 128
C_GROUPS = ((128, 1), (512, 4), (2048, 16))
N_MEM = 256
LR, B1, B2, AEPS, WD, STEP = 0.001, 0.9, 0.999, 1e-08, 0.01, 10
VMEM_LIMIT = 56 * 1024 * 1024
MESH = pl.DeviceIdType.MESH


def _pcall(body, *, name, grid, in_specs, out_specs, out_shape, scratch=()):
    return pl.pallas_call(
        body, name=name, grid=grid, in_specs=in_specs, out_specs=out_specs, out_shape=out_shape,
        scratch_shapes=scratch,
        compiler_params=pltpu.CompilerParams(dimension_semantics=("arbitrary",) * len(grid),
                                             vmem_limit_bytes=VMEM_LIMIT))


def _pcall_behind(body, comm, arrays, comm_out_shape, comm_sems, *, name, grid, in_specs, out_specs, out_shape,
                  scratch=()):
    n_in, n_out, n_scr, n = len(in_specs), len(out_specs), len(scratch), len(arrays)
    last = tuple(g - 1 for g in grid)

    def at(step):
        cond = pl.program_id(0) == step[0]
        for a in range(1, len(grid)):
            cond = cond & (pl.program_id(a) == step[a])
        return cond

    def wrapped(*refs):
        ins, cin = refs[:n_in], refs[n_in:n_in + n]
        outs, cout = refs[n_in + n:n_in + n + n_out], refs[n_in + n + n_out:n_in + 2 * n + n_out]
        scr, sems = refs[n_in + 2 * n + n_out:n_in + 2 * n + n_out + n_scr], refs[n_in + 2 * n + n_out + n_scr:]

        @pl.when(at((0,) * len(grid)))
        def _():
            comm(cin, cout, *sems, start=True)

        body(*ins, *outs, *scr)

        @pl.when(at(last))
        def _():
            comm(cin, cout, *sems, start=False)

    call = _pcall(wrapped, name=name, grid=grid, in_specs=list(in_specs) + [HBM_SPEC] * n,
                  out_specs=list(out_specs) + [HBM_SPEC] * n, out_shape=list(out_shape) + list(comm_out_shape),
                  scratch=list(scratch) + list(comm_sems))

    def run(*args):
        res = call(*args, *arrays)
        return res[:n_out], res[n_out:]

    return run


def _dot(a, b):
    return lax.dot_general(a, b, (((1,), (0,)), ((), ())), preferred_element_type=F32)


def _dot_nt(a, b):
    return lax.dot_general(a, b, (((1,), (1,)), ((), ())), preferred_element_type=F32)


def _lo(shape):
    return lax.broadcasted_iota(jnp.int32, shape, len(shape) - 1) < HD


def _half_sum(x, lo):
    a = jnp.sum(jnp.where(lo, x, 0.0), axis=-1, keepdims=True)
    b = jnp.sum(jnp.where(lo, 0.0, x), axis=-1, keepdims=True)
    return jnp.where(lo, a, b)


def _col(tile, lane):
    idx = lax.broadcasted_iota(jnp.int32, tile.shape, 1)
    return jnp.sum(jnp.where(idx == lane, tile, 0.0), axis=-1, keepdims=True)


def _split(t, lo):
    z = jnp.zeros_like(t)
    return jnp.where(lo, t, z), jnp.where(lo, z, t)


def _rms(xf, g):
    r = lax.rsqrt(jnp.mean(xf * xf, axis=-1, keepdims=True) + EPS)
    return xf * r * g


def _rms_bwd(xf, g, dy):
    r = lax.rsqrt(jnp.mean(xf * xf, axis=-1, keepdims=True) + EPS)
    xr = xf * r
    dg = jnp.sum(dy * xr, axis=0, keepdims=True)
    t = dy * g
    return r * (t - xr * jnp.mean(xr * t, axis=-1, keepdims=True)), dg


def _rope_fwd(y, c, s1, s2, sh):
    return y * c + pltpu.roll(y, sh, 1) * s1 + pltpu.roll(y, LANES - sh, 1) * s2


def _rope_bwd(dy, c, s1, s2, sh):
    return dy * c + pltpu.roll(dy * s1, LANES - sh, 1) + pltpu.roll(dy * s2, sh, 1)


def _tile(t):
    return slice(t * LANES, (t + 1) * LANES)


def inproj_fwd(x, g, w_pad, tabs, qkg, *, axial, tm=512):
    S = x.shape[0]
    sh = 16 if axial else 8

    def body(x_ref, g_ref, w_ref, c_ref, s1_ref, s2_ref, qkg_ref, h_ref, p_ref, *extra):
        h = _rms(x_ref[...], g_ref[...]).astype(BF)
        h_ref[...] = h
        acc = _dot(h, w_ref[...])
        c, s1, s2 = c_ref[...], s1_ref[...], s2_ref[...]
        lo = _lo((tm, LANES))
        if axial:
            raw_ref, nrm_ref = extra

            @pl.when(pl.program_id(0) == 0)
            def _():
                nrm_ref[...] = jnp.zeros_like(nrm_ref)

        for t in range(14):
            y = acc[:, _tile(t)]
            if t < V_T0:
                if axial:
                    raw_ref[:, _tile(t)] = y.astype(BF)
                    gt = qkg_ref[0:1, :] if t < K_T0 else qkg_ref[1:2, :]
                    y = y * lax.rsqrt(_half_sum(y * y, lo) * (1.0 / HD) + EPS) * gt
                y = _rope_fwd(y, c, s1, s2, sh)
            if t < K_T0:
                y = y * (SCALE * LOG2E if axial else SCALE)
            elif t >= M_T0:
                y = y * SCALE
            yb = y.astype(BF)
            p_ref[:, _tile(t)] = yb
            if axial and t < V_T0:
                yf = yb.astype(F32)
                n2 = jnp.max(_half_sum(yf * yf, lo), axis=0, keepdims=True)
                r = 0 if t < K_T0 else 1
                nrm_ref[r:r + 1, :] = jnp.maximum(nrm_ref[r:r + 1, :], n2)

    row = lambda w: pl.BlockSpec((tm, w), lambda i: (i, 0))
    full = lambda a: pl.BlockSpec(a.shape, lambda i: (0, 0))
    out_shape = [jax.ShapeDtypeStruct((S, D), BF), jax.ShapeDtypeStruct((S, PW), BF)]
    out_specs = [row(D), row(PW)]
    if axial:
        out_shape += [jax.ShapeDtypeStruct((S, V_T0 * LANES), BF), jax.ShapeDtypeStruct((8, LANES), F32)]
        out_specs += [row(V_T0 * LANES), pl.BlockSpec((8, LANES), lambda i: (0, 0))]
    return _pcall(body, name="inproj_fwd_axial" if axial else "inproj_fwd", grid=(S // tm,),
                  in_specs=[row(D), full(g), full(w_pad), row(LANES), row(LANES), row(LANES), full(qkg)],
                  out_specs=out_specs, out_shape=out_shape)(x, g, w_pad, *tabs, qkg)


def norm_mm(x, g, w, *, tm):
    S, N = x.shape[0], w.shape[1]

    def body(x_ref, g_ref, w_ref, h_ref, y_ref):
        h = _rms(x_ref[...], g_ref[...]).astype(BF)
        h_ref[...] = h
        y_ref[...] = _dot(h, w_ref[...]).astype(BF)

    return _pcall(body, name="norm_mm", grid=(S // tm,),
                  in_specs=[pl.BlockSpec((tm, D), lambda i: (i, 0)), pl.BlockSpec(g.shape, lambda i: (0, 0)),
                            pl.BlockSpec(w.shape, lambda i: (0, 0))],
                  out_specs=[pl.BlockSpec((tm, D), lambda i: (i, 0)), pl.BlockSpec((tm, N), lambda i: (i, 0))],
                  out_shape=[jax.ShapeDtypeStruct((S, D), BF), jax.ShapeDtypeStruct((S, N), BF)])(x, g, w)


def ffn_up_fwd(x, g, w_gu, next_shards=(), *, tm=512, tn=1408):
    S = x.shape[0]

    def body(x_ref, g_ref, w_ref, h_ref, gate_ref, up_ref, a_ref, h_scr):
        @pl.when(pl.program_id(1) == 0)
        def _():
            h = _rms(x_ref[...], g_ref[...]).astype(BF)
            h_scr[...] = h
            h_ref[...] = h

        acc = _dot(h_scr[...], w_ref[...])
        gate, up = acc[:, :tn], acc[:, tn:]
        sig = pl.reciprocal(1.0 + jnp.exp(-gate), approx=True)
        silu = gate * sig
        gate_ref[...] = (up * (sig * (1.0 + gate * (1.0 - sig)))).astype(BF)
        up_ref[...] = silu.astype(BF)
        a_ref[...] = (silu * up).astype(BF)

    rowd = pl.BlockSpec((tm, D), lambda i, j: (i, 0))
    osp = pl.BlockSpec((tm, tn), lambda i, j: (i, j))
    sd = jax.ShapeDtypeStruct((S, DFF), BF)
    kw = dict(grid=(S // tm, DFF // tn),
              in_specs=[rowd, pl.BlockSpec(g.shape, lambda i, j: (0, 0)), pl.BlockSpec((D, 2 * tn), lambda i, j: (0, j))],
              out_specs=[rowd, osp, osp, osp], out_shape=[jax.ShapeDtypeStruct((S, D), BF), sd, sd, sd],
              scratch=[pltpu.VMEM((tm, D), BF)])
    if next_shards:
        return _pcall_behind(body, _gather_comm, next_shards, _gather_out_shape(next_shards),
                             _gather_sems(len(next_shards)), name="ffn_up_fwd_gather", **kw)(x, g, w_gu)
    return _pcall(body, name="ffn_up_fwd", **kw)(x, g, w_gu), ()


def mm_norm_res(a_parts, w, g, res, *, tm=512):
    S = a_parts[0].shape[0]
    na = len(a_parts)

    def body(*refs):
        w_ref, g_ref, res_ref, y_ref, o_ref = refs[na:]
        a = refs[0][...] if na == 1 else jnp.concatenate([r[...] for r in refs[:na]], axis=1)
        y = _dot(a, w_ref[...])
        y_ref[...] = y
        o_ref[...] = res_ref[...] + _rms(y, g_ref[...])

    rowd = pl.BlockSpec((tm, D), lambda i: (i, 0))
    sd = jax.ShapeDtypeStruct((S, D), F32)
    return _pcall(body, name="mm_norm_res", grid=(S // tm,),
                  in_specs=[pl.BlockSpec((tm, a.shape[1]), lambda i: (i, 0)) for a in a_parts]
                  + [pl.BlockSpec(w.shape, lambda i: (0, 0)), pl.BlockSpec(g.shape, lambda i: (0, 0)), rowd],
                  out_specs=[rowd, rowd], out_shape=[sd, sd])(*a_parts, w, g, res)


def loss_bwd(y, tgt, *, tm=512):
    S = y.shape[0]

    def body(y_ref, t_ref, dy_ref, l_ref):
        @pl.when(pl.program_id(0) == 0)
        def _():
            l_ref[...] = jnp.zeros_like(l_ref)

        e = y_ref[...] - t_ref[...]
        dy_ref[...] = e * (1.0 / D)
        col = jnp.sum(e * e, axis=0, keepdims=True)
        part = col[:, _tile(0)]
        for t in range(1, D // LANES):
            part = part + col[:, _tile(t)]
        l_ref[...] += part * (0.5 / D)

    rowd = pl.BlockSpec((tm, D), lambda i: (i, 0))
    return _pcall(body, name="loss_bwd", grid=(S // tm,), in_specs=[rowd, rowd],
                  out_specs=[rowd, pl.BlockSpec((1, LANES), lambda i: (0, 0))],
                  out_shape=[jax.ShapeDtypeStruct((S, D), F32), jax.ShapeDtypeStruct((1, LANES), F32)])(y, tgt)


def normbwd_mm_cat(dy, ysaved, g, w, tok, mo, *, tm=512):
    S = dy.shape[0]
    n_tok = tok.shape[1] // LANES

    def body(dy_ref, y_ref, g_ref, w_ref, tok_ref, mo_ref, d_ref, dcat_ref, delta_ref, dg_ref):
        @pl.when(pl.program_id(0) == 0)
        def _():
            dg_ref[...] = jnp.zeros_like(dg_ref)

        d, dg = _rms_bwd(y_ref[...], g_ref[...], dy_ref[...])
        dg_ref[...] += dg
        d = d.astype(BF)
        d_ref[...] = d
        z = _dot_nt(d, w_ref[...])
        dcat_ref[...] = z.astype(BF)
        lo = _lo((tm, LANES))
        for t in range(D // LANES):
            c = tok_ref[:, _tile(t)] if t < n_tok else mo_ref[:, _tile(t - n_tok)]
            delta_ref[:, _tile(t)] = _half_sum(z[:, _tile(t)] * c.astype(F32), lo)

    rowd = pl.BlockSpec((tm, D), lambda i: (i, 0))
    return _pcall(body, name="normbwd_mm_cat", grid=(S // tm,),
                  in_specs=[rowd, rowd, pl.BlockSpec(g.shape, lambda i: (0, 0)), pl.BlockSpec(w.shape, lambda i: (0, 0)),
                            pl.BlockSpec((tm, tok.shape[1]), lambda i: (i, 0)),
                            pl.BlockSpec((tm, mo.shape[1]), lambda i: (i, 0))],
                  out_specs=[rowd, rowd, rowd, pl.BlockSpec((1, D), lambda i: (0, 0))],
                  out_shape=[jax.ShapeDtypeStruct((S, D), BF), jax.ShapeDtypeStruct((S, D), BF),
                             jax.ShapeDtypeStruct((S, D), F32), jax.ShapeDtypeStruct((1, D), F32)])(
        dy, ysaved, g, w, tok, mo)


def normbwd_mm_swiglu(dy, fsaved, g, wd, gate, up, grad_parts=(), *, tm=512, tn=1408):
    S = dy.shape[0]

    def body(dy_ref, f_ref, g_ref, w_ref, gate_ref, up_ref, df_ref, dgate_ref, dup_ref, dg_ref, d_scr):
        i, j = pl.program_id(0), pl.program_id(1)

        @pl.when((i == 0) & (j == 0))
        def _():
            dg_ref[...] = jnp.zeros_like(dg_ref)

        @pl.when(j == 0)
        def _():
            d, dg = _rms_bwd(f_ref[...], g_ref[...], dy_ref[...])
            dg_ref[...] += dg
            d_scr[...] = d.astype(BF)
            df_ref[...] = d.astype(BF)

        da = _dot_nt(d_scr[...], w_ref[...])
        dgate_ref[...] = (da * gate_ref[...].astype(F32)).astype(BF)
        dup_ref[...] = (da * up_ref[...].astype(F32)).astype(BF)

    rowd = pl.BlockSpec((tm, D), lambda i, j: (i, 0))
    osp = pl.BlockSpec((tm, tn), lambda i, j: (i, j))
    sd = jax.ShapeDtypeStruct((S, DFF), BF)
    kw = dict(grid=(S // tm, DFF // tn),
              in_specs=[rowd, rowd, pl.BlockSpec(g.shape, lambda i, j: (0, 0)), pl.BlockSpec((tn, D), lambda i, j: (j, 0)),
                        osp, osp],
              out_specs=[rowd, osp, osp, pl.BlockSpec((1, D), lambda i, j: (0, 0))],
              out_shape=[jax.ShapeDtypeStruct((S, D), BF), sd, sd, jax.ShapeDtypeStruct((1, D), F32)],
              scratch=[pltpu.VMEM((tm, D), BF)])
    args = (dy, fsaved, g, wd, gate, up)
    if grad_parts:
        return _pcall_behind(body, _scatter_comm, grad_parts, _scatter_out_shape(grad_parts),
                             _scatter_sems(len(grad_parts)), name="normbwd_mm_swiglu_scatter", **kw)(*args)
    return _pcall(body, name="normbwd_mm_swiglu", **kw)(*args), ()


def mm_nt_normbwd_res(parts, xin, g, dres, *, tm=512):
    S = xin.shape[0]
    npart = len(parts)
    has_res = dres is not None

    def body(*refs):
        prefs = refs[:2 * npart]
        x_ref, g_ref = refs[2 * npart:2 * npart + 2]
        rest = refs[2 * npart + 2:]
        if has_res:
            dres_ref, dx_ref, dg_ref = rest
        else:
            (dg_ref,) = rest

        @pl.when(pl.program_id(0) == 0)
        def _():
            dg_ref[...] = jnp.zeros_like(dg_ref)

        z = _dot_nt(prefs[0][...], prefs[1][...])
        for p in range(1, npart):
            z = z + _dot_nt(prefs[2 * p][...], prefs[2 * p + 1][...])
        dx, dg = _rms_bwd(x_ref[...], g_ref[...], z)
        dg_ref[...] += dg
        if has_res:
            dx_ref[...] = dres_ref[...] + dx

    rowd = pl.BlockSpec((tm, D), lambda i: (i, 0))
    in_specs, args = [], []
    for dy, w in parts:
        in_specs += [pl.BlockSpec((tm, dy.shape[1]), lambda i: (i, 0)),
                     pl.BlockSpec(w.shape, lambda i: (0, 0), pipeline_mode=pl.Buffered(1))]
        args += [dy, w]
    in_specs += [rowd, pl.BlockSpec(g.shape, lambda i: (0, 0))]
    args += [xin, g]
    out_specs = [pl.BlockSpec((1, D), lambda i: (0, 0))]
    out_shape = [jax.ShapeDtypeStruct((1, D), F32)]
    if has_res:
        in_specs.append(rowd)
        args.append(dres)
        out_specs.insert(0, rowd)
        out_shape.insert(0, jax.ShapeDtypeStruct((S, D), F32))
    return _pcall(body, name="mm_nt_normbwd_res" if has_res else "mm_nt_normbwd", grid=(S // tm,),
                  in_specs=in_specs, out_specs=out_specs, out_shape=out_shape)(*args)


def inproj_bwd(dq, dkp, dvp, dqm, tabs, raw, qkg, w_pad, xin, g, dres, *, axial, tm=512):
    S = xin.shape[0]
    sh = 16 if axial else 8

    def body(dq_ref, dk_ref, dv_ref, dm_ref, c_ref, s1_ref, s2_ref, raw_ref, qkg_ref, w_ref, x_ref, g_ref, dres_ref,
             dx_ref, dp_ref, dg_ref, dqk_ref):
        @pl.when(pl.program_id(0) == 0)
        def _():
            dg_ref[...] = jnp.zeros_like(dg_ref)
            dqk_ref[...] = jnp.zeros_like(dqk_ref)

        c, s1, s2 = c_ref[...], s1_ref[...], s2_ref[...]
        lo = _lo((tm, LANES))
        for t in range(14):
            if t < K_T0:
                y = dq_ref[:, _tile(t)] * SCALE
            elif t < V_T0:
                y = dk_ref[:, _tile(2 * (t - K_T0))] + dk_ref[:, _tile(2 * (t - K_T0) + 1)]
                if axial:
                    y = y * LN2
            elif t < M_T0:
                y = dv_ref[:, _tile(2 * (t - V_T0))] + dv_ref[:, _tile(2 * (t - V_T0) + 1)]
            else:
                y = dm_ref[:, _tile(t - M_T0)] * SCALE
            if t < V_T0:
                y = _rope_bwd(y, c, s1, s2, sh)
                if axial:
                    row = 0 if t < K_T0 else 1
                    xr = raw_ref[:, _tile(t)].astype(F32)
                    r = lax.rsqrt(_half_sum(xr * xr, lo) * (1.0 / HD) + EPS)
                    xn = xr * r
                    dqk_ref[row:row + 1, :] += jnp.sum(y * xn, axis=0, keepdims=True)
                    tt = y * qkg_ref[row:row + 1, :]
                    y = r * (tt - xn * (_half_sum(xn * tt, lo) * (1.0 / HD)))
            dp_ref[:, _tile(t)] = y.astype(BF)
        z = _dot_nt(dp_ref[...], w_ref[...])
        dx, dg = _rms_bwd(x_ref[...], g_ref[...], z)
        dg_ref[...] += dg
        dx_ref[...] = dres_ref[...] + dx

    row = lambda w: pl.BlockSpec((tm, w), lambda i: (i, 0))
    full = lambda a: pl.BlockSpec(a.shape, lambda i: (0, 0))
    return _pcall(body, name="inproj_bwd_axial" if axial else "inproj_bwd", grid=(S // tm,),
                  in_specs=[row(768), row(768), row(768), row(256), row(LANES), row(LANES), row(LANES),
                            row(raw.shape[1] if axial else LANES), full(qkg), full(w_pad), row(D), full(g), row(D)],
                  out_specs=[row(D), row(PW), pl.BlockSpec((1, D), lambda i: (0, 0)),
                             pl.BlockSpec((8, LANES), lambda i: (0, 0))],
                  out_shape=[jax.ShapeDtypeStruct((S, D), F32), jax.ShapeDtypeStruct((S, PW), BF),
                             jax.ShapeDtypeStruct((1, D), F32), jax.ShapeDtypeStruct((8, LANES), F32)])(
        dq, dkp, dvp, dqm, *tabs, raw, qkg, w_pad, xin, g, dres)


def mm_acc(a, b, *, tk, tn, ts):
    S, K = a.shape
    N = b.shape[1]
    ts = min(ts, S)

    def body(a_ref, b_ref, o_ref):
        z = lax.dot_general(a_ref[...], b_ref[...], (((0,), (0,)), ((), ())), preferred_element_type=F32)

        @pl.when(pl.program_id(2) == 0)
        def _():
            o_ref[...] = z

        @pl.when(pl.program_id(2) > 0)
        def _():
            o_ref[...] += z

    return _pcall(body, name="mm_acc", grid=(K // tk, N // tn, S // ts),
                  in_specs=[pl.BlockSpec((ts, tk), lambda k, n, s: (s, k)), pl.BlockSpec((ts, tn), lambda k, n, s: (s, n))],
                  out_specs=pl.BlockSpec((tk, tn), lambda k, n, s: (k, n)),
                  out_shape=jax.ShapeDtypeStruct((K, N), F32))(a, b)


def _band_specs(L, d, R, T, width, bw, col_of):
    n = T // R
    nb = width // bw
    last = L // R - 1
    col = lambda g, r: r * nb + col_of(g)
    return [pl.BlockSpec((R, bw), lambda g, r, i: (jnp.maximum(i * n - 1, 0), col(g, r))),
            pl.BlockSpec((T, bw), lambda g, r, i: (i, col(g, r))),
            pl.BlockSpec((R, bw), lambda g, r, i: (jnp.minimum((i + 1) * n, last), col(g, r)))]


def _band_tile(R, L):
    return min(max(2 * R, 256), L)


def _band_bias(T, R):
    w = lax.broadcasted_iota(jnp.int32, (T, T + 2 * R), 1)
    c = lax.broadcasted_iota(jnp.int32, (T, T + 2 * R), 0)
    return jnp.where(jnp.abs(w - R - c) <= R, 0.0, NEG).astype(F32)


def _edge_bias(i, T, R, L):
    wpos = i * T - R + lax.broadcasted_iota(jnp.int32, (1, T + 2 * R), 1)
    return jnp.where((wpos >= 0) & (wpos < L), 0.0, NEG)


def banded_fwd(proj, sink, *, d, R, TQ, pair0, npairs, use_sink, o_dtype, gather=()):
    S = proj.shape[0]
    L = S // d
    pv = proj.reshape(L, d * PW)
    ow = npairs * LANES

    def body(sink_ref, bias_ref, q_ref, kp, kc, kn, vp, vc, vn, o_ref, lse_ref):
        g, i = pl.program_id(0), pl.program_id(2)
        bias = bias_ref[...] + _edge_bias(i, TQ, R, L)
        lo = _lo((TQ, LANES))
        kw = jnp.concatenate([kp[...], kc[...], kn[...]], axis=0)
        vw = jnp.concatenate([vp[...], vc[...], vn[...]], axis=0)
        v_lo, v_hi = _split(vw, _lo(vw.shape))
        vcat = jnp.concatenate([v_lo, v_hi], axis=0)
        for t in range(2):
            qa, qb = _split(q_ref[:, _tile(t)], lo)
            ps, dens, lses = [], [], []
            for h, qh in enumerate((qa, qb)):
                s = _dot_nt(qh, kw) + bias
                m = jnp.max(s, axis=-1, keepdims=True)
                if use_sink:
                    sk = sink_ref[2 * (pair0 + 2 * g + t) + h]
                    m = jnp.maximum(m, sk)
                e = jnp.exp(s - m)
                den = jnp.sum(e, axis=-1, keepdims=True)
                if use_sink:
                    den = den + jnp.exp(sk - m)
                ps.append(e.astype(BF))
                dens.append(den)
                lses.append(m + jnp.log(den))
            o = _dot(jnp.concatenate(ps, axis=1), vcat)
            o_ref[:, _tile(t)] = (o / jnp.where(lo, dens[0], dens[1])).astype(o_dtype)
            lse_ref[:, _tile(t)] = jnp.where(lo, lses[0], lses[1])

    g0 = pair0 // 2
    qspec = pl.BlockSpec((TQ, 2 * LANES), lambda g, r, i: (i, r * 7 + g0 + g))
    kspecs = _band_specs(L, d, R, TQ, PW, LANES, lambda g: K_T0 + g0 + g)
    vspecs = _band_specs(L, d, R, TQ, PW, LANES, lambda g: V_T0 + g0 + g)
    ospec = pl.BlockSpec((TQ, 2 * LANES), lambda g, r, i: (i, r * (npairs // 2) + g))
    bias = _band_bias(TQ, R)
    kw = dict(grid=(npairs // 2, d, L // TQ),
              in_specs=[pl.BlockSpec(memory_space=pltpu.SMEM), pl.BlockSpec(bias.shape, lambda g, r, i: (0, 0)),
                        qspec] + kspecs + vspecs,
              out_specs=[ospec, ospec],
              out_shape=[jax.ShapeDtypeStruct((L, d * ow), o_dtype), jax.ShapeDtypeStruct((L, d * ow), F32)])
    args = (sink, bias, pv, pv, pv, pv, pv, pv, pv)
    if gather:
        (o, lse), gathered = _pcall_behind(body, _gather_comm, gather, _gather_out_shape(gather),
                                           _gather_sems(len(gather)), name="banded_fwd_gather", **kw)(*args)
        return o.reshape(S, ow), lse.reshape(S, ow), gathered
    o, lse = _pcall(body, name="banded_fwd", **kw)(*args)
    return o.reshape(S, ow), lse.reshape(S, ow)


def banded_bwd_dq(proj, do, lse, delta, sink, *, d, R, TQ, pair0, npairs, use_sink, scatter=()):
    S = proj.shape[0]
    L = S // d
    pv = proj.reshape(L, d * PW)
    ow = npairs * LANES

    def body(sink_ref, bias_ref, q_ref, kp, kc, kn, vp, vc, vn, do_ref, lse_ref, delta_ref, dq_ref, dsink_ref):
        g, r, i = pl.program_id(0), pl.program_id(1), pl.program_id(2)

        @pl.when((r == 0) & (i == 0))
        def _():
            dsink_ref[...] = jnp.zeros_like(dsink_ref)

        bias = bias_ref[...] + _edge_bias(i, TQ, R, L)
        lo = _lo((TQ, LANES))
        kw = jnp.concatenate([kp[...], kc[...], kn[...]], axis=0)
        vw = jnp.concatenate([vp[...], vc[...], vn[...]], axis=0)
        k_lo, k_hi = _split(kw, _lo(kw.shape))
        kcat = jnp.concatenate([k_lo, k_hi], axis=0)
        for t in range(2):
            qa, qb = _split(q_ref[:, _tile(t)], lo)
            doa, dob = _split(do_ref[:, _tile(t)], lo)
            lse_t, delta_t = lse_ref[:, _tile(t)], delta_ref[:, _tile(t)]
            dss, dsk = [], []
            for h, (qh, doh) in enumerate(((qa, doa), (qb, dob))):
                lse_h, delta_h = _col(lse_t, h * HD), _col(delta_t, h * HD)
                pr = jnp.exp(_dot_nt(qh, kw) + bias - lse_h)
                dss.append((pr * (_dot_nt(doh, vw) - delta_h)).astype(BF))
                if use_sink:
                    psink = jnp.exp(sink_ref[2 * (pair0 + 2 * g + t) + h] - lse_h)
                    dsk.append(-jnp.sum(psink * delta_h, axis=0, keepdims=True))
            dq_ref[:, _tile(t)] = _dot(jnp.concatenate(dss, axis=1), kcat)
            if use_sink:
                dsink_ref[:, _tile(t)] += jnp.where(_lo((8, LANES)), dsk[0], dsk[1])

    g0 = pair0 // 2
    qspec = pl.BlockSpec((TQ, 2 * LANES), lambda g, r, i: (i, r * 7 + g0 + g))
    kspecs = _band_specs(L, d, R, TQ, PW, LANES, lambda g: K_T0 + g0 + g)
    vspecs = _band_specs(L, d, R, TQ, PW, LANES, lambda g: V_T0 + g0 + g)
    ospec = pl.BlockSpec((TQ, 2 * LANES), lambda g, r, i: (i, r * (npairs // 2) + g))
    view = lambda a: a.reshape(L, d * a.shape[1])
    ispec = lambda a: pl.BlockSpec((TQ, 2 * LANES), lambda g, r, i: (i, r * (a.shape[1] // (2 * LANES)) + g))
    bias = _band_bias(TQ, R)
    kw = dict(grid=(npairs // 2, d, L // TQ),
              in_specs=[pl.BlockSpec(memory_space=pltpu.SMEM), pl.BlockSpec(bias.shape, lambda g, r, i: (0, 0)),
                        qspec] + kspecs + vspecs + [ispec(do), ispec(lse), ispec(delta)],
              out_specs=[ospec, pl.BlockSpec((8, 2 * LANES), lambda g, r, i: (g, 0))],
              out_shape=[jax.ShapeDtypeStruct((L, d * ow), F32),
                         jax.ShapeDtypeStruct((npairs // 2 * 8, 2 * LANES), F32)])
    args = (sink, bias, pv, pv, pv, pv, pv, pv, pv, view(do), view(lse), view(delta))
    if scatter:
        (dq, dsink), got = _pcall_behind(body, _scatter_comm, scatter, _scatter_out_shape(scatter),
                                         _scatter_sems(len(scatter)), name="banded_bwd_dq_scatter", **kw)(*args)
        return dq.reshape(S, ow), dsink, got
    dq, dsink = _pcall(body, name="banded_bwd_dq", **kw)(*args)
    return dq.reshape(S, ow), dsink


def banded_bwd_dkv(proj, do, lse, delta, *, d, R, TK, pair0, npairs):
    S = proj.shape[0]
    L = S // d
    pv = proj.reshape(L, d * PW)
    ow = npairs * LANES

    def body(bias_ref, k_ref, v_ref, qp, qc, qn, dop, doc, don, lp, lc, ln, dp_, dc_, dn_, dk_ref, dv_ref):
        j = pl.program_id(2)
        W = TK + 2 * R
        bias = bias_ref[...] + _edge_bias(j, TK, R, L)
        low = _lo((W, LANES))
        qw = jnp.concatenate([qp[...], qc[...], qn[...]], axis=0)
        dow = jnp.concatenate([dop[...], doc[...], don[...]], axis=0)
        lse_w = jnp.concatenate([lp[...], lc[...], ln[...]], axis=0)
        delta_w = jnp.concatenate([dp_[...], dc_[...], dn_[...]], axis=0)
        k, v = k_ref[...], v_ref[...]
        for t in range(2):
            qa, qb = _split(qw[:, _tile(t)], low)
            doa, dob = _split(dow[:, _tile(t)], low)
            lse_r, delta_r = lse_w[:, _tile(t)].T, delta_w[:, _tile(t)].T
            prs, dss = [], []
            for h, (qh, doh) in enumerate(((qa, doa), (qb, dob))):
                pr = jnp.exp(_dot_nt(k, qh) + bias - lse_r[h * HD:h * HD + 1, :])
                dss.append((pr * (_dot_nt(v, doh) - delta_r[h * HD:h * HD + 1, :])).astype(BF))
                prs.append(pr.astype(BF))
            dv_ref[:, _tile(t)] = _dot(jnp.concatenate(prs, axis=1), jnp.concatenate([doa, dob], axis=0))
            dk_ref[:, _tile(t)] = _dot(jnp.concatenate(dss, axis=1), jnp.concatenate([qa, qb], axis=0))

    g0 = pair0 // 2
    kspec = pl.BlockSpec((TK, LANES), lambda g, r, j: (j, r * 14 + K_T0 + g0 + g))
    vspec = pl.BlockSpec((TK, LANES), lambda g, r, j: (j, r * 14 + V_T0 + g0 + g))
    qspecs = _band_specs(L, d, R, TK, PW, 2 * LANES, lambda g: g0 + g)
    ispecs = lambda a: _band_specs(L, d, R, TK, a.shape[1], 2 * LANES, lambda g: g)
    view = lambda a: a.reshape(L, d * a.shape[1])
    ospec = pl.BlockSpec((TK, 2 * LANES), lambda g, r, j: (j, r * (npairs // 2) + g))
    sd = jax.ShapeDtypeStruct((L, d * ow), F32)
    bias = _band_bias(TK, R)
    dk, dv = _pcall(body, name="banded_bwd_dkv", grid=(npairs // 2, d, L // TK),
                    in_specs=[pl.BlockSpec(bias.shape, lambda g, r, j: (0, 0)), kspec, vspec] + qspecs + ispecs(do)
                    + ispecs(lse) + ispecs(delta),
                    out_specs=[ospec, ospec], out_shape=[sd, sd])(
        bias, pv, pv, pv, pv, pv, *([view(do)] * 3), *([view(lse)] * 3), *([view(delta)] * 3))
    return dk.reshape(S, ow), dv.reshape(S, ow)


def flash_fwd(proj, bound, *, tq=1024, tk=1024):
    S = proj.shape[0]

    def body_general(q_ref, k_ref, v_ref, o_ref, lse_ref):
        lo = _lo((tq, LANES))
        qa, qb = _split(q_ref[...], lo)
        lov = _lo((tk, LANES))

        def step(j, carry):
            ma, la, mb, lb, acc = carry
            rows = pl.ds(pl.multiple_of(j * tk, tk), tk)
            k, v = k_ref[rows, :], v_ref[rows, :]
            outs = []
            for qh, m0, l0 in ((qa, ma, la), (qb, mb, lb)):
                s = _dot_nt(qh, k)
                m1 = jnp.maximum(m0, jnp.max(s, axis=-1, keepdims=True))
                al = jnp.exp2(m0 - m1)
                e = jnp.exp2(s - m1)
                outs.append((m1, al * l0 + jnp.sum(e, axis=-1, keepdims=True), al, e.astype(BF)))
            v_lo, v_hi = _split(v, lov)
            pvv = _dot(jnp.concatenate([outs[0][3], outs[1][3]], axis=1), jnp.concatenate([v_lo, v_hi], axis=0))
            acc = acc * jnp.where(lo, outs[0][2], outs[1][2]) + pvv
            return outs[0][0], outs[0][1], outs[1][0], outs[1][1], acc

        m_init = jnp.full((tq, 1), NEG, F32)
        l_init = jnp.zeros((tq, 1), F32)
        ma, la, mb, lb, acc = lax.fori_loop(0, S // tk, step,
                                            (m_init, l_init, m_init, l_init, jnp.zeros((tq, LANES), F32)))
        o_ref[...] = (acc / jnp.where(lo, la, lb)).astype(BF)
        lse_ref[...] = jnp.where(lo, ma * LN2 + jnp.log(la), mb * LN2 + jnp.log(lb))

    def body_plain(q_ref, k_ref, v_ref, o_ref, lse_ref):
        lo = _lo((tq, LANES))
        qa, qb = _split(q_ref[...], lo)
        lov = _lo((tk, LANES))
        one = jnp.ones((tk, LANES), BF)

        def step(j, carry):
            acc_a, acc_b = carry
            rows = pl.ds(pl.multiple_of(j * tk, tk), tk)
            k, v = k_ref[rows, :], v_ref[rows, :]
            ea = jnp.exp2(_dot_nt(qa, k)).astype(BF)
            eb = jnp.exp2(_dot_nt(qb, k)).astype(BF)
            acc_a = acc_a + _dot(ea, jnp.where(lov, v, one))
            acc_b = acc_b + _dot(eb, jnp.where(lov, one, v))
            return acc_a, acc_b

        z = jnp.zeros((tq, LANES), F32)
        acc_a, acc_b = lax.fori_loop(0, S // tk, step, (z, z))
        den = jnp.where(lo, pltpu.roll(acc_a, HD, 1), pltpu.roll(acc_b, HD, 1))
        o_ref[...] = (jnp.where(lo, acc_a, acc_b) / den).astype(BF)
        lse_ref[...] = jnp.log(den)

    def body(bound_ref, q_ref, k_ref, v_ref, o_ref, lse_ref):
        small = bound_ref[0] <= MAX_PLAIN_SCORE

        @pl.when(small)
        def _():
            body_plain(q_ref, k_ref, v_ref, o_ref, lse_ref)

        @pl.when(jnp.logical_not(small))
        def _():
            body_general(q_ref, k_ref, v_ref, o_ref, lse_ref)

    ospec = pl.BlockSpec((tq, LANES), lambda p, i: (i, p))
    return _pcall(body, name="flash_fwd", grid=(N_PAIRS, S // tq),
                  in_specs=[pl.BlockSpec(memory_space=pltpu.SMEM), ospec,
                            pl.BlockSpec((S, LANES), lambda p, i: (0, K_T0 + p // 2)),
                            pl.BlockSpec((S, LANES), lambda p, i: (0, V_T0 + p // 2))],
                  out_specs=[ospec, ospec],
                  out_shape=[jax.ShapeDtypeStruct((S, 768), BF), jax.ShapeDtypeStruct((S, 768), F32)])(
        bound.reshape(1), proj, proj, proj)


def flash_bwd(proj, do, lse, delta, *, tq=1024, tk=1024):
    S = proj.shape[0]
    nq = S // tq
    lse_t, delta_t = _stats_t(lse, tq), _stats_t(delta, tq)

    def body(k_ref, v_ref, q_ref, do_ref, lse_ref, delta_ref, dk_ref, dv_ref, dqt_ref):
        @pl.when(pl.program_id(1) == 0)
        def _():
            dqt_ref[...] = jnp.zeros_like(dqt_ref)

        k, v = k_ref[...], v_ref[...]
        lo = _lo((tq, LANES))
        k_lo, k_hi = _split(k.astype(F32), _lo((tk, LANES)))
        kt = jnp.concatenate([k_lo.T, k_hi.T], axis=1).astype(BF)

        def step(i, carry):
            dk, dv = carry
            rows = pl.ds(pl.multiple_of(i * tq, tq), tq)
            qa, qb = _split(q_ref[rows, :], lo)
            doa, dob = _split(do_ref[rows, :], lo)
            lse_i, delta_i = lse_ref[i] * LOG2E, delta_ref[i]
            prs, dss = [], []
            for h, (qh, doh) in enumerate(((qa, doa), (qb, dob))):
                pr = jnp.exp2(_dot_nt(k, qh) - lse_i[h:h + 1, :])
                dss.append((pr * (_dot_nt(v, doh) - delta_i[h:h + 1, :])).astype(BF))
                prs.append(pr.astype(BF))
            dv = dv + _dot(jnp.concatenate(prs, axis=1), jnp.concatenate([doa, dob], axis=0))
            dk = dk + _dot(jnp.concatenate(dss, axis=1), jnp.concatenate([qa, qb], axis=0))
            dqt_ref[i] += _dot(kt, jnp.concatenate(dss, axis=0))
            return dk, dv

        z = jnp.zeros((tk, LANES), F32)
        dk, dv = lax.fori_loop(0, nq, step, (z, z))
        dk_ref[...] = dk
        dv_ref[...] = dv

    ospec = pl.BlockSpec((tk, LANES), lambda p, j: (j, p))
    stat = pl.BlockSpec((None, nq, 8, tq), lambda p, j: (p, 0, 0, 0))
    whole = lambda: pl.BlockSpec((S, LANES), lambda p, j: (0, p), pipeline_mode=pl.Buffered(1))
    sd = jax.ShapeDtypeStruct((S, 768), F32)
    dk, dv, dqt = _pcall(body, name="flash_bwd", grid=(N_PAIRS, S // tk),
                         in_specs=[pl.BlockSpec((tk, LANES), lambda p, j: (j, K_T0 + p // 2)),
                                   pl.BlockSpec((tk, LANES), lambda p, j: (j, V_T0 + p // 2)), whole(), whole(),
                                   stat, stat],
                         out_specs=[ospec, ospec,
                                    pl.BlockSpec((None, nq, LANES, tq), lambda p, j: (p, 0, 0, 0),
                                                 pipeline_mode=pl.Buffered(1))],
                         out_shape=[sd, sd, jax.ShapeDtypeStruct((N_PAIRS, nq, LANES, tq), F32)])(
        proj, proj, proj, do, lse_t, delta_t)
    dq = jnp.transpose(dqt, (1, 3, 0, 2)).reshape(S, 768)
    return dq, dk, dv


def mem_fwd(proj, mkv, *, tq=512):
    S = proj.shape[0]

    def body(q_ref, km_ref, vm_ref, o_ref, lse_ref):
        lo = _lo((tq, LANES))
        lov = _lo((N_MEM, LANES))
        for t in range(2):
            qa, qb = _split(q_ref[:, _tile(t)], lo)
            km, vm = km_ref[:, _tile(t)], vm_ref[:, _tile(t)]
            ps, dens, lses = [], [], []
            for qh in (qa, qb):
                s = _dot_nt(qh, km)
                m = jnp.max(s, axis=-1, keepdims=True)
                e = jnp.exp(s - m)
                den = jnp.sum(e, axis=-1, keepdims=True)
                ps.append(e.astype(BF))
                dens.append(den)
                lses.append(m + jnp.log(den))
            v_lo, v_hi = _split(vm, lov)
            o = _dot(jnp.concatenate(ps, axis=1), jnp.concatenate([v_lo, v_hi], axis=0))
            o_ref[:, _tile(t)] = (o / jnp.where(lo, dens[0], dens[1])).astype(BF)
            lse_ref[:, _tile(t)] = jnp.where(lo, lses[0], lses[1])

    ospec = pl.BlockSpec((tq, 256), lambda i: (i, 0))
    return _pcall(body, name="mem_fwd", grid=(S // tq,),
                  in_specs=[pl.BlockSpec((tq, 256), lambda i: (i, M_T0 // 2)),
                            pl.BlockSpec((N_MEM, 256), lambda i: (0, 0)), pl.BlockSpec((N_MEM, 256), lambda i: (0, 1))],
                  out_specs=[ospec, ospec],
                  out_shape=[jax.ShapeDtypeStruct((S, 256), BF), jax.ShapeDtypeStruct((S, 256), F32)])(proj, mkv, mkv)


def mem_bwd(proj, mkv, dcat, lse, delta, *, tq=512):
    S = proj.shape[0]

    def body(q_ref, km_ref, vm_ref, do_ref, lse_ref, delta_ref, dq_ref, dkm_ref, dvm_ref):
        @pl.when(pl.program_id(0) == 0)
        def _():
            dkm_ref[...] = jnp.zeros_like(dkm_ref)
            dvm_ref[...] = jnp.zeros_like(dvm_ref)

        lo = _lo((tq, LANES))
        lov = _lo((N_MEM, LANES))
        for t in range(2):
            qa, qb = _split(q_ref[:, _tile(t)], lo)
            doa, dob = _split(do_ref[:, _tile(t)], lo)
            km, vm = km_ref[:, _tile(t)], vm_ref[:, _tile(t)]
            lse_t, delta_t = lse_ref[:, _tile(t)], delta_ref[:, _tile(t)]
            prs, dss = [], []
            for h, (qh, doh) in enumerate(((qa, doa), (qb, dob))):
                pr = jnp.exp(_dot_nt(qh, km) - _col(lse_t, h * HD))
                dss.append(pr * (_dot_nt(doh, vm) - _col(delta_t, h * HD)))
                prs.append(pr)
            k_lo, k_hi = _split(km, lov)
            dq_ref[:, _tile(t)] = _dot(jnp.concatenate(dss, axis=1).astype(BF), jnp.concatenate([k_lo, k_hi], axis=0))
            dvm_ref[:, _tile(t)] += _dot(jnp.concatenate(prs, axis=0).T.astype(BF), jnp.concatenate([doa, dob], axis=0))
            dkm_ref[:, _tile(t)] += _dot(jnp.concatenate(dss, axis=0).T.astype(BF), jnp.concatenate([qa, qb], axis=0))

    ospec = pl.BlockSpec((tq, 256), lambda i: (i, 0))
    msp = pl.BlockSpec((N_MEM, 256), lambda i: (0, 0))
    md = jax.ShapeDtypeStruct((N_MEM, 256), F32)
    dq, dkm, dvm = _pcall(body, name="mem_bwd", grid=(S // tq,),
                          in_specs=[pl.BlockSpec((tq, 256), lambda i: (i, M_T0 // 2)), msp,
                                    pl.BlockSpec((N_MEM, 256), lambda i: (0, 1)),
                                    pl.BlockSpec((tq, 256), lambda i: (i, 3)), ospec,
                                    pl.BlockSpec((tq, 256), lambda i: (i, 3))],
                          out_specs=[ospec, msp, msp], out_shape=[jax.ShapeDtypeStruct((S, 256), F32), md, md])(
        proj, mkv, mkv, dcat, lse, delta)
    return dq, jnp.concatenate([dkm, dvm], axis=1)


def combine_fwd(os_, lses, *, tm=512):
    S = os_[0].shape[0]

    def body(o0, o1, o2, l0, l1, l2, tok_ref):
        ls = [l0[...], l1[...], l2[...]]
        m = jnp.maximum(jnp.maximum(ls[0], ls[1]), ls[2])
        es = [jnp.exp(l - m) for l in ls]
        den = es[0] + es[1] + es[2]
        for g, o in enumerate((o0, o1, o2)):
            tok_ref[:, 256 * g:256 * (g + 1)] = (o[...] * (es[g] / den)).astype(BF)

    sp = pl.BlockSpec((tm, 256), lambda i: (i, 0))
    return _pcall(body, name="combine_fwd", grid=(S // tm,), in_specs=[sp] * 6,
                  out_specs=pl.BlockSpec((tm, 768), lambda i: (i, 0)),
                  out_shape=jax.ShapeDtypeStruct((S, 768), BF))(*os_, *lses)


def combine_bwd(dcat, os_, lses, *, tm=512):
    S = dcat.shape[0]

    def body(dt_ref, o0, o1, o2, l0, l1, l2, do0, do1, do2, de0, de1, de2):
        ls = [l0[...], l1[...], l2[...]]
        m = jnp.maximum(jnp.maximum(ls[0], ls[1]), ls[2])
        es = [jnp.exp(l - m) for l in ls]
        den = es[0] + es[1] + es[2]
        alphas = [e / den for e in es]
        lo = _lo((tm, LANES))
        dts = [dt_ref[:, 256 * g:256 * (g + 1)].astype(F32) for g in range(3)]
        dal = []
        for g, o in enumerate((o0, o1, o2)):
            pr = dts[g] * o[...]
            dal.append(jnp.concatenate([_half_sum(pr[:, _tile(0)], lo), _half_sum(pr[:, _tile(1)], lo)], axis=1))
        mix = alphas[0] * dal[0] + alphas[1] * dal[1] + alphas[2] * dal[2]
        for g, (do_ref, de_ref) in enumerate(((do0, de0), (do1, de1), (do2, de2))):
            do_ref[...] = (dts[g] * alphas[g]).astype(BF)
            de_ref[...] = alphas[g] * mix

    sp = pl.BlockSpec((tm, 256), lambda i: (i, 0))
    outs = _pcall(body, name="combine_bwd", grid=(S // tm,),
                  in_specs=[pl.BlockSpec((tm, 768), lambda i: (i, 0))] + [sp] * 6, out_specs=[sp] * 6,
                  out_shape=[jax.ShapeDtypeStruct((S, 256), BF)] * 3 + [jax.ShapeDtypeStruct((S, 256), F32)] * 3)(
        dcat, *os_, *lses)
    return outs[:3], outs[3:]


def _coords():
    return lax.axis_index("x"), lax.axis_index("y"), lax.axis_index("c")


def _other_chips(x, y):
    return [(1 - x, y), (x, 1 - y), (1 - x, 1 - y)]


HBM_SPEC = pl.BlockSpec(memory_space=pltpu.HBM)


def _gather_comm(ins, outs, send, recv, lsem, start):
    x, y, c = _coords()
    me = 2 * x + y
    for a in range(len(ins)):
        local = pltpu.make_async_copy(ins[a], outs[a].at[me], lsem.at[a])
        if start:
            local.start()
        for j, (px, py) in enumerate(_other_chips(x, y)):
            sems = dict(send_sem=send.at[3 * a + j], recv_sem=recv.at[3 * a + j], device_id=(px, py, c),
                        device_id_type=MESH)
            cp = pltpu.make_async_remote_copy(src_ref=ins[a], dst_ref=outs[a].at[me], **sems)
            if start:
                cp.start()
            else:
                pltpu.make_async_remote_copy(src_ref=ins[a], dst_ref=outs[a].at[2 * px + py], **sems).wait_recv()
                cp.wait_send()
        if not start:
            local.wait()


def _gather_out_shape(shards):
    return [jax.ShapeDtypeStruct((4,) + s.shape, s.dtype) for s in shards]


def _gather_sems(n):
    return [pltpu.SemaphoreType.DMA((3 * n,)), pltpu.SemaphoreType.DMA((3 * n,)), pltpu.SemaphoreType.DMA((n,))]


def gather_shards(shards):
    n = len(shards)

    def body(*refs):
        _gather_comm(refs[:n], refs[n:2 * n], *refs[2 * n:], start=True)
        _gather_comm(refs[:n], refs[n:2 * n], *refs[2 * n:], start=False)

    return pl.pallas_call(body, name="gather_shards", in_specs=[HBM_SPEC] * n, out_specs=[HBM_SPEC] * n,
                          out_shape=_gather_out_shape(shards), scratch_shapes=_gather_sems(n))(*shards)


def _scatter_comm(ins, outs, send, recv, start):
    x, y, c = _coords()
    for a in range(len(ins)):
        for j, (px, py) in enumerate(_other_chips(x, y)):
            cp = pltpu.make_async_remote_copy(src_ref=ins[a].at[2 * px + py], dst_ref=outs[a].at[j],
                                              send_sem=send.at[3 * a + j], recv_sem=recv.at[3 * a + j],
                                              device_id=(px, py, c), device_id_type=MESH)
            if start:
                cp.start()
            else:
                cp.wait_recv()
                cp.wait_send()


def _scatter_out_shape(parts):
    return [jax.ShapeDtypeStruct((3,) + p.shape[1:], p.dtype) for p in parts]


def _scatter_sems(n):
    return [pltpu.SemaphoreType.DMA((3 * n,)), pltpu.SemaphoreType.DMA((3 * n,))]


def scatter_grads(parts):
    n = len(parts)

    def body(*refs):
        _scatter_comm(refs[:n], refs[n:2 * n], *refs[2 * n:], start=True)
        _scatter_comm(refs[:n], refs[n:2 * n], *refs[2 * n:], start=False)

    return pl.pallas_call(body, name="scatter_grads", in_specs=[HBM_SPEC] * n, out_specs=[HBM_SPEC] * n,
                          out_shape=_scatter_out_shape(parts), scratch_shapes=_scatter_sems(n))(*parts)


def sibling_swap(arrs):
    n = len(arrs)

    def body(*refs):
        ins, outs = refs[:n], refs[n:2 * n]
        send, recv = refs[2 * n:]
        x, y, c = _coords()
        cps = []
        for a in range(n):
            cp = pltpu.make_async_remote_copy(src_ref=ins[a], dst_ref=outs[a], send_sem=send.at[a], recv_sem=recv.at[a],
                                              device_id=(x, y, 1 - c), device_id_type=MESH)
            cp.start()
            cps.append(cp)
        for cp in cps:
            cp.wait_recv()
        for cp in cps:
            cp.wait_send()

    return pl.pallas_call(
        body, name="sibling_swap", in_specs=[HBM_SPEC] * n, out_specs=[HBM_SPEC] * n,
        out_shape=[jax.ShapeDtypeStruct(a.shape, a.dtype) for a in arrs],
        scratch_shapes=[pltpu.SemaphoreType.DMA((n,)), pltpu.SemaphoreType.DMA((n,))])(*arrs)


def allsum_small(v):
    rows = v.shape[0]

    def body(v_ref, tot_ref, gath_ref, send, recv):
        x, y, c = _coords()
        me = 4 * x + 2 * y + c
        gath_ref[me] = v_ref[...]
        cps = []
        for k in range(1, 8):
            fx, fy, fc = (k >> 2) & 1, (k >> 1) & 1, k & 1
            peer = (1 - x if fx else x, 1 - y if fy else y, 1 - c if fc else c)
            cp = pltpu.make_async_remote_copy(src_ref=v_ref, dst_ref=gath_ref.at[me], send_sem=send.at[k - 1],
                                              recv_sem=recv.at[k - 1], device_id=peer, device_id_type=MESH)
            cp.start()
            cps.append(cp)
        for cp in cps:
            cp.wait_recv()
        for cp in cps:
            cp.wait_send()
        tot = gath_ref[0]
        for k in range(1, 8):
            tot = tot + gath_ref[k]
        tot_ref[...] = tot

    vm = pl.BlockSpec(memory_space=pltpu.VMEM)
    tot, _ = pl.pallas_call(
        body, name="allsum_small", in_specs=[vm], out_specs=[vm, vm],
        out_shape=[jax.ShapeDtypeStruct((rows, LANES), F32), jax.ShapeDtypeStruct((8, rows, LANES), F32)],
        scratch_shapes=[pltpu.SemaphoreType.DMA((7,)), pltpu.SemaphoreType.DMA((7,))])(v)
    return tot


def sum_parts(own, recv, *, tr=256):
    R, C = own.shape
    tr = min(tr, R)

    def body(o_ref, r_ref, out_ref):
        out_ref[...] = ((o_ref[...] + r_ref[0].astype(F32)) + r_ref[1].astype(F32)) + r_ref[2].astype(F32)

    sp = pl.BlockSpec((tr, C), lambda i: (i, 0))
    return _pcall(body, name="sum_parts", grid=(R // tr,),
                  in_specs=[sp, pl.BlockSpec((3, tr, C), lambda i: (0, i, 0))], out_specs=sp,
                  out_shape=jax.ShapeDtypeStruct((R, C), F32))(own, recv)


def adamw(w, ga, gb, m, v, *, tr=256):
    R, C = w.shape
    tr = min(tr, R)
    two = gb is not None

    def body(*refs):
        if two:
            w_ref, ga_ref, gb_ref, m_ref, v_ref, g_out, d_out, m_out, v_out = refs
            g = ga_ref[...] + gb_ref[...]
        else:
            w_ref, ga_ref, m_ref, v_ref, g_out, d_out, m_out, v_out = refs
            g = ga_ref[...]
        mn = B1 * m_ref[...] + (1.0 - B1) * g
        vn = B2 * v_ref[...] + (1.0 - B2) * (g * g)
        m_hat = mn / (1.0 - B1 ** STEP)
        v_hat = vn / (1.0 - B2 ** STEP)
        g_out[...] = g
        d_out[...] = -LR * (m_hat / (jnp.sqrt(v_hat) + AEPS) + WD * w_ref[...])
        m_out[...] = mn
        v_out[...] = vn

    sp = pl.BlockSpec((tr, C), lambda i: (i, 0))
    args = [w, ga, gb, m, v] if two else [w, ga, m, v]
    sd = jax.ShapeDtypeStruct((R, C), F32)
    return _pcall(body, name="adamw", grid=(R // tr,), in_specs=[sp] * len(args), out_specs=[sp] * 4,
                  out_shape=[sd] * 4)(*args)


def _rope_tables(S):
    def inv_freq(n_dims, theta):
        return theta ** (-(jnp.arange(0, n_dims, 2, dtype=jnp.float32) / n_dims))

    pos = lax.broadcasted_iota(jnp.int32, (S, LANES), 0)
    d = lax.broadcasted_iota(jnp.int32, (S, LANES), 1) % HD
    d1 = lax.iota(jnp.int32, LANES) % HD
    ang = pos.astype(F32) * inv_freq(HD // 4, ROPE_THETA)[d1 % 8][None, :]
    sin = jnp.sin(ang)
    partial = (jnp.where(d < 16, jnp.cos(ang), 1.0), jnp.where((d >= 8) & (d < 16), sin, 0.0),
               jnp.where(d < 8, -sin, 0.0))
    grid_pos = jnp.where(d < 32, pos // GRID_W, pos % GRID_W)
    ang = grid_pos.astype(F32) * inv_freq(HD // 2, AXIAL_THETA)[d1 % 16][None, :]
    sin = jnp.sin(ang)
    axial = (jnp.cos(ang), jnp.where(d % 32 >= 16, sin, 0.0), jnp.where(d % 32 < 16, -sin, 0.0))
    return partial, axial


def _pad_w_in(w):
    cols = [w[:, :768]]
    for base in (768, 960):
        for g in range(3):
            kg = w[:, base + g * HD:base + (g + 1) * HD]
            cols += [kg, kg]
    cols.append(w[:, 1152:])
    return jnp.concatenate(cols, axis=1)


def _unpad_dw_in(dw):
    cols = [dw[:, :768]]
    for t0 in (K_T0, V_T0):
        for g in range(3):
            b = (t0 + g) * LANES
            cols.append(dw[:, b:b + HD] + dw[:, b + HD:b + LANES])
    cols.append(dw[:, M_T0 * LANES:])
    return jnp.concatenate(cols, axis=1)


def _stats_t(a, tq):
    S = a.shape[0]
    t = a.reshape(S, -1, 2, HD)[:, :N_PAIRS, :, 0]
    t = jnp.transpose(t, (1, 2, 0))
    t = jnp.pad(t, ((0, 0), (0, 6), (0, 0)))
    return jnp.transpose(t.reshape(N_PAIRS, 8, S // tq, tq), (0, 2, 1, 3))


def _grad_slices(dW_in, dW_mkv, dW_o, dW_gu, dW_d):
    return [None if dW_in is None else jnp.transpose(dW_in.reshape(D, 4, IN_W // 4), (1, 0, 2)),
            dW_mkv.reshape(4, D // 4, 512), dW_o.reshape(4, D // 4, D),
            jnp.transpose(dW_gu.reshape(D, 4, 2 * DFF // 4), (1, 0, 2)), dW_d.reshape(4, DFF // 4, D)]


def _fold(t):
    return t[..., :HD] + t[..., HD:]


def kernel(x, mem, mem_norm_g, w_in, w_mem_kv, w_o, g_mix_pre, g_mix_post, attn_sink, qk_norm_g, w_gate_up, w_down, g_ffn_pre, g_ffn_post, loss_target, m_mem_norm_g, m_w_in, m_w_mem_kv, m_w_o, m_g_mix_pre, m_g_mix_post, m_attn_sink, m_qk_norm_g, m_w_gate_up, m_w_down, m_g_ffn_pre, m_g_ffn_post, v_mem_norm_g, v_w_in, v_w_mem_kv, v_w_o, v_g_mix_pre, v_g_mix_post, v_attn_sink, v_qk_norm_g, v_w_gate_up, v_w_down, v_g_ffn_pre, v_g_ffn_post):
    S = x.shape[1]
    depth = w_in.shape[0]
    xs, memx, tgt = x[0], mem[0], loss_target[0]
    tab_p, tab_a = _rope_tables(S)
    row = lambda a: a.reshape(1, -1)

    shards_bf = [w.astype(BF) for w in (w_in, w_mem_kv, w_o, w_gate_up, w_down)]
    layer_shards = lambda i: [s[i] for s in shards_bf]
    W_in, W_mkv, W_o, W_g, W_u, W_gu, W_d = ([None] * depth for _ in range(7))

    def set_weights(i, gathered):
        if len(gathered) == 5:
            W_in[i] = jnp.concatenate([gathered[0][s] for s in range(4)], axis=1)
        gm, go, gg, gd = gathered[-4:]
        W_mkv[i] = jnp.concatenate([gm[s] for s in range(4)], axis=0)
        W_o[i] = jnp.concatenate([go[s] for s in range(4)], axis=0)
        W_g[i] = jnp.concatenate([gg[0], gg[1]], axis=1)
        W_u[i] = jnp.concatenate([gg[2], gg[3]], axis=1)
        W_gu[i] = jnp.concatenate([gg[0], gg[2], gg[1], gg[3]], axis=1)
        W_d[i] = jnp.concatenate([gd[s] for s in range(4)], axis=0)

    (g_in0,) = gather_shards(layer_shards(0)[:1])
    W_in[0] = jnp.concatenate([g_in0[s] for s in range(4)], axis=1)
    mem_g = row(mem_norm_g)
    zero_sink = jnp.zeros((12,), F32)
    qkg = jnp.pad(jnp.concatenate([qk_norm_g[0], qk_norm_g[0]], axis=1), ((0, 6), (0, 0)))
    no_qkg = jnp.zeros((8, LANES), F32)

    saved = []
    cur = xs
    for i in range(depth):
        kind = i % 3
        wp = _pad_w_in(W_in[i])
        sv = dict(x=cur, wp=wp)
        if kind == 1:
            h1, proj, raw, nrm = inproj_fwd(cur, row(g_mix_pre[i]), wp, tab_a, qkg, axial=True)
            sv["raw"] = raw
        else:
            h1, proj = inproj_fwd(cur, row(g_mix_pre[i]), wp, tab_p, no_qkg, axial=False)
        if kind == 0:
            res = banded_fwd(proj, attn_sink[i // 3], d=1, R=A_RADIUS, TQ=_band_tile(A_RADIUS, S), pair0=0, npairs=6,
                             use_sink=True, o_dtype=BF, gather=layer_shards(0)[1:] if i == 0 else ())
            tok, lse = res[:2]
            if i == 0:
                set_weights(0, res[2])
        elif kind == 1:
            bound = jnp.sqrt(jnp.max(nrm[0]) * jnp.max(nrm[1])) * LN2
            tok, lse = flash_fwd(proj, bound)
        else:
            os_, lses = [], []
            for g, (window, dil) in enumerate(C_GROUPS):
                rad = window // (2 * dil)
                o_g, l_g = banded_fwd(proj, zero_sink, d=dil, R=rad, TQ=_band_tile(rad, S // dil), pair0=2 * g, npairs=2,
                                      use_sink=False, o_dtype=F32)
                os_.append(o_g)
                lses.append(l_g)
            tok = combine_fwd(os_, lses)
            sv["os"], lse = os_, lses
        mem_n, mkv = norm_mm(memx, mem_g, W_mkv[i], tm=N_MEM)
        mo, mlse = mem_fwd(proj, mkv)
        o, x2 = mm_norm_res([tok, mo], W_o[i], row(g_mix_post[i]), cur)
        (h2, gate, up, act), gathered = ffn_up_fwd(x2, row(g_ffn_pre[i]), W_gu[i],
                                                   layer_shards(i + 1) if i + 1 < depth else ())
        if i + 1 < depth:
            set_weights(i + 1, gathered)
        f, x3 = mm_norm_res([act], W_d[i], row(g_ffn_post[i]), x2)
        sv.update(h1=h1, proj=proj, lse=lse, mem_n=mem_n, mkv=mkv, mlse=mlse, tok=tok, mo=mo, o=o, x2=x2, h2=h2, gate=gate,
                  up=up, act=act, f=f)
        saved.append(sv)
        cur = x3

    dcur, loss_vec = loss_bwd(cur, tgt)

    grad_parts, grad_recv = [None] * depth, [None] * depth
    dg_pre, dg_post, dg_fpre, dg_fpost = [None] * depth, [None] * depth, [None] * depth, [None] * depth
    dg_mem = jnp.zeros((1, D), F32)
    dsinks, dqk = {}, None
    for i in reversed(range(depth)):
        sv = saved[i]
        kind = i % 3
        proj = sv["proj"]
        pending = [p.astype(BF) for p in grad_parts[i + 1]] if i + 1 < depth else ()
        (df, dgate, dup, dg_fpost[i]), got = normbwd_mm_swiglu(dcur, sv["f"], row(g_ffn_post[i]), W_d[i], sv["gate"],
                                                               sv["up"], pending)
        if i + 1 < depth:
            grad_recv[i + 1] = got
        dx2, dg_fpre[i] = mm_nt_normbwd_res([(dgate, W_g[i]), (dup, W_u[i])], sv["x2"], row(g_ffn_pre[i]), dcur)
        dW_d = mm_acc(sv["act"], df, tk=1408, tn=D, ts=2048)
        dW_gu = jnp.concatenate([mm_acc(sv["h2"], dgate, tk=D, tn=1408, ts=2048),
                                 mm_acc(sv["h2"], dup, tk=D, tn=1408, ts=2048)], axis=1)
        do, dcat, delta, dg_post[i] = normbwd_mm_cat(dx2, sv["o"], row(g_mix_post[i]), W_o[i], sv["tok"], sv["mo"])
        dW_o = jnp.concatenate([mm_acc(sv["tok"], do, tk=768, tn=D, ts=2048), mm_acc(sv["mo"], do, tk=256, tn=D, ts=2048)],
                               axis=0)
        dqm, dmkv = mem_bwd(proj, sv["mkv"], dcat, sv["mlse"], delta)
        dmkv = dmkv.astype(BF)
        (dgm,) = mm_nt_normbwd_res([(dmkv, W_mkv[i])], memx, mem_g, None, tm=N_MEM)
        dg_mem = dg_mem + dgm
        dW_mkv = mm_acc(sv["mem_n"], dmkv, tk=D, tn=512, ts=N_MEM)
        if kind == 0:
            sink = attn_sink[i // 3]
            args = dict(d=1, R=A_RADIUS, pair0=0, npairs=6)
            tile = _band_tile(A_RADIUS, S)
            early = [p.astype(BF) for p in _grad_slices(None, dW_mkv, dW_o, dW_gu, dW_d)[1:]] if i == 0 else ()
            res = banded_bwd_dq(proj, dcat, sv["lse"], delta, sink, TQ=tile, use_sink=True, scatter=early, **args)
            dq, dsk = res[:2]
            if i == 0:
                early_recv = res[2]
            dkp, dvp = banded_bwd_dkv(proj, dcat, sv["lse"], delta, TK=tile, **args)
            dsinks[i // 3] = dsk.reshape(3, 8, 2, 2, HD)[:, 0, :, :, 0].reshape(12)
        elif kind == 1:
            dq, dkp, dvp = flash_bwd(proj, dcat, sv["lse"], delta)
        else:
            dos, des = combine_bwd(dcat, sv["os"], sv["lse"])
            dqs, dks, dvs = [], [], []
            for g, (window, dil) in enumerate(C_GROUPS):
                rad = window // (2 * dil)
                args = dict(d=dil, R=rad, pair0=2 * g, npairs=2)
                tile = _band_tile(rad, S // dil)
                dq_g, _ = banded_bwd_dq(proj, dos[g], sv["lse"][g], des[g], zero_sink, TQ=tile, use_sink=False, **args)
                dk_g, dv_g = banded_bwd_dkv(proj, dos[g], sv["lse"][g], des[g], TK=tile, **args)
                dqs.append(dq_g)
                dks.append(dk_g)
                dvs.append(dv_g)
            dq, dkp, dvp = (jnp.concatenate(t, axis=1) for t in (dqs, dks, dvs))
        if kind == 1:
            dcur, dproj, dg_pre[i], dqk_t = inproj_bwd(dq, dkp, dvp, dqm, tab_a, sv["raw"], qkg, sv["wp"], sv["x"],
                                                       row(g_mix_pre[i]), dx2, axial=True)
            dqk = _fold(dqk_t[:2]).reshape(1, 2, HD)
        else:
            dcur, dproj, dg_pre[i], _ = inproj_bwd(dq, dkp, dvp, dqm, tab_p, proj, no_qkg, sv["wp"],
                                                   sv["x"], row(g_mix_pre[i]), dx2, axial=False)
        dW_in = _unpad_dw_in(mm_acc(sv["h1"], dproj, tk=D, tn=896, ts=2048))
        grad_parts[i] = _grad_slices(dW_in, dW_mkv, dW_o, dW_gu, dW_d)
    grad_recv[0] = list(scatter_grads([grad_parts[0][0].astype(BF)])) + list(early_recv)

    x_i, y_i, _ = _coords()
    me = 2 * x_i + y_i
    big = [(w_in, m_w_in, v_w_in), (w_mem_kv, m_w_mem_kv, v_w_mem_kv), (w_o, m_w_o, v_w_o),
           (w_gate_up, m_w_gate_up, v_w_gate_up), (w_down, m_w_down, v_w_down)]
    parts = []
    for a, (w, _, _) in enumerate(big):
        C = w.shape[-1]
        own = jnp.stack([lax.dynamic_index_in_dim(grad_parts[l][a], me, 0, keepdims=False) for l in range(depth)])
        rc = jnp.stack([grad_recv[l][a] for l in range(depth)], axis=1)
        parts.append(sum_parts(own.reshape(-1, C), rc.reshape(3, -1, C)))
    sibs = sibling_swap(parts)
    big_out = []
    for (w, m, v), pa, pb in zip(big, parts, sibs):
        C = w.shape[-1]
        outs = adamw(w.reshape(-1, C), pa, pb, m.reshape(-1, C), v.reshape(-1, C))
        big_out.append([o.reshape(w.shape) for o in outs])

    small_w = [mem_norm_g, g_mix_pre, g_mix_post, attn_sink, qk_norm_g, g_ffn_pre, g_ffn_post]
    small_m = [m_mem_norm_g, m_g_mix_pre, m_g_mix_post, m_attn_sink, m_qk_norm_g, m_g_ffn_pre, m_g_ffn_post]
    small_v = [v_mem_norm_g, v_g_mix_pre, v_g_mix_post, v_attn_sink, v_qk_norm_g, v_g_ffn_pre, v_g_ffn_post]
    small_g = [dg_mem.reshape(D), jnp.concatenate(dg_pre, axis=0), jnp.concatenate(dg_post, axis=0),
               jnp.stack([dsinks[k] for k in sorted(dsinks)]), dqk, jnp.concatenate(dg_fpre, axis=0),
               jnp.concatenate(dg_fpost, axis=0)]
    sizes = [a.size for a in small_w]
    total = sum(sizes)
    rows_s = -(-(total + LANES) // (8 * LANES)) * 8

    def pack(arrs, extra=None):
        flat = jnp.concatenate([a.reshape(-1).astype(F32) for a in arrs])
        flat = jnp.pad(flat, (0, rows_s * LANES - LANES - total))
        tail = jnp.zeros((LANES,), F32) if extra is None else extra.reshape(LANES)
        return jnp.concatenate([flat, tail]).reshape(rows_s, LANES)

    tot = allsum_small(pack(small_g, loss_vec))
    loss = jnp.sum(tot[rows_s - 1])
    s_out = adamw(pack(small_w), tot, None, pack(small_m), pack(small_v))

    def unpack(buf):
        flat = buf.reshape(-1)
        out, off = [], 0
        for a, n in zip(small_w, sizes):
            out.append(flat[off:off + n].reshape(a.shape))
            off += n
        return out

    sg, sd_, sm, sv_ = (unpack(b) for b in s_out)

    def ordered(k):
        sm_ = (sg, sd_, sm, sv_)[k]
        b = [bo[k] for bo in big_out]
        return [sm_[0], b[0], b[1], b[2], sm_[1], sm_[2], sm_[3], sm_[4], b[3], b[4], sm_[5], sm_[6]]

    dx_out = dcur.reshape(1, S, D)
    return (loss, dx_out, *ordered(0), *ordered(1), *ordered(2), *ordered(3))
```

```python
import functools

import jax
import jax.numpy as jnp
from jax import lax
from jax.experimental import pallas as pl
from jax.experimental.pallas import tpu as pltpu

F32 = jnp.float32
BF = jnp.bfloat16

D = 1024
HD = 64
LANES = 128
N_PAIRS = 6
DFF = 2816
IN_W = 1408
PW = 14 * LANES
K_T0, V_T0, M_T0 = 6, 9, 12
EPS = 1e-6
SCALE = HD ** -0.5
NEG = -1e30
LOG2E = 1.4426950408889634
LN2 = 0.6931471805599453
MAX_PLAIN_SCORE = 40.0
ROPE_THETA = 500000.0
AXIAL_THETA = 10000.0
GRID_W = 64
A_RADIUS = 128
C_GROUPS = ((128, 1), (512, 4), (2048, 16))
N_MEM = 256
LR, B1, B2, AEPS, WD, STEP = 0.001, 0.9, 0.999, 1e-08, 0.01, 10
VMEM_LIMIT = 56 * 1024 * 1024
MESH = pl.DeviceIdType.MESH


def _pcall(body, *, name, grid, in_specs, out_specs, out_shape, scratch=()):
    return pl.pallas_call(
        body, name=name, grid=grid, in_specs=in_specs, out_specs=out_specs, out_shape=out_shape,
        scratch_shapes=scratch,
        compiler_params=pltpu.CompilerParams(dimension_semantics=("arbitrary",) * len(grid),
                                             vmem_limit_bytes=VMEM_LIMIT))


def _pcall_behind(body, comm, arrays, comm_out_shape, comm_sems, *, name, grid, in_specs, out_specs, out_shape,
                  scratch=()):
    n_in, n_out, n_scr, n = len(in_specs), len(out_specs), len(scratch), len(arrays)
    last = tuple(g - 1 for g in grid)

    def at(step):
        cond = pl.program_id(0) == step[0]
        for a in range(1, len(grid)):
            cond = cond & (pl.program_id(a) == step[a])
        return cond

    def wrapped(*refs):
        ins, cin = refs[:n_in], refs[n_in:n_in + n]
        outs, cout = refs[n_in + n:n_in + n + n_out], refs[n_in + n + n_out:n_in + 2 * n + n_out]
        scr, sems = refs[n_in + 2 * n + n_out:n_in + 2 * n + n_out + n_scr], refs[n_in + 2 * n + n_out + n_scr:]

        @pl.when(at((0,) * len(grid)))
        def _():
            comm(cin, cout, *sems, start=True)

        body(*ins, *outs, *scr)

        @pl.when(at(last))
        def _():
            comm(cin, cout, *sems, start=False)

    call = _pcall(wrapped, name=name, grid=grid, in_specs=list(in_specs) + [HBM_SPEC] * n,
                  out_specs=list(out_specs) + [HBM_SPEC] * n, out_shape=list(out_shape) + list(comm_out_shape),
                  scratch=list(scratch) + list(comm_sems))

    def run(*args):
        res = call(*args, *arrays)
        return res[:n_out], res[n_out:]

    return run


def _dot(a, b):
    return lax.dot_general(a, b, (((1,), (0,)), ((), ())), preferred_element_type=F32)


def _dot_nt(a, b):
    return lax.dot_general(a, b, (((1,), (1,)), ((), ())), preferred_element_type=F32)


def _lo(shape):
    return lax.broadcasted_iota(jnp.int32, shape, len(shape) - 1) < HD


def _half_sum(x):
    r = lax.broadcasted_iota(jnp.int32, (LANES, LANES), 0) // HD
    c = lax.broadcasted_iota(jnp.int32, (LANES, LANES), 1) // HD
    ones = (r == c).astype(BF)
    hi = x.astype(BF)
    return _dot(hi, ones) + _dot((x - hi.astype(F32)).astype(BF), ones)


def _col(tile, lane):
    idx = lax.broadcasted_iota(jnp.int32, tile.shape, 1)
    return jnp.sum(jnp.where(idx == lane, tile, 0.0), axis=-1, keepdims=True)


def _split(t, lo):
    z = jnp.zeros_like(t)
    return jnp.where(lo, t, z), jnp.where(lo, z, t)


def _rms(xf, g):
    r = lax.rsqrt(jnp.mean(xf * xf, axis=-1, keepdims=True) + EPS)
    return xf * r * g


def _rms_bwd(xf, g, dy):
    r = lax.rsqrt(jnp.mean(xf * xf, axis=-1, keepdims=True) + EPS)
    xr = xf * r
    dg = jnp.sum(dy * xr, axis=0, keepdims=True)
    t = dy * g
    return r * (t - xr * jnp.mean(xr * t, axis=-1, keepdims=True)), dg


def _rope_fwd(y, c, s1, s2, sh):
    return y * c + pltpu.roll(y, sh, 1) * s1 + pltpu.roll(y, LANES - sh, 1) * s2


def _rope_bwd(dy, c, s1, s2, sh):
    return dy * c + pltpu.roll(dy * s1, LANES - sh, 1) + pltpu.roll(dy * s2, sh, 1)


def _tile(t):
    return slice(t * LANES, (t + 1) * LANES)


def inproj_fwd(x, g, w_pad, tabs, qkg, *, axial, tm=512):
    S = x.shape[0]
    sh = 16 if axial else 8

    def body(x_ref, g_ref, w_ref, c_ref, s1_ref, s2_ref, qkg_ref, h_ref, p_ref, *extra):
        h = _rms(x_ref[...], g_ref[...]).astype(BF)
        h_ref[...] = h
        acc = _dot(h, w_ref[...])
        c, s1, s2 = c_ref[...], s1_ref[...], s2_ref[...]
        lo = _lo((tm, LANES))
        if axial:
            raw_ref, nrm_ref = extra

            @pl.when(pl.program_id(0) == 0)
            def _():
                nrm_ref[...] = jnp.zeros_like(nrm_ref)

        for t in range(14):
            y = acc[:, _tile(t)]
            if t < V_T0:
                if axial:
                    raw_ref[:, _tile(t)] = y.astype(BF)
                    gt = qkg_ref[0:1, :] if t < K_T0 else qkg_ref[1:2, :]
                    y = y * lax.rsqrt(_half_sum(y * y) * (1.0 / HD) + EPS) * gt
                y = _rope_fwd(y, c, s1, s2, sh)
            if t < K_T0:
                y = y * (SCALE * LOG2E if axial else SCALE)
            elif t >= M_T0:
                y = y * SCALE
            yb = y.astype(BF)
            p_ref[:, _tile(t)] = yb
            if axial and t < V_T0:
                yf = yb.astype(F32)
                n2 = jnp.max(_half_sum(yf * yf), axis=0, keepdims=True)
                r = 0 if t < K_T0 else 1
                nrm_ref[r:r + 1, :] = jnp.maximum(nrm_ref[r:r + 1, :], n2)

    row = lambda w: pl.BlockSpec((tm, w), lambda i: (i, 0))
    full = lambda a: pl.BlockSpec(a.shape, lambda i: (0, 0))
    out_shape = [jax.ShapeDtypeStruct((S, D), BF), jax.ShapeDtypeStruct((S, PW), BF)]
    out_specs = [row(D), row(PW)]
    if axial:
        out_shape += [jax.ShapeDtypeStruct((S, V_T0 * LANES), BF), jax.ShapeDtypeStruct((8, LANES), F32)]
        out_specs += [row(V_T0 * LANES), pl.BlockSpec((8, LANES), lambda i: (0, 0))]
    return _pcall(body, name="inproj_fwd_axial" if axial else "inproj_fwd", grid=(S // tm,),
                  in_specs=[row(D), full(g), full(w_pad), row(LANES), row(LANES), row(LANES), full(qkg)],
                  out_specs=out_specs, out_shape=out_shape)(x, g, w_pad, *tabs, qkg)


def norm_mm(x, g, w, *, tm):
    S, N = x.shape[0], w.shape[1]

    def body(x_ref, g_ref, w_ref, h_ref, y_ref):
        h = _rms(x_ref[...], g_ref[...]).astype(BF)
        h_ref[...] = h
        y_ref[...] = _dot(h, w_ref[...]).astype(BF)

    return _pcall(body, name="norm_mm", grid=(S // tm,),
                  in_specs=[pl.BlockSpec((tm, D), lambda i: (i, 0)), pl.BlockSpec(g.shape, lambda i: (0, 0)),
                            pl.BlockSpec(w.shape, lambda i: (0, 0))],
                  out_specs=[pl.BlockSpec((tm, D), lambda i: (i, 0)), pl.BlockSpec((tm, N), lambda i: (i, 0))],
                  out_shape=[jax.ShapeDtypeStruct((S, D), BF), jax.ShapeDtypeStruct((S, N), BF)])(x, g, w)


def ffn_up_fwd(x, g, w_gu, next_shards=(), *, tm=512, tn=1408):
    S = x.shape[0]

    def body(x_ref, g_ref, w_ref, h_ref, gate_ref, up_ref, a_ref, h_scr):
        @pl.when(pl.program_id(1) == 0)
        def _():
            h = _rms(x_ref[...], g_ref[...]).astype(BF)
            h_scr[...] = h
            h_ref[...] = h

        acc = _dot(h_scr[...], w_ref[...])
        gate, up = acc[:, :tn], acc[:, tn:]
        sig = pl.reciprocal(1.0 + jnp.exp(-gate), approx=True)
        silu = gate * sig
        gate_ref[...] = (up * (sig * (1.0 + gate * (1.0 - sig)))).astype(BF)
        up_ref[...] = silu.astype(BF)
        a_ref[...] = (silu * up).astype(BF)

    rowd = pl.BlockSpec((tm, D), lambda i, j: (i, 0))
    osp = pl.BlockSpec((tm, tn), lambda i, j: (i, j))
    sd = jax.ShapeDtypeStruct((S, DFF), BF)
    kw = dict(grid=(S // tm, DFF // tn),
              in_specs=[rowd, pl.BlockSpec(g.shape, lambda i, j: (0, 0)), pl.BlockSpec((D, 2 * tn), lambda i, j: (0, j))],
              out_specs=[rowd, osp, osp, osp], out_shape=[jax.ShapeDtypeStruct((S, D), BF), sd, sd, sd],
              scratch=[pltpu.VMEM((tm, D), BF)])
    if next_shards:
        return _pcall_behind(body, _gather_comm, next_shards, _gather_out_shape(next_shards),
                             _gather_sems(len(next_shards)), name="ffn_up_fwd_gather", **kw)(x, g, w_gu)
    return _pcall(body, name="ffn_up_fwd", **kw)(x, g, w_gu), ()


def mm_norm_res(a_parts, w, g, res, *, tm=512):
    S = a_parts[0].shape[0]
    na = len(a_parts)

    def body(*refs):
        w_ref, g_ref, res_ref, y_ref, o_ref = refs[na:]
        a = refs[0][...] if na == 1 else jnp.concatenate([r[...] for r in refs[:na]], axis=1)
        y = _dot(a, w_ref[...])
        y_ref[...] = y
        o_ref[...] = res_ref[...] + _rms(y, g_ref[...])

    rowd = pl.BlockSpec((tm, D), lambda i: (i, 0))
    sd = jax.ShapeDtypeStruct((S, D), F32)
    return _pcall(body, name="mm_norm_res", grid=(S // tm,),
                  in_specs=[pl.BlockSpec((tm, a.shape[1]), lambda i: (i, 0)) for a in a_parts]
                  + [pl.BlockSpec(w.shape, lambda i: (0, 0)), pl.BlockSpec(g.shape, lambda i: (0, 0)), rowd],
                  out_specs=[rowd, rowd], out_shape=[sd, sd])(*a_parts, w, g, res)


def loss_bwd(y, tgt, *, tm=512):
    S = y.shape[0]

    def body(y_ref, t_ref, dy_ref, l_ref):
        @pl.when(pl.program_id(0) == 0)
        def _():
            l_ref[...] = jnp.zeros_like(l_ref)

        e = y_ref[...] - t_ref[...]
        dy_ref[...] = e * (1.0 / D)
        col = jnp.sum(e * e, axis=0, keepdims=True)
        part = col[:, _tile(0)]
        for t in range(1, D // LANES):
            part = part + col[:, _tile(t)]
        l_ref[...] += part * (0.5 / D)

    rowd = pl.BlockSpec((tm, D), lambda i: (i, 0))
    return _pcall(body, name="loss_bwd", grid=(S // tm,), in_specs=[rowd, rowd],
                  out_specs=[rowd, pl.BlockSpec((1, LANES), lambda i: (0, 0))],
                  out_shape=[jax.ShapeDtypeStruct((S, D), F32), jax.ShapeDtypeStruct((1, LANES), F32)])(y, tgt)


def normbwd_mm_cat(dy, ysaved, g, w, tok, mo, *, tm=512):
    S = dy.shape[0]
    n_tok = tok.shape[1] // LANES

    def body(dy_ref, y_ref, g_ref, w_ref, tok_ref, mo_ref, d_ref, dcat_ref, delta_ref, dg_ref):
        @pl.when(pl.program_id(0) == 0)
        def _():
            dg_ref[...] = jnp.zeros_like(dg_ref)

        d, dg = _rms_bwd(y_ref[...], g_ref[...], dy_ref[...])
        dg_ref[...] += dg
        d = d.astype(BF)
        d_ref[...] = d
        z = _dot_nt(d, w_ref[...])
        dcat_ref[...] = z.astype(BF)
        lo = _lo((tm, LANES))
        for t in range(D // LANES):
            c = tok_ref[:, _tile(t)] if t < n_tok else mo_ref[:, _tile(t - n_tok)]
            delta_ref[:, _tile(t)] = _half_sum(z[:, _tile(t)] * c.astype(F32))

    rowd = pl.BlockSpec((tm, D), lambda i: (i, 0))
    return _pcall(body, name="normbwd_mm_cat", grid=(S // tm,),
                  in_specs=[rowd, rowd, pl.BlockSpec(g.shape, lambda i: (0, 0)), pl.BlockSpec(w.shape, lambda i: (0, 0)),
                            pl.BlockSpec((tm, tok.shape[1]), lambda i: (i, 0)),
                            pl.BlockSpec((tm, mo.shape[1]), lambda i: (i, 0))],
                  out_specs=[rowd, rowd, rowd, pl.BlockSpec((1, D), lambda i: (0, 0))],
                  out_shape=[jax.ShapeDtypeStruct((S, D), BF), jax.ShapeDtypeStruct((S, D), BF),
                             jax.ShapeDtypeStruct((S, D), F32), jax.ShapeDtypeStruct((1, D), F32)])(
        dy, ysaved, g, w, tok, mo)


def normbwd_mm_swiglu(dy, fsaved, g, wd, gate, up, grad_parts=(), *, tm=512, tn=1408):
    S = dy.shape[0]

    def body(dy_ref, f_ref, g_ref, w_ref, gate_ref, up_ref, df_ref, dgate_ref, dup_ref, dg_ref, d_scr):
        i, j = pl.program_id(0), pl.program_id(1)

        @pl.when((i == 0) & (j == 0))
        def _():
            dg_ref[...] = jnp.zeros_like(dg_ref)

        @pl.when(j == 0)
        def _():
            d, dg = _rms_bwd(f_ref[...], g_ref[...], dy_ref[...])
            dg_ref[...] += dg
            d_scr[...] = d.astype(BF)
            df_ref[...] = d.astype(BF)

        da = _dot_nt(d_scr[...], w_ref[...])
        dgate_ref[...] = (da * gate_ref[...].astype(F32)).astype(BF)
        dup_ref[...] = (da * up_ref[...].astype(F32)).astype(BF)

    rowd = pl.BlockSpec((tm, D), lambda i, j: (i, 0))
    osp = pl.BlockSpec((tm, tn), lambda i, j: (i, j))
    sd = jax.ShapeDtypeStruct((S, DFF), BF)
    kw = dict(grid=(S // tm, DFF // tn),
              in_specs=[rowd, rowd, pl.BlockSpec(g.shape, lambda i, j: (0, 0)), pl.BlockSpec((tn, D), lambda i, j: (j, 0)),
                        osp, osp],
              out_specs=[rowd, osp, osp, pl.BlockSpec((1, D), lambda i, j: (0, 0))],
              out_shape=[jax.ShapeDtypeStruct((S, D), BF), sd, sd, jax.ShapeDtypeStruct((1, D), F32)],
              scratch=[pltpu.VMEM((tm, D), BF)])
    args = (dy, fsaved, g, wd, gate, up)
    if grad_parts:
        return _pcall_behind(body, _scatter_comm, grad_parts, _scatter_out_shape(grad_parts),
                             _scatter_sems(len(grad_parts)), name="normbwd_mm_swiglu_scatter", **kw)(*args)
    return _pcall(body, name="normbwd_mm_swiglu", **kw)(*args), ()


def mm_nt_normbwd_res(parts, xin, g, dres, *, tm=512):
    S = xin.shape[0]
    npart = len(parts)
    has_res = dres is not None

    def body(*refs):
        prefs = refs[:2 * npart]
        x_ref, g_ref = refs[2 * npart:2 * npart + 2]
        rest = refs[2 * npart + 2:]
        if has_res:
            dres_ref, dx_ref, dg_ref = rest
        else:
            (dg_ref,) = rest

        @pl.when(pl.program_id(0) == 0)
        def _():
            dg_ref[...] = jnp.zeros_like(dg_ref)

        z = _dot_nt(prefs[0][...], prefs[1][...])
        for p in range(1, npart):
            z = z + _dot_nt(prefs[2 * p][...], prefs[2 * p + 1][...])
        dx, dg = _rms_bwd(x_ref[...], g_ref[...], z)
        dg_ref[...] += dg
        if has_res:
            dx_ref[...] = dres_ref[...] + dx

    rowd = pl.BlockSpec((tm, D), lambda i: (i, 0))
    in_specs, args = [], []
    for dy, w in parts:
        in_specs += [pl.BlockSpec((tm, dy.shape[1]), lambda i: (i, 0)),
                     pl.BlockSpec(w.shape, lambda i: (0, 0), pipeline_mode=pl.Buffered(1))]
        args += [dy, w]
    in_specs += [rowd, pl.BlockSpec(g.shape, lambda i: (0, 0))]
    args += [xin, g]
    out_specs = [pl.BlockSpec((1, D), lambda i: (0, 0))]
    out_shape = [jax.ShapeDtypeStruct((1, D), F32)]
    if has_res:
        in_specs.append(rowd)
        args.append(dres)
        out_specs.insert(0, rowd)
        out_shape.insert(0, jax.ShapeDtypeStruct((S, D), F32))
    return _pcall(body, name="mm_nt_normbwd_res" if has_res else "mm_nt_normbwd", grid=(S // tm,),
                  in_specs=in_specs, out_specs=out_specs, out_shape=out_shape)(*args)


def inproj_bwd(dq, dkp, dvp, dqm, tabs, raw, qkg, w_pad, xin, g, dres, *, axial, tm=512):
    S = xin.shape[0]
    sh = 16 if axial else 8

    def body(dq_ref, dk_ref, dv_ref, dm_ref, c_ref, s1_ref, s2_ref, raw_ref, qkg_ref, w_ref, x_ref, g_ref, dres_ref,
             dx_ref, dp_ref, dg_ref, dqk_ref):
        @pl.when(pl.program_id(0) == 0)
        def _():
            dg_ref[...] = jnp.zeros_like(dg_ref)
            dqk_ref[...] = jnp.zeros_like(dqk_ref)

        c, s1, s2 = c_ref[...], s1_ref[...], s2_ref[...]
        lo = _lo((tm, LANES))
        for t in range(14):
            if t < K_T0:
                y = dq_ref[:, _tile(t)].astype(F32) * SCALE
            elif t < V_T0:
                y = dk_ref[:, _tile(2 * (t - K_T0))].astype(F32) + dk_ref[:, _tile(2 * (t - K_T0) + 1)].astype(F32)
                if axial:
                    y = y * LN2
            elif t < M_T0:
                y = dv_ref[:, _tile(2 * (t - V_T0))].astype(F32) + dv_ref[:, _tile(2 * (t - V_T0) + 1)].astype(F32)
            else:
                y = dm_ref[:, _tile(t - M_T0)] * SCALE
            if t < V_T0:
                y = _rope_bwd(y, c, s1, s2, sh)
                if axial:
                    row = 0 if t < K_T0 else 1
                    xr = raw_ref[:, _tile(t)].astype(F32)
                    r = lax.rsqrt(_half_sum(xr * xr) * (1.0 / HD) + EPS)
                    xn = xr * r
                    dqk_ref[row:row + 1, :] += jnp.sum(y * xn, axis=0, keepdims=True)
                    tt = y * qkg_ref[row:row + 1, :]
                    y = r * (tt - xn * (_half_sum(xn * tt) * (1.0 / HD)))
            dp_ref[:, _tile(t)] = y.astype(BF)
        z = _dot_nt(dp_ref[...], w_ref[...])
        dx, dg = _rms_bwd(x_ref[...], g_ref[...], z)
        dg_ref[...] += dg
        dx_ref[...] = dres_ref[...] + dx

    row = lambda w: pl.BlockSpec((tm, w), lambda i: (i, 0))
    full = lambda a: pl.BlockSpec(a.shape, lambda i: (0, 0))
    return _pcall(body, name="inproj_bwd_axial" if axial else "inproj_bwd", grid=(S // tm,),
                  in_specs=[row(768), row(768), row(768), row(256), row(LANES), row(LANES), row(LANES),
                            row(raw.shape[1] if axial else LANES), full(qkg), full(w_pad), row(D), full(g), row(D)],
                  out_specs=[row(D), row(PW), pl.BlockSpec((1, D), lambda i: (0, 0)),
                             pl.BlockSpec((8, LANES), lambda i: (0, 0))],
                  out_shape=[jax.ShapeDtypeStruct((S, D), F32), jax.ShapeDtypeStruct((S, PW), BF),
                             jax.ShapeDtypeStruct((1, D), F32), jax.ShapeDtypeStruct((8, LANES), F32)])(
        dq, dkp, dvp, dqm, *tabs, raw, qkg, w_pad, xin, g, dres)


def mm_acc(a, b, *, tk, tn, ts):
    S, K = a.shape
    N = b.shape[1]
    ts = min(ts, S)

    def body(a_ref, b_ref, o_ref):
        z = lax.dot_general(a_ref[...], b_ref[...], (((0,), (0,)), ((), ())), preferred_element_type=F32)

        @pl.when(pl.program_id(2) == 0)
        def _():
            o_ref[...] = z

        @pl.when(pl.program_id(2) > 0)
        def _():
            o_ref[...] += z

    return _pcall(body, name="mm_acc", grid=(K // tk, N // tn, S // ts),
                  in_specs=[pl.BlockSpec((ts, tk), lambda k, n, s: (s, k)), pl.BlockSpec((ts, tn), lambda k, n, s: (s, n))],
                  out_specs=pl.BlockSpec((tk, tn), lambda k, n, s: (k, n)),
                  out_shape=jax.ShapeDtypeStruct((K, N), F32))(a, b)


def _band_specs(L, d, R, T, width, bw, col_of):
    n = T // R
    nb = width // bw
    last = L // R - 1
    col = lambda g, r: r * nb + col_of(g)
    return [pl.BlockSpec((R, bw), lambda g, r, i: (jnp.maximum(i * n - 1, 0), col(g, r))),
            pl.BlockSpec((T, bw), lambda g, r, i: (i, col(g, r))),
            pl.BlockSpec((R, bw), lambda g, r, i: (jnp.minimum((i + 1) * n, last), col(g, r)))]


def _band_tile(R, L):
    return min(max(2 * R, 256), L)


def _band_bias(T, R):
    w = lax.broadcasted_iota(jnp.int32, (T, T + 2 * R), 1)
    c = lax.broadcasted_iota(jnp.int32, (T, T + 2 * R), 0)
    return jnp.where(jnp.abs(w - R - c) <= R, 0.0, NEG).astype(F32)


def _edge_bias(i, T, R, L):
    wpos = i * T - R + lax.broadcasted_iota(jnp.int32, (1, T + 2 * R), 1)
    return jnp.where((wpos >= 0) & (wpos < L), 0.0, NEG)


def banded_fwd(proj, sink, *, d, R, TQ, pair0, npairs, use_sink, o_dtype, gather=()):
    S = proj.shape[0]
    L = S // d
    pv = proj.reshape(L, d * PW)
    ow = npairs * LANES

    def body(sink_ref, bias_ref, q_ref, kp, kc, kn, vp, vc, vn, o_ref, lse_ref):
        g, i = pl.program_id(0), pl.program_id(2)
        bias = bias_ref[...] + _edge_bias(i, TQ, R, L)
        lo = _lo((TQ, LANES))
        kw = jnp.concatenate([kp[...], kc[...], kn[...]], axis=0)
        vw = jnp.concatenate([vp[...], vc[...], vn[...]], axis=0)
        v_lo, v_hi = _split(vw, _lo(vw.shape))
        vcat = jnp.concatenate([v_lo, v_hi], axis=0)
        for t in range(2):
            qa, qb = _split(q_ref[:, _tile(t)], lo)
            ps, dens, lses = [], [], []
            for h, qh in enumerate((qa, qb)):
                s = _dot_nt(qh, kw) + bias
                m = jnp.max(s, axis=-1, keepdims=True)
                if use_sink:
                    sk = sink_ref[2 * (pair0 + 2 * g + t) + h]
                    m = jnp.maximum(m, sk)
                e = jnp.exp(s - m)
                den = jnp.sum(e, axis=-1, keepdims=True)
                if use_sink:
                    den = den + jnp.exp(sk - m)
                ps.append(e.astype(BF))
                dens.append(den)
                lses.append(m + jnp.log(den))
            o = _dot(jnp.concatenate(ps, axis=1), vcat)
            o_ref[:, _tile(t)] = (o / jnp.where(lo, dens[0], dens[1])).astype(o_dtype)
            lse_ref[:, _tile(t)] = jnp.where(lo, lses[0], lses[1])

    g0 = pair0 // 2
    qspec = pl.BlockSpec((TQ, 2 * LANES), lambda g, r, i: (i, r * 7 + g0 + g))
    kspecs = _band_specs(L, d, R, TQ, PW, LANES, lambda g: K_T0 + g0 + g)
    vspecs = _band_specs(L, d, R, TQ, PW, LANES, lambda g: V_T0 + g0 + g)
    ospec = pl.BlockSpec((TQ, 2 * LANES), lambda g, r, i: (i, r * (npairs // 2) + g))
    bias = _band_bias(TQ, R)
    kw = dict(grid=(npairs // 2, d, L // TQ),
              in_specs=[pl.BlockSpec(memory_space=pltpu.SMEM), pl.BlockSpec(bias.shape, lambda g, r, i: (0, 0)),
                        qspec] + kspecs + vspecs,
              out_specs=[ospec, ospec],
              out_shape=[jax.ShapeDtypeStruct((L, d * ow), o_dtype), jax.ShapeDtypeStruct((L, d * ow), F32)])
    args = (sink, bias, pv, pv, pv, pv, pv, pv, pv)
    if gather:
        (o, lse), gathered = _pcall_behind(body, _gather_comm, gather, _gather_out_shape(gather),
                                           _gather_sems(len(gather)), name="banded_fwd_gather", **kw)(*args)
        return o.reshape(S, ow), lse.reshape(S, ow), gathered
    o, lse = _pcall(body, name="banded_fwd", **kw)(*args)
    return o.reshape(S, ow), lse.reshape(S, ow)


def banded_bwd_dq(proj, do, lse, delta, sink, *, d, R, TQ, pair0, npairs, use_sink, scatter=()):
    S = proj.shape[0]
    L = S // d
    pv = proj.reshape(L, d * PW)
    ow = npairs * LANES

    def body(sink_ref, bias_ref, q_ref, kp, kc, kn, vp, vc, vn, do_ref, lse_ref, delta_ref, dq_ref, dsink_ref):
        g, r, i = pl.program_id(0), pl.program_id(1), pl.program_id(2)

        @pl.when((r == 0) & (i == 0))
        def _():
            dsink_ref[...] = jnp.zeros_like(dsink_ref)

        bias = bias_ref[...] + _edge_bias(i, TQ, R, L)
        lo = _lo((TQ, LANES))
        kw = jnp.concatenate([kp[...], kc[...], kn[...]], axis=0)
        vw = jnp.concatenate([vp[...], vc[...], vn[...]], axis=0)
        k_lo, k_hi = _split(kw, _lo(kw.shape))
        kcat = jnp.concatenate([k_lo, k_hi], axis=0)
        for t in range(2):
            qa, qb = _split(q_ref[:, _tile(t)], lo)
            doa, dob = _split(do_ref[:, _tile(t)], lo)
            lse_t, delta_t = lse_ref[:, _tile(t)], delta_ref[:, _tile(t)]
            dss, dsk = [], []
            for h, (qh, doh) in enumerate(((qa, doa), (qb, dob))):
                lse_h, delta_h = _col(lse_t, h * HD), _col(delta_t, h * HD)
                pr = jnp.exp(_dot_nt(qh, kw) + bias - lse_h)
                dss.append((pr * (_dot_nt(doh, vw) - delta_h)).astype(BF))
                if use_sink:
                    psink = jnp.exp(sink_ref[2 * (pair0 + 2 * g + t) + h] - lse_h)
                    dsk.append(-jnp.sum(psink * delta_h, axis=0, keepdims=True))
            dq_ref[:, _tile(t)] = _dot(jnp.concatenate(dss, axis=1), kcat).astype(BF)
            if use_sink:
                dsink_ref[:, _tile(t)] += jnp.where(_lo((8, LANES)), dsk[0], dsk[1])

    g0 = pair0 // 2
    qspec = pl.BlockSpec((TQ, 2 * LANES), lambda g, r, i: (i, r * 7 + g0 + g))
    kspecs = _band_specs(L, d, R, TQ, PW, LANES, lambda g: K_T0 + g0 + g)
    vspecs = _band_specs(L, d, R, TQ, PW, LANES, lambda g: V_T0 + g0 + g)
    ospec = pl.BlockSpec((TQ, 2 * LANES), lambda g, r, i: (i, r * (npairs // 2) + g))
    view = lambda a: a.reshape(L, d * a.shape[1])
    ispec = lambda a: pl.BlockSpec((TQ, 2 * LANES), lambda g, r, i: (i, r * (a.shape[1] // (2 * LANES)) + g))
    bias = _band_bias(TQ, R)
    kw = dict(grid=(npairs // 2, d, L // TQ),
              in_specs=[pl.BlockSpec(memory_space=pltpu.SMEM), pl.BlockSpec(bias.shape, lambda g, r, i: (0, 0)),
                        qspec] + kspecs + vspecs + [ispec(do), ispec(lse), ispec(delta)],
              out_specs=[ospec, pl.BlockSpec((8, 2 * LANES), lambda g, r, i: (g, 0))],
              out_shape=[jax.ShapeDtypeStruct((L, d * ow), BF),
                         jax.ShapeDtypeStruct((npairs // 2 * 8, 2 * LANES), F32)])
    args = (sink, bias, pv, pv, pv, pv, pv, pv, pv, view(do), view(lse), view(delta))
    if scatter:
        (dq, dsink), got = _pcall_behind(body, _scatter_comm, scatter, _scatter_out_shape(scatter),
                                         _scatter_sems(len(scatter)), name="banded_bwd_dq_scatter", **kw)(*args)
        return dq.reshape(S, ow), dsink, got
    dq, dsink = _pcall(body, name="banded_bwd_dq", **kw)(*args)
    return dq.reshape(S, ow), dsink


def banded_bwd_dkv(proj, do, lse, delta, *, d, R, TK, pair0, npairs):
    S = proj.shape[0]
    L = S // d
    pv = proj.reshape(L, d * PW)
    ow = npairs * LANES

    def body(bias_ref, k_ref, v_ref, qp, qc, qn, dop, doc, don, lp, lc, ln, dp_, dc_, dn_, dk_ref, dv_ref):
        j = pl.program_id(2)
        W = TK + 2 * R
        bias = bias_ref[...] + _edge_bias(j, TK, R, L)
        low = _lo((W, LANES))
        qw = jnp.concatenate([qp[...], qc[...], qn[...]], axis=0)
        dow = jnp.concatenate([dop[...], doc[...], don[...]], axis=0)
        lse_w = jnp.concatenate([lp[...], lc[...], ln[...]], axis=0)
        delta_w = jnp.concatenate([dp_[...], dc_[...], dn_[...]], axis=0)
        k, v = k_ref[...], v_ref[...]
        for t in range(2):
            qa, qb = _split(qw[:, _tile(t)], low)
            doa, dob = _split(dow[:, _tile(t)], low)
            lse_r, delta_r = lse_w[:, _tile(t)].T, delta_w[:, _tile(t)].T
            prs, dss = [], []
            for h, (qh, doh) in enumerate(((qa, doa), (qb, dob))):
                pr = jnp.exp(_dot_nt(k, qh) + bias - lse_r[h * HD:h * HD + 1, :])
                dss.append((pr * (_dot_nt(v, doh) - delta_r[h * HD:h * HD + 1, :])).astype(BF))
                prs.append(pr.astype(BF))
            dv_ref[:, _tile(t)] = _dot(jnp.concatenate(prs, axis=1), jnp.concatenate([doa, dob], axis=0)).astype(BF)
            dk_ref[:, _tile(t)] = _dot(jnp.concatenate(dss, axis=1), jnp.concatenate([qa, qb], axis=0)).astype(BF)

    g0 = pair0 // 2
    kspec = pl.BlockSpec((TK, LANES), lambda g, r, j: (j, r * 14 + K_T0 + g0 + g))
    vspec = pl.BlockSpec((TK, LANES), lambda g, r, j: (j, r * 14 + V_T0 + g0 + g))
    qspecs = _band_specs(L, d, R, TK, PW, 2 * LANES, lambda g: g0 + g)
    ispecs = lambda a: _band_specs(L, d, R, TK, a.shape[1], 2 * LANES, lambda g: g)
    view = lambda a: a.reshape(L, d * a.shape[1])
    ospec = pl.BlockSpec((TK, 2 * LANES), lambda g, r, j: (j, r * (npairs // 2) + g))
    sd = jax.ShapeDtypeStruct((L, d * ow), BF)
    bias = _band_bias(TK, R)
    dk, dv = _pcall(body, name="banded_bwd_dkv", grid=(npairs // 2, d, L // TK),
                    in_specs=[pl.BlockSpec(bias.shape, lambda g, r, j: (0, 0)), kspec, vspec] + qspecs + ispecs(do)
                    + ispecs(lse) + ispecs(delta),
                    out_specs=[ospec, ospec], out_shape=[sd, sd])(
        bias, pv, pv, pv, pv, pv, *([view(do)] * 3), *([view(lse)] * 3), *([view(delta)] * 3))
    return dk.reshape(S, ow), dv.reshape(S, ow)


def flash_fwd(proj, bound, *, tq=1024, tk=1024):
    S = proj.shape[0]

    def body_general(q_ref, k_ref, v_ref, o_ref, lse_ref):
        lo = _lo((tq, LANES))
        qa, qb = _split(q_ref[...], lo)
        lov = _lo((tk, LANES))

        def step(j, carry):
            ma, la, mb, lb, acc = carry
            rows = pl.ds(pl.multiple_of(j * tk, tk), tk)
            k, v = k_ref[rows, :], v_ref[rows, :]
            outs = []
            for qh, m0, l0 in ((qa, ma, la), (qb, mb, lb)):
                s = _dot_nt(qh, k)
                m1 = jnp.maximum(m0, jnp.max(s, axis=-1, keepdims=True))
                al = jnp.exp2(m0 - m1)
                e = jnp.exp2(s - m1)
                outs.append((m1, al * l0 + jnp.sum(e, axis=-1, keepdims=True), al, e.astype(BF)))
            v_lo, v_hi = _split(v, lov)
            pvv = _dot(jnp.concatenate([outs[0][3], outs[1][3]], axis=1), jnp.concatenate([v_lo, v_hi], axis=0))
            acc = acc * jnp.where(lo, outs[0][2], outs[1][2]) + pvv
            return outs[0][0], outs[0][1], outs[1][0], outs[1][1], acc

        m_init = jnp.full((tq, 1), NEG, F32)
        l_init = jnp.zeros((tq, 1), F32)
        ma, la, mb, lb, acc = lax.fori_loop(0, S // tk, step,
                                            (m_init, l_init, m_init, l_init, jnp.zeros((tq, LANES), F32)))
        o_ref[...] = (acc / jnp.where(lo, la, lb)).astype(BF)
        lse_ref[...] = jnp.where(lo, ma * LN2 + jnp.log(la), mb * LN2 + jnp.log(lb))

    def body_plain(q_ref, k_ref, v_ref, o_ref, lse_ref):
        lo = _lo((tq, LANES))
        qa, qb = _split(q_ref[...], lo)
        lov = _lo((tk, LANES))
        one = jnp.ones((tk, LANES), BF)

        def step(j, carry):
            acc_a, acc_b = carry
            rows = pl.ds(pl.multiple_of(j * tk, tk), tk)
            k, v = k_ref[rows, :], v_ref[rows, :]
            ea = jnp.exp2(_dot_nt(qa, k)).astype(BF)
            eb = jnp.exp2(_dot_nt(qb, k)).astype(BF)
            acc_a = acc_a + _dot(ea, jnp.where(lov, v, one))
            acc_b = acc_b + _dot(eb, jnp.where(lov, one, v))
            return acc_a, acc_b

        z = jnp.zeros((tq, LANES), F32)
        acc_a, acc_b = lax.fori_loop(0, S // tk, step, (z, z))
        den = jnp.where(lo, pltpu.roll(acc_a, HD, 1), pltpu.roll(acc_b, HD, 1))
        o_ref[...] = (jnp.where(lo, acc_a, acc_b) / den).astype(BF)
        lse_ref[...] = jnp.log(den)

    def body(bound_ref, q_ref, k_ref, v_ref, o_ref, lse_ref):
        small = bound_ref[0] <= MAX_PLAIN_SCORE

        @pl.when(small)
        def _():
            body_plain(q_ref, k_ref, v_ref, o_ref, lse_ref)

        @pl.when(jnp.logical_not(small))
        def _():
            body_general(q_ref, k_ref, v_ref, o_ref, lse_ref)

    ospec = pl.BlockSpec((tq, LANES), lambda p, i: (i, p))
    return _pcall(body, name="flash_fwd", grid=(N_PAIRS, S // tq),
                  in_specs=[pl.BlockSpec(memory_space=pltpu.SMEM), ospec,
                            pl.BlockSpec((S, LANES), lambda p, i: (0, K_T0 + p // 2)),
                            pl.BlockSpec((S, LANES), lambda p, i: (0, V_T0 + p // 2))],
                  out_specs=[ospec, ospec],
                  out_shape=[jax.ShapeDtypeStruct((S, 768), BF), jax.ShapeDtypeStruct((S, 768), F32)])(
        bound.reshape(1), proj, proj, proj)


def flash_bwd(proj, do, lse, delta, *, tq=1024, tk=1024):
    S = proj.shape[0]
    nq = S // tq
    lse_t, delta_t = _stats_t(lse, tq), _stats_t(delta, tq)

    def body(k_ref, v_ref, q_ref, do_ref, lse_ref, delta_ref, dk_ref, dv_ref, dqt_ref):
        @pl.when(pl.program_id(1) == 0)
        def _():
            dqt_ref[...] = jnp.zeros_like(dqt_ref)

        k, v = k_ref[...], v_ref[...]
        lo = _lo((tq, LANES))
        k_lo, k_hi = _split(k.astype(F32), _lo((tk, LANES)))
        kt = jnp.concatenate([k_lo.T, k_hi.T], axis=1).astype(BF)

        def step(i, carry):
            dk, dv = carry
            rows = pl.ds(pl.multiple_of(i * tq, tq), tq)
            qa, qb = _split(q_ref[rows, :], lo)
            doa, dob = _split(do_ref[rows, :], lo)
            lse_i, delta_i = lse_ref[i] * LOG2E, delta_ref[i]
            prs, dss = [], []
            for h, (qh, doh) in enumerate(((qa, doa), (qb, dob))):
                pr = jnp.exp2(_dot_nt(k, qh) - lse_i[h:h + 1, :])
                dss.append((pr * (_dot_nt(v, doh) - delta_i[h:h + 1, :])).astype(BF))
                prs.append(pr.astype(BF))
            dv = dv + _dot(jnp.concatenate(prs, axis=1), jnp.concatenate([doa, dob], axis=0))
            dk = dk + _dot(jnp.concatenate(dss, axis=1), jnp.concatenate([qa, qb], axis=0))
            dqt_ref[i] += _dot(kt, jnp.concatenate(dss, axis=0))
            return dk, dv

        z = jnp.zeros((tk, LANES), F32)
        dk, dv = lax.fori_loop(0, nq, step, (z, z))
        dk_ref[...] = dk.astype(BF)
        dv_ref[...] = dv.astype(BF)

    ospec = pl.BlockSpec((tk, LANES), lambda p, j: (j, p))
    stat = pl.BlockSpec((None, nq, 8, tq), lambda p, j: (p, 0, 0, 0))
    whole = lambda: pl.BlockSpec((S, LANES), lambda p, j: (0, p), pipeline_mode=pl.Buffered(1))
    sd = jax.ShapeDtypeStruct((S, 768), BF)
    dk, dv, dqt = _pcall(body, name="flash_bwd", grid=(N_PAIRS, S // tk),
                         in_specs=[pl.BlockSpec((tk, LANES), lambda p, j: (j, K_T0 + p // 2)),
                                   pl.BlockSpec((tk, LANES), lambda p, j: (j, V_T0 + p // 2)), whole(), whole(),
                                   stat, stat],
                         out_specs=[ospec, ospec,
                                    pl.BlockSpec((None, nq, LANES, tq), lambda p, j: (p, 0, 0, 0),
                                                 pipeline_mode=pl.Buffered(1))],
                         out_shape=[sd, sd, jax.ShapeDtypeStruct((N_PAIRS, nq, LANES, tq), F32)])(
        proj, proj, proj, do, lse_t, delta_t)
    dq = jnp.transpose(dqt, (1, 3, 0, 2)).reshape(S, 768).astype(BF)
    return dq, dk, dv


def mem_fwd(proj, mkv, *, tq=512):
    S = proj.shape[0]

    def body(q_ref, km_ref, vm_ref, o_ref, lse_ref):
        lo = _lo((tq, LANES))
        lov = _lo((N_MEM, LANES))
        for t in range(2):
            qa, qb = _split(q_ref[:, _tile(t)], lo)
            km, vm = km_ref[:, _tile(t)], vm_ref[:, _tile(t)]
            ps, dens, lses = [], [], []
            for qh in (qa, qb):
                s = _dot_nt(qh, km)
                m = jnp.max(s, axis=-1, keepdims=True)
                e = jnp.exp(s - m)
                den = jnp.sum(e, axis=-1, keepdims=True)
                ps.append(e.astype(BF))
                dens.append(den)
                lses.append(m + jnp.log(den))
            v_lo, v_hi = _split(vm, lov)
            o = _dot(jnp.concatenate(ps, axis=1), jnp.concatenate([v_lo, v_hi], axis=0))
            o_ref[:, _tile(t)] = (o / jnp.where(lo, dens[0], dens[1])).astype(BF)
            lse_ref[:, _tile(t)] = jnp.where(lo, lses[0], lses[1])

    ospec = pl.BlockSpec((tq, 256), lambda i: (i, 0))
    return _pcall(body, name="mem_fwd", grid=(S // tq,),
                  in_specs=[pl.BlockSpec((tq, 256), lambda i: (i, M_T0 // 2)),
                            pl.BlockSpec((N_MEM, 256), lambda i: (0, 0)), pl.BlockSpec((N_MEM, 256), lambda i: (0, 1))],
                  out_specs=[ospec, ospec],
                  out_shape=[jax.ShapeDtypeStruct((S, 256), BF), jax.ShapeDtypeStruct((S, 256), F32)])(proj, mkv, mkv)


def mem_bwd(proj, mkv, dcat, lse, delta, *, tq=512):
    S = proj.shape[0]

    def body(q_ref, km_ref, vm_ref, do_ref, lse_ref, delta_ref, dq_ref, dkm_ref, dvm_ref):
        @pl.when(pl.program_id(0) == 0)
        def _():
            dkm_ref[...] = jnp.zeros_like(dkm_ref)
            dvm_ref[...] = jnp.zeros_like(dvm_ref)

        lo = _lo((tq, LANES))
        lov = _lo((N_MEM, LANES))
        for t in range(2):
            qa, qb = _split(q_ref[:, _tile(t)], lo)
            doa, dob = _split(do_ref[:, _tile(t)], lo)
            km, vm = km_ref[:, _tile(t)], vm_ref[:, _tile(t)]
            lse_t, delta_t = lse_ref[:, _tile(t)], delta_ref[:, _tile(t)]
            prs, dss = [], []
            for h, (qh, doh) in enumerate(((qa, doa), (qb, dob))):
                pr = jnp.exp(_dot_nt(qh, km) - _col(lse_t, h * HD))
                dss.append(pr * (_dot_nt(doh, vm) - _col(delta_t, h * HD)))
                prs.append(pr)
            k_lo, k_hi = _split(km, lov)
            dq_ref[:, _tile(t)] = _dot(jnp.concatenate(dss, axis=1).astype(BF), jnp.concatenate([k_lo, k_hi], axis=0))
            dvm_ref[:, _tile(t)] += _dot(jnp.concatenate(prs, axis=0).T.astype(BF), jnp.concatenate([doa, dob], axis=0))
            dkm_ref[:, _tile(t)] += _dot(jnp.concatenate(dss, axis=0).T.astype(BF), jnp.concatenate([qa, qb], axis=0))

    ospec = pl.BlockSpec((tq, 256), lambda i: (i, 0))
    msp = pl.BlockSpec((N_MEM, 256), lambda i: (0, 0))
    md = jax.ShapeDtypeStruct((N_MEM, 256), F32)
    dq, dkm, dvm = _pcall(body, name="mem_bwd", grid=(S // tq,),
                          in_specs=[pl.BlockSpec((tq, 256), lambda i: (i, M_T0 // 2)), msp,
                                    pl.BlockSpec((N_MEM, 256), lambda i: (0, 1)),
                                    pl.BlockSpec((tq, 256), lambda i: (i, 3)), ospec,
                                    pl.BlockSpec((tq, 256), lambda i: (i, 3))],
                          out_specs=[ospec, msp, msp], out_shape=[jax.ShapeDtypeStruct((S, 256), F32), md, md])(
        proj, mkv, mkv, dcat, lse, delta)
    return dq, jnp.concatenate([dkm, dvm], axis=1)


def combine_fwd(os_, lses, *, tm=512):
    S = os_[0].shape[0]

    def body(o0, o1, o2, l0, l1, l2, tok_ref):
        ls = [l0[...], l1[...], l2[...]]
        m = jnp.maximum(jnp.maximum(ls[0], ls[1]), ls[2])
        es = [jnp.exp(l - m) for l in ls]
        den = es[0] + es[1] + es[2]
        for g, o in enumerate((o0, o1, o2)):
            tok_ref[:, 256 * g:256 * (g + 1)] = (o[...] * (es[g] / den)).astype(BF)

    sp = pl.BlockSpec((tm, 256), lambda i: (i, 0))
    return _pcall(body, name="combine_fwd", grid=(S // tm,), in_specs=[sp] * 6,
                  out_specs=pl.BlockSpec((tm, 768), lambda i: (i, 0)),
                  out_shape=jax.ShapeDtypeStruct((S, 768), BF))(*os_, *lses)


def combine_bwd(dcat, os_, lses, *, tm=512):
    S = dcat.shape[0]

    def body(dt_ref, o0, o1, o2, l0, l1, l2, do0, do1, do2, de0, de1, de2):
        ls = [l0[...], l1[...], l2[...]]
        m = jnp.maximum(jnp.maximum(ls[0], ls[1]), ls[2])
        es = [jnp.exp(l - m) for l in ls]
        den = es[0] + es[1] + es[2]
        alphas = [e / den for e in es]
        lo = _lo((tm, LANES))
        dts = [dt_ref[:, 256 * g:256 * (g + 1)].astype(F32) for g in range(3)]
        dal = []
        for g, o in enumerate((o0, o1, o2)):
            pr = dts[g] * o[...]
            dal.append(jnp.concatenate([_half_sum(pr[:, _tile(0)]), _half_sum(pr[:, _tile(1)])], axis=1))
        mix = alphas[0] * dal[0] + alphas[1] * dal[1] + alphas[2] * dal[2]
        for g, (do_ref, de_ref) in enumerate(((do0, de0), (do1, de1), (do2, de2))):
            do_ref[...] = (dts[g] * alphas[g]).astype(BF)
            de_ref[...] = alphas[g] * mix

    sp = pl.BlockSpec((tm, 256), lambda i: (i, 0))
    outs = _pcall(body, name="combine_bwd", grid=(S // tm,),
                  in_specs=[pl.BlockSpec((tm, 768), lambda i: (i, 0))] + [sp] * 6, out_specs=[sp] * 6,
                  out_shape=[jax.ShapeDtypeStruct((S, 256), BF)] * 3 + [jax.ShapeDtypeStruct((S, 256), F32)] * 3)(
        dcat, *os_, *lses)
    return outs[:3], outs[3:]


def _coords():
    return lax.axis_index("x"), lax.axis_index("y"), lax.axis_index("c")


def _other_chips(x, y):
    return [(1 - x, y), (x, 1 - y), (1 - x, 1 - y)]


HBM_SPEC = pl.BlockSpec(memory_space=pltpu.HBM)


def _gather_comm(ins, outs, send, recv, lsem, start):
    x, y, c = _coords()
    me = 2 * x + y
    for a in range(len(ins)):
        local = pltpu.make_async_copy(ins[a], outs[a].at[me], lsem.at[a])
        if start:
            local.start()
        for j, (px, py) in enumerate(_other_chips(x, y)):
            sems = dict(send_sem=send.at[3 * a + j], recv_sem=recv.at[3 * a + j], device_id=(px, py, c),
                        device_id_type=MESH)
            cp = pltpu.make_async_remote_copy(src_ref=ins[a], dst_ref=outs[a].at[me], **sems)
            if start:
                cp.start()
            else:
                pltpu.make_async_remote_copy(src_ref=ins[a], dst_ref=outs[a].at[2 * px + py], **sems).wait_recv()
                cp.wait_send()
        if not start:
            local.wait()


def _gather_out_shape(shards):
    return [jax.ShapeDtypeStruct((4,) + s.shape, s.dtype) for s in shards]


def _gather_sems(n):
    return [pltpu.SemaphoreType.DMA((3 * n,)), pltpu.SemaphoreType.DMA((3 * n,)), pltpu.SemaphoreType.DMA((n,))]


def gather_shards(shards):
    n = len(shards)

    def body(*refs):
        _gather_comm(refs[:n], refs[n:2 * n], *refs[2 * n:], start=True)
        _gather_comm(refs[:n], refs[n:2 * n], *refs[2 * n:], start=False)

    return pl.pallas_call(body, name="gather_shards", in_specs=[HBM_SPEC] * n, out_specs=[HBM_SPEC] * n,
                          out_shape=_gather_out_shape(shards), scratch_shapes=_gather_sems(n))(*shards)


def _scatter_comm(ins, outs, send, recv, start):
    x, y, c = _coords()
    for a in range(len(ins)):
        for j, (px, py) in enumerate(_other_chips(x, y)):
            cp = pltpu.make_async_remote_copy(src_ref=ins[a].at[2 * px + py], dst_ref=outs[a].at[j],
                                              send_sem=send.at[3 * a + j], recv_sem=recv.at[3 * a + j],
                                              device_id=(px, py, c), device_id_type=MESH)
            if start:
                cp.start()
            else:
                cp.wait_recv()
                cp.wait_send()


def _scatter_out_shape(parts):
    return [jax.ShapeDtypeStruct((3,) + p.shape[1:], p.dtype) for p in parts]


def _scatter_sems(n):
    return [pltpu.SemaphoreType.DMA((3 * n,)), pltpu.SemaphoreType.DMA((3 * n,))]


def scatter_grads(parts):
    n = len(parts)

    def body(*refs):
        _scatter_comm(refs[:n], refs[n:2 * n], *refs[2 * n:], start=True)
        _scatter_comm(refs[:n], refs[n:2 * n], *refs[2 * n:], start=False)

    return pl.pallas_call(body, name="scatter_grads", in_specs=[HBM_SPEC] * n, out_specs=[HBM_SPEC] * n,
                          out_shape=_scatter_out_shape(parts), scratch_shapes=_scatter_sems(n))(*parts)


def sibling_swap(arrs):
    n = len(arrs)

    def body(*refs):
        ins, outs = refs[:n], refs[n:2 * n]
        send, recv = refs[2 * n:]
        x, y, c = _coords()
        cps = []
        for a in range(n):
            cp = pltpu.make_async_remote_copy(src_ref=ins[a], dst_ref=outs[a], send_sem=send.at[a], recv_sem=recv.at[a],
                                              device_id=(x, y, 1 - c), device_id_type=MESH)
            cp.start()
            cps.append(cp)
        for cp in cps:
            cp.wait_recv()
        for cp in cps:
            cp.wait_send()

    return pl.pallas_call(
        body, name="sibling_swap", in_specs=[HBM_SPEC] * n, out_specs=[HBM_SPEC] * n,
        out_shape=[jax.ShapeDtypeStruct(a.shape, a.dtype) for a in arrs],
        scratch_shapes=[pltpu.SemaphoreType.DMA((n,)), pltpu.SemaphoreType.DMA((n,))])(*arrs)


def allsum_small(v):
    rows = v.shape[0]

    def body(v_ref, tot_ref, gath_ref, send, recv):
        x, y, c = _coords()
        me = 4 * x + 2 * y + c
        gath_ref[me] = v_ref[...]
        cps = []
        for k in range(1, 8):
            fx, fy, fc = (k >> 2) & 1, (k >> 1) & 1, k & 1
            peer = (1 - x if fx else x, 1 - y if fy else y, 1 - c if fc else c)
            cp = pltpu.make_async_remote_copy(src_ref=v_ref, dst_ref=gath_ref.at[me], send_sem=send.at[k - 1],
                                              recv_sem=recv.at[k - 1], device_id=peer, device_id_type=MESH)
            cp.start()
            cps.append(cp)
        for cp in cps:
            cp.wait_recv()
        for cp in cps:
            cp.wait_send()
        tot = gath_ref[0]
        for k in range(1, 8):
            tot = tot + gath_ref[k]
        tot_ref[...] = tot

    vm = pl.BlockSpec(memory_space=pltpu.VMEM)
    tot, _ = pl.pallas_call(
        body, name="allsum_small", in_specs=[vm], out_specs=[vm, vm],
        out_shape=[jax.ShapeDtypeStruct((rows, LANES), F32), jax.ShapeDtypeStruct((8, rows, LANES), F32)],
        scratch_shapes=[pltpu.SemaphoreType.DMA((7,)), pltpu.SemaphoreType.DMA((7,))])(v)
    return tot


def sum_parts(own, recv, *, tr=256):
    R, C = own.shape
    tr = min(tr, R)

    def body(o_ref, r_ref, out_ref):
        out_ref[...] = ((o_ref[...] + r_ref[0].astype(F32)) + r_ref[1].astype(F32)) + r_ref[2].astype(F32)

    sp = pl.BlockSpec((tr, C), lambda i: (i, 0))
    return _pcall(body, name="sum_parts", grid=(R // tr,),
                  in_specs=[sp, pl.BlockSpec((3, tr, C), lambda i: (0, i, 0))], out_specs=sp,
                  out_shape=jax.ShapeDtypeStruct((R, C), F32))(own, recv)


def adamw(w, ga, gb, m, v, *, tr=256):
    R, C = w.shape
    tr = min(tr, R)
    two = gb is not None

    def body(*refs):
        if two:
            w_ref, ga_ref, gb_ref, m_ref, v_ref, g_out, d_out, m_out, v_out = refs
            g = ga_ref[...] + gb_ref[...]
        else:
            w_ref, ga_ref, m_ref, v_ref, g_out, d_out, m_out, v_out = refs
            g = ga_ref[...]
        mn = B1 * m_ref[...] + (1.0 - B1) * g
        vn = B2 * v_ref[...] + (1.0 - B2) * (g * g)
        m_hat = mn / (1.0 - B1 ** STEP)
        v_hat = vn / (1.0 - B2 ** STEP)
        g_out[...] = g
        d_out[...] = -LR * (m_hat / (jnp.sqrt(v_hat) + AEPS) + WD * w_ref[...])
        m_out[...] = mn
        v_out[...] = vn

    sp = pl.BlockSpec((tr, C), lambda i: (i, 0))
    args = [w, ga, gb, m, v] if two else [w, ga, m, v]
    sd = jax.ShapeDtypeStruct((R, C), F32)
    return _pcall(body, name="adamw", grid=(R // tr,), in_specs=[sp] * len(args), out_specs=[sp] * 4,
                  out_shape=[sd] * 4)(*args)


def _rope_tables(S):
    def inv_freq(n_dims, theta):
        return theta ** (-(jnp.arange(0, n_dims, 2, dtype=jnp.float32) / n_dims))

    pos = lax.broadcasted_iota(jnp.int32, (S, LANES), 0)
    d = lax.broadcasted_iota(jnp.int32, (S, LANES), 1) % HD
    d1 = lax.iota(jnp.int32, LANES) % HD
    ang = pos.astype(F32) * inv_freq(HD // 4, ROPE_THETA)[d1 % 8][None, :]
    sin = jnp.sin(ang)
    partial = (jnp.where(d < 16, jnp.cos(ang), 1.0), jnp.where((d >= 8) & (d < 16), sin, 0.0),
               jnp.where(d < 8, -sin, 0.0))
    grid_pos = jnp.where(d < 32, pos // GRID_W, pos % GRID_W)
    ang = grid_pos.astype(F32) * inv_freq(HD // 2, AXIAL_THETA)[d1 % 16][None, :]
    sin = jnp.sin(ang)
    axial = (jnp.cos(ang), jnp.where(d % 32 >= 16, sin, 0.0), jnp.where(d % 32 < 16, -sin, 0.0))
    return partial, axial


def _pad_w_in(w):
    cols = [w[:, :768]]
    for base in (768, 960):
        for g in range(3):
            kg = w[:, base + g * HD:base + (g + 1) * HD]
            cols += [kg, kg]
    cols.append(w[:, 1152:])
    return jnp.concatenate(cols, axis=1)


def _unpad_dw_in(dw):
    cols = [dw[:, :768]]
    for t0 in (K_T0, V_T0):
        for g in range(3):
            b = (t0 + g) * LANES
            cols.append(dw[:, b:b + HD] + dw[:, b + HD:b + LANES])
    cols.append(dw[:, M_T0 * LANES:])
    return jnp.concatenate(cols, axis=1)


def _stats_t(a, tq):
    S = a.shape[0]
    t = a.reshape(S, -1, 2, HD)[:, :N_PAIRS, :, 0]
    t = jnp.transpose(t, (1, 2, 0))
    t = jnp.pad(t, ((0, 0), (0, 6), (0, 0)))
    return jnp.transpose(t.reshape(N_PAIRS, 8, S // tq, tq), (0, 2, 1, 3))


def _grad_slices(dW_in, dW_mkv, dW_o, dW_gu, dW_d):
    return [None if dW_in is None else jnp.transpose(dW_in.reshape(D, 4, IN_W // 4), (1, 0, 2)),
            dW_mkv.reshape(4, D // 4, 512), dW_o.reshape(4, D // 4, D),
            jnp.transpose(dW_gu.reshape(D, 4, 2 * DFF // 4), (1, 0, 2)), dW_d.reshape(4, DFF // 4, D)]


def _fold(t):
    return t[..., :HD] + t[..., HD:]


def kernel(x, mem, mem_norm_g, w_in, w_mem_kv, w_o, g_mix_pre, g_mix_post, attn_sink, qk_norm_g, w_gate_up, w_down, g_ffn_pre, g_ffn_post, loss_target, m_mem_norm_g, m_w_in, m_w_mem_kv, m_w_o, m_g_mix_pre, m_g_mix_post, m_attn_sink, m_qk_norm_g, m_w_gate_up, m_w_down, m_g_ffn_pre, m_g_ffn_post, v_mem_norm_g, v_w_in, v_w_mem_kv, v_w_o, v_g_mix_pre, v_g_mix_post, v_attn_sink, v_qk_norm_g, v_w_gate_up, v_w_down, v_g_ffn_pre, v_g_ffn_post):
    S = x.shape[1]
    depth = w_in.shape[0]
    xs, memx, tgt = x[0], mem[0], loss_target[0]
    tab_p, tab_a = _rope_tables(S)
    row = lambda a: a.reshape(1, -1)

    shards_bf = [w.astype(BF) for w in (w_in, w_mem_kv, w_o, w_gate_up, w_down)]
    layer_shards = lambda i: [s[i] for s in shards_bf]
    W_in, W_mkv, W_o, W_g, W_u, W_gu, W_d = ([None] * depth for _ in range(7))

    def set_weights(i, gathered):
        if len(gathered) == 5:
            W_in[i] = jnp.concatenate([gathered[0][s] for s in range(4)], axis=1)
        gm, go, gg, gd = gathered[-4:]
        W_mkv[i] = jnp.concatenate([gm[s] for s in range(4)], axis=0)
        W_o[i] = jnp.concatenate([go[s] for s in range(4)], axis=0)
        W_g[i] = jnp.concatenate([gg[0], gg[1]], axis=1)
        W_u[i] = jnp.concatenate([gg[2], gg[3]], axis=1)
        W_gu[i] = jnp.concatenate([gg[0], gg[2], gg[1], gg[3]], axis=1)
        W_d[i] = jnp.concatenate([gd[s] for s in range(4)], axis=0)

    (g_in0,) = gather_shards(layer_shards(0)[:1])
    W_in[0] = jnp.concatenate([g_in0[s] for s in range(4)], axis=1)
    mem_g = row(mem_norm_g)
    zero_sink = jnp.zeros((12,), F32)
    qkg = jnp.pad(jnp.concatenate([qk_norm_g[0], qk_norm_g[0]], axis=1), ((0, 6), (0, 0)))
    no_qkg = jnp.zeros((8, LANES), F32)

    saved = []
    cur = xs
    for i in range(depth):
        kind = i % 3
        wp = _pad_w_in(W_in[i])
        sv = dict(x=cur, wp=wp)
        if kind == 1:
            h1, proj, raw, nrm = inproj_fwd(cur, row(g_mix_pre[i]), wp, tab_a, qkg, axial=True)
            sv["raw"] = raw
        else:
            h1, proj = inproj_fwd(cur, row(g_mix_pre[i]), wp, tab_p, no_qkg, axial=False)
        if kind == 0:
            res = banded_fwd(proj, attn_sink[i // 3], d=1, R=A_RADIUS, TQ=_band_tile(A_RADIUS, S), pair0=0, npairs=6,
                             use_sink=True, o_dtype=BF, gather=layer_shards(0)[1:] if i == 0 else ())
            tok, lse = res[:2]
            if i == 0:
                set_weights(0, res[2])
        elif kind == 1:
            bound = jnp.sqrt(jnp.max(nrm[0]) * jnp.max(nrm[1])) * LN2
            tok, lse = flash_fwd(proj, bound)
        else:
            os_, lses = [], []
            for g, (window, dil) in enumerate(C_GROUPS):
                rad = window // (2 * dil)
                o_g, l_g = banded_fwd(proj, zero_sink, d=dil, R=rad, TQ=_band_tile(rad, S // dil), pair0=2 * g, npairs=2,
                                      use_sink=False, o_dtype=F32)
                os_.append(o_g)
                lses.append(l_g)
            tok = combine_fwd(os_, lses)
            sv["os"], lse = os_, lses
        mem_n, mkv = norm_mm(memx, mem_g, W_mkv[i], tm=N_MEM)
        mo, mlse = mem_fwd(proj, mkv)
        o, x2 = mm_norm_res([tok, mo], W_o[i], row(g_mix_post[i]), cur)
        (h2, gate, up, act), gathered = ffn_up_fwd(x2, row(g_ffn_pre[i]), W_gu[i],
                                                   layer_shards(i + 1) if i + 1 < depth else ())
        if i + 1 < depth:
            set_weights(i + 1, gathered)
        f, x3 = mm_norm_res([act], W_d[i], row(g_ffn_post[i]), x2)
        sv.update(h1=h1, proj=proj, lse=lse, mem_n=mem_n, mkv=mkv, mlse=mlse, tok=tok, mo=mo, o=o, x2=x2, h2=h2, gate=gate,
                  up=up, act=act, f=f)
        saved.append(sv)
        cur = x3

    dcur, loss_vec = loss_bwd(cur, tgt)

    grad_parts, grad_recv = [None] * depth, [None] * depth
    dg_pre, dg_post, dg_fpre, dg_fpost = [None] * depth, [None] * depth, [None] * depth, [None] * depth
    dg_mem = jnp.zeros((1, D), F32)
    dsinks, dqk = {}, None
    for i in reversed(range(depth)):
        sv = saved[i]
        kind = i % 3
        proj = sv["proj"]
        pending = [p.astype(BF) for p in grad_parts[i + 1]] if i + 1 < depth else ()
        (df, dgate, dup, dg_fpost[i]), got = normbwd_mm_swiglu(dcur, sv["f"], row(g_ffn_post[i]), W_d[i], sv["gate"],
                                                               sv["up"], pending)
        if i + 1 < depth:
            grad_recv[i + 1] = got
        dx2, dg_fpre[i] = mm_nt_normbwd_res([(dgate, W_g[i]), (dup, W_u[i])], sv["x2"], row(g_ffn_pre[i]), dcur)
        dW_d = mm_acc(sv["act"], df, tk=1408, tn=D, ts=2048)
        dW_gu = jnp.concatenate([mm_acc(sv["h2"], dgate, tk=D, tn=1408, ts=2048),
                                 mm_acc(sv["h2"], dup, tk=D, tn=1408, ts=2048)], axis=1)
        do, dcat, delta, dg_post[i] = normbwd_mm_cat(dx2, sv["o"], row(g_mix_post[i]), W_o[i], sv["tok"], sv["mo"])
        dW_o = jnp.concatenate([mm_acc(sv["tok"], do, tk=768, tn=D, ts=2048), mm_acc(sv["mo"], do, tk=256, tn=D, ts=2048)],
                               axis=0)
        dqm, dmkv = mem_bwd(proj, sv["mkv"], dcat, sv["mlse"], delta)
        dmkv = dmkv.astype(BF)
        (dgm,) = mm_nt_normbwd_res([(dmkv, W_mkv[i])], memx, mem_g, None, tm=N_MEM)
        dg_mem = dg_mem + dgm
        dW_mkv = mm_acc(sv["mem_n"], dmkv, tk=D, tn=512, ts=N_MEM)
        if kind == 0:
            sink = attn_sink[i // 3]
            args = dict(d=1, R=A_RADIUS, pair0=0, npairs=6)
            tile = _band_tile(A_RADIUS, S)
            early = [p.astype(BF) for p in _grad_slices(None, dW_mkv, dW_o, dW_gu, dW_d)[1:]] if i == 0 else ()
            res = banded_bwd_dq(proj, dcat, sv["lse"], delta, sink, TQ=tile, use_sink=True, scatter=early, **args)
            dq, dsk = res[:2]
            if i == 0:
                early_recv = res[2]
            dkp, dvp = banded_bwd_dkv(proj, dcat, sv["lse"], delta, TK=tile, **args)
            dsinks[i // 3] = dsk.reshape(3, 8, 2, 2, HD)[:, 0, :, :, 0].reshape(12)
        elif kind == 1:
            dq, dkp, dvp = flash_bwd(proj, dcat, sv["lse"], delta)
        else:
            dos, des = combine_bwd(dcat, sv["os"], sv["lse"])
            dqs, dks, dvs = [], [], []
            for g, (window, dil) in enumerate(C_GROUPS):
                rad = window // (2 * dil)
                args = dict(d=dil, R=rad, pair0=2 * g, npairs=2)
                tile = _band_tile(rad, S // dil)
                dq_g, _ = banded_bwd_dq(proj, dos[g], sv["lse"][g], des[g], zero_sink, TQ=tile, use_sink=False, **args)
                dk_g, dv_g = banded_bwd_dkv(proj, dos[g], sv["lse"][g], des[g], TK=tile, **args)
                dqs.append(dq_g)
                dks.append(dk_g)
                dvs.append(dv_g)
            dq, dkp, dvp = (jnp.concatenate(t, axis=1) for t in (dqs, dks, dvs))
        if kind == 1:
            dcur, dproj, dg_pre[i], dqk_t = inproj_bwd(dq, dkp, dvp, dqm, tab_a, sv["raw"], qkg, sv["wp"], sv["x"],
                                                       row(g_mix_pre[i]), dx2, axial=True)
            dqk = _fold(dqk_t[:2]).reshape(1, 2, HD)
        else:
            dcur, dproj, dg_pre[i], _ = inproj_bwd(dq, dkp, dvp, dqm, tab_p, proj, no_qkg, sv["wp"],
                                                   sv["x"], row(g_mix_pre[i]), dx2, axial=False)
        dW_in = _unpad_dw_in(mm_acc(sv["h1"], dproj, tk=D, tn=896, ts=2048))
        grad_parts[i] = _grad_slices(dW_in, dW_mkv, dW_o, dW_gu, dW_d)
    grad_recv[0] = list(scatter_grads([grad_parts[0][0].astype(BF)])) + list(early_recv)

    x_i, y_i, _ = _coords()
    me = 2 * x_i + y_i
    big = [(w_in, m_w_in, v_w_in), (w_mem_kv, m_w_mem_kv, v_w_mem_kv), (w_o, m_w_o, v_w_o),
           (w_gate_up, m_w_gate_up, v_w_gate_up), (w_down, m_w_down, v_w_down)]
    parts = []
    for a, (w, _, _) in enumerate(big):
        C = w.shape[-1]
        own = jnp.stack([lax.dynamic_index_in_dim(grad_parts[l][a], me, 0, keepdims=False) for l in range(depth)])
        rc = jnp.stack([grad_recv[l][a] for l in range(depth)], axis=1)
        parts.append(sum_parts(own.reshape(-1, C), rc.reshape(3, -1, C)))
    sibs = sibling_swap(parts)
    big_out = []
    for (w, m, v), pa, pb in zip(big, parts, sibs):
        C = w.shape[-1]
        outs = adamw(w.reshape(-1, C), pa, pb, m.reshape(-1, C), v.reshape(-1, C))
        big_out.append([o.reshape(w.shape) for o in outs])

    small_w = [mem_norm_g, g_mix_pre, g_mix_post, attn_sink, qk_norm_g, g_ffn_pre, g_ffn_post]
    small_m = [m_mem_norm_g, m_g_mix_pre, m_g_mix_post, m_attn_sink, m_qk_norm_g, m_g_ffn_pre, m_g_ffn_post]
    small_v = [v_mem_norm_g, v_g_mix_pre, v_g_mix_post, v_attn_sink, v_qk_norm_g, v_g_ffn_pre, v_g_ffn_post]
    small_g = [dg_mem.reshape(D), jnp.concatenate(dg_pre, axis=0), jnp.concatenate(dg_post, axis=0),
               jnp.stack([dsinks[k] for k in sorted(dsinks)]), dqk, jnp.concatenate(dg_fpre, axis=0),
               jnp.concatenate(dg_fpost, axis=0)]
    sizes = [a.size for a in small_w]
    total = sum(sizes)
    rows_s = -(-(total + LANES) // (8 * LANES)) * 8

    def pack(arrs, extra=None):
        flat = jnp.concatenate([a.reshape(-1).astype(F32) for a in arrs])
        flat = jnp.pad(flat, (0, rows_s * LANES - LANES - total))
        tail = jnp.zeros((LANES,), F32) if extra is None else extra.reshape(LANES)
        return jnp.concatenate([flat, tail]).reshape(rows_s, LANES)

    tot = allsum_small(pack(small_g, loss_vec))
    loss = jnp.sum(tot[rows_s - 1])
    s_out = adamw(pack(small_w), tot, None, pack(small_m), pack(small_v))

    def unpack(buf):
        flat = buf.reshape(-1)
        out, off = [], 0
        for a, n in zip(small_w, sizes):
            out.append(flat[off:off + n].reshape(a.shape))
            off += n
        return out

    sg, sd_, sm, sv_ = (unpack(b) for b in s_out)

    def ordered(k):
        sm_ = (sg, sd_, sm, sv_)[k]
        b = [bo[k] for bo in big_out]
        return [sm_[0], b[0], b[1], b[2], sm_[1], sm_[2], sm_[3], sm_[4], b[3], b[4], sm_[5], sm_[6]]

    dx_out = dcur.reshape(1, S, D)
    return (loss, dx_out, *ordered(0), *ordered(1), *ordered(2), *ordered(3))
```

```python
import functools

import jax
import jax.numpy as jnp
from jax import lax
from jax.experimental import pallas as pl
from jax.experimental.pallas import tpu as pltpu

F32 = jnp.float32
BF = jnp.bfloat16

D = 1024
HD = 64
LANES = 128
N_PAIRS = 6
DFF = 2816
IN_W = 1408
PW = 14 * LANES
K_T0, V_T0, M_T0 = 6, 9, 12
EPS = 1e-6
SCALE = HD ** -0.5
NEG = -1e30
LOG2E = 1.4426950408889634
LN2 = 0.6931471805599453
MAX_PLAIN_SCORE = 40.0
FLASH_TQ = 1024
ROPE_THETA = 500000.0
AXIAL_THETA = 10000.0
GRID_W = 64
A_RADIUS = 128
C_GROUPS = ((128, 1), (512, 4), (2048, 16))
N_MEM = 256
LR, B1, B2, AEPS, WD, STEP = 0.001, 0.9, 0.999, 1e-08, 0.01, 10
VMEM_LIMIT = 56 * 1024 * 1024
MESH = pl.DeviceIdType.MESH


def _pcall(body, *, name, grid, in_specs, out_specs, out_shape, scratch=()):
    return pl.pallas_call(
        body, name=name, grid=grid, in_specs=in_specs, out_specs=out_specs, out_shape=out_shape,
        scratch_shapes=scratch,
        compiler_params=pltpu.CompilerParams(dimension_semantics=("arbitrary",) * len(grid),
                                             vmem_limit_bytes=VMEM_LIMIT))


def _pcall_behind(body, comm, arrays, comm_out_shape, comm_sems, *, name, grid, in_specs, out_specs, out_shape,
                  scratch=()):
    n_in, n_out, n_scr, n = len(in_specs), len(out_specs), len(scratch), len(arrays)
    last = tuple(g - 1 for g in grid)

    def at(step):
        cond = pl.program_id(0) == step[0]
        for a in range(1, len(grid)):
            cond = cond & (pl.program_id(a) == step[a])
        return cond

    def wrapped(*refs):
        ins, cin = refs[:n_in], refs[n_in:n_in + n]
        outs, cout = refs[n_in + n:n_in + n + n_out], refs[n_in + n + n_out:n_in + 2 * n + n_out]
        scr, sems = refs[n_in + 2 * n + n_out:n_in + 2 * n + n_out + n_scr], refs[n_in + 2 * n + n_out + n_scr:]

        @pl.when(at((0,) * len(grid)))
        def _():
            comm(cin, cout, *sems, start=True)

        body(*ins, *outs, *scr)

        @pl.when(at(last))
        def _():
            comm(cin, cout, *sems, start=False)

    call = _pcall(wrapped, name=name, grid=grid, in_specs=list(in_specs) + [HBM_SPEC] * n,
                  out_specs=list(out_specs) + [HBM_SPEC] * n, out_shape=list(out_shape) + list(comm_out_shape),
                  scratch=list(scratch) + list(comm_sems))

    def run(*args):
        res = call(*args, *arrays)
        return res[:n_out], res[n_out:]

    return run


def _dot(a, b):
    return lax.dot_general(a, b, (((1,), (0,)), ((), ())), preferred_element_type=F32)


def _dot_nt(a, b):
    return lax.dot_general(a, b, (((1,), (1,)), ((), ())), preferred_element_type=F32)


def _lo(shape):
    return lax.broadcasted_iota(jnp.int32, shape, len(shape) - 1) < HD


def _half_sum(x):
    r = lax.broadcasted_iota(jnp.int32, (LANES, LANES), 0) // HD
    c = lax.broadcasted_iota(jnp.int32, (LANES, LANES), 1) // HD
    ones = (r == c).astype(BF)
    hi = x.astype(BF)
    return _dot(hi, ones) + _dot((x - hi.astype(F32)).astype(BF), ones)


def _col(tile, lane):
    idx = lax.broadcasted_iota(jnp.int32, tile.shape, 1)
    return jnp.sum(jnp.where(idx == lane, tile, 0.0), axis=-1, keepdims=True)


def _split(t, lo):
    z = jnp.zeros_like(t)
    return jnp.where(lo, t, z), jnp.where(lo, z, t)


def _rms(xf, g):
    r = lax.rsqrt(jnp.mean(xf * xf, axis=-1, keepdims=True) + EPS)
    return xf * r * g


def _rms_bwd(xf, g, dy):
    r = lax.rsqrt(jnp.mean(xf * xf, axis=-1, keepdims=True) + EPS)
    xr = xf * r
    dg = jnp.sum(dy * xr, axis=0, keepdims=True)
    t = dy * g
    return r * (t - xr * jnp.mean(xr * t, axis=-1, keepdims=True)), dg


def _rope_fwd(y, c, s1, s2, sh):
    return y * c + pltpu.roll(y, sh, 1) * s1 + pltpu.roll(y, LANES - sh, 1) * s2


def _rope_bwd(dy, c, s1, s2, sh):
    return dy * c + pltpu.roll(dy * s1, LANES - sh, 1) + pltpu.roll(dy * s2, sh, 1)


def _tile(t):
    return slice(t * LANES, (t + 1) * LANES)


def inproj_fwd(x, g, w_pad, tabs, qkg, *, axial, tm=512):
    S = x.shape[0]
    sh = 16 if axial else 8

    def body(x_ref, g_ref, w_ref, c_ref, s1_ref, s2_ref, qkg_ref, h_ref, p_ref, *extra):
        h = _rms(x_ref[...], g_ref[...]).astype(BF)
        h_ref[...] = h
        acc = _dot(h, w_ref[...])
        c, s1, s2 = c_ref[...], s1_ref[...], s2_ref[...]
        lo = _lo((tm, LANES))
        if axial:
            raw_ref, nrm_ref = extra

            @pl.when(pl.program_id(0) == 0)
            def _():
                nrm_ref[...] = jnp.zeros_like(nrm_ref)

        for t in range(14):
            y = acc[:, _tile(t)]
            if t < V_T0:
                if axial:
                    raw_ref[:, _tile(t)] = y.astype(BF)
                    gt = qkg_ref[0:1, :] if t < K_T0 else qkg_ref[1:2, :]
                    y = y * lax.rsqrt(_half_sum(y * y) * (1.0 / HD) + EPS) * gt
                y = _rope_fwd(y, c, s1, s2, sh)
            if t < K_T0:
                y = y * (SCALE * LOG2E if axial else SCALE)
            elif t >= M_T0:
                y = y * SCALE
            yb = y.astype(BF)
            p_ref[:, _tile(t)] = yb
            if axial and t < V_T0:
                yf = yb.astype(F32)
                n2 = jnp.max(_half_sum(yf * yf), axis=0, keepdims=True)
                r = 0 if t < K_T0 else 1
                nrm_ref[r:r + 1, :] = jnp.maximum(nrm_ref[r:r + 1, :], n2)

    row = lambda w: pl.BlockSpec((tm, w), lambda i: (i, 0))
    full = lambda a: pl.BlockSpec(a.shape, lambda i: (0, 0))
    out_shape = [jax.ShapeDtypeStruct((S, D), BF), jax.ShapeDtypeStruct((S, PW), BF)]
    out_specs = [row(D), row(PW)]
    if axial:
        out_shape += [jax.ShapeDtypeStruct((S, V_T0 * LANES), BF), jax.ShapeDtypeStruct((8, LANES), F32)]
        out_specs += [row(V_T0 * LANES), pl.BlockSpec((8, LANES), lambda i: (0, 0))]
    return _pcall(body, name="inproj_fwd_axial" if axial else "inproj_fwd", grid=(S // tm,),
                  in_specs=[row(D), full(g), full(w_pad), row(LANES), row(LANES), row(LANES), full(qkg)],
                  out_specs=out_specs, out_shape=out_shape)(x, g, w_pad, *tabs, qkg)


def norm_mm(x, g, w, *, tm):
    S, N = x.shape[0], w.shape[1]

    def body(x_ref, g_ref, w_ref, h_ref, y_ref):
        h = _rms(x_ref[...], g_ref[...]).astype(BF)
        h_ref[...] = h
        y_ref[...] = _dot(h, w_ref[...]).astype(BF)

    return _pcall(body, name="norm_mm", grid=(S // tm,),
                  in_specs=[pl.BlockSpec((tm, D), lambda i: (i, 0)), pl.BlockSpec(g.shape, lambda i: (0, 0)),
                            pl.BlockSpec(w.shape, lambda i: (0, 0))],
                  out_specs=[pl.BlockSpec((tm, D), lambda i: (i, 0)), pl.BlockSpec((tm, N), lambda i: (i, 0))],
                  out_shape=[jax.ShapeDtypeStruct((S, D), BF), jax.ShapeDtypeStruct((S, N), BF)])(x, g, w)


def ffn_up_fwd(x, g, w_gu, next_shards=(), *, tm=512, tn=1408):
    S = x.shape[0]

    def body(x_ref, g_ref, w_ref, h_ref, gate_ref, up_ref, a_ref, h_scr):
        @pl.when(pl.program_id(1) == 0)
        def _():
            h = _rms(x_ref[...], g_ref[...]).astype(BF)
            h_scr[...] = h
            h_ref[...] = h

        acc = _dot(h_scr[...], w_ref[...])
        gate, up = acc[:, :tn], acc[:, tn:]
        sig = pl.reciprocal(1.0 + jnp.exp(-gate), approx=True)
        silu = gate * sig
        gate_ref[...] = (up * (sig * (1.0 + gate * (1.0 - sig)))).astype(BF)
        up_ref[...] = silu.astype(BF)
        a_ref[...] = (silu * up).astype(BF)

    rowd = pl.BlockSpec((tm, D), lambda i, j: (i, 0))
    osp = pl.BlockSpec((tm, tn), lambda i, j: (i, j))
    sd = jax.ShapeDtypeStruct((S, DFF), BF)
    kw = dict(grid=(S // tm, DFF // tn),
              in_specs=[rowd, pl.BlockSpec(g.shape, lambda i, j: (0, 0)), pl.BlockSpec((D, 2 * tn), lambda i, j: (0, j))],
              out_specs=[rowd, osp, osp, osp], out_shape=[jax.ShapeDtypeStruct((S, D), BF), sd, sd, sd],
              scratch=[pltpu.VMEM((tm, D), BF)])
    if next_shards:
        return _pcall_behind(body, _gather_comm, next_shards, _gather_out_shape(next_shards),
                             _gather_sems(len(next_shards)), name="ffn_up_fwd_gather", **kw)(x, g, w_gu)
    return _pcall(body, name="ffn_up_fwd", **kw)(x, g, w_gu), ()


def mm_norm_res(a_parts, w, g, res, *, tm=512):
    S = a_parts[0].shape[0]
    na = len(a_parts)

    def body(*refs):
        w_ref, g_ref, res_ref, y_ref, o_ref = refs[na:]
        a = refs[0][...] if na == 1 else jnp.concatenate([r[...] for r in refs[:na]], axis=1)
        y = _dot(a, w_ref[...])
        y_ref[...] = y
        o_ref[...] = res_ref[...] + _rms(y, g_ref[...])

    rowd = pl.BlockSpec((tm, D), lambda i: (i, 0))
    sd = jax.ShapeDtypeStruct((S, D), F32)
    return _pcall(body, name="mm_norm_res", grid=(S // tm,),
                  in_specs=[pl.BlockSpec((tm, a.shape[1]), lambda i: (i, 0)) for a in a_parts]
                  + [pl.BlockSpec(w.shape, lambda i: (0, 0)), pl.BlockSpec(g.shape, lambda i: (0, 0)), rowd],
                  out_specs=[rowd, rowd], out_shape=[sd, sd])(*a_parts, w, g, res)


def loss_bwd(y, tgt, *, tm=512):
    S = y.shape[0]

    def body(y_ref, t_ref, dy_ref, l_ref):
        @pl.when(pl.program_id(0) == 0)
        def _():
            l_ref[...] = jnp.zeros_like(l_ref)

        e = y_ref[...] - t_ref[...]
        dy_ref[...] = e * (1.0 / D)
        col = jnp.sum(e * e, axis=0, keepdims=True)
        part = col[:, _tile(0)]
        for t in range(1, D // LANES):
            part = part + col[:, _tile(t)]
        l_ref[...] += part * (0.5 / D)

    rowd = pl.BlockSpec((tm, D), lambda i: (i, 0))
    return _pcall(body, name="loss_bwd", grid=(S // tm,), in_specs=[rowd, rowd],
                  out_specs=[rowd, pl.BlockSpec((1, LANES), lambda i: (0, 0))],
                  out_shape=[jax.ShapeDtypeStruct((S, D), F32), jax.ShapeDtypeStruct((1, LANES), F32)])(y, tgt)


def normbwd_mm_cat(dy, ysaved, g, w, tok, mo, *, stats_t=False, tm=512):
    S = dy.shape[0]
    n_tok = tok.shape[1] // LANES
    per = FLASH_TQ // tm

    def body(dy_ref, y_ref, g_ref, w_ref, tok_ref, mo_ref, d_ref, dcat_ref, delta_ref, dg_ref, *dt_ref):
        @pl.when(pl.program_id(0) == 0)
        def _():
            dg_ref[...] = jnp.zeros_like(dg_ref)

        d, dg = _rms_bwd(y_ref[...], g_ref[...], dy_ref[...])
        dg_ref[...] += dg
        d = d.astype(BF)
        d_ref[...] = d
        z = _dot_nt(d, w_ref[...])
        dcat_ref[...] = z.astype(BF)
        lo = _lo((tm, LANES))
        for t in range(D // LANES):
            c = tok_ref[:, _tile(t)] if t < n_tok else mo_ref[:, _tile(t - n_tok)]
            dl = _half_sum(z[:, _tile(t)] * c.astype(F32))
            delta_ref[:, _tile(t)] = dl
            if stats_t and t < N_PAIRS:
                _store_stats_t(dt_ref[0].at[t], dl)

    rowd = pl.BlockSpec((tm, D), lambda i: (i, 0))
    out_specs = [rowd, rowd, rowd, pl.BlockSpec((1, D), lambda i: (0, 0))]
    out_shape = [jax.ShapeDtypeStruct((S, D), BF), jax.ShapeDtypeStruct((S, D), BF), jax.ShapeDtypeStruct((S, D), F32),
                 jax.ShapeDtypeStruct((1, D), F32)]
    if stats_t:
        out_specs.append(pl.BlockSpec((N_PAIRS, None, 8, tm), lambda i: (0, i // per, 0, i % per)))
        out_shape.append(jax.ShapeDtypeStruct((N_PAIRS, S // FLASH_TQ, 8, FLASH_TQ), F32))
    return _pcall(body, name="normbwd_mm_cat_t" if stats_t else "normbwd_mm_cat", grid=(S // tm,),
                  in_specs=[rowd, rowd, pl.BlockSpec(g.shape, lambda i: (0, 0)), pl.BlockSpec(w.shape, lambda i: (0, 0)),
                            pl.BlockSpec((tm, tok.shape[1]), lambda i: (i, 0)),
                            pl.BlockSpec((tm, mo.shape[1]), lambda i: (i, 0))],
                  out_specs=out_specs, out_shape=out_shape)(dy, ysaved, g, w, tok, mo)


def normbwd_mm_swiglu(dy, fsaved, g, wd, gate, up, grad_parts=(), *, tm=512, tn=1408):
    S = dy.shape[0]

    def body(dy_ref, f_ref, g_ref, w_ref, gate_ref, up_ref, df_ref, dgate_ref, dup_ref, dg_ref, d_scr):
        i, j = pl.program_id(0), pl.program_id(1)

        @pl.when((i == 0) & (j == 0))
        def _():
            dg_ref[...] = jnp.zeros_like(dg_ref)

        @pl.when(j == 0)
        def _():
            d, dg = _rms_bwd(f_ref[...], g_ref[...], dy_ref[...])
            dg_ref[...] += dg
            d_scr[...] = d.astype(BF)
            df_ref[...] = d.astype(BF)

        da = _dot_nt(d_scr[...], w_ref[...])
        dgate_ref[...] = (da * gate_ref[...].astype(F32)).astype(BF)
        dup_ref[...] = (da * up_ref[...].astype(F32)).astype(BF)

    rowd = pl.BlockSpec((tm, D), lambda i, j: (i, 0))
    osp = pl.BlockSpec((tm, tn), lambda i, j: (i, j))
    sd = jax.ShapeDtypeStruct((S, DFF), BF)
    kw = dict(grid=(S // tm, DFF // tn),
              in_specs=[rowd, rowd, pl.BlockSpec(g.shape, lambda i, j: (0, 0)), pl.BlockSpec((tn, D), lambda i, j: (j, 0)),
                        osp, osp],
              out_specs=[rowd, osp, osp, pl.BlockSpec((1, D), lambda i, j: (0, 0))],
              out_shape=[jax.ShapeDtypeStruct((S, D), BF), sd, sd, jax.ShapeDtypeStruct((1, D), F32)],
              scratch=[pltpu.VMEM((tm, D), BF)])
    args = (dy, fsaved, g, wd, gate, up)
    if grad_parts:
        return _pcall_behind(body, _scatter_comm, grad_parts, _scatter_out_shape(grad_parts),
                             _scatter_sems(len(grad_parts)), name="normbwd_mm_swiglu_scatter", **kw)(*args)
    return _pcall(body, name="normbwd_mm_swiglu", **kw)(*args), ()


def mm_nt_normbwd_res(parts, xin, g, dres, *, tm=512):
    S = xin.shape[0]
    npart = len(parts)
    has_res = dres is not None

    def body(*refs):
        prefs = refs[:2 * npart]
        x_ref, g_ref = refs[2 * npart:2 * npart + 2]
        rest = refs[2 * npart + 2:]
        if has_res:
            dres_ref, dx_ref, dg_ref = rest
        else:
            (dg_ref,) = rest

        @pl.when(pl.program_id(0) == 0)
        def _():
            dg_ref[...] = jnp.zeros_like(dg_ref)

        z = _dot_nt(prefs[0][...], prefs[1][...])
        for p in range(1, npart):
            z = z + _dot_nt(prefs[2 * p][...], prefs[2 * p + 1][...])
        dx, dg = _rms_bwd(x_ref[...], g_ref[...], z)
        dg_ref[...] += dg
        if has_res:
            dx_ref[...] = dres_ref[...] + dx

    rowd = pl.BlockSpec((tm, D), lambda i: (i, 0))
    in_specs, args = [], []
    for dy, w in parts:
        in_specs += [pl.BlockSpec((tm, dy.shape[1]), lambda i: (i, 0)),
                     pl.BlockSpec(w.shape, lambda i: (0, 0), pipeline_mode=pl.Buffered(1))]
        args += [dy, w]
    in_specs += [rowd, pl.BlockSpec(g.shape, lambda i: (0, 0))]
    args += [xin, g]
    out_specs = [pl.BlockSpec((1, D), lambda i: (0, 0))]
    out_shape = [jax.ShapeDtypeStruct((1, D), F32)]
    if has_res:
        in_specs.append(rowd)
        args.append(dres)
        out_specs.insert(0, rowd)
        out_shape.insert(0, jax.ShapeDtypeStruct((S, D), F32))
    return _pcall(body, name="mm_nt_normbwd_res" if has_res else "mm_nt_normbwd", grid=(S // tm,),
                  in_specs=in_specs, out_specs=out_specs, out_shape=out_shape)(*args)


def inproj_bwd(dq, dkp, dvp, dqm, tabs, raw, qkg, w_pad, xin, g, dres, *, axial, tm=512):
    S = xin.shape[0]
    sh = 16 if axial else 8

    def body(dq_ref, dk_ref, dv_ref, dm_ref, c_ref, s1_ref, s2_ref, raw_ref, qkg_ref, w_ref, x_ref, g_ref, dres_ref,
             dx_ref, dp_ref, dg_ref, dqk_ref):
        @pl.when(pl.program_id(0) == 0)
        def _():
            dg_ref[...] = jnp.zeros_like(dg_ref)
            dqk_ref[...] = jnp.zeros_like(dqk_ref)

        c, s1, s2 = c_ref[...], s1_ref[...], s2_ref[...]
        lo = _lo((tm, LANES))
        for t in range(14):
            if t < K_T0:
                y = dq_ref[:, _tile(t)].astype(F32) * SCALE
            elif t < V_T0:
                y = dk_ref[:, _tile(2 * (t - K_T0))].astype(F32) + dk_ref[:, _tile(2 * (t - K_T0) + 1)].astype(F32)
                if axial:
                    y = y * LN2
            elif t < M_T0:
                y = dv_ref[:, _tile(2 * (t - V_T0))].astype(F32) + dv_ref[:, _tile(2 * (t - V_T0) + 1)].astype(F32)
            else:
                y = dm_ref[:, _tile(t - M_T0)] * SCALE
            if t < V_T0:
                y = _rope_bwd(y, c, s1, s2, sh)
                if axial:
                    row = 0 if t < K_T0 else 1
                    xr = raw_ref[:, _tile(t)].astype(F32)
                    r = lax.rsqrt(_half_sum(xr * xr) * (1.0 / HD) + EPS)
                    xn = xr * r
                    dqk_ref[row:row + 1, :] += jnp.sum(y * xn, axis=0, keepdims=True)
                    tt = y * qkg_ref[row:row + 1, :]
                    y = r * (tt - xn * (_half_sum(xn * tt) * (1.0 / HD)))
            dp_ref[:, _tile(t)] = y.astype(BF)
        z = _dot_nt(dp_ref[...], w_ref[...])
        dx, dg = _rms_bwd(x_ref[...], g_ref[...], z)
        dg_ref[...] += dg
        dx_ref[...] = dres_ref[...] + dx

    row = lambda w: pl.BlockSpec((tm, w), lambda i: (i, 0))
    full = lambda a: pl.BlockSpec(a.shape, lambda i: (0, 0))
    return _pcall(body, name="inproj_bwd_axial" if axial else "inproj_bwd", grid=(S // tm,),
                  in_specs=[row(768), row(768), row(768), row(256), row(LANES), row(LANES), row(LANES),
                            row(raw.shape[1] if axial else LANES), full(qkg), full(w_pad), row(D), full(g), row(D)],
                  out_specs=[row(D), row(PW), pl.BlockSpec((1, D), lambda i: (0, 0)),
                             pl.BlockSpec((8, LANES), lambda i: (0, 0))],
                  out_shape=[jax.ShapeDtypeStruct((S, D), F32), jax.ShapeDtypeStruct((S, PW), BF),
                             jax.ShapeDtypeStruct((1, D), F32), jax.ShapeDtypeStruct((8, LANES), F32)])(
        dq, dkp, dvp, dqm, *tabs, raw, qkg, w_pad, xin, g, dres)


def mm_acc(a, b, *, tk, tn, ts):
    S, K = a.shape
    N = b.shape[1]
    ts = min(ts, S)

    def body(a_ref, b_ref, o_ref):
        z = lax.dot_general(a_ref[...], b_ref[...], (((0,), (0,)), ((), ())), preferred_element_type=F32)

        @pl.when(pl.program_id(2) == 0)
        def _():
            o_ref[...] = z

        @pl.when(pl.program_id(2) > 0)
        def _():
            o_ref[...] += z

    return _pcall(body, name="mm_acc", grid=(K // tk, N // tn, S // ts),
                  in_specs=[pl.BlockSpec((ts, tk), lambda k, n, s: (s, k)), pl.BlockSpec((ts, tn), lambda k, n, s: (s, n))],
                  out_specs=pl.BlockSpec((tk, tn), lambda k, n, s: (k, n)),
                  out_shape=jax.ShapeDtypeStruct((K, N), F32))(a, b)


def _band_specs(L, d, R, T, width, bw, col_of):
    n = T // R
    nb = width // bw
    last = L // R - 1
    col = lambda g, r: r * nb + col_of(g)
    return [pl.BlockSpec((R, bw), lambda g, r, i: (jnp.maximum(i * n - 1, 0), col(g, r))),
            pl.BlockSpec((T, bw), lambda g, r, i: (i, col(g, r))),
            pl.BlockSpec((R, bw), lambda g, r, i: (jnp.minimum((i + 1) * n, last), col(g, r)))]


def _band_tile(R, L):
    return min(max(2 * R, 256), L)


def _band_bias(T, R):
    w = lax.broadcasted_iota(jnp.int32, (T, T + 2 * R), 1)
    c = lax.broadcasted_iota(jnp.int32, (T, T + 2 * R), 0)
    return jnp.where(jnp.abs(w - R - c) <= R, 0.0, NEG).astype(F32)


def _edge_bias(i, T, R, L):
    wpos = i * T - R + lax.broadcasted_iota(jnp.int32, (1, T + 2 * R), 1)
    return jnp.where((wpos >= 0) & (wpos < L), 0.0, NEG)


def banded_fwd(proj, sink, *, d, R, TQ, pair0, npairs, use_sink, o_dtype, gather=()):
    S = proj.shape[0]
    L = S // d
    pv = proj.reshape(L, d * PW)
    ow = npairs * LANES

    def body(sink_ref, bias_ref, q_ref, kp, kc, kn, vp, vc, vn, o_ref, lse_ref):
        g, i = pl.program_id(0), pl.program_id(2)
        bias = bias_ref[...] + _edge_bias(i, TQ, R, L)
        lo = _lo((TQ, LANES))
        kw = jnp.concatenate([kp[...], kc[...], kn[...]], axis=0)
        vw = jnp.concatenate([vp[...], vc[...], vn[...]], axis=0)
        v_lo, v_hi = _split(vw, _lo(vw.shape))
        vcat = jnp.concatenate([v_lo, v_hi], axis=0)
        for t in range(2):
            qa, qb = _split(q_ref[:, _tile(t)], lo)
            ps, dens, lses = [], [], []
            for h, qh in enumerate((qa, qb)):
                s = _dot_nt(qh, kw) + bias
                m = jnp.max(s, axis=-1, keepdims=True)
                if use_sink:
                    sk = sink_ref[2 * (pair0 + 2 * g + t) + h]
                    m = jnp.maximum(m, sk)
                e = jnp.exp(s - m)
                den = jnp.sum(e, axis=-1, keepdims=True)
                if use_sink:
                    den = den + jnp.exp(sk - m)
                ps.append(e.astype(BF))
                dens.append(den)
                lses.append(m + jnp.log(den))
            o = _dot(jnp.concatenate(ps, axis=1), vcat)
            o_ref[:, _tile(t)] = (o / jnp.where(lo, dens[0], dens[1])).astype(o_dtype)
            lse_ref[:, _tile(t)] = jnp.where(lo, lses[0], lses[1])

    g0 = pair0 // 2
    qspec = pl.BlockSpec((TQ, 2 * LANES), lambda g, r, i: (i, r * 7 + g0 + g))
    kspecs = _band_specs(L, d, R, TQ, PW, LANES, lambda g: K_T0 + g0 + g)
    vspecs = _band_specs(L, d, R, TQ, PW, LANES, lambda g: V_T0 + g0 + g)
    ospec = pl.BlockSpec((TQ, 2 * LANES), lambda g, r, i: (i, r * (npairs // 2) + g))
    bias = _band_bias(TQ, R)
    kw = dict(grid=(npairs // 2, d, L // TQ),
              in_specs=[pl.BlockSpec(memory_space=pltpu.SMEM), pl.BlockSpec(bias.shape, lambda g, r, i: (0, 0)),
                        qspec] + kspecs + vspecs,
              out_specs=[ospec, ospec],
              out_shape=[jax.ShapeDtypeStruct((L, d * ow), o_dtype), jax.ShapeDtypeStruct((L, d * ow), F32)])
    args = (sink, bias, pv, pv, pv, pv, pv, pv, pv)
    if gather:
        (o, lse), gathered = _pcall_behind(body, _gather_comm, gather, _gather_out_shape(gather),
                                           _gather_sems(len(gather)), name="banded_fwd_gather", **kw)(*args)
        return o.reshape(S, ow), lse.reshape(S, ow), gathered
    o, lse = _pcall(body, name="banded_fwd", **kw)(*args)
    return o.reshape(S, ow), lse.reshape(S, ow)


def banded_bwd_dq(proj, do, lse, delta, sink, *, d, R, TQ, pair0, npairs, use_sink, scatter=()):
    S = proj.shape[0]
    L = S // d
    pv = proj.reshape(L, d * PW)
    ow = npairs * LANES

    def body(sink_ref, bias_ref, q_ref, kp, kc, kn, vp, vc, vn, do_ref, lse_ref, delta_ref, dq_ref, dsink_ref):
        g, r, i = pl.program_id(0), pl.program_id(1), pl.program_id(2)

        @pl.when((r == 0) & (i == 0))
        def _():
            dsink_ref[...] = jnp.zeros_like(dsink_ref)

        bias = bias_ref[...] + _edge_bias(i, TQ, R, L)
        lo = _lo((TQ, LANES))
        kw = jnp.concatenate([kp[...], kc[...], kn[...]], axis=0)
        vw = jnp.concatenate([vp[...], vc[...], vn[...]], axis=0)
        k_lo, k_hi = _split(kw, _lo(kw.shape))
        kcat = jnp.concatenate([k_lo, k_hi], axis=0)
        for t in range(2):
            qa, qb = _split(q_ref[:, _tile(t)], lo)
            doa, dob = _split(do_ref[:, _tile(t)], lo)
            lse_t, delta_t = lse_ref[:, _tile(t)], delta_ref[:, _tile(t)]
            dss, dsk = [], []
            for h, (qh, doh) in enumerate(((qa, doa), (qb, dob))):
                lse_h, delta_h = _col(lse_t, h * HD), _col(delta_t, h * HD)
                pr = jnp.exp(_dot_nt(qh, kw) + bias - lse_h)
                dss.append((pr * (_dot_nt(doh, vw) - delta_h)).astype(BF))
                if use_sink:
                    psink = jnp.exp(sink_ref[2 * (pair0 + 2 * g + t) + h] - lse_h)
                    dsk.append(-jnp.sum(psink * delta_h, axis=0, keepdims=True))
            dq_ref[:, _tile(t)] = _dot(jnp.concatenate(dss, axis=1), kcat).astype(BF)
            if use_sink:
                dsink_ref[:, _tile(t)] += jnp.where(_lo((8, LANES)), dsk[0], dsk[1])

    g0 = pair0 // 2
    qspec = pl.BlockSpec((TQ, 2 * LANES), lambda g, r, i: (i, r * 7 + g0 + g))
    kspecs = _band_specs(L, d, R, TQ, PW, LANES, lambda g: K_T0 + g0 + g)
    vspecs = _band_specs(L, d, R, TQ, PW, LANES, lambda g: V_T0 + g0 + g)
    ospec = pl.BlockSpec((TQ, 2 * LANES), lambda g, r, i: (i, r * (npairs // 2) + g))
    view = lambda a: a.reshape(L, d * a.shape[1])
    ispec = lambda a: pl.BlockSpec((TQ, 2 * LANES), lambda g, r, i: (i, r * (a.shape[1] // (2 * LANES)) + g))
    bias = _band_bias(TQ, R)
    kw = dict(grid=(npairs // 2, d, L // TQ),
              in_specs=[pl.BlockSpec(memory_space=pltpu.SMEM), pl.BlockSpec(bias.shape, lambda g, r, i: (0, 0)),
                        qspec] + kspecs + vspecs + [ispec(do), ispec(lse), ispec(delta)],
              out_specs=[ospec, pl.BlockSpec((8, 2 * LANES), lambda g, r, i: (g, 0))],
              out_shape=[jax.ShapeDtypeStruct((L, d * ow), BF),
                         jax.ShapeDtypeStruct((npairs // 2 * 8, 2 * LANES), F32)])
    args = (sink, bias, pv, pv, pv, pv, pv, pv, pv, view(do), view(lse), view(delta))
    if scatter:
        (dq, dsink), got = _pcall_behind(body, _scatter_comm, scatter, _scatter_out_shape(scatter),
                                         _scatter_sems(len(scatter)), name="banded_bwd_dq_scatter", **kw)(*args)
        return dq.reshape(S, ow), dsink, got
    dq, dsink = _pcall(body, name="banded_bwd_dq", **kw)(*args)
    return dq.reshape(S, ow), dsink


def banded_bwd_dkv(proj, do, lse, delta, *, d, R, TK, pair0, npairs):
    S = proj.shape[0]
    L = S // d
    pv = proj.reshape(L, d * PW)
    ow = npairs * LANES

    def body(bias_ref, k_ref, v_ref, qp, qc, qn, dop, doc, don, lp, lc, ln, dp_, dc_, dn_, dk_ref, dv_ref):
        j = pl.program_id(2)
        W = TK + 2 * R
        bias = bias_ref[...] + _edge_bias(j, TK, R, L)
        low = _lo((W, LANES))
        qw = jnp.concatenate([qp[...], qc[...], qn[...]], axis=0)
        dow = jnp.concatenate([dop[...], doc[...], don[...]], axis=0)
        lse_w = jnp.concatenate([lp[...], lc[...], ln[...]], axis=0)
        delta_w = jnp.concatenate([dp_[...], dc_[...], dn_[...]], axis=0)
        k, v = k_ref[...], v_ref[...]
        for t in range(2):
            qa, qb = _split(qw[:, _tile(t)], low)
            doa, dob = _split(dow[:, _tile(t)], low)
            lse_r, delta_r = lse_w[:, _tile(t)].T, delta_w[:, _tile(t)].T
            prs, dss = [], []
            for h, (qh, doh) in enumerate(((qa, doa), (qb, dob))):
                pr = jnp.exp(_dot_nt(k, qh) + bias - lse_r[h * HD:h * HD + 1, :])
                dss.append((pr * (_dot_nt(v, doh) - delta_r[h * HD:h * HD + 1, :])).astype(BF))
                prs.append(pr.astype(BF))
            dv_ref[:, _tile(t)] = _dot(jnp.concatenate(prs, axis=1), jnp.concatenate([doa, dob], axis=0)).astype(BF)
            dk_ref[:, _tile(t)] = _dot(jnp.concatenate(dss, axis=1), jnp.concatenate([qa, qb], axis=0)).astype(BF)

    g0 = pair0 // 2
    kspec = pl.BlockSpec((TK, LANES), lambda g, r, j: (j, r * 14 + K_T0 + g0 + g))
    vspec = pl.BlockSpec((TK, LANES), lambda g, r, j: (j, r * 14 + V_T0 + g0 + g))
    qspecs = _band_specs(L, d, R, TK, PW, 2 * LANES, lambda g: g0 + g)
    ispecs = lambda a: _band_specs(L, d, R, TK, a.shape[1], 2 * LANES, lambda g: g)
    view = lambda a: a.reshape(L, d * a.shape[1])
    ospec = pl.BlockSpec((TK, 2 * LANES), lambda g, r, j: (j, r * (npairs // 2) + g))
    sd = jax.ShapeDtypeStruct((L, d * ow), BF)
    bias = _band_bias(TK, R)
    dk, dv = _pcall(body, name="banded_bwd_dkv", grid=(npairs // 2, d, L // TK),
                    in_specs=[pl.BlockSpec(bias.shape, lambda g, r, j: (0, 0)), kspec, vspec] + qspecs + ispecs(do)
                    + ispecs(lse) + ispecs(delta),
                    out_specs=[ospec, ospec], out_shape=[sd, sd])(
        bias, pv, pv, pv, pv, pv, *([view(do)] * 3), *([view(lse)] * 3), *([view(delta)] * 3))
    return dk.reshape(S, ow), dv.reshape(S, ow)


def _store_stats_t(ref, tile):
    t = tile.T
    ref[...] = jnp.zeros_like(ref)
    ref[0:1, :] = t[0:1, :]
    ref[1:2, :] = t[HD:HD + 1, :]


def flash_fwd(proj, bound, *, tq=FLASH_TQ, tk=1024):
    S = proj.shape[0]

    def body_general(q_ref, k_ref, v_ref, o_ref, lse_ref):
        lo = _lo((tq, LANES))
        qa, qb = _split(q_ref[...], lo)
        lov = _lo((tk, LANES))

        def step(j, carry):
            ma, la, mb, lb, acc = carry
            rows = pl.ds(pl.multiple_of(j * tk, tk), tk)
            k, v = k_ref[rows, :], v_ref[rows, :]
            outs = []
            for qh, m0, l0 in ((qa, ma, la), (qb, mb, lb)):
                s = _dot_nt(qh, k)
                m1 = jnp.maximum(m0, jnp.max(s, axis=-1, keepdims=True))
                al = jnp.exp2(m0 - m1)
                e = jnp.exp2(s - m1)
                outs.append((m1, al * l0 + jnp.sum(e, axis=-1, keepdims=True), al, e.astype(BF)))
            v_lo, v_hi = _split(v, lov)
            pvv = _dot(jnp.concatenate([outs[0][3], outs[1][3]], axis=1), jnp.concatenate([v_lo, v_hi], axis=0))
            acc = acc * jnp.where(lo, outs[0][2], outs[1][2]) + pvv
            return outs[0][0], outs[0][1], outs[1][0], outs[1][1], acc

        m_init = jnp.full((tq, 1), NEG, F32)
        l_init = jnp.zeros((tq, 1), F32)
        ma, la, mb, lb, acc = lax.fori_loop(0, S // tk, step,
                                            (m_init, l_init, m_init, l_init, jnp.zeros((tq, LANES), F32)))
        o_ref[...] = (acc / jnp.where(lo, la, lb)).astype(BF)
        _store_stats_t(lse_ref, jnp.where(lo, ma * LN2 + jnp.log(la), mb * LN2 + jnp.log(lb)))

    def body_plain(q_ref, k_ref, v_ref, o_ref, lse_ref):
        lo = _lo((tq, LANES))
        qa, qb = _split(q_ref[...], lo)
        lov = _lo((tk, LANES))
        one = jnp.ones((tk, LANES), BF)

        def step(j, carry):
            acc_a, acc_b = carry
            rows = pl.ds(pl.multiple_of(j * tk, tk), tk)
            k, v = k_ref[rows, :], v_ref[rows, :]
            ea = jnp.exp2(_dot_nt(qa, k)).astype(BF)
            eb = jnp.exp2(_dot_nt(qb, k)).astype(BF)
            acc_a = acc_a + _dot(ea, jnp.where(lov, v, one))
            acc_b = acc_b + _dot(eb, jnp.where(lov, one, v))
            return acc_a, acc_b

        z = jnp.zeros((tq, LANES), F32)
        acc_a, acc_b = lax.fori_loop(0, S // tk, step, (z, z))
        den = jnp.where(lo, pltpu.roll(acc_a, HD, 1), pltpu.roll(acc_b, HD, 1))
        o_ref[...] = (jnp.where(lo, acc_a, acc_b) / den).astype(BF)
        _store_stats_t(lse_ref, jnp.log(den))

    def body(bound_ref, q_ref, k_ref, v_ref, o_ref, lse_ref):
        small = bound_ref[0] <= MAX_PLAIN_SCORE

        @pl.when(small)
        def _():
            body_plain(q_ref, k_ref, v_ref, o_ref, lse_ref)

        @pl.when(jnp.logical_not(small))
        def _():
            body_general(q_ref, k_ref, v_ref, o_ref, lse_ref)

    ospec = pl.BlockSpec((tq, LANES), lambda p, i: (i, p))
    return _pcall(body, name="flash_fwd", grid=(N_PAIRS, S // tq),
                  in_specs=[pl.BlockSpec(memory_space=pltpu.SMEM), ospec,
                            pl.BlockSpec((S, LANES), lambda p, i: (0, K_T0 + p // 2)),
                            pl.BlockSpec((S, LANES), lambda p, i: (0, V_T0 + p // 2))],
                  out_specs=[ospec, pl.BlockSpec((None, None, 8, tq), lambda p, i: (p, i, 0, 0))],
                  out_shape=[jax.ShapeDtypeStruct((S, 768), BF), jax.ShapeDtypeStruct((N_PAIRS, S // tq, 8, tq), F32)])(
        bound.reshape(1), proj, proj, proj)


def flash_bwd(proj, do, lse_t, delta_t, *, tk=1024):
    S = proj.shape[0]
    nq, tq = lse_t.shape[1], lse_t.shape[3]

    def body(k_ref, v_ref, q_ref, do_ref, lse_ref, delta_ref, dk_ref, dv_ref, dqt_ref):
        @pl.when(pl.program_id(1) == 0)
        def _():
            dqt_ref[...] = jnp.zeros_like(dqt_ref)

        k, v = k_ref[...], v_ref[...]
        lo = _lo((tq, LANES))
        k_lo, k_hi = _split(k.astype(F32), _lo((tk, LANES)))
        kt = jnp.concatenate([k_lo.T, k_hi.T], axis=1).astype(BF)

        def step(i, carry):
            dk, dv = carry
            rows = pl.ds(pl.multiple_of(i * tq, tq), tq)
            qa, qb = _split(q_ref[rows, :], lo)
            doa, dob = _split(do_ref[rows, :], lo)
            lse_i, delta_i = lse_ref[i] * LOG2E, delta_ref[i]
            prs, dss = [], []
            for h, (qh, doh) in enumerate(((qa, doa), (qb, dob))):
                pr = jnp.exp2(_dot_nt(k, qh) - lse_i[h:h + 1, :])
                dss.append((pr * (_dot_nt(v, doh) - delta_i[h:h + 1, :])).astype(BF))
                prs.append(pr.astype(BF))
            dv = dv + _dot(jnp.concatenate(prs, axis=1), jnp.concatenate([doa, dob], axis=0))
            dk = dk + _dot(jnp.concatenate(dss, axis=1), jnp.concatenate([qa, qb], axis=0))
            dqt_ref[i] += _dot(kt, jnp.concatenate(dss, axis=0))
            return dk, dv

        z = jnp.zeros((tk, LANES), F32)
        dk, dv = lax.fori_loop(0, nq, step, (z, z))
        dk_ref[...] = dk.astype(BF)
        dv_ref[...] = dv.astype(BF)

    ospec = pl.BlockSpec((tk, LANES), lambda p, j: (j, p))
    stat = pl.BlockSpec((None, nq, 8, tq), lambda p, j: (p, 0, 0, 0))
    whole = lambda: pl.BlockSpec((S, LANES), lambda p, j: (0, p), pipeline_mode=pl.Buffered(1))
    sd = jax.ShapeDtypeStruct((S, 768), BF)
    dk, dv, dqt = _pcall(body, name="flash_bwd", grid=(N_PAIRS, S // tk),
                         in_specs=[pl.BlockSpec((tk, LANES), lambda p, j: (j, K_T0 + p // 2)),
                                   pl.BlockSpec((tk, LANES), lambda p, j: (j, V_T0 + p // 2)), whole(), whole(),
                                   stat, stat],
                         out_specs=[ospec, ospec,
                                    pl.BlockSpec((None, nq, LANES, tq), lambda p, j: (p, 0, 0, 0),
                                                 pipeline_mode=pl.Buffered(1))],
                         out_shape=[sd, sd, jax.ShapeDtypeStruct((N_PAIRS, nq, LANES, tq), F32)])(
        proj, proj, proj, do, lse_t, delta_t)
    dq = jnp.transpose(dqt, (1, 3, 0, 2)).reshape(S, 768).astype(BF)
    return dq, dk, dv


def mem_fwd(proj, mkv, *, tq=512):
    S = proj.shape[0]

    def body(q_ref, km_ref, vm_ref, o_ref, lse_ref):
        lo = _lo((tq, LANES))
        lov = _lo((N_MEM, LANES))
        for t in range(2):
            qa, qb = _split(q_ref[:, _tile(t)], lo)
            km, vm = km_ref[:, _tile(t)], vm_ref[:, _tile(t)]
            ps, dens, lses = [], [], []
            for qh in (qa, qb):
                s = _dot_nt(qh, km)
                m = jnp.max(s, axis=-1, keepdims=True)
                e = jnp.exp(s - m)
                den = jnp.sum(e, axis=-1, keepdims=True)
                ps.append(e.astype(BF))
                dens.append(den)
                lses.append(m + jnp.log(den))
            v_lo, v_hi = _split(vm, lov)
            o = _dot(jnp.concatenate(ps, axis=1), jnp.concatenate([v_lo, v_hi], axis=0))
            o_ref[:, _tile(t)] = (o / jnp.where(lo, dens[0], dens[1])).astype(BF)
            lse_ref[:, _tile(t)] = jnp.where(lo, lses[0], lses[1])

    ospec = pl.BlockSpec((tq, 256), lambda i: (i, 0))
    return _pcall(body, name="mem_fwd", grid=(S // tq,),
                  in_specs=[pl.BlockSpec((tq, 256), lambda i: (i, M_T0 // 2)),
                            pl.BlockSpec((N_MEM, 256), lambda i: (0, 0)), pl.BlockSpec((N_MEM, 256), lambda i: (0, 1))],
                  out_specs=[ospec, ospec],
                  out_shape=[jax.ShapeDtypeStruct((S, 256), BF), jax.ShapeDtypeStruct((S, 256), F32)])(proj, mkv, mkv)


def mem_bwd(proj, mkv, dcat, lse, delta, *, tq=512):
    S = proj.shape[0]

    def body(q_ref, km_ref, vm_ref, do_ref, lse_ref, delta_ref, dq_ref, dkm_ref, dvm_ref):
        @pl.when(pl.program_id(0) == 0)
        def _():
            dkm_ref[...] = jnp.zeros_like(dkm_ref)
            dvm_ref[...] = jnp.zeros_like(dvm_ref)

        lo = _lo((tq, LANES))
        lov = _lo((N_MEM, LANES))
        for t in range(2):
            qa, qb = _split(q_ref[:, _tile(t)], lo)
            doa, dob = _split(do_ref[:, _tile(t)], lo)
            km, vm = km_ref[:, _tile(t)], vm_ref[:, _tile(t)]
            lse_t, delta_t = lse_ref[:, _tile(t)], delta_ref[:, _tile(t)]
            prs, dss = [], []
            for h, (qh, doh) in enumerate(((qa, doa), (qb, dob))):
                pr = jnp.exp(_dot_nt(qh, km) - _col(lse_t, h * HD))
                dss.append(pr * (_dot_nt(doh, vm) - _col(delta_t, h * HD)))
                prs.append(pr)
            k_lo, k_hi = _split(km, lov)
            dq_ref[:, _tile(t)] = _dot(jnp.concatenate(dss, axis=1).astype(BF), jnp.concatenate([k_lo, k_hi], axis=0))
            dvm_ref[:, _tile(t)] += _dot(jnp.concatenate(prs, axis=0).T.astype(BF), jnp.concatenate([doa, dob], axis=0))
            dkm_ref[:, _tile(t)] += _dot(jnp.concatenate(dss, axis=0).T.astype(BF), jnp.concatenate([qa, qb], axis=0))

    ospec = pl.BlockSpec((tq, 256), lambda i: (i, 0))
    msp = pl.BlockSpec((N_MEM, 256), lambda i: (0, 0))
    md = jax.ShapeDtypeStruct((N_MEM, 256), F32)
    dq, dkm, dvm = _pcall(body, name="mem_bwd", grid=(S // tq,),
                          in_specs=[pl.BlockSpec((tq, 256), lambda i: (i, M_T0 // 2)), msp,
                                    pl.BlockSpec((N_MEM, 256), lambda i: (0, 1)),
                                    pl.BlockSpec((tq, 256), lambda i: (i, 3)), ospec,
                                    pl.BlockSpec((tq, 256), lambda i: (i, 3))],
                          out_specs=[ospec, msp, msp], out_shape=[jax.ShapeDtypeStruct((S, 256), F32), md, md])(
        proj, mkv, mkv, dcat, lse, delta)
    return dq, jnp.concatenate([dkm, dvm], axis=1)


def combine_fwd(os_, lses, *, tm=512):
    S = os_[0].shape[0]

    def body(o0, o1, o2, l0, l1, l2, tok_ref):
        ls = [l0[...], l1[...], l2[...]]
        m = jnp.maximum(jnp.maximum(ls[0], ls[1]), ls[2])
        es = [jnp.exp(l - m) for l in ls]
        den = es[0] + es[1] + es[2]
        for g, o in enumerate((o0, o1, o2)):
            tok_ref[:, 256 * g:256 * (g + 1)] = (o[...] * (es[g] / den)).astype(BF)

    sp = pl.BlockSpec((tm, 256), lambda i: (i, 0))
    return _pcall(body, name="combine_fwd", grid=(S // tm,), in_specs=[sp] * 6,
                  out_specs=pl.BlockSpec((tm, 768), lambda i: (i, 0)),
                  out_shape=jax.ShapeDtypeStruct((S, 768), BF))(*os_, *lses)


def combine_bwd(dcat, os_, lses, *, tm=512):
    S = dcat.shape[0]

    def body(dt_ref, o0, o1, o2, l0, l1, l2, do0, do1, do2, de0, de1, de2):
        ls = [l0[...], l1[...], l2[...]]
        m = jnp.maximum(jnp.maximum(ls[0], ls[1]), ls[2])
        es = [jnp.exp(l - m) for l in ls]
        den = es[0] + es[1] + es[2]
        alphas = [e / den for e in es]
        lo = _lo((tm, LANES))
        dts = [dt_ref[:, 256 * g:256 * (g + 1)].astype(F32) for g in range(3)]
        dal = []
        for g, o in enumerate((o0, o1, o2)):
            pr = dts[g] * o[...]
            dal.append(jnp.concatenate([_half_sum(pr[:, _tile(0)]), _half_sum(pr[:, _tile(1)])], axis=1))
        mix = alphas[0] * dal[0] + alphas[1] * dal[1] + alphas[2] * dal[2]
        for g, (do_ref, de_ref) in enumerate(((do0, de0), (do1, de1), (do2, de2))):
            do_ref[...] = (dts[g] * alphas[g]).astype(BF)
            de_ref[...] = alphas[g] * mix

    sp = pl.BlockSpec((tm, 256), lambda i: (i, 0))
    outs = _pcall(body, name="combine_bwd", grid=(S // tm,),
                  in_specs=[pl.BlockSpec((tm, 768), lambda i: (i, 0))] + [sp] * 6, out_specs=[sp] * 6,
                  out_shape=[jax.ShapeDtypeStruct((S, 256), BF)] * 3 + [jax.ShapeDtypeStruct((S, 256), F32)] * 3)(
        dcat, *os_, *lses)
    return outs[:3], outs[3:]


def _coords():
    return lax.axis_index("x"), lax.axis_index("y"), lax.axis_index("c")


def _other_chips(x, y):
    return [(1 - x, y), (x, 1 - y), (1 - x, 1 - y)]


HBM_SPEC = pl.BlockSpec(memory_space=pltpu.HBM)


def _gather_comm(ins, outs, send, recv, lsem, start):
    x, y, c = _coords()
    me = 2 * x + y
    for a in range(len(ins)):
        local = pltpu.make_async_copy(ins[a], outs[a].at[me], lsem.at[a])
        if start:
            local.start()
        for j, (px, py) in enumerate(_other_chips(x, y)):
            sems = dict(send_sem=send.at[3 * a + j], recv_sem=recv.at[3 * a + j], device_id=(px, py, c),
                        device_id_type=MESH)
            cp = pltpu.make_async_remote_copy(src_ref=ins[a], dst_ref=outs[a].at[me], **sems)
            if start:
                cp.start()
            else:
                pltpu.make_async_remote_copy(src_ref=ins[a], dst_ref=outs[a].at[2 * px + py], **sems).wait_recv()
                cp.wait_send()
        if not start:
            local.wait()


def _gather_out_shape(shards):
    return [jax.ShapeDtypeStruct((4,) + s.shape, s.dtype) for s in shards]


def _gather_sems(n):
    return [pltpu.SemaphoreType.DMA((3 * n,)), pltpu.SemaphoreType.DMA((3 * n,)), pltpu.SemaphoreType.DMA((n,))]


def gather_shards(shards):
    n = len(shards)

    def body(*refs):
        _gather_comm(refs[:n], refs[n:2 * n], *refs[2 * n:], start=True)
        _gather_comm(refs[:n], refs[n:2 * n], *refs[2 * n:], start=False)

    return pl.pallas_call(body, name="gather_shards", in_specs=[HBM_SPEC] * n, out_specs=[HBM_SPEC] * n,
                          out_shape=_gather_out_shape(shards), scratch_shapes=_gather_sems(n))(*shards)


def _scatter_comm(ins, outs, send, recv, start):
    x, y, c = _coords()
    for a in range(len(ins)):
        for j, (px, py) in enumerate(_other_chips(x, y)):
            cp = pltpu.make_async_remote_copy(src_ref=ins[a].at[2 * px + py], dst_ref=outs[a].at[j],
                                              send_sem=send.at[3 * a + j], recv_sem=recv.at[3 * a + j],
                                              device_id=(px, py, c), device_id_type=MESH)
            if start:
                cp.start()
            else:
                cp.wait_recv()
                cp.wait_send()


def _scatter_out_shape(parts):
    return [jax.ShapeDtypeStruct((3,) + p.shape[1:], p.dtype) for p in parts]


def _scatter_sems(n):
    return [pltpu.SemaphoreType.DMA((3 * n,)), pltpu.SemaphoreType.DMA((3 * n,))]


def scatter_grads(parts):
    n = len(parts)

    def body(*refs):
        _scatter_comm(refs[:n], refs[n:2 * n], *refs[2 * n:], start=True)
        _scatter_comm(refs[:n], refs[n:2 * n], *refs[2 * n:], start=False)

    return pl.pallas_call(body, name="scatter_grads", in_specs=[HBM_SPEC] * n, out_specs=[HBM_SPEC] * n,
                          out_shape=_scatter_out_shape(parts), scratch_shapes=_scatter_sems(n))(*parts)


def sibling_swap(arrs):
    n = len(arrs)

    def body(*refs):
        ins, outs = refs[:n], refs[n:2 * n]
        send, recv = refs[2 * n:]
        x, y, c = _coords()
        cps = []
        for a in range(n):
            cp = pltpu.make_async_remote_copy(src_ref=ins[a], dst_ref=outs[a], send_sem=send.at[a], recv_sem=recv.at[a],
                                              device_id=(x, y, 1 - c), device_id_type=MESH)
            cp.start()
            cps.append(cp)
        for cp in cps:
            cp.wait_recv()
        for cp in cps:
            cp.wait_send()

    return pl.pallas_call(
        body, name="sibling_swap", in_specs=[HBM_SPEC] * n, out_specs=[HBM_SPEC] * n,
        out_shape=[jax.ShapeDtypeStruct(a.shape, a.dtype) for a in arrs],
        scratch_shapes=[pltpu.SemaphoreType.DMA((n,)), pltpu.SemaphoreType.DMA((n,))])(*arrs)


def allsum_small(v):
    rows = v.shape[0]

    def body(v_ref, tot_ref, gath_ref, send, recv):
        x, y, c = _coords()
        me = 4 * x + 2 * y + c
        gath_ref[me] = v_ref[...]
        cps = []
        for k in range(1, 8):
            fx, fy, fc = (k >> 2) & 1, (k >> 1) & 1, k & 1
            peer = (1 - x if fx else x, 1 - y if fy else y, 1 - c if fc else c)
            cp = pltpu.make_async_remote_copy(src_ref=v_ref, dst_ref=gath_ref.at[me], send_sem=send.at[k - 1],
                                              recv_sem=recv.at[k - 1], device_id=peer, device_id_type=MESH)
            cp.start()
            cps.append(cp)
        for cp in cps:
            cp.wait_recv()
        for cp in cps:
            cp.wait_send()
        tot = gath_ref[0]
        for k in range(1, 8):
            tot = tot + gath_ref[k]
        tot_ref[...] = tot

    vm = pl.BlockSpec(memory_space=pltpu.VMEM)
    tot, _ = pl.pallas_call(
        body, name="allsum_small", in_specs=[vm], out_specs=[vm, vm],
        out_shape=[jax.ShapeDtypeStruct((rows, LANES), F32), jax.ShapeDtypeStruct((8, rows, LANES), F32)],
        scratch_shapes=[pltpu.SemaphoreType.DMA((7,)), pltpu.SemaphoreType.DMA((7,))])(v)
    return tot


def sum_parts(own, recv, *, tr=256):
    R, C = own.shape
    tr = min(tr, R)

    def body(o_ref, r_ref, out_ref):
        out_ref[...] = ((o_ref[...] + r_ref[0].astype(F32)) + r_ref[1].astype(F32)) + r_ref[2].astype(F32)

    sp = pl.BlockSpec((tr, C), lambda i: (i, 0))
    return _pcall(body, name="sum_parts", grid=(R // tr,),
                  in_specs=[sp, pl.BlockSpec((3, tr, C), lambda i: (0, i, 0))], out_specs=sp,
                  out_shape=jax.ShapeDtypeStruct((R, C), F32))(own, recv)


def adamw(w, ga, gb, m, v, *, tr=256):
    R, C = w.shape
    tr = min(tr, R)
    two = gb is not None

    def body(*refs):
        if two:
            w_ref, ga_ref, gb_ref, m_ref, v_ref, g_out, d_out, m_out, v_out = refs
            g = ga_ref[...] + gb_ref[...]
        else:
            w_ref, ga_ref, m_ref, v_ref, g_out, d_out, m_out, v_out = refs
            g = ga_ref[...]
        mn = B1 * m_ref[...] + (1.0 - B1) * g
        vn = B2 * v_ref[...] + (1.0 - B2) * (g * g)
        m_hat = mn / (1.0 - B1 ** STEP)
        v_hat = vn / (1.0 - B2 ** STEP)
        g_out[...] = g
        d_out[...] = -LR * (m_hat / (jnp.sqrt(v_hat) + AEPS) + WD * w_ref[...])
        m_out[...] = mn
        v_out[...] = vn

    sp = pl.BlockSpec((tr, C), lambda i: (i, 0))
    args = [w, ga, gb, m, v] if two else [w, ga, m, v]
    sd = jax.ShapeDtypeStruct((R, C), F32)
    return _pcall(body, name="adamw", grid=(R // tr,), in_specs=[sp] * len(args), out_specs=[sp] * 4,
                  out_shape=[sd] * 4)(*args)


def _rope_tables(S):
    def inv_freq(n_dims, theta):
        return theta ** (-(jnp.arange(0, n_dims, 2, dtype=jnp.float32) / n_dims))

    pos = lax.broadcasted_iota(jnp.int32, (S, LANES), 0)
    d = lax.broadcasted_iota(jnp.int32, (S, LANES), 1) % HD
    d1 = lax.iota(jnp.int32, LANES) % HD
    ang = pos.astype(F32) * inv_freq(HD // 4, ROPE_THETA)[d1 % 8][None, :]
    sin = jnp.sin(ang)
    partial = (jnp.where(d < 16, jnp.cos(ang), 1.0), jnp.where((d >= 8) & (d < 16), sin, 0.0),
               jnp.where(d < 8, -sin, 0.0))
    grid_pos = jnp.where(d < 32, pos // GRID_W, pos % GRID_W)
    ang = grid_pos.astype(F32) * inv_freq(HD // 2, AXIAL_THETA)[d1 % 16][None, :]
    sin = jnp.sin(ang)
    axial = (jnp.cos(ang), jnp.where(d % 32 >= 16, sin, 0.0), jnp.where(d % 32 < 16, -sin, 0.0))
    return partial, axial


def _pad_w_in(w):
    cols = [w[:, :768]]
    for base in (768, 960):
        for g in range(3):
            kg = w[:, base + g * HD:base + (g + 1) * HD]
            cols += [kg, kg]
    cols.append(w[:, 1152:])
    return jnp.concatenate(cols, axis=1)


def _unpad_dw_in(dw):
    cols = [dw[:, :768]]
    for t0 in (K_T0, V_T0):
        for g in range(3):
            b = (t0 + g) * LANES
            cols.append(dw[:, b:b + HD] + dw[:, b + HD:b + LANES])
    cols.append(dw[:, M_T0 * LANES:])
    return jnp.concatenate(cols, axis=1)


def _grad_slices(dW_in, dW_mkv, dW_o, dW_gu, dW_d):
    return [None if dW_in is None else jnp.transpose(dW_in.reshape(D, 4, IN_W // 4), (1, 0, 2)),
            dW_mkv.reshape(4, D // 4, 512), dW_o.reshape(4, D // 4, D),
            jnp.transpose(dW_gu.reshape(D, 4, 2 * DFF // 4), (1, 0, 2)), dW_d.reshape(4, DFF // 4, D)]


def _fold(t):
    return t[..., :HD] + t[..., HD:]


def kernel(x, mem, mem_norm_g, w_in, w_mem_kv, w_o, g_mix_pre, g_mix_post, attn_sink, qk_norm_g, w_gate_up, w_down, g_ffn_pre, g_ffn_post, loss_target, m_mem_norm_g, m_w_in, m_w_mem_kv, m_w_o, m_g_mix_pre, m_g_mix_post, m_attn_sink, m_qk_norm_g, m_w_gate_up, m_w_down, m_g_ffn_pre, m_g_ffn_post, v_mem_norm_g, v_w_in, v_w_mem_kv, v_w_o, v_g_mix_pre, v_g_mix_post, v_attn_sink, v_qk_norm_g, v_w_gate_up, v_w_down, v_g_ffn_pre, v_g_ffn_post):
    S = x.shape[1]
    depth = w_in.shape[0]
    xs, memx, tgt = x[0], mem[0], loss_target[0]
    tab_p, tab_a = _rope_tables(S)
    row = lambda a: a.reshape(1, -1)

    shards_bf = [w.astype(BF) for w in (w_in, w_mem_kv, w_o, w_gate_up, w_down)]
    layer_shards = lambda i: [s[i] for s in shards_bf]
    W_in, W_mkv, W_o, W_g, W_u, W_gu, W_d = ([None] * depth for _ in range(7))

    def set_weights(i, gathered):
        if len(gathered) == 5:
            W_in[i] = jnp.concatenate([gathered[0][s] for s in range(4)], axis=1)
        gm, go, gg, gd = gathered[-4:]
        W_mkv[i] = jnp.concatenate([gm[s] for s in range(4)], axis=0)
        W_o[i] = jnp.concatenate([go[s] for s in range(4)], axis=0)
        W_g[i] = jnp.concatenate([gg[0], gg[1]], axis=1)
        W_u[i] = jnp.concatenate([gg[2], gg[3]], axis=1)
        W_gu[i] = jnp.concatenate([gg[0], gg[2], gg[1], gg[3]], axis=1)
        W_d[i] = jnp.concatenate([gd[s] for s in range(4)], axis=0)

    (g_in0,) = gather_shards(layer_shards(0)[:1])
    W_in[0] = jnp.concatenate([g_in0[s] for s in range(4)], axis=1)
    mem_g = row(mem_norm_g)
    zero_sink = jnp.zeros((12,), F32)
    qkg = jnp.pad(jnp.concatenate([qk_norm_g[0], qk_norm_g[0]], axis=1), ((0, 6), (0, 0)))
    no_qkg = jnp.zeros((8, LANES), F32)

    saved = []
    cur = xs
    for i in range(depth):
        kind = i % 3
        wp = _pad_w_in(W_in[i])
        sv = dict(x=cur, wp=wp)
        if kind == 1:
            h1, proj, raw, nrm = inproj_fwd(cur, row(g_mix_pre[i]), wp, tab_a, qkg, axial=True)
            sv["raw"] = raw
        else:
            h1, proj = inproj_fwd(cur, row(g_mix_pre[i]), wp, tab_p, no_qkg, axial=False)
        if kind == 0:
            res = banded_fwd(proj, attn_sink[i // 3], d=1, R=A_RADIUS, TQ=_band_tile(A_RADIUS, S), pair0=0, npairs=6,
                             use_sink=True, o_dtype=BF, gather=layer_shards(0)[1:] if i == 0 else ())
            tok, lse = res[:2]
            if i == 0:
                set_weights(0, res[2])
        elif kind == 1:
            bound = jnp.sqrt(jnp.max(nrm[0]) * jnp.max(nrm[1])) * LN2
            tok, lse = flash_fwd(proj, bound)
        else:
            os_, lses = [], []
            for g, (window, dil) in enumerate(C_GROUPS):
                rad = window // (2 * dil)
                o_g, l_g = banded_fwd(proj, zero_sink, d=dil, R=rad, TQ=_band_tile(rad, S // dil), pair0=2 * g, npairs=2,
                                      use_sink=False, o_dtype=F32)
                os_.append(o_g)
                lses.append(l_g)
            tok = combine_fwd(os_, lses)
            sv["os"], lse = os_, lses
        mem_n, mkv = norm_mm(memx, mem_g, W_mkv[i], tm=N_MEM)
        mo, mlse = mem_fwd(proj, mkv)
        o, x2 = mm_norm_res([tok, mo], W_o[i], row(g_mix_post[i]), cur)
        (h2, gate, up, act), gathered = ffn_up_fwd(x2, row(g_ffn_pre[i]), W_gu[i],
                                                   layer_shards(i + 1) if i + 1 < depth else ())
        if i + 1 < depth:
            set_weights(i + 1, gathered)
        f, x3 = mm_norm_res([act], W_d[i], row(g_ffn_post[i]), x2)
        sv.update(h1=h1, proj=proj, lse=lse, mem_n=mem_n, mkv=mkv, mlse=mlse, tok=tok, mo=mo, o=o, x2=x2, h2=h2, gate=gate,
                  up=up, act=act, f=f)
        saved.append(sv)
        cur = x3

    dcur, loss_vec = loss_bwd(cur, tgt)

    grad_parts, grad_recv = [None] * depth, [None] * depth
    dg_pre, dg_post, dg_fpre, dg_fpost = [None] * depth, [None] * depth, [None] * depth, [None] * depth
    dg_mem = jnp.zeros((1, D), F32)
    dsinks, dqk = {}, None
    for i in reversed(range(depth)):
        sv = saved[i]
        kind = i % 3
        proj = sv["proj"]
        pending = [p.astype(BF) for p in grad_parts[i + 1]] if i + 1 < depth else ()
        (df, dgate, dup, dg_fpost[i]), got = normbwd_mm_swiglu(dcur, sv["f"], row(g_ffn_post[i]), W_d[i], sv["gate"],
                                                               sv["up"], pending)
        if i + 1 < depth:
            grad_recv[i + 1] = got
        dx2, dg_fpre[i] = mm_nt_normbwd_res([(dgate, W_g[i]), (dup, W_u[i])], sv["x2"], row(g_ffn_pre[i]), dcur)
        dW_d = mm_acc(sv["act"], df, tk=1408, tn=D, ts=2048)
        dW_gu = jnp.concatenate([mm_acc(sv["h2"], dgate, tk=D, tn=1408, ts=2048),
                                 mm_acc(sv["h2"], dup, tk=D, tn=1408, ts=2048)], axis=1)
        do, dcat, delta, dg_post[i], *delta_t = normbwd_mm_cat(dx2, sv["o"], row(g_mix_post[i]), W_o[i], sv["tok"],
                                                               sv["mo"], stats_t=kind == 1)
        dW_o = jnp.concatenate([mm_acc(sv["tok"], do, tk=768, tn=D, ts=2048), mm_acc(sv["mo"], do, tk=256, tn=D, ts=2048)],
                               axis=0)
        dqm, dmkv = mem_bwd(proj, sv["mkv"], dcat, sv["mlse"], delta)
        dmkv = dmkv.astype(BF)
        (dgm,) = mm_nt_normbwd_res([(dmkv, W_mkv[i])], memx, mem_g, None, tm=N_MEM)
        dg_mem = dg_mem + dgm
        dW_mkv = mm_acc(sv["mem_n"], dmkv, tk=D, tn=512, ts=N_MEM)
        if kind == 0:
            sink = attn_sink[i // 3]
            args = dict(d=1, R=A_RADIUS, pair0=0, npairs=6)
            tile = _band_tile(A_RADIUS, S)
            early = [p.astype(BF) for p in _grad_slices(None, dW_mkv, dW_o, dW_gu, dW_d)[1:]] if i == 0 else ()
            res = banded_bwd_dq(proj, dcat, sv["lse"], delta, sink, TQ=tile, use_sink=True, scatter=early, **args)
            dq, dsk = res[:2]
            if i == 0:
                early_recv = res[2]
            dkp, dvp = banded_bwd_dkv(proj, dcat, sv["lse"], delta, TK=tile, **args)
            dsinks[i // 3] = dsk.reshape(3, 8, 2, 2, HD)[:, 0, :, :, 0].reshape(12)
        elif kind == 1:
            dq, dkp, dvp = flash_bwd(proj, dcat, sv["lse"], delta_t[0])
        else:
            dos, des = combine_bwd(dcat, sv["os"], sv["lse"])
            dqs, dks, dvs = [], [], []
            for g, (window, dil) in enumerate(C_GROUPS):
                rad = window // (2 * dil)
                args = dict(d=dil, R=rad, pair0=2 * g, npairs=2)
                tile = _band_tile(rad, S // dil)
                dq_g, _ = banded_bwd_dq(proj, dos[g], sv["lse"][g], des[g], zero_sink, TQ=tile, use_sink=False, **args)
                dk_g, dv_g = banded_bwd_dkv(proj, dos[g], sv["lse"][g], des[g], TK=tile, **args)
                dqs.append(dq_g)
                dks.append(dk_g)
                dvs.append(dv_g)
            dq, dkp, dvp = (jnp.concatenate(t, axis=1) for t in (dqs, dks, dvs))
        if kind == 1:
            dcur, dproj, dg_pre[i], dqk_t = inproj_bwd(dq, dkp, dvp, dqm, tab_a, sv["raw"], qkg, sv["wp"], sv["x"],
                                                       row(g_mix_pre[i]), dx2, axial=True)
            dqk = _fold(dqk_t[:2]).reshape(1, 2, HD)
        else:
            dcur, dproj, dg_pre[i], _ = inproj_bwd(dq, dkp, dvp, dqm, tab_p, proj, no_qkg, sv["wp"],
                                                   sv["x"], row(g_mix_pre[i]), dx2, axial=False)
        dW_in = _unpad_dw_in(mm_acc(sv["h1"], dproj, tk=D, tn=896, ts=2048))
        grad_parts[i] = _grad_slices(dW_in, dW_mkv, dW_o, dW_gu, dW_d)
    grad_recv[0] = list(scatter_grads([grad_parts[0][0].astype(BF)])) + list(early_recv)

    x_i, y_i, _ = _coords()
    me = 2 * x_i + y_i
    big = [(w_in, m_w_in, v_w_in), (w_mem_kv, m_w_mem_kv, v_w_mem_kv), (w_o, m_w_o, v_w_o),
           (w_gate_up, m_w_gate_up, v_w_gate_up), (w_down, m_w_down, v_w_down)]
    parts = []
    for a, (w, _, _) in enumerate(big):
        C = w.shape[-1]
        own = jnp.stack([lax.dynamic_index_in_dim(grad_parts[l][a], me, 0, keepdims=False) for l in range(depth)])
        rc = jnp.stack([grad_recv[l][a] for l in range(depth)], axis=1)
        parts.append(sum_parts(own.reshape(-1, C), rc.reshape(3, -1, C)))
    sibs = sibling_swap(parts)
    big_out = []
    for (w, m, v), pa, pb in zip(big, parts, sibs):
        C = w.shape[-1]
        outs = adamw(w.reshape(-1, C), pa, pb, m.reshape(-1, C), v.reshape(-1, C))
        big_out.append([o.reshape(w.shape) for o in outs])

    small_w = [mem_norm_g, g_mix_pre, g_mix_post, attn_sink, qk_norm_g, g_ffn_pre, g_ffn_post]
    small_m = [m_mem_norm_g, m_g_mix_pre, m_g_mix_post, m_attn_sink, m_qk_norm_g, m_g_ffn_pre, m_g_ffn_post]
    small_v = [v_mem_norm_g, v_g_mix_pre, v_g_mix_post, v_attn_sink, v_qk_norm_g, v_g_ffn_pre, v_g_ffn_post]
    small_g = [dg_mem.reshape(D), jnp.concatenate(dg_pre, axis=0), jnp.concatenate(dg_post, axis=0),
               jnp.stack([dsinks[k] for k in sorted(dsinks)]), dqk, jnp.concatenate(dg_fpre, axis=0),
               jnp.concatenate(dg_fpost, axis=0)]
    sizes = [a.size for a in small_w]
    total = sum(sizes)
    rows_s = -(-(total + LANES) // (8 * LANES)) * 8

    def pack(arrs, extra=None):
        flat = jnp.concatenate([a.reshape(-1).astype(F32) for a in arrs])
        flat = jnp.pad(flat, (0, rows_s * LANES - LANES - total))
        tail = jnp.zeros((LANES,), F32) if extra is None else extra.reshape(LANES)
        return jnp.concatenate([flat, tail]).reshape(rows_s, LANES)

    tot = allsum_small(pack(small_g, loss_vec))
    loss = jnp.sum(tot[rows_s - 1])
    s_out = adamw(pack(small_w), tot, None, pack(small_m), pack(small_v))

    def unpack(buf):
        flat = buf.reshape(-1)
        out, off = [], 0
        for a, n in zip(small_w, sizes):
            out.append(flat[off:off + n].reshape(a.shape))
            off += n
        return out

    sg, sd_, sm, sv_ = (unpack(b) for b in s_out)

    def ordered(k):
        sm_ = (sg, sd_, sm, sv_)[k]
        b = [bo[k] for bo in big_out]
        return [sm_[0], b[0], b[1], b[2], sm_[1], sm_[2], sm_[3], sm_[4], b[3], b[4], sm_[5], sm_[6]]

    dx_out = dcur.reshape(1, S, D)
    return (loss, dx_out, *ordered(0), *ordered(1), *ordered(2), *ordered(3))
```

```python
import jax
import jax.numpy as jnp
from jax import lax
from jax.experimental import pallas as pl
from jax.experimental.pallas import tpu as pltpu

F32 = jnp.float32
BF = jnp.bfloat16

D = 1024
HD = 64
LANES = 128
N_PAIRS = 6
DFF = 2816
IN_W = 1408
PW = 14 * LANES
K_T0, V_T0, M_T0 = 6, 9, 12
EPS = 1e-6
SCALE = HD ** -0.5
NEG = -1e30
LOG2E = 1.4426950408889634
LN2 = 0.6931471805599453
MAX_PLAIN_SCORE = 40.0
FLASH_TQ = 1024
ROPE_THETA = 500000.0
AXIAL_THETA = 10000.0
GRID_W = 64
A_RADIUS = 128
C_GROUPS = ((128, 1), (512, 4), (2048, 16))
N_MEM = 256
LR, B1, B2, AEPS, WD, STEP = 0.001, 0.9, 0.999, 1e-08, 0.01, 10
VMEM_LIMIT = 56 * 1024 * 1024
MESH = pl.DeviceIdType.MESH


def _pcall(body, *, name, grid, in_specs, out_specs, out_shape, scratch=()):
    return pl.pallas_call(
        body, name=name, grid=grid, in_specs=in_specs, out_specs=out_specs, out_shape=out_shape,
        scratch_shapes=scratch,
        compiler_params=pltpu.CompilerParams(dimension_semantics=("arbitrary",) * len(grid),
                                             vmem_limit_bytes=VMEM_LIMIT))


def _pcall_behind(body, comm, arrays, comm_out_shape, comm_sems, *, name, grid, in_specs, out_specs, out_shape,
                  scratch=()):
    n_in, n_out, n_scr, n = len(in_specs), len(out_specs), len(scratch), len(arrays)
    last = tuple(g - 1 for g in grid)

    def at(step):
        cond = pl.program_id(0) == step[0]
        for a in range(1, len(grid)):
            cond = cond & (pl.program_id(a) == step[a])
        return cond

    def wrapped(*refs):
        ins, cin = refs[:n_in], refs[n_in:n_in + n]
        outs, cout = refs[n_in + n:n_in + n + n_out], refs[n_in + n + n_out:n_in + 2 * n + n_out]
        scr, sems = refs[n_in + 2 * n + n_out:n_in + 2 * n + n_out + n_scr], refs[n_in + 2 * n + n_out + n_scr:]

        @pl.when(at((0,) * len(grid)))
        def _():
            comm(cin, cout, *sems, start=True)

        body(*ins, *outs, *scr)

        @pl.when(at(last))
        def _():
            comm(cin, cout, *sems, start=False)

    call = _pcall(wrapped, name=name, grid=grid, in_specs=list(in_specs) + [HBM_SPEC] * n,
                  out_specs=list(out_specs) + [HBM_SPEC] * n, out_shape=list(out_shape) + list(comm_out_shape),
                  scratch=list(scratch) + list(comm_sems))

    def run(*args):
        res = call(*args, *arrays)
        return res[:n_out], res[n_out:]

    return run


def _dot(a, b):
    return lax.dot_general(a, b, (((1,), (0,)), ((), ())), preferred_element_type=F32)


def _dot_nt(a, b):
    return lax.dot_general(a, b, (((1,), (1,)), ((), ())), preferred_element_type=F32)


def _lo(shape):
    return lax.broadcasted_iota(jnp.int32, shape, len(shape) - 1) < HD


def _half_sum(x):
    r = lax.broadcasted_iota(jnp.int32, (LANES, LANES), 0) // HD
    c = lax.broadcasted_iota(jnp.int32, (LANES, LANES), 1) // HD
    ones = (r == c).astype(BF)
    hi = x.astype(BF)
    return _dot(hi, ones) + _dot((x - hi.astype(F32)).astype(BF), ones)


def _col(tile, lane):
    idx = lax.broadcasted_iota(jnp.int32, tile.shape, 1)
    return jnp.sum(jnp.where(idx == lane, tile, 0.0), axis=-1, keepdims=True)


def _split(t, lo):
    z = jnp.zeros_like(t)
    return jnp.where(lo, t, z), jnp.where(lo, z, t)


def _rms(xf, g):
    r = lax.rsqrt(jnp.mean(xf * xf, axis=-1, keepdims=True) + EPS)
    return xf * r * g


def _rms_bwd(xf, g, dy):
    r = lax.rsqrt(jnp.mean(xf * xf, axis=-1, keepdims=True) + EPS)
    xr = xf * r
    dg = jnp.sum(dy * xr, axis=0, keepdims=True)
    t = dy * g
    return r * (t - xr * jnp.mean(xr * t, axis=-1, keepdims=True)), dg


def _rope_fwd(y, c, s1, s2, sh):
    return y * c + pltpu.roll(y, sh, 1) * s1 + pltpu.roll(y, LANES - sh, 1) * s2


def _rope_bwd(dy, c, s1, s2, sh):
    return dy * c + pltpu.roll(dy * s1, LANES - sh, 1) + pltpu.roll(dy * s2, sh, 1)


def _tile(t):
    return slice(t * LANES, (t + 1) * LANES)


def inproj_fwd(x, g, w_pad, tabs, qkg, *, axial, tm=512):
    S = x.shape[0]
    sh = 16 if axial else 8

    def body(x_ref, g_ref, w_ref, c_ref, s1_ref, s2_ref, qkg_ref, h_ref, p_ref, *extra):
        h = _rms(x_ref[...], g_ref[...]).astype(BF)
        h_ref[...] = h
        acc = _dot(h, w_ref[...])
        c, s1, s2 = c_ref[...], s1_ref[...], s2_ref[...]
        lo = _lo((tm, LANES))
        if axial:
            raw_ref, nrm_ref = extra

            @pl.when(pl.program_id(0) == 0)
            def _():
                nrm_ref[...] = jnp.zeros_like(nrm_ref)

        for t in range(14):
            y = acc[:, _tile(t)]
            if t < V_T0:
                if axial:
                    raw_ref[:, _tile(t)] = y.astype(BF)
                    gt = qkg_ref[0:1, :] if t < K_T0 else qkg_ref[1:2, :]
                    y = y * lax.rsqrt(_half_sum(y * y) * (1.0 / HD) + EPS) * gt
                y = _rope_fwd(y, c, s1, s2, sh)
            if t < K_T0:
                y = y * (SCALE * LOG2E if axial else SCALE)
            elif t >= M_T0:
                y = y * SCALE
            yb = y.astype(BF)
            p_ref[:, _tile(t)] = yb
            if axial and t < V_T0:
                yf = yb.astype(F32)
                n2 = jnp.max(_half_sum(yf * yf), axis=0, keepdims=True)
                r = 0 if t < K_T0 else 1
                nrm_ref[r:r + 1, :] = jnp.maximum(nrm_ref[r:r + 1, :], n2)

    row = lambda w: pl.BlockSpec((tm, w), lambda i: (i, 0))
    full = lambda a: pl.BlockSpec(a.shape, lambda i: (0, 0))
    out_shape = [jax.ShapeDtypeStruct((S, D), BF), jax.ShapeDtypeStruct((S, PW), BF)]
    out_specs = [row(D), row(PW)]
    if axial:
        out_shape += [jax.ShapeDtypeStruct((S, V_T0 * LANES), BF), jax.ShapeDtypeStruct((8, LANES), F32)]
        out_specs += [row(V_T0 * LANES), pl.BlockSpec((8, LANES), lambda i: (0, 0))]
    return _pcall(body, name="inproj_fwd_axial" if axial else "inproj_fwd", grid=(S // tm,),
                  in_specs=[row(D), full(g), full(w_pad), row(LANES), row(LANES), row(LANES), full(qkg)],
                  out_specs=out_specs, out_shape=out_shape)(x, g, w_pad, *tabs, qkg)


def norm_mm(x, g, w, *, tm):
    S, N = x.shape[0], w.shape[1]

    def body(x_ref, g_ref, w_ref, h_ref, y_ref):
        h = _rms(x_ref[...], g_ref[...]).astype(BF)
        h_ref[...] = h
        y_ref[...] = _dot(h, w_ref[...]).astype(BF)

    return _pcall(body, name="norm_mm", grid=(S // tm,),
                  in_specs=[pl.BlockSpec((tm, D), lambda i: (i, 0)), pl.BlockSpec(g.shape, lambda i: (0, 0)),
                            pl.BlockSpec(w.shape, lambda i: (0, 0))],
                  out_specs=[pl.BlockSpec((tm, D), lambda i: (i, 0)), pl.BlockSpec((tm, N), lambda i: (i, 0))],
                  out_shape=[jax.ShapeDtypeStruct((S, D), BF), jax.ShapeDtypeStruct((S, N), BF)])(x, g, w)


def ffn_up_fwd(x, g, w_gu, next_shards=(), *, tm=512, tn=1408):
    S = x.shape[0]

    def body(x_ref, g_ref, w_ref, h_ref, gate_ref, up_ref, a_ref, h_scr):
        @pl.when(pl.program_id(1) == 0)
        def _():
            h = _rms(x_ref[...], g_ref[...]).astype(BF)
            h_scr[...] = h
            h_ref[...] = h

        acc = _dot(h_scr[...], w_ref[...])
        gate, up = acc[:, :tn], acc[:, tn:]
        sig = pl.reciprocal(1.0 + jnp.exp(-gate), approx=True)
        silu = gate * sig
        gate_ref[...] = (up * (sig * (1.0 + gate * (1.0 - sig)))).astype(BF)
        up_ref[...] = silu.astype(BF)
        a_ref[...] = (silu * up).astype(BF)

    rowd = pl.BlockSpec((tm, D), lambda i, j: (i, 0))
    osp = pl.BlockSpec((tm, tn), lambda i, j: (i, j))
    sd = jax.ShapeDtypeStruct((S, DFF), BF)
    kw = dict(grid=(S // tm, DFF // tn),
              in_specs=[rowd, pl.BlockSpec(g.shape, lambda i, j: (0, 0)), pl.BlockSpec((D, 2 * tn), lambda i, j: (0, j))],
              out_specs=[rowd, osp, osp, osp], out_shape=[jax.ShapeDtypeStruct((S, D), BF), sd, sd, sd],
              scratch=[pltpu.VMEM((tm, D), BF)])
    if next_shards:
        return _pcall_behind(body, _gather_comm, next_shards, _gather_out_shape(next_shards),
                             _gather_sems(len(next_shards)), name="ffn_up_fwd_gather", **kw)(x, g, w_gu)
    return _pcall(body, name="ffn_up_fwd", **kw)(x, g, w_gu), ()


def mm_norm_res(a_parts, w, g, res, *, tm=512):
    S = a_parts[0].shape[0]
    na = len(a_parts)

    def body(*refs):
        w_ref, g_ref, res_ref, y_ref, o_ref = refs[na:]
        a = refs[0][...] if na == 1 else jnp.concatenate([r[...] for r in refs[:na]], axis=1)
        y = _dot(a, w_ref[...])
        y_ref[...] = y
        o_ref[...] = res_ref[...] + _rms(y, g_ref[...])

    rowd = pl.BlockSpec((tm, D), lambda i: (i, 0))
    sd = jax.ShapeDtypeStruct((S, D), F32)
    return _pcall(body, name="mm_norm_res", grid=(S // tm,),
                  in_specs=[pl.BlockSpec((tm, a.shape[1]), lambda i: (i, 0)) for a in a_parts]
                  + [pl.BlockSpec(w.shape, lambda i: (0, 0)), pl.BlockSpec(g.shape, lambda i: (0, 0)), rowd],
                  out_specs=[rowd, rowd], out_shape=[sd, sd])(*a_parts, w, g, res)


def loss_bwd(y, tgt, *, tm=512):
    S = y.shape[0]

    def body(y_ref, t_ref, dy_ref, l_ref):
        @pl.when(pl.program_id(0) == 0)
        def _():
            l_ref[...] = jnp.zeros_like(l_ref)

        e = y_ref[...] - t_ref[...]
        dy_ref[...] = e * (1.0 / D)
        col = jnp.sum(e * e, axis=0, keepdims=True)
        part = col[:, _tile(0)]
        for t in range(1, D // LANES):
            part = part + col[:, _tile(t)]
        l_ref[...] += part * (0.5 / D)

    rowd = pl.BlockSpec((tm, D), lambda i: (i, 0))
    return _pcall(body, name="loss_bwd", grid=(S // tm,), in_specs=[rowd, rowd],
                  out_specs=[rowd, pl.BlockSpec((1, LANES), lambda i: (0, 0))],
                  out_shape=[jax.ShapeDtypeStruct((S, D), F32), jax.ShapeDtypeStruct((1, LANES), F32)])(y, tgt)


def normbwd_mm_cat(dy, ysaved, g, w, tok, mo, *, stats_t=False, tm=512):
    S = dy.shape[0]
    n_tok = tok.shape[1] // LANES
    per = FLASH_TQ // tm

    def body(dy_ref, y_ref, g_ref, w_ref, tok_ref, mo_ref, d_ref, dcat_ref, delta_ref, dg_ref, *dt_ref):
        @pl.when(pl.program_id(0) == 0)
        def _():
            dg_ref[...] = jnp.zeros_like(dg_ref)

        d, dg = _rms_bwd(y_ref[...], g_ref[...], dy_ref[...])
        dg_ref[...] += dg
        d = d.astype(BF)
        d_ref[...] = d
        z = _dot_nt(d, w_ref[...])
        dcat_ref[...] = z.astype(BF)
        lo = _lo((tm, LANES))
        for t in range(D // LANES):
            c = tok_ref[:, _tile(t)] if t < n_tok else mo_ref[:, _tile(t - n_tok)]
            dl = _half_sum(z[:, _tile(t)] * c.astype(F32))
            delta_ref[:, _tile(t)] = dl
            if stats_t and t < N_PAIRS:
                _store_stats_t(dt_ref[0].at[t], dl)

    rowd = pl.BlockSpec((tm, D), lambda i: (i, 0))
    out_specs = [rowd, rowd, rowd, pl.BlockSpec((1, D), lambda i: (0, 0))]
    out_shape = [jax.ShapeDtypeStruct((S, D), BF), jax.ShapeDtypeStruct((S, D), BF), jax.ShapeDtypeStruct((S, D), F32),
                 jax.ShapeDtypeStruct((1, D), F32)]
    if stats_t:
        out_specs.append(pl.BlockSpec((N_PAIRS, None, 8, tm), lambda i: (0, i // per, 0, i % per)))
        out_shape.append(jax.ShapeDtypeStruct((N_PAIRS, S // FLASH_TQ, 8, FLASH_TQ), F32))
    return _pcall(body, name="normbwd_mm_cat_t" if stats_t else "normbwd_mm_cat", grid=(S // tm,),
                  in_specs=[rowd, rowd, pl.BlockSpec(g.shape, lambda i: (0, 0)), pl.BlockSpec(w.shape, lambda i: (0, 0)),
                            pl.BlockSpec((tm, tok.shape[1]), lambda i: (i, 0)),
                            pl.BlockSpec((tm, mo.shape[1]), lambda i: (i, 0))],
                  out_specs=out_specs, out_shape=out_shape)(dy, ysaved, g, w, tok, mo)


def normbwd_mm_swiglu(dy, fsaved, g, wd, gate, up, grad_parts=(), *, tm=512, tn=1408):
    S = dy.shape[0]

    def body(dy_ref, f_ref, g_ref, w_ref, gate_ref, up_ref, df_ref, dgate_ref, dup_ref, dg_ref, d_scr):
        i, j = pl.program_id(0), pl.program_id(1)

        @pl.when((i == 0) & (j == 0))
        def _():
            dg_ref[...] = jnp.zeros_like(dg_ref)

        @pl.when(j == 0)
        def _():
            d, dg = _rms_bwd(f_ref[...], g_ref[...], dy_ref[...])
            dg_ref[...] += dg
            d_scr[...] = d.astype(BF)
            df_ref[...] = d.astype(BF)

        da = _dot_nt(d_scr[...], w_ref[...])
        dgate_ref[...] = (da * gate_ref[...].astype(F32)).astype(BF)
        dup_ref[...] = (da * up_ref[...].astype(F32)).astype(BF)

    rowd = pl.BlockSpec((tm, D), lambda i, j: (i, 0))
    osp = pl.BlockSpec((tm, tn), lambda i, j: (i, j))
    sd = jax.ShapeDtypeStruct((S, DFF), BF)
    kw = dict(grid=(S // tm, DFF // tn),
              in_specs=[rowd, rowd, pl.BlockSpec(g.shape, lambda i, j: (0, 0)), pl.BlockSpec((tn, D), lambda i, j: (j, 0)),
                        osp, osp],
              out_specs=[rowd, osp, osp, pl.BlockSpec((1, D), lambda i, j: (0, 0))],
              out_shape=[jax.ShapeDtypeStruct((S, D), BF), sd, sd, jax.ShapeDtypeStruct((1, D), F32)],
              scratch=[pltpu.VMEM((tm, D), BF)])
    args = (dy, fsaved, g, wd, gate, up)
    if grad_parts:
        return _pcall_behind(body, _scatter_comm, grad_parts, _scatter_out_shape(grad_parts),
                             _scatter_sems(len(grad_parts)), name="normbwd_mm_swiglu_scatter", **kw)(*args)
    return _pcall(body, name="normbwd_mm_swiglu", **kw)(*args), ()


def mm_nt_normbwd_res(parts, xin, g, dres, *, tm=512):
    S = xin.shape[0]
    npart = len(parts)
    has_res = dres is not None

    def body(*refs):
        prefs = refs[:2 * npart]
        x_ref, g_ref = refs[2 * npart:2 * npart + 2]
        rest = refs[2 * npart + 2:]
        if has_res:
            dres_ref, dx_ref, dg_ref = rest
        else:
            (dg_ref,) = rest

        @pl.when(pl.program_id(0) == 0)
        def _():
            dg_ref[...] = jnp.zeros_like(dg_ref)

        z = _dot_nt(prefs[0][...], prefs[1][...])
        for p in range(1, npart):
            z = z + _dot_nt(prefs[2 * p][...], prefs[2 * p + 1][...])
        dx, dg = _rms_bwd(x_ref[...], g_ref[...], z)
        dg_ref[...] += dg
        if has_res:
            dx_ref[...] = dres_ref[...] + dx

    rowd = pl.BlockSpec((tm, D), lambda i: (i, 0))
    in_specs, args = [], []
    for dy, w in parts:
        in_specs += [pl.BlockSpec((tm, dy.shape[1]), lambda i: (i, 0)),
                     pl.BlockSpec(w.shape, lambda i: (0, 0), pipeline_mode=pl.Buffered(1))]
        args += [dy, w]
    in_specs += [rowd, pl.BlockSpec(g.shape, lambda i: (0, 0))]
    args += [xin, g]
    out_specs = [pl.BlockSpec((1, D), lambda i: (0, 0))]
    out_shape = [jax.ShapeDtypeStruct((1, D), F32)]
    if has_res:
        in_specs.append(rowd)
        args.append(dres)
        out_specs.insert(0, rowd)
        out_shape.insert(0, jax.ShapeDtypeStruct((S, D), F32))
    return _pcall(body, name="mm_nt_normbwd_res" if has_res else "mm_nt_normbwd", grid=(S // tm,),
                  in_specs=in_specs, out_specs=out_specs, out_shape=out_shape)(*args)


def inproj_bwd(dq, dkp, dvp, dqm, tabs, raw, qkg, w_pad, xin, g, dres, *, axial, tm=512):
    S = xin.shape[0]
    sh = 16 if axial else 8

    def body(dq_ref, dk_ref, dv_ref, dm_ref, c_ref, s1_ref, s2_ref, raw_ref, qkg_ref, w_ref, x_ref, g_ref, dres_ref,
             dx_ref, dp_ref, dg_ref, dqk_ref):
        @pl.when(pl.program_id(0) == 0)
        def _():
            dg_ref[...] = jnp.zeros_like(dg_ref)
            dqk_ref[...] = jnp.zeros_like(dqk_ref)

        c, s1, s2 = c_ref[...], s1_ref[...], s2_ref[...]
        lo = _lo((tm, LANES))
        for t in range(14):
            if t < K_T0:
                y = dq_ref[:, _tile(t)].astype(F32) * SCALE
            elif t < V_T0:
                y = dk_ref[:, _tile(2 * (t - K_T0))].astype(F32) + dk_ref[:, _tile(2 * (t - K_T0) + 1)].astype(F32)
                if axial:
                    y = y * LN2
            elif t < M_T0:
                y = dv_ref[:, _tile(2 * (t - V_T0))].astype(F32) + dv_ref[:, _tile(2 * (t - V_T0) + 1)].astype(F32)
            else:
                y = dm_ref[:, _tile(t - M_T0)] * SCALE
            if t < V_T0:
                y = _rope_bwd(y, c, s1, s2, sh)
                if axial:
                    row = 0 if t < K_T0 else 1
                    xr = raw_ref[:, _tile(t)].astype(F32)
                    r = lax.rsqrt(_half_sum(xr * xr) * (1.0 / HD) + EPS)
                    xn = xr * r
                    dqk_ref[row:row + 1, :] += jnp.sum(y * xn, axis=0, keepdims=True)
                    tt = y * qkg_ref[row:row + 1, :]
                    y = r * (tt - xn * (_half_sum(xn * tt) * (1.0 / HD)))
            dp_ref[:, _tile(t)] = y.astype(BF)
        z = _dot_nt(dp_ref[...], w_ref[...])
        dx, dg = _rms_bwd(x_ref[...], g_ref[...], z)
        dg_ref[...] += dg
        dx_ref[...] = dres_ref[...] + dx

    row = lambda w: pl.BlockSpec((tm, w), lambda i: (i, 0))
    full = lambda a: pl.BlockSpec(a.shape, lambda i: (0, 0))
    return _pcall(body, name="inproj_bwd_axial" if axial else "inproj_bwd", grid=(S // tm,),
                  in_specs=[row(768), row(768), row(768), row(256), row(LANES), row(LANES), row(LANES),
                            row(raw.shape[1] if axial else LANES), full(qkg), full(w_pad), row(D), full(g), row(D)],
                  out_specs=[row(D), row(PW), pl.BlockSpec((1, D), lambda i: (0, 0)),
                             pl.BlockSpec((8, LANES), lambda i: (0, 0))],
                  out_shape=[jax.ShapeDtypeStruct((S, D), F32), jax.ShapeDtypeStruct((S, PW), BF),
                             jax.ShapeDtypeStruct((1, D), F32), jax.ShapeDtypeStruct((8, LANES), F32)])(
        dq, dkp, dvp, dqm, *tabs, raw, qkg, w_pad, xin, g, dres)


def mm_acc(a, b, *, tk, tn, ts):
    S, K = a.shape
    N = b.shape[1]
    ts = min(ts, S)

    def body(a_ref, b_ref, o_ref):
        z = lax.dot_general(a_ref[...], b_ref[...], (((0,), (0,)), ((), ())), preferred_element_type=F32)

        @pl.when(pl.program_id(2) == 0)
        def _():
            o_ref[...] = z

        @pl.when(pl.program_id(2) > 0)
        def _():
            o_ref[...] += z

    return _pcall(body, name="mm_acc", grid=(K // tk, N // tn, S // ts),
                  in_specs=[pl.BlockSpec((ts, tk), lambda k, n, s: (s, k)), pl.BlockSpec((ts, tn), lambda k, n, s: (s, n))],
                  out_specs=pl.BlockSpec((tk, tn), lambda k, n, s: (k, n)),
                  out_shape=jax.ShapeDtypeStruct((K, N), F32))(a, b)


def _band_specs(L, d, R, T, width, bw, col_of):
    n = T // R
    nb = width // bw
    last = L // R - 1
    col = lambda g, r: r * nb + col_of(g)
    return [pl.BlockSpec((R, bw), lambda g, r, i: (jnp.maximum(i * n - 1, 0), col(g, r))),
            pl.BlockSpec((T, bw), lambda g, r, i: (i, col(g, r))),
            pl.BlockSpec((R, bw), lambda g, r, i: (jnp.minimum((i + 1) * n, last), col(g, r)))]


def _band_tile(R, L):
    return min(max(2 * R, 256), L)


def _band_bias(T, R):
    w = lax.broadcasted_iota(jnp.int32, (T, T + 2 * R), 1)
    c = lax.broadcasted_iota(jnp.int32, (T, T + 2 * R), 0)
    return jnp.where(jnp.abs(w - R - c) <= R, 0.0, NEG).astype(F32)


def _edge_bias(i, T, R, L):
    wpos = i * T - R + lax.broadcasted_iota(jnp.int32, (1, T + 2 * R), 1)
    return jnp.where((wpos >= 0) & (wpos < L), 0.0, NEG)


def banded_fwd(proj, sink, *, d, R, TQ, pair0, npairs, use_sink, o_dtype, gather=()):
    S = proj.shape[0]
    L = S // d
    pv = proj.reshape(L, d * PW)
    ow = npairs * LANES

    def body(sink_ref, bias_ref, q_ref, kp, kc, kn, vp, vc, vn, o_ref, lse_ref):
        g, i = pl.program_id(0), pl.program_id(2)
        bias = bias_ref[...] + _edge_bias(i, TQ, R, L)
        lo = _lo((TQ, LANES))
        kw = jnp.concatenate([kp[...], kc[...], kn[...]], axis=0)
        vw = jnp.concatenate([vp[...], vc[...], vn[...]], axis=0)
        v_lo, v_hi = _split(vw, _lo(vw.shape))
        vcat = jnp.concatenate([v_lo, v_hi], axis=0)
        for t in range(2):
            qa, qb = _split(q_ref[:, _tile(t)], lo)
            ps, dens, lses = [], [], []
            for h, qh in enumerate((qa, qb)):
                s = _dot_nt(qh, kw) + bias
                m = jnp.max(s, axis=-1, keepdims=True)
                if use_sink:
                    sk = sink_ref[2 * (pair0 + 2 * g + t) + h]
                    m = jnp.maximum(m, sk)
                e = jnp.exp(s - m)
                den = jnp.sum(e, axis=-1, keepdims=True)
                if use_sink:
                    den = den + jnp.exp(sk - m)
                ps.append(e.astype(BF))
                dens.append(den)
                lses.append(m + jnp.log(den))
            o = _dot(jnp.concatenate(ps, axis=1), vcat)
            o_ref[:, _tile(t)] = (o / jnp.where(lo, dens[0], dens[1])).astype(o_dtype)
            lse_ref[:, _tile(t)] = jnp.where(lo, lses[0], lses[1])

    g0 = pair0 // 2
    qspec = pl.BlockSpec((TQ, 2 * LANES), lambda g, r, i: (i, r * 7 + g0 + g))
    kspecs = _band_specs(L, d, R, TQ, PW, LANES, lambda g: K_T0 + g0 + g)
    vspecs = _band_specs(L, d, R, TQ, PW, LANES, lambda g: V_T0 + g0 + g)
    ospec = pl.BlockSpec((TQ, 2 * LANES), lambda g, r, i: (i, r * (npairs // 2) + g))
    bias = _band_bias(TQ, R)
    kw = dict(grid=(npairs // 2, d, L // TQ),
              in_specs=[pl.BlockSpec(memory_space=pltpu.SMEM), pl.BlockSpec(bias.shape, lambda g, r, i: (0, 0)),
                        qspec] + kspecs + vspecs,
              out_specs=[ospec, ospec],
              out_shape=[jax.ShapeDtypeStruct((L, d * ow), o_dtype), jax.ShapeDtypeStruct((L, d * ow), F32)])
    args = (sink, bias, pv, pv, pv, pv, pv, pv, pv)
    if gather:
        (o, lse), gathered = _pcall_behind(body, _gather_comm, gather, _gather_out_shape(gather),
                                           _gather_sems(len(gather)), name="banded_fwd_gather", **kw)(*args)
        return o.reshape(S, ow), lse.reshape(S, ow), gathered
    o, lse = _pcall(body, name="banded_fwd", **kw)(*args)
    return o.reshape(S, ow), lse.reshape(S, ow)


def banded_bwd_dq(proj, do, lse, delta, sink, *, d, R, TQ, pair0, npairs, use_sink, scatter=()):
    S = proj.shape[0]
    L = S // d
    pv = proj.reshape(L, d * PW)
    ow = npairs * LANES

    def body(sink_ref, bias_ref, q_ref, kp, kc, kn, vp, vc, vn, do_ref, lse_ref, delta_ref, dq_ref, dsink_ref):
        g, r, i = pl.program_id(0), pl.program_id(1), pl.program_id(2)

        @pl.when((r == 0) & (i == 0))
        def _():
            dsink_ref[...] = jnp.zeros_like(dsink_ref)

        bias = bias_ref[...] + _edge_bias(i, TQ, R, L)
        lo = _lo((TQ, LANES))
        kw = jnp.concatenate([kp[...], kc[...], kn[...]], axis=0)
        vw = jnp.concatenate([vp[...], vc[...], vn[...]], axis=0)
        k_lo, k_hi = _split(kw, _lo(kw.shape))
        kcat = jnp.concatenate([k_lo, k_hi], axis=0)
        for t in range(2):
            qa, qb = _split(q_ref[:, _tile(t)], lo)
            doa, dob = _split(do_ref[:, _tile(t)], lo)
            lse_t, delta_t = lse_ref[:, _tile(t)], delta_ref[:, _tile(t)]
            dss, dsk = [], []
            for h, (qh, doh) in enumerate(((qa, doa), (qb, dob))):
                lse_h, delta_h = _col(lse_t, h * HD), _col(delta_t, h * HD)
                pr = jnp.exp(_dot_nt(qh, kw) + bias - lse_h)
                dss.append((pr * (_dot_nt(doh, vw) - delta_h)).astype(BF))
                if use_sink:
                    psink = jnp.exp(sink_ref[2 * (pair0 + 2 * g + t) + h] - lse_h)
                    dsk.append(-jnp.sum(psink * delta_h, axis=0, keepdims=True))
            dq_ref[:, _tile(t)] = _dot(jnp.concatenate(dss, axis=1), kcat).astype(BF)
            if use_sink:
                dsink_ref[:, _tile(t)] += jnp.where(_lo((8, LANES)), dsk[0], dsk[1])

    g0 = pair0 // 2
    qspec = pl.BlockSpec((TQ, 2 * LANES), lambda g, r, i: (i, r * 7 + g0 + g))
    kspecs = _band_specs(L, d, R, TQ, PW, LANES, lambda g: K_T0 + g0 + g)
    vspecs = _band_specs(L, d, R, TQ, PW, LANES, lambda g: V_T0 + g0 + g)
    ospec = pl.BlockSpec((TQ, 2 * LANES), lambda g, r, i: (i, r * (npairs // 2) + g))
    view = lambda a: a.reshape(L, d * a.shape[1])
    ispec = lambda a: pl.BlockSpec((TQ, 2 * LANES), lambda g, r, i: (i, r * (a.shape[1] // (2 * LANES)) + g))
    bias = _band_bias(TQ, R)
    kw = dict(grid=(npairs // 2, d, L // TQ),
              in_specs=[pl.BlockSpec(memory_space=pltpu.SMEM), pl.BlockSpec(bias.shape, lambda g, r, i: (0, 0)),
                        qspec] + kspecs + vspecs + [ispec(do), ispec(lse), ispec(delta)],
              out_specs=[ospec, pl.BlockSpec((8, 2 * LANES), lambda g, r, i: (g, 0))],
              out_shape=[jax.ShapeDtypeStruct((L, d * ow), BF),
                         jax.ShapeDtypeStruct((npairs // 2 * 8, 2 * LANES), F32)])
    args = (sink, bias, pv, pv, pv, pv, pv, pv, pv, view(do), view(lse), view(delta))
    if scatter:
        (dq, dsink), got = _pcall_behind(body, _scatter_comm, scatter, _scatter_out_shape(scatter),
                                         _scatter_sems(len(scatter)), name="banded_bwd_dq_scatter", **kw)(*args)
        return dq.reshape(S, ow), dsink, got
    dq, dsink = _pcall(body, name="banded_bwd_dq", **kw)(*args)
    return dq.reshape(S, ow), dsink


def banded_bwd_dkv(proj, do, lse, delta, *, d, R, TK, pair0, npairs):
    S = proj.shape[0]
    L = S // d
    pv = proj.reshape(L, d * PW)
    ow = npairs * LANES

    def body(bias_ref, k_ref, v_ref, qp, qc, qn, dop, doc, don, lp, lc, ln, dp_, dc_, dn_, dk_ref, dv_ref):
        j = pl.program_id(2)
        W = TK + 2 * R
        bias = bias_ref[...] + _edge_bias(j, TK, R, L)
        low = _lo((W, LANES))
        qw = jnp.concatenate([qp[...], qc[...], qn[...]], axis=0)
        dow = jnp.concatenate([dop[...], doc[...], don[...]], axis=0)
        lse_w = jnp.concatenate([lp[...], lc[...], ln[...]], axis=0)
        delta_w = jnp.concatenate([dp_[...], dc_[...], dn_[...]], axis=0)
        k, v = k_ref[...], v_ref[...]
        for t in range(2):
            qa, qb = _split(qw[:, _tile(t)], low)
            doa, dob = _split(dow[:, _tile(t)], low)
            lse_r, delta_r = lse_w[:, _tile(t)].T, delta_w[:, _tile(t)].T
            prs, dss = [], []
            for h, (qh, doh) in enumerate(((qa, doa), (qb, dob))):
                pr = jnp.exp(_dot_nt(k, qh) + bias - lse_r[h * HD:h * HD + 1, :])
                dss.append((pr * (_dot_nt(v, doh) - delta_r[h * HD:h * HD + 1, :])).astype(BF))
                prs.append(pr.astype(BF))
            dv_ref[:, _tile(t)] = _dot(jnp.concatenate(prs, axis=1), jnp.concatenate([doa, dob], axis=0)).astype(BF)
            dk_ref[:, _tile(t)] = _dot(jnp.concatenate(dss, axis=1), jnp.concatenate([qa, qb], axis=0)).astype(BF)

    g0 = pair0 // 2
    kspec = pl.BlockSpec((TK, LANES), lambda g, r, j: (j, r * 14 + K_T0 + g0 + g))
    vspec = pl.BlockSpec((TK, LANES), lambda g, r, j: (j, r * 14 + V_T0 + g0 + g))
    qspecs = _band_specs(L, d, R, TK, PW, 2 * LANES, lambda g: g0 + g)
    ispecs = lambda a: _band_specs(L, d, R, TK, a.shape[1], 2 * LANES, lambda g: g)
    view = lambda a: a.reshape(L, d * a.shape[1])
    ospec = pl.BlockSpec((TK, 2 * LANES), lambda g, r, j: (j, r * (npairs // 2) + g))
    sd = jax.ShapeDtypeStruct((L, d * ow), BF)
    bias = _band_bias(TK, R)
    dk, dv = _pcall(body, name="banded_bwd_dkv", grid=(npairs // 2, d, L // TK),
                    in_specs=[pl.BlockSpec(bias.shape, lambda g, r, j: (0, 0)), kspec, vspec] + qspecs + ispecs(do)
                    + ispecs(lse) + ispecs(delta),
                    out_specs=[ospec, ospec], out_shape=[sd, sd])(
        bias, pv, pv, pv, pv, pv, *([view(do)] * 3), *([view(lse)] * 3), *([view(delta)] * 3))
    return dk.reshape(S, ow), dv.reshape(S, ow)


def _store_stats_t(ref, tile):
    t = tile.T
    ref[...] = jnp.zeros_like(ref)
    ref[0:1, :] = t[0:1, :]
    ref[1:2, :] = t[HD:HD + 1, :]


def flash_fwd(proj, bound, *, tq=FLASH_TQ, tk=1024):
    S = proj.shape[0]

    def body_general(q_ref, k_ref, v_ref, o_ref, lse_ref):
        lo = _lo((tq, LANES))
        qa, qb = _split(q_ref[...], lo)
        lov = _lo((tk, LANES))

        def step(j, carry):
            ma, la, mb, lb, acc = carry
            rows = pl.ds(pl.multiple_of(j * tk, tk), tk)
            k, v = k_ref[rows, :], v_ref[rows, :]
            outs = []
            for qh, m0, l0 in ((qa, ma, la), (qb, mb, lb)):
                s = _dot_nt(qh, k)
                m1 = jnp.maximum(m0, jnp.max(s, axis=-1, keepdims=True))
                al = jnp.exp2(m0 - m1)
                e = jnp.exp2(s - m1)
                outs.append((m1, al * l0 + jnp.sum(e, axis=-1, keepdims=True), al, e.astype(BF)))
            v_lo, v_hi = _split(v, lov)
            pvv = _dot(jnp.concatenate([outs[0][3], outs[1][3]], axis=1), jnp.concatenate([v_lo, v_hi], axis=0))
            acc = acc * jnp.where(lo, outs[0][2], outs[1][2]) + pvv
            return outs[0][0], outs[0][1], outs[1][0], outs[1][1], acc

        m_init = jnp.full((tq, 1), NEG, F32)
        l_init = jnp.zeros((tq, 1), F32)
        ma, la, mb, lb, acc = lax.fori_loop(0, S // tk, step,
                                            (m_init, l_init, m_init, l_init, jnp.zeros((tq, LANES), F32)))
        o_ref[...] = (acc / jnp.where(lo, la, lb)).astype(BF)
        _store_stats_t(lse_ref, jnp.where(lo, ma * LN2 + jnp.log(la), mb * LN2 + jnp.log(lb)))

    def body_plain(q_ref, k_ref, v_ref, o_ref, lse_ref):
        lo = _lo((tq, LANES))
        qa, qb = _split(q_ref[...], lo)
        lov = _lo((tk, LANES))
        one = jnp.ones((tk, LANES), BF)

        def step(j, carry):
            acc_a, acc_b = carry
            rows = pl.ds(pl.multiple_of(j * tk, tk), tk)
            k, v = k_ref[rows, :], v_ref[rows, :]
            ea = jnp.exp2(_dot_nt(qa, k)).astype(BF)
            eb = jnp.exp2(_dot_nt(qb, k)).astype(BF)
            acc_a = acc_a + _dot(ea, jnp.where(lov, v, one))
            acc_b = acc_b + _dot(eb, jnp.where(lov, one, v))
            return acc_a, acc_b

        z = jnp.zeros((tq, LANES), F32)
        acc_a, acc_b = lax.fori_loop(0, S // tk, step, (z, z))
        den = jnp.where(lo, pltpu.roll(acc_a, HD, 1), pltpu.roll(acc_b, HD, 1))
        o_ref[...] = (jnp.where(lo, acc_a, acc_b) / den).astype(BF)
        _store_stats_t(lse_ref, jnp.log(den))

    def body(bound_ref, q_ref, k_ref, v_ref, o_ref, lse_ref):
        small = bound_ref[0] <= MAX_PLAIN_SCORE

        @pl.when(small)
        def _():
            body_plain(q_ref, k_ref, v_ref, o_ref, lse_ref)

        @pl.when(jnp.logical_not(small))
        def _():
            body_general(q_ref, k_ref, v_ref, o_ref, lse_ref)

    ospec = pl.BlockSpec((tq, LANES), lambda p, i: (i, p))
    return _pcall(body, name="flash_fwd", grid=(N_PAIRS, S // tq),
                  in_specs=[pl.BlockSpec(memory_space=pltpu.SMEM), ospec,
                            pl.BlockSpec((S, LANES), lambda p, i: (0, K_T0 + p // 2)),
                            pl.BlockSpec((S, LANES), lambda p, i: (0, V_T0 + p // 2))],
                  out_specs=[ospec, pl.BlockSpec((None, None, 8, tq), lambda p, i: (p, i, 0, 0))],
                  out_shape=[jax.ShapeDtypeStruct((S, 768), BF), jax.ShapeDtypeStruct((N_PAIRS, S // tq, 8, tq), F32)])(
        bound.reshape(1), proj, proj, proj)


def flash_bwd(proj, do, lse_t, delta_t, *, tk=1024):
    S = proj.shape[0]
    nq, tq = lse_t.shape[1], lse_t.shape[3]

    def body(k_ref, v_ref, q_ref, do_ref, lse_ref, delta_ref, dk_ref, dv_ref, dqt_ref):
        @pl.when(pl.program_id(1) == 0)
        def _():
            dqt_ref[...] = jnp.zeros_like(dqt_ref)

        k, v = k_ref[...], v_ref[...]
        lo = _lo((tq, LANES))
        k_lo, k_hi = _split(k.astype(F32), _lo((tk, LANES)))
        kt = jnp.concatenate([k_lo.T, k_hi.T], axis=1).astype(BF)

        def step(i, carry):
            dk, dv = carry
            rows = pl.ds(pl.multiple_of(i * tq, tq), tq)
            qa, qb = _split(q_ref[rows, :], lo)
            doa, dob = _split(do_ref[rows, :], lo)
            lse_i, delta_i = lse_ref[i] * LOG2E, delta_ref[i]
            prs, dss = [], []
            for h, (qh, doh) in enumerate(((qa, doa), (qb, dob))):
                pr = jnp.exp2(_dot_nt(k, qh) - lse_i[h:h + 1, :])
                dss.append((pr * (_dot_nt(v, doh) - delta_i[h:h + 1, :])).astype(BF))
                prs.append(pr.astype(BF))
            dv = dv + _dot(jnp.concatenate(prs, axis=1), jnp.concatenate([doa, dob], axis=0))
            dk = dk + _dot(jnp.concatenate(dss, axis=1), jnp.concatenate([qa, qb], axis=0))
            dqt_ref[i] += _dot(kt, jnp.concatenate(dss, axis=0))
            return dk, dv

        z = jnp.zeros((tk, LANES), F32)
        dk, dv = lax.fori_loop(0, nq, step, (z, z))
        dk_ref[...] = dk.astype(BF)
        dv_ref[...] = dv.astype(BF)

    ospec = pl.BlockSpec((tk, LANES), lambda p, j: (j, p))
    stat = pl.BlockSpec((None, nq, 8, tq), lambda p, j: (p, 0, 0, 0))
    whole = lambda: pl.BlockSpec((S, LANES), lambda p, j: (0, p), pipeline_mode=pl.Buffered(1))
    sd = jax.ShapeDtypeStruct((S, 768), BF)
    dk, dv, dqt = _pcall(body, name="flash_bwd", grid=(N_PAIRS, S // tk),
                         in_specs=[pl.BlockSpec((tk, LANES), lambda p, j: (j, K_T0 + p // 2)),
                                   pl.BlockSpec((tk, LANES), lambda p, j: (j, V_T0 + p // 2)), whole(), whole(),
                                   stat, stat],
                         out_specs=[ospec, ospec,
                                    pl.BlockSpec((None, nq, LANES, tq), lambda p, j: (p, 0, 0, 0),
                                                 pipeline_mode=pl.Buffered(1))],
                         out_shape=[sd, sd, jax.ShapeDtypeStruct((N_PAIRS, nq, LANES, tq), F32)])(
        proj, proj, proj, do, lse_t, delta_t)
    dq = jnp.transpose(dqt, (1, 3, 0, 2)).reshape(S, 768).astype(BF)
    return dq, dk, dv


def mem_fwd(proj, mkv, *, tq=512):
    S = proj.shape[0]

    def body(q_ref, km_ref, vm_ref, o_ref, lse_ref):
        lo = _lo((tq, LANES))
        lov = _lo((N_MEM, LANES))
        for t in range(2):
            qa, qb = _split(q_ref[:, _tile(t)], lo)
            km, vm = km_ref[:, _tile(t)], vm_ref[:, _tile(t)]
            ps, dens, lses = [], [], []
            for qh in (qa, qb):
                s = _dot_nt(qh, km)
                m = jnp.max(s, axis=-1, keepdims=True)
                e = jnp.exp(s - m)
                den = jnp.sum(e, axis=-1, keepdims=True)
                ps.append(e.astype(BF))
                dens.append(den)
                lses.append(m + jnp.log(den))
            v_lo, v_hi = _split(vm, lov)
            o = _dot(jnp.concatenate(ps, axis=1), jnp.concatenate([v_lo, v_hi], axis=0))
            o_ref[:, _tile(t)] = (o / jnp.where(lo, dens[0], dens[1])).astype(BF)
            lse_ref[:, _tile(t)] = jnp.where(lo, lses[0], lses[1])

    ospec = pl.BlockSpec((tq, 256), lambda i: (i, 0))
    return _pcall(body, name="mem_fwd", grid=(S // tq,),
                  in_specs=[pl.BlockSpec((tq, 256), lambda i: (i, M_T0 // 2)),
                            pl.BlockSpec((N_MEM, 256), lambda i: (0, 0)), pl.BlockSpec((N_MEM, 256), lambda i: (0, 1))],
                  out_specs=[ospec, ospec],
                  out_shape=[jax.ShapeDtypeStruct((S, 256), BF), jax.ShapeDtypeStruct((S, 256), F32)])(proj, mkv, mkv)


def mem_bwd(proj, mkv, dcat, lse, delta, *, tq=512):
    S = proj.shape[0]

    def body(q_ref, km_ref, vm_ref, do_ref, lse_ref, delta_ref, dq_ref, dkm_ref, dvm_ref):
        @pl.when(pl.program_id(0) == 0)
        def _():
            dkm_ref[...] = jnp.zeros_like(dkm_ref)
            dvm_ref[...] = jnp.zeros_like(dvm_ref)

        lo = _lo((tq, LANES))
        lov = _lo((N_MEM, LANES))
        for t in range(2):
            qa, qb = _split(q_ref[:, _tile(t)], lo)
            doa, dob = _split(do_ref[:, _tile(t)], lo)
            km, vm = km_ref[:, _tile(t)], vm_ref[:, _tile(t)]
            lse_t, delta_t = lse_ref[:, _tile(t)], delta_ref[:, _tile(t)]
            prs, dss = [], []
            for h, (qh, doh) in enumerate(((qa, doa), (qb, dob))):
                pr = jnp.exp(_dot_nt(qh, km) - _col(lse_t, h * HD))
                dss.append(pr * (_dot_nt(doh, vm) - _col(delta_t, h * HD)))
                prs.append(pr)
            k_lo, k_hi = _split(km, lov)
            dq_ref[:, _tile(t)] = _dot(jnp.concatenate(dss, axis=1).astype(BF), jnp.concatenate([k_lo, k_hi], axis=0))
            dvm_ref[:, _tile(t)] += _dot(jnp.concatenate(prs, axis=0).T.astype(BF), jnp.concatenate([doa, dob], axis=0))
            dkm_ref[:, _tile(t)] += _dot(jnp.concatenate(dss, axis=0).T.astype(BF), jnp.concatenate([qa, qb], axis=0))

    ospec = pl.BlockSpec((tq, 256), lambda i: (i, 0))
    msp = pl.BlockSpec((N_MEM, 256), lambda i: (0, 0))
    md = jax.ShapeDtypeStruct((N_MEM, 256), F32)
    dq, dkm, dvm = _pcall(body, name="mem_bwd", grid=(S // tq,),
                          in_specs=[pl.BlockSpec((tq, 256), lambda i: (i, M_T0 // 2)), msp,
                                    pl.BlockSpec((N_MEM, 256), lambda i: (0, 1)),
                                    pl.BlockSpec((tq, 256), lambda i: (i, 3)), ospec,
                                    pl.BlockSpec((tq, 256), lambda i: (i, 3))],
                          out_specs=[ospec, msp, msp], out_shape=[jax.ShapeDtypeStruct((S, 256), F32), md, md])(
        proj, mkv, mkv, dcat, lse, delta)
    return dq, jnp.concatenate([dkm, dvm], axis=1)


def combine_fwd(os_, lses, *, tm=512):
    S = os_[0].shape[0]

    def body(o0, o1, o2, l0, l1, l2, tok_ref):
        ls = [l0[...], l1[...], l2[...]]
        m = jnp.maximum(jnp.maximum(ls[0], ls[1]), ls[2])
        es = [jnp.exp(l - m) for l in ls]
        den = es[0] + es[1] + es[2]
        for g, o in enumerate((o0, o1, o2)):
            tok_ref[:, 256 * g:256 * (g + 1)] = (o[...] * (es[g] / den)).astype(BF)

    sp = pl.BlockSpec((tm, 256), lambda i: (i, 0))
    return _pcall(body, name="combine_fwd", grid=(S // tm,), in_specs=[sp] * 6,
                  out_specs=pl.BlockSpec((tm, 768), lambda i: (i, 0)),
                  out_shape=jax.ShapeDtypeStruct((S, 768), BF))(*os_, *lses)


def combine_bwd(dcat, os_, lses, *, tm=512):
    S = dcat.shape[0]

    def body(dt_ref, o0, o1, o2, l0, l1, l2, do0, do1, do2, de0, de1, de2):
        ls = [l0[...], l1[...], l2[...]]
        m = jnp.maximum(jnp.maximum(ls[0], ls[1]), ls[2])
        es = [jnp.exp(l - m) for l in ls]
        den = es[0] + es[1] + es[2]
        alphas = [e / den for e in es]
        lo = _lo((tm, LANES))
        dts = [dt_ref[:, 256 * g:256 * (g + 1)].astype(F32) for g in range(3)]
        dal = []
        for g, o in enumerate((o0, o1, o2)):
            pr = dts[g] * o[...]
            dal.append(jnp.concatenate([_half_sum(pr[:, _tile(0)]), _half_sum(pr[:, _tile(1)])], axis=1))
        mix = alphas[0] * dal[0] + alphas[1] * dal[1] + alphas[2] * dal[2]
        for g, (do_ref, de_ref) in enumerate(((do0, de0), (do1, de1), (do2, de2))):
            do_ref[...] = (dts[g] * alphas[g]).astype(BF)
            de_ref[...] = alphas[g] * mix

    sp = pl.BlockSpec((tm, 256), lambda i: (i, 0))
    outs = _pcall(body, name="combine_bwd", grid=(S // tm,),
                  in_specs=[pl.BlockSpec((tm, 768), lambda i: (i, 0))] + [sp] * 6, out_specs=[sp] * 6,
                  out_shape=[jax.ShapeDtypeStruct((S, 256), BF)] * 3 + [jax.ShapeDtypeStruct((S, 256), F32)] * 3)(
        dcat, *os_, *lses)
    return outs[:3], outs[3:]


def _coords():
    return lax.axis_index("x"), lax.axis_index("y"), lax.axis_index("c")


def _other_chips(x, y):
    return [(1 - x, y), (x, 1 - y), (1 - x, 1 - y)]


HBM_SPEC = pl.BlockSpec(memory_space=pltpu.HBM)


def _gather_comm(ins, outs, send, recv, lsem, start):
    x, y, c = _coords()
    me = 2 * x + y
    for a in range(len(ins)):
        local = pltpu.make_async_copy(ins[a], outs[a].at[me], lsem.at[a])
        if start:
            local.start()
        for j, (px, py) in enumerate(_other_chips(x, y)):
            sems = dict(send_sem=send.at[3 * a + j], recv_sem=recv.at[3 * a + j], device_id=(px, py, c),
                        device_id_type=MESH)
            cp = pltpu.make_async_remote_copy(src_ref=ins[a], dst_ref=outs[a].at[me], **sems)
            if start:
                cp.start()
            else:
                pltpu.make_async_remote_copy(src_ref=ins[a], dst_ref=outs[a].at[2 * px + py], **sems).wait_recv()
                cp.wait_send()
        if not start:
            local.wait()


def _gather_out_shape(shards):
    return [jax.ShapeDtypeStruct((4,) + s.shape, s.dtype) for s in shards]


def _gather_sems(n):
    return [pltpu.SemaphoreType.DMA((3 * n,)), pltpu.SemaphoreType.DMA((3 * n,)), pltpu.SemaphoreType.DMA((n,))]


def gather_shards(shards):
    n = len(shards)

    def body(*refs):
        _gather_comm(refs[:n], refs[n:2 * n], *refs[2 * n:], start=True)
        _gather_comm(refs[:n], refs[n:2 * n], *refs[2 * n:], start=False)

    return pl.pallas_call(body, name="gather_shards", in_specs=[HBM_SPEC] * n, out_specs=[HBM_SPEC] * n,
                          out_shape=_gather_out_shape(shards), scratch_shapes=_gather_sems(n))(*shards)


def _scatter_comm(ins, outs, send, recv, start):
    x, y, c = _coords()
    for a in range(len(ins)):
        for j, (px, py) in enumerate(_other_chips(x, y)):
            cp = pltpu.make_async_remote_copy(src_ref=ins[a].at[2 * px + py], dst_ref=outs[a].at[j],
                                              send_sem=send.at[3 * a + j], recv_sem=recv.at[3 * a + j],
                                              device_id=(px, py, c), device_id_type=MESH)
            if start:
                cp.start()
            else:
                cp.wait_recv()
                cp.wait_send()


def _scatter_out_shape(parts):
    return [jax.ShapeDtypeStruct((3,) + p.shape[1:], p.dtype) for p in parts]


def _scatter_sems(n):
    return [pltpu.SemaphoreType.DMA((3 * n,)), pltpu.SemaphoreType.DMA((3 * n,))]


def scatter_grads(parts):
    n = len(parts)

    def body(*refs):
        _scatter_comm(refs[:n], refs[n:2 * n], *refs[2 * n:], start=True)
        _scatter_comm(refs[:n], refs[n:2 * n], *refs[2 * n:], start=False)

    return pl.pallas_call(body, name="scatter_grads", in_specs=[HBM_SPEC] * n, out_specs=[HBM_SPEC] * n,
                          out_shape=_scatter_out_shape(parts), scratch_shapes=_scatter_sems(n))(*parts)


def sibling_swap(arrs):
    n = len(arrs)

    def body(*refs):
        ins, outs = refs[:n], refs[n:2 * n]
        send, recv = refs[2 * n:]
        x, y, c = _coords()
        cps = []
        for a in range(n):
            cp = pltpu.make_async_remote_copy(src_ref=ins[a], dst_ref=outs[a], send_sem=send.at[a], recv_sem=recv.at[a],
                                              device_id=(x, y, 1 - c), device_id_type=MESH)
            cp.start()
            cps.append(cp)
        for cp in cps:
            cp.wait_recv()
        for cp in cps:
            cp.wait_send()

    return pl.pallas_call(
        body, name="sibling_swap", in_specs=[HBM_SPEC] * n, out_specs=[HBM_SPEC] * n,
        out_shape=[jax.ShapeDtypeStruct(a.shape, a.dtype) for a in arrs],
        scratch_shapes=[pltpu.SemaphoreType.DMA((n,)), pltpu.SemaphoreType.DMA((n,))])(*arrs)


def allsum_small(v):
    rows = v.shape[0]

    def body(v_ref, tot_ref, gath_ref, send, recv):
        x, y, c = _coords()
        me = 4 * x + 2 * y + c
        gath_ref[me] = v_ref[...]
        cps = []
        for k in range(1, 8):
            fx, fy, fc = (k >> 2) & 1, (k >> 1) & 1, k & 1
            peer = (1 - x if fx else x, 1 - y if fy else y, 1 - c if fc else c)
            cp = pltpu.make_async_remote_copy(src_ref=v_ref, dst_ref=gath_ref.at[me], send_sem=send.at[k - 1],
                                              recv_sem=recv.at[k - 1], device_id=peer, device_id_type=MESH)
            cp.start()
            cps.append(cp)
        for cp in cps:
            cp.wait_recv()
        for cp in cps:
            cp.wait_send()
        tot = gath_ref[0]
        for k in range(1, 8):
            tot = tot + gath_ref[k]
        tot_ref[...] = tot

    vm = pl.BlockSpec(memory_space=pltpu.VMEM)
    tot, _ = pl.pallas_call(
        body, name="allsum_small", in_specs=[vm], out_specs=[vm, vm],
        out_shape=[jax.ShapeDtypeStruct((rows, LANES), F32), jax.ShapeDtypeStruct((8, rows, LANES), F32)],
        scratch_shapes=[pltpu.SemaphoreType.DMA((7,)), pltpu.SemaphoreType.DMA((7,))])(v)
    return tot


def sum_parts(own, recv, *, tr=256):
    R, C = own.shape
    tr = min(tr, R)

    def body(o_ref, r_ref, out_ref):
        out_ref[...] = ((o_ref[...] + r_ref[0].astype(F32)) + r_ref[1].astype(F32)) + r_ref[2].astype(F32)

    sp = pl.BlockSpec((tr, C), lambda i: (i, 0))
    return _pcall(body, name="sum_parts", grid=(R // tr,),
                  in_specs=[sp, pl.BlockSpec((3, tr, C), lambda i: (0, i, 0))], out_specs=sp,
                  out_shape=jax.ShapeDtypeStruct((R, C), F32))(own, recv)


def adamw(w, ga, gb, m, v, *, tr=256):
    R, C = w.shape
    tr = min(tr, R)
    two = gb is not None

    def body(*refs):
        if two:
            w_ref, ga_ref, gb_ref, m_ref, v_ref, g_out, d_out, m_out, v_out = refs
            g = ga_ref[...] + gb_ref[...]
        else:
            w_ref, ga_ref, m_ref, v_ref, g_out, d_out, m_out, v_out = refs
            g = ga_ref[...]
        mn = B1 * m_ref[...] + (1.0 - B1) * g
        vn = B2 * v_ref[...] + (1.0 - B2) * (g * g)
        m_hat = mn / (1.0 - B1 ** STEP)
        v_hat = vn / (1.0 - B2 ** STEP)
        g_out[...] = g
        d_out[...] = -LR * (m_hat / (jnp.sqrt(v_hat) + AEPS) + WD * w_ref[...])
        m_out[...] = mn
        v_out[...] = vn

    sp = pl.BlockSpec((tr, C), lambda i: (i, 0))
    args = [w, ga, gb, m, v] if two else [w, ga, m, v]
    sd = jax.ShapeDtypeStruct((R, C), F32)
    return _pcall(body, name="adamw", grid=(R // tr,), in_specs=[sp] * len(args), out_specs=[sp] * 4,
                  out_shape=[sd] * 4)(*args)


def _rope_tables(S):
    def inv_freq(n_dims, theta):
        return theta ** (-(jnp.arange(0, n_dims, 2, dtype=jnp.float32) / n_dims))

    pos = lax.broadcasted_iota(jnp.int32, (S, LANES), 0)
    d = lax.broadcasted_iota(jnp.int32, (S, LANES), 1) % HD
    d1 = lax.iota(jnp.int32, LANES) % HD
    ang = pos.astype(F32) * inv_freq(HD // 4, ROPE_THETA)[d1 % 8][None, :]
    sin = jnp.sin(ang)
    partial = (jnp.where(d < 16, jnp.cos(ang), 1.0), jnp.where((d >= 8) & (d < 16), sin, 0.0),
               jnp.where(d < 8, -sin, 0.0))
    grid_pos = jnp.where(d < 32, pos // GRID_W, pos % GRID_W)
    ang = grid_pos.astype(F32) * inv_freq(HD // 2, AXIAL_THETA)[d1 % 16][None, :]
    sin = jnp.sin(ang)
    axial = (jnp.cos(ang), jnp.where(d % 32 >= 16, sin, 0.0), jnp.where(d % 32 < 16, -sin, 0.0))
    return partial, axial


def _pad_w_in(w):
    cols = [w[:, :768]]
    for base in (768, 960):
        for g in range(3):
            kg = w[:, base + g * HD:base + (g + 1) * HD]
            cols += [kg, kg]
    cols.append(w[:, 1152:])
    return jnp.concatenate(cols, axis=1)


def _unpad_dw_in(dw):
    cols = [dw[:, :768]]
    for t0 in (K_T0, V_T0):
        for g in range(3):
            b = (t0 + g) * LANES
            cols.append(dw[:, b:b + HD] + dw[:, b + HD:b + LANES])
    cols.append(dw[:, M_T0 * LANES:])
    return jnp.concatenate(cols, axis=1)


def _grad_slices(dW_in, dW_mkv, dW_o, dW_gu, dW_d):
    return [None if dW_in is None else jnp.transpose(dW_in.reshape(D, 4, IN_W // 4), (1, 0, 2)),
            dW_mkv.reshape(4, D // 4, 512), dW_o.reshape(4, D // 4, D),
            jnp.transpose(dW_gu.reshape(D, 4, 2 * DFF // 4), (1, 0, 2)), dW_d.reshape(4, DFF // 4, D)]


def _fold(t):
    return t[..., :HD] + t[..., HD:]


def kernel(x, mem, mem_norm_g, w_in, w_mem_kv, w_o, g_mix_pre, g_mix_post, attn_sink, qk_norm_g, w_gate_up, w_down, g_ffn_pre, g_ffn_post, loss_target, m_mem_norm_g, m_w_in, m_w_mem_kv, m_w_o, m_g_mix_pre, m_g_mix_post, m_attn_sink, m_qk_norm_g, m_w_gate_up, m_w_down, m_g_ffn_pre, m_g_ffn_post, v_mem_norm_g, v_w_in, v_w_mem_kv, v_w_o, v_g_mix_pre, v_g_mix_post, v_attn_sink, v_qk_norm_g, v_w_gate_up, v_w_down, v_g_ffn_pre, v_g_ffn_post):
    S = x.shape[1]
    depth = w_in.shape[0]
    xs, memx, tgt = x[0], mem[0], loss_target[0]
    tab_p, tab_a = _rope_tables(S)
    row = lambda a: a.reshape(1, -1)

    shards_bf = [w.astype(BF) for w in (w_in, w_mem_kv, w_o, w_gate_up, w_down)]
    layer_shards = lambda i: [s[i] for s in shards_bf]
    W_in, W_mkv, W_o, W_g, W_u, W_gu, W_d = ([None] * depth for _ in range(7))

    def set_weights(i, gathered):
        if len(gathered) == 5:
            W_in[i] = jnp.concatenate([gathered[0][s] for s in range(4)], axis=1)
        gm, go, gg, gd = gathered[-4:]
        W_mkv[i] = jnp.concatenate([gm[s] for s in range(4)], axis=0)
        W_o[i] = jnp.concatenate([go[s] for s in range(4)], axis=0)
        W_g[i] = jnp.concatenate([gg[0], gg[1]], axis=1)
        W_u[i] = jnp.concatenate([gg[2], gg[3]], axis=1)
        W_gu[i] = jnp.concatenate([gg[0], gg[2], gg[1], gg[3]], axis=1)
        W_d[i] = jnp.concatenate([gd[s] for s in range(4)], axis=0)

    (g_in0,) = gather_shards(layer_shards(0)[:1])
    W_in[0] = jnp.concatenate([g_in0[s] for s in range(4)], axis=1)
    mem_g = row(mem_norm_g)
    zero_sink = jnp.zeros((12,), F32)
    qkg = jnp.pad(jnp.concatenate([qk_norm_g[0], qk_norm_g[0]], axis=1), ((0, 6), (0, 0)))
    no_qkg = jnp.zeros((8, LANES), F32)

    saved = []
    cur = xs
    for i in range(depth):
        kind = i % 3
        wp = _pad_w_in(W_in[i])
        sv = dict(x=cur, wp=wp)
        if kind == 1:
            h1, proj, raw, nrm = inproj_fwd(cur, row(g_mix_pre[i]), wp, tab_a, qkg, axial=True)
            sv["raw"] = raw
        else:
            h1, proj = inproj_fwd(cur, row(g_mix_pre[i]), wp, tab_p, no_qkg, axial=False)
        if kind == 0:
            res = banded_fwd(proj, attn_sink[i // 3], d=1, R=A_RADIUS, TQ=_band_tile(A_RADIUS, S), pair0=0, npairs=6,
                             use_sink=True, o_dtype=BF, gather=layer_shards(0)[1:] if i == 0 else ())
            tok, lse = res[:2]
            if i == 0:
                set_weights(0, res[2])
        elif kind == 1:
            bound = jnp.sqrt(jnp.max(nrm[0]) * jnp.max(nrm[1])) * LN2
            tok, lse = flash_fwd(proj, bound)
        else:
            os_, lses = [], []
            for g, (window, dil) in enumerate(C_GROUPS):
                rad = window // (2 * dil)
                o_g, l_g = banded_fwd(proj, zero_sink, d=dil, R=rad, TQ=_band_tile(rad, S // dil), pair0=2 * g, npairs=2,
                                      use_sink=False, o_dtype=F32)
                os_.append(o_g)
                lses.append(l_g)
            tok = combine_fwd(os_, lses)
            sv["os"], lse = os_, lses
        mem_n, mkv = norm_mm(memx, mem_g, W_mkv[i], tm=N_MEM)
        mo, mlse = mem_fwd(proj, mkv)
        o, x2 = mm_norm_res([tok, mo], W_o[i], row(g_mix_post[i]), cur, tm=1024)
        (h2, gate, up, act), gathered = ffn_up_fwd(x2, row(g_ffn_pre[i]), W_gu[i],
                                                   layer_shards(i + 1) if i + 1 < depth else ())
        if i + 1 < depth:
            set_weights(i + 1, gathered)
        f, x3 = mm_norm_res([act], W_d[i], row(g_ffn_post[i]), x2)
        sv.update(h1=h1, proj=proj, lse=lse, mem_n=mem_n, mkv=mkv, mlse=mlse, tok=tok, mo=mo, o=o, x2=x2, h2=h2, gate=gate,
                  up=up, act=act, f=f)
        saved.append(sv)
        cur = x3

    dcur, loss_vec = loss_bwd(cur, tgt)

    grad_parts, grad_recv = [None] * depth, [None] * depth
    dg_pre, dg_post, dg_fpre, dg_fpost = [None] * depth, [None] * depth, [None] * depth, [None] * depth
    dg_mem = jnp.zeros((1, D), F32)
    dsinks, dqk = {}, None
    for i in reversed(range(depth)):
        sv = saved[i]
        kind = i % 3
        proj = sv["proj"]
        pending = [p.astype(BF) for p in grad_parts[i + 1]] if i + 1 < depth else ()
        (df, dgate, dup, dg_fpost[i]), got = normbwd_mm_swiglu(dcur, sv["f"], row(g_ffn_post[i]), W_d[i], sv["gate"],
                                                               sv["up"], pending)
        if i + 1 < depth:
            grad_recv[i + 1] = got
        dx2, dg_fpre[i] = mm_nt_normbwd_res([(dgate, W_g[i]), (dup, W_u[i])], sv["x2"], row(g_ffn_pre[i]), dcur)
        dW_d = mm_acc(sv["act"], df, tk=1408, tn=D, ts=2048)
        dW_gu = jnp.concatenate([mm_acc(sv["h2"], dgate, tk=D, tn=1408, ts=2048),
                                 mm_acc(sv["h2"], dup, tk=D, tn=1408, ts=2048)], axis=1)
        do, dcat, delta, dg_post[i], *delta_t = normbwd_mm_cat(dx2, sv["o"], row(g_mix_post[i]), W_o[i], sv["tok"],
                                                               sv["mo"], stats_t=kind == 1)
        dW_o = jnp.concatenate([mm_acc(sv["tok"], do, tk=768, tn=D, ts=2048), mm_acc(sv["mo"], do, tk=256, tn=D, ts=2048)],
                               axis=0)
        dqm, dmkv = mem_bwd(proj, sv["mkv"], dcat, sv["mlse"], delta)
        dmkv = dmkv.astype(BF)
        (dgm,) = mm_nt_normbwd_res([(dmkv, W_mkv[i])], memx, mem_g, None, tm=N_MEM)
        dg_mem = dg_mem + dgm
        dW_mkv = mm_acc(sv["mem_n"], dmkv, tk=D, tn=512, ts=N_MEM)
        if kind == 0:
            sink = attn_sink[i // 3]
            args = dict(d=1, R=A_RADIUS, pair0=0, npairs=6)
            tile = _band_tile(A_RADIUS, S)
            early = [p.astype(BF) for p in _grad_slices(None, dW_mkv, dW_o, dW_gu, dW_d)[1:]] if i == 0 else ()
            res = banded_bwd_dq(proj, dcat, sv["lse"], delta, sink, TQ=tile, use_sink=True, scatter=early, **args)
            dq, dsk = res[:2]
            if i == 0:
                early_recv = res[2]
            dkp, dvp = banded_bwd_dkv(proj, dcat, sv["lse"], delta, TK=tile, **args)
            dsinks[i // 3] = dsk.reshape(3, 8, 2, 2, HD)[:, 0, :, :, 0].reshape(12)
        elif kind == 1:
            dq, dkp, dvp = flash_bwd(proj, dcat, sv["lse"], delta_t[0])
        else:
            dos, des = combine_bwd(dcat, sv["os"], sv["lse"])
            dqs, dks, dvs = [], [], []
            for g, (window, dil) in enumerate(C_GROUPS):
                rad = window // (2 * dil)
                args = dict(d=dil, R=rad, pair0=2 * g, npairs=2)
                tile = _band_tile(rad, S // dil)
                dq_g, _ = banded_bwd_dq(proj, dos[g], sv["lse"][g], des[g], zero_sink, TQ=tile, use_sink=False, **args)
                dk_g, dv_g = banded_bwd_dkv(proj, dos[g], sv["lse"][g], des[g], TK=tile, **args)
                dqs.append(dq_g)
                dks.append(dk_g)
                dvs.append(dv_g)
            dq, dkp, dvp = (jnp.concatenate(t, axis=1) for t in (dqs, dks, dvs))
        if kind == 1:
            dcur, dproj, dg_pre[i], dqk_t = inproj_bwd(dq, dkp, dvp, dqm, tab_a, sv["raw"], qkg, sv["wp"], sv["x"],
                                                       row(g_mix_pre[i]), dx2, axial=True)
            dqk = _fold(dqk_t[:2]).reshape(1, 2, HD)
        else:
            dcur, dproj, dg_pre[i], _ = inproj_bwd(dq, dkp, dvp, dqm, tab_p, proj, no_qkg, sv["wp"],
                                                   sv["x"], row(g_mix_pre[i]), dx2, axial=False)
        dW_in = _unpad_dw_in(mm_acc(sv["h1"], dproj, tk=D, tn=PW, ts=2048))
        grad_parts[i] = _grad_slices(dW_in, dW_mkv, dW_o, dW_gu, dW_d)
    grad_recv[0] = list(scatter_grads([grad_parts[0][0].astype(BF)])) + list(early_recv)

    x_i, y_i, _ = _coords()
    me = 2 * x_i + y_i
    big = [(w_in, m_w_in, v_w_in), (w_mem_kv, m_w_mem_kv, v_w_mem_kv), (w_o, m_w_o, v_w_o),
           (w_gate_up, m_w_gate_up, v_w_gate_up), (w_down, m_w_down, v_w_down)]
    parts = []
    for a, (w, _, _) in enumerate(big):
        C = w.shape[-1]
        own = jnp.stack([lax.dynamic_index_in_dim(grad_parts[l][a], me, 0, keepdims=False) for l in range(depth)])
        rc = jnp.stack([grad_recv[l][a] for l in range(depth)], axis=1)
        parts.append(sum_parts(own.reshape(-1, C), rc.reshape(3, -1, C)))
    sibs = sibling_swap(parts)
    big_out = []
    for (w, m, v), pa, pb in zip(big, parts, sibs):
        C = w.shape[-1]
        outs = adamw(w.reshape(-1, C), pa, pb, m.reshape(-1, C), v.reshape(-1, C))
        big_out.append([o.reshape(w.shape) for o in outs])

    small_w = [mem_norm_g, g_mix_pre, g_mix_post, attn_sink, qk_norm_g, g_ffn_pre, g_ffn_post]
    small_m = [m_mem_norm_g, m_g_mix_pre, m_g_mix_post, m_attn_sink, m_qk_norm_g, m_g_ffn_pre, m_g_ffn_post]
    small_v = [v_mem_norm_g, v_g_mix_pre, v_g_mix_post, v_attn_sink, v_qk_norm_g, v_g_ffn_pre, v_g_ffn_post]
    small_g = [dg_mem.reshape(D), jnp.concatenate(dg_pre, axis=0), jnp.concatenate(dg_post, axis=0),
               jnp.stack([dsinks[k] for k in sorted(dsinks)]), dqk, jnp.concatenate(dg_fpre, axis=0),
               jnp.concatenate(dg_fpost, axis=0)]
    sizes = [a.size for a in small_w]
    total = sum(sizes)
    rows_s = -(-(total + LANES) // (8 * LANES)) * 8

    def pack(arrs, extra=None):
        flat = jnp.concatenate([a.reshape(-1).astype(F32) for a in arrs])
        flat = jnp.pad(flat, (0, rows_s * LANES - LANES - total))
        tail = jnp.zeros((LANES,), F32) if extra is None else extra.reshape(LANES)
        return jnp.concatenate([flat, tail]).reshape(rows_s, LANES)

    tot = allsum_small(pack(small_g, loss_vec))
    loss = jnp.sum(tot[rows_s - 1])
    s_out = adamw(pack(small_w), tot, None, pack(small_m), pack(small_v))

    def unpack(buf):
        flat = buf.reshape(-1)
        out, off = [], 0
        for a, n in zip(small_w, sizes):
            out.append(flat[off:off + n].reshape(a.shape))
            off += n
        return out

    sg, sd_, sm, sv_ = (unpack(b) for b in s_out)

    def ordered(k):
        sm_ = (sg, sd_, sm, sv_)[k]
        b = [bo[k] for bo in big_out]
        return [sm_[0], b[0], b[1], b[2], sm_[1], sm_[2], sm_[3], sm_[4], b[3], b[4], sm_[5], sm_[6]]

    dx_out = dcur.reshape(1, S, D)
    return (loss, dx_out, *ordered(0), *ordered(1), *ordered(2), *ordered(3))
```

```python
import jax
import jax.numpy as jnp
from jax import lax
from jax.experimental import pallas as pl
from jax.experimental.pallas import tpu as pltpu

F32 = jnp.float32
BF = jnp.bfloat16

D = 1024
HD = 64
LANES = 128
N_PAIRS = 6
DFF = 2816
IN_W = 1408
PW = 14 * LANES
K_T0, V_T0, M_T0 = 6, 9, 12
EPS = 1e-6
SCALE = HD ** -0.5
NEG = -1e30
LOG2E = 1.4426950408889634
LN2 = 0.6931471805599453
MAX_PLAIN_SCORE = 40.0
FLASH_TQ = 1024
ROPE_THETA = 500000.0
AXIAL_THETA = 10000.0
GRID_W = 64
A_RADIUS = 128
C_GROUPS = ((128, 1), (512, 4), (2048, 16))
N_MEM = 256
LR, B1, B2, AEPS, WD, STEP = 0.001, 0.9, 0.999, 1e-08, 0.01, 10
VMEM_LIMIT = 56 * 1024 * 1024
MESH = pl.DeviceIdType.MESH


def _pcall(body, *, name, grid, in_specs, out_specs, out_shape, scratch=()):
    return pl.pallas_call(
        body, name=name, grid=grid, in_specs=in_specs, out_specs=out_specs, out_shape=out_shape,
        scratch_shapes=scratch,
        compiler_params=pltpu.CompilerParams(dimension_semantics=("arbitrary",) * len(grid),
                                             vmem_limit_bytes=VMEM_LIMIT))


def _pcall_behind(body, comm, arrays, comm_out_shape, comm_sems, *, name, grid, in_specs, out_specs, out_shape,
                  scratch=()):
    n_in, n_out, n_scr, n = len(in_specs), len(out_specs), len(scratch), len(arrays)
    last = tuple(g - 1 for g in grid)

    def at(step):
        cond = pl.program_id(0) == step[0]
        for a in range(1, len(grid)):
            cond = cond & (pl.program_id(a) == step[a])
        return cond

    def wrapped(*refs):
        ins, cin = refs[:n_in], refs[n_in:n_in + n]
        outs, cout = refs[n_in + n:n_in + n + n_out], refs[n_in + n + n_out:n_in + 2 * n + n_out]
        scr, sems = refs[n_in + 2 * n + n_out:n_in + 2 * n + n_out + n_scr], refs[n_in + 2 * n + n_out + n_scr:]

        @pl.when(at((0,) * len(grid)))
        def _():
            comm(cin, cout, *sems, start=True)

        body(*ins, *outs, *scr)

        @pl.when(at(last))
        def _():
            comm(cin, cout, *sems, start=False)

    call = _pcall(wrapped, name=name, grid=grid, in_specs=list(in_specs) + [HBM_SPEC] * n,
                  out_specs=list(out_specs) + [HBM_SPEC] * n, out_shape=list(out_shape) + list(comm_out_shape),
                  scratch=list(scratch) + list(comm_sems))

    def run(*args):
        res = call(*args, *arrays)
        return res[:n_out], res[n_out:]

    return run


def _dot(a, b):
    return lax.dot_general(a, b, (((1,), (0,)), ((), ())), preferred_element_type=F32)


def _dot_nt(a, b):
    return lax.dot_general(a, b, (((1,), (1,)), ((), ())), preferred_element_type=F32)


def _lo(shape):
    return lax.broadcasted_iota(jnp.int32, shape, len(shape) - 1) < HD


def _half_sum(x):
    r = lax.broadcasted_iota(jnp.int32, (LANES, LANES), 0) // HD
    c = lax.broadcasted_iota(jnp.int32, (LANES, LANES), 1) // HD
    ones = (r == c).astype(BF)
    hi = x.astype(BF)
    return _dot(hi, ones) + _dot((x - hi.astype(F32)).astype(BF), ones)


def _col(tile, lane):
    idx = lax.broadcasted_iota(jnp.int32, tile.shape, 1)
    return jnp.sum(jnp.where(idx == lane, tile, 0.0), axis=-1, keepdims=True)


def _split(t, lo):
    z = jnp.zeros_like(t)
    return jnp.where(lo, t, z), jnp.where(lo, z, t)


def _rms(xf, g):
    r = lax.rsqrt(jnp.mean(xf * xf, axis=-1, keepdims=True) + EPS)
    return xf * r * g


def _rms_bwd(xf, g, dy):
    r = lax.rsqrt(jnp.mean(xf * xf, axis=-1, keepdims=True) + EPS)
    xr = xf * r
    dg = jnp.sum(dy * xr, axis=0, keepdims=True)
    t = dy * g
    return r * (t - xr * jnp.mean(xr * t, axis=-1, keepdims=True)), dg


def _rope_fwd(y, c, s1, s2, sh):
    return y * c + pltpu.roll(y, sh, 1) * s1 + pltpu.roll(y, LANES - sh, 1) * s2


def _rope_bwd(dy, c, s1, s2, sh):
    return dy * c + pltpu.roll(dy * s1, LANES - sh, 1) + pltpu.roll(dy * s2, sh, 1)


def _tile(t):
    return slice(t * LANES, (t + 1) * LANES)


def inproj_fwd(x, g, w_pad, tabs, qkg, *, axial, tm=512):
    S = x.shape[0]
    sh = 16 if axial else 8

    def body(x_ref, g_ref, w_ref, c_ref, s1_ref, s2_ref, qkg_ref, h_ref, p_ref, *extra):
        h = _rms(x_ref[...], g_ref[...]).astype(BF)
        h_ref[...] = h
        acc = _dot(h, w_ref[...])
        c, s1, s2 = c_ref[...], s1_ref[...], s2_ref[...]
        lo = _lo((tm, LANES))
        if axial:
            raw_ref, nrm_ref = extra

            @pl.when(pl.program_id(0) == 0)
            def _():
                nrm_ref[...] = jnp.zeros_like(nrm_ref)

        for t in range(14):
            y = acc[:, _tile(t)]
            if t < V_T0:
                if axial:
                    raw_ref[:, _tile(t)] = y.astype(BF)
                    gt = qkg_ref[0:1, :] if t < K_T0 else qkg_ref[1:2, :]
                    y = y * lax.rsqrt(_half_sum(y * y) * (1.0 / HD) + EPS) * gt
                y = _rope_fwd(y, c, s1, s2, sh)
            if t < K_T0:
                y = y * (SCALE * LOG2E if axial else SCALE)
            elif t >= M_T0:
                y = y * SCALE
            yb = y.astype(BF)
            p_ref[:, _tile(t)] = yb
            if axial and t < V_T0:
                yf = yb.astype(F32)
                n2 = jnp.max(_half_sum(yf * yf), axis=0, keepdims=True)
                r = 0 if t < K_T0 else 1
                nrm_ref[r:r + 1, :] = jnp.maximum(nrm_ref[r:r + 1, :], n2)

    row = lambda w: pl.BlockSpec((tm, w), lambda i: (i, 0))
    full = lambda a: pl.BlockSpec(a.shape, lambda i: (0, 0))
    out_shape = [jax.ShapeDtypeStruct((S, D), BF), jax.ShapeDtypeStruct((S, PW), BF)]
    out_specs = [row(D), row(PW)]
    if axial:
        out_shape += [jax.ShapeDtypeStruct((S, V_T0 * LANES), BF), jax.ShapeDtypeStruct((8, LANES), F32)]
        out_specs += [row(V_T0 * LANES), pl.BlockSpec((8, LANES), lambda i: (0, 0))]
    return _pcall(body, name="inproj_fwd_axial" if axial else "inproj_fwd", grid=(S // tm,),
                  in_specs=[row(D), full(g), full(w_pad), row(LANES), row(LANES), row(LANES), full(qkg)],
                  out_specs=out_specs, out_shape=out_shape)(x, g, w_pad, *tabs, qkg)


def norm_mm(x, g, w, *, tm):
    S, N = x.shape[0], w.shape[1]

    def body(x_ref, g_ref, w_ref, h_ref, y_ref):
        h = _rms(x_ref[...], g_ref[...]).astype(BF)
        h_ref[...] = h
        y_ref[...] = _dot(h, w_ref[...]).astype(BF)

    return _pcall(body, name="norm_mm", grid=(S // tm,),
                  in_specs=[pl.BlockSpec((tm, D), lambda i: (i, 0)), pl.BlockSpec(g.shape, lambda i: (0, 0)),
                            pl.BlockSpec(w.shape, lambda i: (0, 0))],
                  out_specs=[pl.BlockSpec((tm, D), lambda i: (i, 0)), pl.BlockSpec((tm, N), lambda i: (i, 0))],
                  out_shape=[jax.ShapeDtypeStruct((S, D), BF), jax.ShapeDtypeStruct((S, N), BF)])(x, g, w)


def ffn_up_fwd(x, g, w_gu, next_shards=(), *, tm=512, tn=1408):
    S = x.shape[0]

    def body(x_ref, g_ref, w_ref, h_ref, gate_ref, up_ref, a_ref, h_scr):
        @pl.when(pl.program_id(1) == 0)
        def _():
            h = _rms(x_ref[...], g_ref[...]).astype(BF)
            h_scr[...] = h
            h_ref[...] = h

        acc = _dot(h_scr[...], w_ref[...])
        gate, up = acc[:, :tn], acc[:, tn:]
        sig = pl.reciprocal(1.0 + jnp.exp(-gate), approx=True)
        silu = gate * sig
        gate_ref[...] = (up * (sig * (1.0 + gate * (1.0 - sig)))).astype(BF)
        up_ref[...] = silu.astype(BF)
        a_ref[...] = (silu * up).astype(BF)

    rowd = pl.BlockSpec((tm, D), lambda i, j: (i, 0))
    osp = pl.BlockSpec((tm, tn), lambda i, j: (i, j))
    sd = jax.ShapeDtypeStruct((S, DFF), BF)
    kw = dict(grid=(S // tm, DFF // tn),
              in_specs=[rowd, pl.BlockSpec(g.shape, lambda i, j: (0, 0)), pl.BlockSpec((D, 2 * tn), lambda i, j: (0, j))],
              out_specs=[rowd, osp, osp, osp], out_shape=[jax.ShapeDtypeStruct((S, D), BF), sd, sd, sd],
              scratch=[pltpu.VMEM((tm, D), BF)])
    if next_shards:
        return _pcall_behind(body, _gather_comm, next_shards, _gather_out_shape(next_shards),
                             _gather_sems(len(next_shards)), name="ffn_up_fwd_gather", **kw)(x, g, w_gu)
    return _pcall(body, name="ffn_up_fwd", **kw)(x, g, w_gu), ()


def mm_norm_res(a_parts, w, g, res, *, tm=512):
    S = a_parts[0].shape[0]
    na = len(a_parts)

    def body(*refs):
        w_ref, g_ref, res_ref, y_ref, o_ref = refs[na:]
        a = refs[0][...] if na == 1 else jnp.concatenate([r[...] for r in refs[:na]], axis=1)
        y = _dot(a, w_ref[...])
        y_ref[...] = y
        o_ref[...] = res_ref[...] + _rms(y, g_ref[...])

    rowd = pl.BlockSpec((tm, D), lambda i: (i, 0))
    sd = jax.ShapeDtypeStruct((S, D), F32)
    return _pcall(body, name="mm_norm_res", grid=(S // tm,),
                  in_specs=[pl.BlockSpec((tm, a.shape[1]), lambda i: (i, 0)) for a in a_parts]
                  + [pl.BlockSpec(w.shape, lambda i: (0, 0)), pl.BlockSpec(g.shape, lambda i: (0, 0)), rowd],
                  out_specs=[rowd, rowd], out_shape=[sd, sd])(*a_parts, w, g, res)


def loss_bwd(y, tgt, *, tm=512):
    S = y.shape[0]

    def body(y_ref, t_ref, dy_ref, l_ref):
        @pl.when(pl.program_id(0) == 0)
        def _():
            l_ref[...] = jnp.zeros_like(l_ref)

        e = y_ref[...] - t_ref[...]
        dy_ref[...] = e * (1.0 / D)
        col = jnp.sum(e * e, axis=0, keepdims=True)
        part = col[:, _tile(0)]
        for t in range(1, D // LANES):
            part = part + col[:, _tile(t)]
        l_ref[...] += part * (0.5 / D)

    rowd = pl.BlockSpec((tm, D), lambda i: (i, 0))
    return _pcall(body, name="loss_bwd", grid=(S // tm,), in_specs=[rowd, rowd],
                  out_specs=[rowd, pl.BlockSpec((1, LANES), lambda i: (0, 0))],
                  out_shape=[jax.ShapeDtypeStruct((S, D), F32), jax.ShapeDtypeStruct((1, LANES), F32)])(y, tgt)


def normbwd_mm_cat(dy, ysaved, g, w, tok, mo, *, stats_t=False, tm=512):
    S = dy.shape[0]
    n_tok = tok.shape[1] // LANES
    per = FLASH_TQ // tm

    def body(dy_ref, y_ref, g_ref, w_ref, tok_ref, mo_ref, d_ref, dcat_ref, delta_ref, dg_ref, *dt_ref):
        @pl.when(pl.program_id(0) == 0)
        def _():
            dg_ref[...] = jnp.zeros_like(dg_ref)

        d, dg = _rms_bwd(y_ref[...], g_ref[...], dy_ref[...])
        dg_ref[...] += dg
        d = d.astype(BF)
        d_ref[...] = d
        z = _dot_nt(d, w_ref[...])
        dcat_ref[...] = z.astype(BF)
        lo = _lo((tm, LANES))
        for t in range(D // LANES):
            c = tok_ref[:, _tile(t)] if t < n_tok else mo_ref[:, _tile(t - n_tok)]
            dl = _half_sum(z[:, _tile(t)] * c.astype(F32))
            delta_ref[:, _tile(t)] = dl
            if stats_t and t < N_PAIRS:
                _store_stats_t(dt_ref[0].at[t], dl)

    rowd = pl.BlockSpec((tm, D), lambda i: (i, 0))
    out_specs = [rowd, rowd, rowd, pl.BlockSpec((1, D), lambda i: (0, 0))]
    out_shape = [jax.ShapeDtypeStruct((S, D), BF), jax.ShapeDtypeStruct((S, D), BF), jax.ShapeDtypeStruct((S, D), F32),
                 jax.ShapeDtypeStruct((1, D), F32)]
    if stats_t:
        out_specs.append(pl.BlockSpec((N_PAIRS, None, 8, tm), lambda i: (0, i // per, 0, i % per)))
        out_shape.append(jax.ShapeDtypeStruct((N_PAIRS, S // FLASH_TQ, 8, FLASH_TQ), F32))
    return _pcall(body, name="normbwd_mm_cat_t" if stats_t else "normbwd_mm_cat", grid=(S // tm,),
                  in_specs=[rowd, rowd, pl.BlockSpec(g.shape, lambda i: (0, 0)), pl.BlockSpec(w.shape, lambda i: (0, 0)),
                            pl.BlockSpec((tm, tok.shape[1]), lambda i: (i, 0)),
                            pl.BlockSpec((tm, mo.shape[1]), lambda i: (i, 0))],
                  out_specs=out_specs, out_shape=out_shape)(dy, ysaved, g, w, tok, mo)


def normbwd_mm_swiglu(dy, fsaved, g, wd, gate, up, grad_parts=(), *, tm=512, tn=1408):
    S = dy.shape[0]

    def body(dy_ref, f_ref, g_ref, w_ref, gate_ref, up_ref, df_ref, dgate_ref, dup_ref, dg_ref, d_scr):
        i, j = pl.program_id(0), pl.program_id(1)

        @pl.when((i == 0) & (j == 0))
        def _():
            dg_ref[...] = jnp.zeros_like(dg_ref)

        @pl.when(j == 0)
        def _():
            d, dg = _rms_bwd(f_ref[...], g_ref[...], dy_ref[...])
            dg_ref[...] += dg
            d_scr[...] = d.astype(BF)
            df_ref[...] = d.astype(BF)

        da = _dot_nt(d_scr[...], w_ref[...])
        dgate_ref[...] = (da * gate_ref[...].astype(F32)).astype(BF)
        dup_ref[...] = (da * up_ref[...].astype(F32)).astype(BF)

    rowd = pl.BlockSpec((tm, D), lambda i, j: (i, 0))
    osp = pl.BlockSpec((tm, tn), lambda i, j: (i, j))
    sd = jax.ShapeDtypeStruct((S, DFF), BF)
    kw = dict(grid=(S // tm, DFF // tn),
              in_specs=[rowd, rowd, pl.BlockSpec(g.shape, lambda i, j: (0, 0)), pl.BlockSpec((tn, D), lambda i, j: (j, 0)),
                        osp, osp],
              out_specs=[rowd, osp, osp, pl.BlockSpec((1, D), lambda i, j: (0, 0))],
              out_shape=[jax.ShapeDtypeStruct((S, D), BF), sd, sd, jax.ShapeDtypeStruct((1, D), F32)],
              scratch=[pltpu.VMEM((tm, D), BF)])
    args = (dy, fsaved, g, wd, gate, up)
    if grad_parts:
        return _pcall_behind(body, _scatter_comm, grad_parts, _scatter_out_shape(grad_parts),
                             _scatter_sems(len(grad_parts)), name="normbwd_mm_swiglu_scatter", **kw)(*args)
    return _pcall(body, name="normbwd_mm_swiglu", **kw)(*args), ()


def mm_nt_normbwd_res(parts, xin, g, dres, *, tm=512):
    S = xin.shape[0]
    npart = len(parts)
    has_res = dres is not None

    def body(*refs):
        prefs = refs[:2 * npart]
        x_ref, g_ref = refs[2 * npart:2 * npart + 2]
        rest = refs[2 * npart + 2:]
        if has_res:
            dres_ref, dx_ref, dg_ref = rest
        else:
            (dg_ref,) = rest

        @pl.when(pl.program_id(0) == 0)
        def _():
            dg_ref[...] = jnp.zeros_like(dg_ref)

        z = _dot_nt(prefs[0][...], prefs[1][...])
        for p in range(1, npart):
            z = z + _dot_nt(prefs[2 * p][...], prefs[2 * p + 1][...])
        dx, dg = _rms_bwd(x_ref[...], g_ref[...], z)
        dg_ref[...] += dg
        if has_res:
            dx_ref[...] = dres_ref[...] + dx

    rowd = pl.BlockSpec((tm, D), lambda i: (i, 0))
    in_specs, args = [], []
    for dy, w in parts:
        in_specs += [pl.BlockSpec((tm, dy.shape[1]), lambda i: (i, 0)),
                     pl.BlockSpec(w.shape, lambda i: (0, 0), pipeline_mode=pl.Buffered(1))]
        args += [dy, w]
    in_specs += [rowd, pl.BlockSpec(g.shape, lambda i: (0, 0))]
    args += [xin, g]
    out_specs = [pl.BlockSpec((1, D), lambda i: (0, 0))]
    out_shape = [jax.ShapeDtypeStruct((1, D), F32)]
    if has_res:
        in_specs.append(rowd)
        args.append(dres)
        out_specs.insert(0, rowd)
        out_shape.insert(0, jax.ShapeDtypeStruct((S, D), F32))
    return _pcall(body, name="mm_nt_normbwd_res" if has_res else "mm_nt_normbwd", grid=(S // tm,),
                  in_specs=in_specs, out_specs=out_specs, out_shape=out_shape)(*args)


def inproj_bwd(dq, dkp, dvp, dqm, tabs, raw, qkg, w_pad, xin, g, dres, *, axial, tm=512):
    S = xin.shape[0]
    sh = 16 if axial else 8

    def body(dq_ref, dk_ref, dv_ref, dm_ref, c_ref, s1_ref, s2_ref, raw_ref, qkg_ref, w_ref, x_ref, g_ref, dres_ref,
             dx_ref, dp_ref, dg_ref, dqk_ref):
        @pl.when(pl.program_id(0) == 0)
        def _():
            dg_ref[...] = jnp.zeros_like(dg_ref)
            dqk_ref[...] = jnp.zeros_like(dqk_ref)

        c, s1, s2 = c_ref[...], s1_ref[...], s2_ref[...]
        lo = _lo((tm, LANES))
        for t in range(14):
            if t < K_T0:
                y = dq_ref[:, _tile(t)].astype(F32) * SCALE
            elif t < V_T0:
                y = dk_ref[:, _tile(2 * (t - K_T0))].astype(F32) + dk_ref[:, _tile(2 * (t - K_T0) + 1)].astype(F32)
                if axial:
                    y = y * LN2
            elif t < M_T0:
                y = dv_ref[:, _tile(2 * (t - V_T0))].astype(F32) + dv_ref[:, _tile(2 * (t - V_T0) + 1)].astype(F32)
            else:
                y = dm_ref[:, _tile(t - M_T0)] * SCALE
            if t < V_T0:
                y = _rope_bwd(y, c, s1, s2, sh)
                if axial:
                    row = 0 if t < K_T0 else 1
                    xr = raw_ref[:, _tile(t)].astype(F32)
                    r = lax.rsqrt(_half_sum(xr * xr) * (1.0 / HD) + EPS)
                    xn = xr * r
                    dqk_ref[row:row + 1, :] += jnp.sum(y * xn, axis=0, keepdims=True)
                    tt = y * qkg_ref[row:row + 1, :]
                    y = r * (tt - xn * (_half_sum(xn * tt) * (1.0 / HD)))
            dp_ref[:, _tile(t)] = y.astype(BF)
        z = _dot_nt(dp_ref[...], w_ref[...])
        dx, dg = _rms_bwd(x_ref[...], g_ref[...], z)
        dg_ref[...] += dg
        dx_ref[...] = dres_ref[...] + dx

    row = lambda w: pl.BlockSpec((tm, w), lambda i: (i, 0))
    full = lambda a: pl.BlockSpec(a.shape, lambda i: (0, 0))
    return _pcall(body, name="inproj_bwd_axial" if axial else "inproj_bwd", grid=(S // tm,),
                  in_specs=[row(768), row(768), row(768), row(256), row(LANES), row(LANES), row(LANES),
                            row(raw.shape[1] if axial else LANES), full(qkg), full(w_pad), row(D), full(g), row(D)],
                  out_specs=[row(D), row(PW), pl.BlockSpec((1, D), lambda i: (0, 0)),
                             pl.BlockSpec((8, LANES), lambda i: (0, 0))],
                  out_shape=[jax.ShapeDtypeStruct((S, D), F32), jax.ShapeDtypeStruct((S, PW), BF),
                             jax.ShapeDtypeStruct((1, D), F32), jax.ShapeDtypeStruct((8, LANES), F32)])(
        dq, dkp, dvp, dqm, *tabs, raw, qkg, w_pad, xin, g, dres)


def mm_acc(a, b, *, tk, tn, ts):
    S, K = a.shape
    N = b.shape[1]
    ts = min(ts, S)

    def body(a_ref, b_ref, o_ref):
        z = lax.dot_general(a_ref[...], b_ref[...], (((0,), (0,)), ((), ())), preferred_element_type=F32)

        @pl.when(pl.program_id(2) == 0)
        def _():
            o_ref[...] = z

        @pl.when(pl.program_id(2) > 0)
        def _():
            o_ref[...] += z

    return _pcall(body, name="mm_acc", grid=(K // tk, N // tn, S // ts),
                  in_specs=[pl.BlockSpec((ts, tk), lambda k, n, s: (s, k)), pl.BlockSpec((ts, tn), lambda k, n, s: (s, n))],
                  out_specs=pl.BlockSpec((tk, tn), lambda k, n, s: (k, n)),
                  out_shape=jax.ShapeDtypeStruct((K, N), F32))(a, b)


def _band_specs(L, d, R, T, width, bw, col_of):
    n = T // R
    nb = width // bw
    last = L // R - 1
    col = lambda g, r: r * nb + col_of(g)
    return [pl.BlockSpec((R, bw), lambda g, r, i: (jnp.maximum(i * n - 1, 0), col(g, r))),
            pl.BlockSpec((T, bw), lambda g, r, i: (i, col(g, r))),
            pl.BlockSpec((R, bw), lambda g, r, i: (jnp.minimum((i + 1) * n, last), col(g, r)))]


def _band_tile(R, L):
    return min(max(2 * R, 256), L)


def _band_bias(T, R):
    w = lax.broadcasted_iota(jnp.int32, (T, T + 2 * R), 1)
    c = lax.broadcasted_iota(jnp.int32, (T, T + 2 * R), 0)
    return jnp.where(jnp.abs(w - R - c) <= R, 0.0, NEG).astype(F32)


def _edge_bias(i, T, R, L):
    wpos = i * T - R + lax.broadcasted_iota(jnp.int32, (1, T + 2 * R), 1)
    return jnp.where((wpos >= 0) & (wpos < L), 0.0, NEG)


def banded_fwd(proj, sink, *, d, R, TQ, pair0, npairs, use_sink, o_dtype, gather=()):
    S = proj.shape[0]
    L = S // d
    pv = proj.reshape(L, d * PW)
    ow = npairs * LANES

    def body(sink_ref, bias_ref, q_ref, kp, kc, kn, vp, vc, vn, o_ref, lse_ref):
        g, i = pl.program_id(0), pl.program_id(2)
        bias = bias_ref[...] + _edge_bias(i, TQ, R, L)
        lo = _lo((TQ, LANES))
        kw = jnp.concatenate([kp[...], kc[...], kn[...]], axis=0)
        vw = jnp.concatenate([vp[...], vc[...], vn[...]], axis=0)
        v_lo, v_hi = _split(vw, _lo(vw.shape))
        if TQ == 2 * R:
            parts = [(slice(0, R), slice(0, 3 * R)), (slice(R, 2 * R), slice(R, 4 * R))]
        else:
            parts = [(slice(0, TQ), slice(0, TQ + 2 * R))]
        for rows, cols in parts:
            n_rows = rows.stop - rows.start
            lo = _lo((n_rows, LANES))
            kpart, bpart = kw[cols, :], bias[rows, cols]
            vcat = jnp.concatenate([v_lo[cols, :], v_hi[cols, :]], axis=0)
            for t in range(2):
                qa, qb = _split(q_ref[rows, _tile(t)], lo)
                ps, dens, lses = [], [], []
                for h, qh in enumerate((qa, qb)):
                    s = _dot_nt(qh, kpart) + bpart
                    m = jnp.max(s, axis=-1, keepdims=True)
                    if use_sink:
                        sk = sink_ref[2 * (pair0 + 2 * g + t) + h]
                        m = jnp.maximum(m, sk)
                    e = jnp.exp(s - m)
                    den = jnp.sum(e, axis=-1, keepdims=True)
                    if use_sink:
                        den = den + jnp.exp(sk - m)
                    ps.append(e.astype(BF))
                    dens.append(den)
                    lses.append(m + jnp.log(den))
                o = _dot(jnp.concatenate(ps, axis=1), vcat)
                o_ref[rows, _tile(t)] = (o / jnp.where(lo, dens[0], dens[1])).astype(o_dtype)
                lse_ref[rows, _tile(t)] = jnp.where(lo, lses[0], lses[1])

    g0 = pair0 // 2
    qspec = pl.BlockSpec((TQ, 2 * LANES), lambda g, r, i: (i, r * 7 + g0 + g))
    kspecs = _band_specs(L, d, R, TQ, PW, LANES, lambda g: K_T0 + g0 + g)
    vspecs = _band_specs(L, d, R, TQ, PW, LANES, lambda g: V_T0 + g0 + g)
    ospec = pl.BlockSpec((TQ, 2 * LANES), lambda g, r, i: (i, r * (npairs // 2) + g))
    bias = _band_bias(TQ, R)
    kw = dict(grid=(npairs // 2, d, L // TQ),
              in_specs=[pl.BlockSpec(memory_space=pltpu.SMEM), pl.BlockSpec(bias.shape, lambda g, r, i: (0, 0)),
                        qspec] + kspecs + vspecs,
              out_specs=[ospec, ospec],
              out_shape=[jax.ShapeDtypeStruct((L, d * ow), o_dtype), jax.ShapeDtypeStruct((L, d * ow), F32)])
    args = (sink, bias, pv, pv, pv, pv, pv, pv, pv)
    if gather:
        (o, lse), gathered = _pcall_behind(body, _gather_comm, gather, _gather_out_shape(gather),
                                           _gather_sems(len(gather)), name="banded_fwd_gather", **kw)(*args)
        return o.reshape(S, ow), lse.reshape(S, ow), gathered
    o, lse = _pcall(body, name="banded_fwd", **kw)(*args)
    return o.reshape(S, ow), lse.reshape(S, ow)


def banded_bwd_dq(proj, do, lse, delta, sink, *, d, R, TQ, pair0, npairs, use_sink, scatter=()):
    S = proj.shape[0]
    L = S // d
    pv = proj.reshape(L, d * PW)
    ow = npairs * LANES

    def body(sink_ref, bias_ref, q_ref, kp, kc, kn, vp, vc, vn, do_ref, lse_ref, delta_ref, dq_ref, dsink_ref):
        g, r, i = pl.program_id(0), pl.program_id(1), pl.program_id(2)

        @pl.when((r == 0) & (i == 0))
        def _():
            dsink_ref[...] = jnp.zeros_like(dsink_ref)

        bias = bias_ref[...] + _edge_bias(i, TQ, R, L)
        lo = _lo((TQ, LANES))
        kw = jnp.concatenate([kp[...], kc[...], kn[...]], axis=0)
        vw = jnp.concatenate([vp[...], vc[...], vn[...]], axis=0)
        k_lo, k_hi = _split(kw, _lo(kw.shape))
        kcat = jnp.concatenate([k_lo, k_hi], axis=0)
        for t in range(2):
            qa, qb = _split(q_ref[:, _tile(t)], lo)
            doa, dob = _split(do_ref[:, _tile(t)], lo)
            lse_t, delta_t = lse_ref[:, _tile(t)], delta_ref[:, _tile(t)]
            dss, dsk = [], []
            for h, (qh, doh) in enumerate(((qa, doa), (qb, dob))):
                lse_h, delta_h = _col(lse_t, h * HD), _col(delta_t, h * HD)
                pr = jnp.exp(_dot_nt(qh, kw) + bias - lse_h)
                dss.append((pr * (_dot_nt(doh, vw) - delta_h)).astype(BF))
                if use_sink:
                    psink = jnp.exp(sink_ref[2 * (pair0 + 2 * g + t) + h] - lse_h)
                    dsk.append(-jnp.sum(psink * delta_h, axis=0, keepdims=True))
            dq_ref[:, _tile(t)] = _dot(jnp.concatenate(dss, axis=1), kcat).astype(BF)
            if use_sink:
                dsink_ref[:, _tile(t)] += jnp.where(_lo((8, LANES)), dsk[0], dsk[1])

    g0 = pair0 // 2
    qspec = pl.BlockSpec((TQ, 2 * LANES), lambda g, r, i: (i, r * 7 + g0 + g))
    kspecs = _band_specs(L, d, R, TQ, PW, LANES, lambda g: K_T0 + g0 + g)
    vspecs = _band_specs(L, d, R, TQ, PW, LANES, lambda g: V_T0 + g0 + g)
    ospec = pl.BlockSpec((TQ, 2 * LANES), lambda g, r, i: (i, r * (npairs // 2) + g))
    view = lambda a: a.reshape(L, d * a.shape[1])
    ispec = lambda a: pl.BlockSpec((TQ, 2 * LANES), lambda g, r, i: (i, r * (a.shape[1] // (2 * LANES)) + g))
    bias = _band_bias(TQ, R)
    kw = dict(grid=(npairs // 2, d, L // TQ),
              in_specs=[pl.BlockSpec(memory_space=pltpu.SMEM), pl.BlockSpec(bias.shape, lambda g, r, i: (0, 0)),
                        qspec] + kspecs + vspecs + [ispec(do), ispec(lse), ispec(delta)],
              out_specs=[ospec, pl.BlockSpec((8, 2 * LANES), lambda g, r, i: (g, 0))],
              out_shape=[jax.ShapeDtypeStruct((L, d * ow), BF),
                         jax.ShapeDtypeStruct((npairs // 2 * 8, 2 * LANES), F32)])
    args = (sink, bias, pv, pv, pv, pv, pv, pv, pv, view(do), view(lse), view(delta))
    if scatter:
        (dq, dsink), got = _pcall_behind(body, _scatter_comm, scatter, _scatter_out_shape(scatter),
                                         _scatter_sems(len(scatter)), name="banded_bwd_dq_scatter", **kw)(*args)
        return dq.reshape(S, ow), dsink, got
    dq, dsink = _pcall(body, name="banded_bwd_dq", **kw)(*args)
    return dq.reshape(S, ow), dsink


def banded_bwd_dkv(proj, do, lse, delta, *, d, R, TK, pair0, npairs):
    S = proj.shape[0]
    L = S // d
    pv = proj.reshape(L, d * PW)
    ow = npairs * LANES

    def body(bias_ref, k_ref, v_ref, qp, qc, qn, dop, doc, don, lp, lc, ln, dp_, dc_, dn_, dk_ref, dv_ref):
        j = pl.program_id(2)
        W = TK + 2 * R
        bias = bias_ref[...] + _edge_bias(j, TK, R, L)
        low = _lo((W, LANES))
        qw = jnp.concatenate([qp[...], qc[...], qn[...]], axis=0)
        dow = jnp.concatenate([dop[...], doc[...], don[...]], axis=0)
        lse_w = jnp.concatenate([lp[...], lc[...], ln[...]], axis=0)
        delta_w = jnp.concatenate([dp_[...], dc_[...], dn_[...]], axis=0)
        k, v = k_ref[...], v_ref[...]
        for t in range(2):
            qa, qb = _split(qw[:, _tile(t)], low)
            doa, dob = _split(dow[:, _tile(t)], low)
            lse_r, delta_r = lse_w[:, _tile(t)].T, delta_w[:, _tile(t)].T
            prs, dss = [], []
            for h, (qh, doh) in enumerate(((qa, doa), (qb, dob))):
                pr = jnp.exp(_dot_nt(k, qh) + bias - lse_r[h * HD:h * HD + 1, :])
                dss.append((pr * (_dot_nt(v, doh) - delta_r[h * HD:h * HD + 1, :])).astype(BF))
                prs.append(pr.astype(BF))
            dv_ref[:, _tile(t)] = _dot(jnp.concatenate(prs, axis=1), jnp.concatenate([doa, dob], axis=0)).astype(BF)
            dk_ref[:, _tile(t)] = _dot(jnp.concatenate(dss, axis=1), jnp.concatenate([qa, qb], axis=0)).astype(BF)

    g0 = pair0 // 2
    kspec = pl.BlockSpec((TK, LANES), lambda g, r, j: (j, r * 14 + K_T0 + g0 + g))
    vspec = pl.BlockSpec((TK, LANES), lambda g, r, j: (j, r * 14 + V_T0 + g0 + g))
    qspecs = _band_specs(L, d, R, TK, PW, 2 * LANES, lambda g: g0 + g)
    ispecs = lambda a: _band_specs(L, d, R, TK, a.shape[1], 2 * LANES, lambda g: g)
    view = lambda a: a.reshape(L, d * a.shape[1])
    ospec = pl.BlockSpec((TK, 2 * LANES), lambda g, r, j: (j, r * (npairs // 2) + g))
    sd = jax.ShapeDtypeStruct((L, d * ow), BF)
    bias = _band_bias(TK, R)
    dk, dv = _pcall(body, name="banded_bwd_dkv", grid=(npairs // 2, d, L // TK),
                    in_specs=[pl.BlockSpec(bias.shape, lambda g, r, j: (0, 0)), kspec, vspec] + qspecs + ispecs(do)
                    + ispecs(lse) + ispecs(delta),
                    out_specs=[ospec, ospec], out_shape=[sd, sd])(
        bias, pv, pv, pv, pv, pv, *([view(do)] * 3), *([view(lse)] * 3), *([view(delta)] * 3))
    return dk.reshape(S, ow), dv.reshape(S, ow)


def _store_stats_t(ref, tile):
    t = tile.T
    ref[...] = jnp.zeros_like(ref)
    ref[0:1, :] = t[0:1, :]
    ref[1:2, :] = t[HD:HD + 1, :]


def flash_fwd(proj, bound, *, tq=FLASH_TQ, tk=1024):
    S = proj.shape[0]

    def body_general(q_ref, k_ref, v_ref, o_ref, lse_ref):
        lo = _lo((tq, LANES))
        qa, qb = _split(q_ref[...], lo)
        lov = _lo((tk, LANES))

        def step(j, carry):
            ma, la, mb, lb, acc = carry
            rows = pl.ds(pl.multiple_of(j * tk, tk), tk)
            k, v = k_ref[rows, :], v_ref[rows, :]
            outs = []
            for qh, m0, l0 in ((qa, ma, la), (qb, mb, lb)):
                s = _dot_nt(qh, k)
                m1 = jnp.maximum(m0, jnp.max(s, axis=-1, keepdims=True))
                al = jnp.exp2(m0 - m1)
                e = jnp.exp2(s - m1)
                outs.append((m1, al * l0 + jnp.sum(e, axis=-1, keepdims=True), al, e.astype(BF)))
            v_lo, v_hi = _split(v, lov)
            pvv = _dot(jnp.concatenate([outs[0][3], outs[1][3]], axis=1), jnp.concatenate([v_lo, v_hi], axis=0))
            acc = acc * jnp.where(lo, outs[0][2], outs[1][2]) + pvv
            return outs[0][0], outs[0][1], outs[1][0], outs[1][1], acc

        m_init = jnp.full((tq, 1), NEG, F32)
        l_init = jnp.zeros((tq, 1), F32)
        ma, la, mb, lb, acc = lax.fori_loop(0, S // tk, step,
                                            (m_init, l_init, m_init, l_init, jnp.zeros((tq, LANES), F32)))
        o_ref[...] = (acc / jnp.where(lo, la, lb)).astype(BF)
        _store_stats_t(lse_ref, jnp.where(lo, ma * LN2 + jnp.log(la), mb * LN2 + jnp.log(lb)))

    def body_plain(q_ref, k_ref, v_ref, o_ref, lse_ref):
        lo = _lo((tq, LANES))
        qa, qb = _split(q_ref[...], lo)
        lov = _lo((tk, LANES))
        one = jnp.ones((tk, LANES), BF)

        def step(j, carry):
            acc_a, acc_b = carry
            rows = pl.ds(pl.multiple_of(j * tk, tk), tk)
            k, v = k_ref[rows, :], v_ref[rows, :]
            ea = jnp.exp2(_dot_nt(qa, k)).astype(BF)
            eb = jnp.exp2(_dot_nt(qb, k)).astype(BF)
            acc_a = acc_a + _dot(ea, jnp.where(lov, v, one))
            acc_b = acc_b + _dot(eb, jnp.where(lov, one, v))
            return acc_a, acc_b

        z = jnp.zeros((tq, LANES), F32)
        acc_a, acc_b = lax.fori_loop(0, S // tk, step, (z, z))
        den = jnp.where(lo, pltpu.roll(acc_a, HD, 1), pltpu.roll(acc_b, HD, 1))
        o_ref[...] = (jnp.where(lo, acc_a, acc_b) / den).astype(BF)
        _store_stats_t(lse_ref, jnp.log(den))

    def body(bound_ref, q_ref, k_ref, v_ref, o_ref, lse_ref):
        small = bound_ref[0] <= MAX_PLAIN_SCORE

        @pl.when(small)
        def _():
            body_plain(q_ref, k_ref, v_ref, o_ref, lse_ref)

        @pl.when(jnp.logical_not(small))
        def _():
            body_general(q_ref, k_ref, v_ref, o_ref, lse_ref)

    ospec = pl.BlockSpec((tq, LANES), lambda p, i: (i, p))
    return _pcall(body, name="flash_fwd", grid=(N_PAIRS, S // tq),
                  in_specs=[pl.BlockSpec(memory_space=pltpu.SMEM), ospec,
                            pl.BlockSpec((S, LANES), lambda p, i: (0, K_T0 + p // 2)),
                            pl.BlockSpec((S, LANES), lambda p, i: (0, V_T0 + p // 2))],
                  out_specs=[ospec, pl.BlockSpec((None, None, 8, tq), lambda p, i: (p, i, 0, 0))],
                  out_shape=[jax.ShapeDtypeStruct((S, 768), BF), jax.ShapeDtypeStruct((N_PAIRS, S // tq, 8, tq), F32)])(
        bound.reshape(1), proj, proj, proj)


def flash_bwd(proj, do, lse_t, delta_t, *, tk=1024):
    S = proj.shape[0]
    nq, tq = lse_t.shape[1], lse_t.shape[3]

    def body(k_ref, v_ref, q_ref, do_ref, lse_ref, delta_ref, dk_ref, dv_ref, dqt_ref):
        @pl.when(pl.program_id(1) == 0)
        def _():
            dqt_ref[...] = jnp.zeros_like(dqt_ref)

        k, v = k_ref[...], v_ref[...]
        lo = _lo((tq, LANES))
        k_lo, k_hi = _split(k.astype(F32), _lo((tk, LANES)))
        kt = jnp.concatenate([k_lo.T, k_hi.T], axis=1).astype(BF)

        def step(i, carry):
            dk, dv = carry
            rows = pl.ds(pl.multiple_of(i * tq, tq), tq)
            qa, qb = _split(q_ref[rows, :], lo)
            doa, dob = _split(do_ref[rows, :], lo)
            lse_i, delta_i = lse_ref[i] * LOG2E, delta_ref[i]
            prs, dss = [], []
            for h, (qh, doh) in enumerate(((qa, doa), (qb, dob))):
                pr = jnp.exp2(_dot_nt(k, qh) - lse_i[h:h + 1, :])
                dss.append((pr * (_dot_nt(v, doh) - delta_i[h:h + 1, :])).astype(BF))
                prs.append(pr.astype(BF))
            dv = dv + _dot(jnp.concatenate(prs, axis=1), jnp.concatenate([doa, dob], axis=0))
            dk = dk + _dot(jnp.concatenate(dss, axis=1), jnp.concatenate([qa, qb], axis=0))
            dqt_ref[i] += _dot(kt, jnp.concatenate(dss, axis=0))
            return dk, dv

        z = jnp.zeros((tk, LANES), F32)
        dk, dv = lax.fori_loop(0, nq, step, (z, z))
        dk_ref[...] = dk.astype(BF)
        dv_ref[...] = dv.astype(BF)

    ospec = pl.BlockSpec((tk, LANES), lambda p, j: (j, p))
    stat = pl.BlockSpec((None, nq, 8, tq), lambda p, j: (p, 0, 0, 0))
    whole = lambda: pl.BlockSpec((S, LANES), lambda p, j: (0, p), pipeline_mode=pl.Buffered(1))
    sd = jax.ShapeDtypeStruct((S, 768), BF)
    dk, dv, dqt = _pcall(body, name="flash_bwd", grid=(N_PAIRS, S // tk),
                         in_specs=[pl.BlockSpec((tk, LANES), lambda p, j: (j, K_T0 + p // 2)),
                                   pl.BlockSpec((tk, LANES), lambda p, j: (j, V_T0 + p // 2)), whole(), whole(),
                                   stat, stat],
                         out_specs=[ospec, ospec,
                                    pl.BlockSpec((None, nq, LANES, tq), lambda p, j: (p, 0, 0, 0),
                                                 pipeline_mode=pl.Buffered(1))],
                         out_shape=[sd, sd, jax.ShapeDtypeStruct((N_PAIRS, nq, LANES, tq), F32)])(
        proj, proj, proj, do, lse_t, delta_t)
    dq = jnp.transpose(dqt, (1, 3, 0, 2)).reshape(S, 768).astype(BF)
    return dq, dk, dv


def mem_fwd(proj, mkv, *, tq=512):
    S = proj.shape[0]

    def body(q_ref, km_ref, vm_ref, o_ref, lse_ref):
        lo = _lo((tq, LANES))
        lov = _lo((N_MEM, LANES))
        for t in range(2):
            qa, qb = _split(q_ref[:, _tile(t)], lo)
            km, vm = km_ref[:, _tile(t)], vm_ref[:, _tile(t)]
            ps, dens, lses = [], [], []
            for qh in (qa, qb):
                s = _dot_nt(qh, km)
                m = jnp.max(s, axis=-1, keepdims=True)
                e = jnp.exp(s - m)
                den = jnp.sum(e, axis=-1, keepdims=True)
                ps.append(e.astype(BF))
                dens.append(den)
                lses.append(m + jnp.log(den))
            v_lo, v_hi = _split(vm, lov)
            o = _dot(jnp.concatenate(ps, axis=1), jnp.concatenate([v_lo, v_hi], axis=0))
            o_ref[:, _tile(t)] = (o / jnp.where(lo, dens[0], dens[1])).astype(BF)
            lse_ref[:, _tile(t)] = jnp.where(lo, lses[0], lses[1])

    ospec = pl.BlockSpec((tq, 256), lambda i: (i, 0))
    return _pcall(body, name="mem_fwd", grid=(S // tq,),
                  in_specs=[pl.BlockSpec((tq, 256), lambda i: (i, M_T0 // 2)),
                            pl.BlockSpec((N_MEM, 256), lambda i: (0, 0)), pl.BlockSpec((N_MEM, 256), lambda i: (0, 1))],
                  out_specs=[ospec, ospec],
                  out_shape=[jax.ShapeDtypeStruct((S, 256), BF), jax.ShapeDtypeStruct((S, 256), F32)])(proj, mkv, mkv)


def mem_bwd(proj, mkv, dcat, lse, delta, *, tq=512):
    S = proj.shape[0]

    def body(q_ref, km_ref, vm_ref, do_ref, lse_ref, delta_ref, dq_ref, dkm_ref, dvm_ref):
        @pl.when(pl.program_id(0) == 0)
        def _():
            dkm_ref[...] = jnp.zeros_like(dkm_ref)
            dvm_ref[...] = jnp.zeros_like(dvm_ref)

        lo = _lo((tq, LANES))
        lov = _lo((N_MEM, LANES))
        for t in range(2):
            qa, qb = _split(q_ref[:, _tile(t)], lo)
            doa, dob = _split(do_ref[:, _tile(t)], lo)
            km, vm = km_ref[:, _tile(t)], vm_ref[:, _tile(t)]
            lse_t, delta_t = lse_ref[:, _tile(t)], delta_ref[:, _tile(t)]
            prs, dss = [], []
            for h, (qh, doh) in enumerate(((qa, doa), (qb, dob))):
                pr = jnp.exp(_dot_nt(qh, km) - _col(lse_t, h * HD))
                dss.append(pr * (_dot_nt(doh, vm) - _col(delta_t, h * HD)))
                prs.append(pr)
            k_lo, k_hi = _split(km, lov)
            dq_ref[:, _tile(t)] = _dot(jnp.concatenate(dss, axis=1).astype(BF), jnp.concatenate([k_lo, k_hi], axis=0))
            dvm_ref[:, _tile(t)] += _dot(jnp.concatenate(prs, axis=0).T.astype(BF), jnp.concatenate([doa, dob], axis=0))
            dkm_ref[:, _tile(t)] += _dot(jnp.concatenate(dss, axis=0).T.astype(BF), jnp.concatenate([qa, qb], axis=0))

    ospec = pl.BlockSpec((tq, 256), lambda i: (i, 0))
    msp = pl.BlockSpec((N_MEM, 256), lambda i: (0, 0))
    md = jax.ShapeDtypeStruct((N_MEM, 256), F32)
    dq, dkm, dvm = _pcall(body, name="mem_bwd", grid=(S // tq,),
                          in_specs=[pl.BlockSpec((tq, 256), lambda i: (i, M_T0 // 2)), msp,
                                    pl.BlockSpec((N_MEM, 256), lambda i: (0, 1)),
                                    pl.BlockSpec((tq, 256), lambda i: (i, 3)), ospec,
                                    pl.BlockSpec((tq, 256), lambda i: (i, 3))],
                          out_specs=[ospec, msp, msp], out_shape=[jax.ShapeDtypeStruct((S, 256), F32), md, md])(
        proj, mkv, mkv, dcat, lse, delta)
    return dq, jnp.concatenate([dkm, dvm], axis=1)


def combine_fwd(os_, lses, *, tm=512):
    S = os_[0].shape[0]

    def body(o0, o1, o2, l0, l1, l2, tok_ref):
        ls = [l0[...], l1[...], l2[...]]
        m = jnp.maximum(jnp.maximum(ls[0], ls[1]), ls[2])
        es = [jnp.exp(l - m) for l in ls]
        den = es[0] + es[1] + es[2]
        for g, o in enumerate((o0, o1, o2)):
            tok_ref[:, 256 * g:256 * (g + 1)] = (o[...] * (es[g] / den)).astype(BF)

    sp = pl.BlockSpec((tm, 256), lambda i: (i, 0))
    return _pcall(body, name="combine_fwd", grid=(S // tm,), in_specs=[sp] * 6,
                  out_specs=pl.BlockSpec((tm, 768), lambda i: (i, 0)),
                  out_shape=jax.ShapeDtypeStruct((S, 768), BF))(*os_, *lses)


def combine_bwd(dcat, os_, lses, *, tm=512):
    S = dcat.shape[0]

    def body(dt_ref, o0, o1, o2, l0, l1, l2, do0, do1, do2, de0, de1, de2):
        ls = [l0[...], l1[...], l2[...]]
        m = jnp.maximum(jnp.maximum(ls[0], ls[1]), ls[2])
        es = [jnp.exp(l - m) for l in ls]
        den = es[0] + es[1] + es[2]
        alphas = [e / den for e in es]
        lo = _lo((tm, LANES))
        dts = [dt_ref[:, 256 * g:256 * (g + 1)].astype(F32) for g in range(3)]
        dal = []
        for g, o in enumerate((o0, o1, o2)):
            pr = dts[g] * o[...]
            dal.append(jnp.concatenate([_half_sum(pr[:, _tile(0)]), _half_sum(pr[:, _tile(1)])], axis=1))
        mix = alphas[0] * dal[0] + alphas[1] * dal[1] + alphas[2] * dal[2]
        for g, (do_ref, de_ref) in enumerate(((do0, de0), (do1, de1), (do2, de2))):
            do_ref[...] = (dts[g] * alphas[g]).astype(BF)
            de_ref[...] = alphas[g] * mix

    sp = pl.BlockSpec((tm, 256), lambda i: (i, 0))
    outs = _pcall(body, name="combine_bwd", grid=(S // tm,),
                  in_specs=[pl.BlockSpec((tm, 768), lambda i: (i, 0))] + [sp] * 6, out_specs=[sp] * 6,
                  out_shape=[jax.ShapeDtypeStruct((S, 256), BF)] * 3 + [jax.ShapeDtypeStruct((S, 256), F32)] * 3)(
        dcat, *os_, *lses)
    return outs[:3], outs[3:]


def _coords():
    return lax.axis_index("x"), lax.axis_index("y"), lax.axis_index("c")


def _other_chips(x, y):
    return [(1 - x, y), (x, 1 - y), (1 - x, 1 - y)]


HBM_SPEC = pl.BlockSpec(memory_space=pltpu.HBM)


def _gather_comm(ins, outs, send, recv, lsem, start):
    x, y, c = _coords()
    me = 2 * x + y
    for a in range(len(ins)):
        local = pltpu.make_async_copy(ins[a], outs[a].at[me], lsem.at[a])
        if start:
            local.start()
        for j, (px, py) in enumerate(_other_chips(x, y)):
            sems = dict(send_sem=send.at[3 * a + j], recv_sem=recv.at[3 * a + j], device_id=(px, py, c),
                        device_id_type=MESH)
            cp = pltpu.make_async_remote_copy(src_ref=ins[a], dst_ref=outs[a].at[me], **sems)
            if start:
                cp.start()
            else:
                pltpu.make_async_remote_copy(src_ref=ins[a], dst_ref=outs[a].at[2 * px + py], **sems).wait_recv()
                cp.wait_send()
        if not start:
            local.wait()


def _gather_out_shape(shards):
    return [jax.ShapeDtypeStruct((4,) + s.shape, s.dtype) for s in shards]


def _gather_sems(n):
    return [pltpu.SemaphoreType.DMA((3 * n,)), pltpu.SemaphoreType.DMA((3 * n,)), pltpu.SemaphoreType.DMA((n,))]


def gather_shards(shards):
    n = len(shards)

    def body(*refs):
        _gather_comm(refs[:n], refs[n:2 * n], *refs[2 * n:], start=True)
        _gather_comm(refs[:n], refs[n:2 * n], *refs[2 * n:], start=False)

    return pl.pallas_call(body, name="gather_shards", in_specs=[HBM_SPEC] * n, out_specs=[HBM_SPEC] * n,
                          out_shape=_gather_out_shape(shards), scratch_shapes=_gather_sems(n))(*shards)


def _scatter_comm(ins, outs, send, recv, start):
    x, y, c = _coords()
    for a in range(len(ins)):
        for j, (px, py) in enumerate(_other_chips(x, y)):
            cp = pltpu.make_async_remote_copy(src_ref=ins[a].at[2 * px + py], dst_ref=outs[a].at[j],
                                              send_sem=send.at[3 * a + j], recv_sem=recv.at[3 * a + j],
                                              device_id=(px, py, c), device_id_type=MESH)
            if start:
                cp.start()
            else:
                cp.wait_recv()
                cp.wait_send()


def _scatter_out_shape(parts):
    return [jax.ShapeDtypeStruct((3,) + p.shape[1:], p.dtype) for p in parts]


def _scatter_sems(n):
    return [pltpu.SemaphoreType.DMA((3 * n,)), pltpu.SemaphoreType.DMA((3 * n,))]


def scatter_grads(parts):
    n = len(parts)

    def body(*refs):
        _scatter_comm(refs[:n], refs[n:2 * n], *refs[2 * n:], start=True)
        _scatter_comm(refs[:n], refs[n:2 * n], *refs[2 * n:], start=False)

    return pl.pallas_call(body, name="scatter_grads", in_specs=[HBM_SPEC] * n, out_specs=[HBM_SPEC] * n,
                          out_shape=_scatter_out_shape(parts), scratch_shapes=_scatter_sems(n))(*parts)


def sibling_swap(arrs):
    n = len(arrs)

    def body(*refs):
        ins, outs = refs[:n], refs[n:2 * n]
        send, recv = refs[2 * n:]
        x, y, c = _coords()
        cps = []
        for a in range(n):
            cp = pltpu.make_async_remote_copy(src_ref=ins[a], dst_ref=outs[a], send_sem=send.at[a], recv_sem=recv.at[a],
                                              device_id=(x, y, 1 - c), device_id_type=MESH)
            cp.start()
            cps.append(cp)
        for cp in cps:
            cp.wait_recv()
        for cp in cps:
            cp.wait_send()

    return pl.pallas_call(
        body, name="sibling_swap", in_specs=[HBM_SPEC] * n, out_specs=[HBM_SPEC] * n,
        out_shape=[jax.ShapeDtypeStruct(a.shape, a.dtype) for a in arrs],
        scratch_shapes=[pltpu.SemaphoreType.DMA((n,)), pltpu.SemaphoreType.DMA((n,))])(*arrs)


def allsum_small(v):
    rows = v.shape[0]

    def body(v_ref, tot_ref, gath_ref, send, recv):
        x, y, c = _coords()
        me = 4 * x + 2 * y + c
        gath_ref[me] = v_ref[...]
        cps = []
        for k in range(1, 8):
            fx, fy, fc = (k >> 2) & 1, (k >> 1) & 1, k & 1
            peer = (1 - x if fx else x, 1 - y if fy else y, 1 - c if fc else c)
            cp = pltpu.make_async_remote_copy(src_ref=v_ref, dst_ref=gath_ref.at[me], send_sem=send.at[k - 1],
                                              recv_sem=recv.at[k - 1], device_id=peer, device_id_type=MESH)
            cp.start()
            cps.append(cp)
        for cp in cps:
            cp.wait_recv()
        for cp in cps:
            cp.wait_send()
        tot = gath_ref[0]
        for k in range(1, 8):
            tot = tot + gath_ref[k]
        tot_ref[...] = tot

    vm = pl.BlockSpec(memory_space=pltpu.VMEM)
    tot, _ = pl.pallas_call(
        body, name="allsum_small", in_specs=[vm], out_specs=[vm, vm],
        out_shape=[jax.ShapeDtypeStruct((rows, LANES), F32), jax.ShapeDtypeStruct((8, rows, LANES), F32)],
        scratch_shapes=[pltpu.SemaphoreType.DMA((7,)), pltpu.SemaphoreType.DMA((7,))])(v)
    return tot


def sum_parts(own, recv, *, tr=256):
    R, C = own.shape
    tr = min(tr, R)

    def body(o_ref, r_ref, out_ref):
        out_ref[...] = ((o_ref[...] + r_ref[0].astype(F32)) + r_ref[1].astype(F32)) + r_ref[2].astype(F32)

    sp = pl.BlockSpec((tr, C), lambda i: (i, 0))
    return _pcall(body, name="sum_parts", grid=(R // tr,),
                  in_specs=[sp, pl.BlockSpec((3, tr, C), lambda i: (0, i, 0))], out_specs=sp,
                  out_shape=jax.ShapeDtypeStruct((R, C), F32))(own, recv)


def adamw(w, ga, gb, m, v, *, tr=256):
    R, C = w.shape
    tr = min(tr, R)
    two = gb is not None

    def body(*refs):
        if two:
            w_ref, ga_ref, gb_ref, m_ref, v_ref, g_out, d_out, m_out, v_out = refs
            g = ga_ref[...] + gb_ref[...]
        else:
            w_ref, ga_ref, m_ref, v_ref, g_out, d_out, m_out, v_out = refs
            g = ga_ref[...]
        mn = B1 * m_ref[...] + (1.0 - B1) * g
        vn = B2 * v_ref[...] + (1.0 - B2) * (g * g)
        m_hat = mn / (1.0 - B1 ** STEP)
        v_hat = vn / (1.0 - B2 ** STEP)
        g_out[...] = g
        d_out[...] = -LR * (m_hat / (jnp.sqrt(v_hat) + AEPS) + WD * w_ref[...])
        m_out[...] = mn
        v_out[...] = vn

    sp = pl.BlockSpec((tr, C), lambda i: (i, 0))
    args = [w, ga, gb, m, v] if two else [w, ga, m, v]
    sd = jax.ShapeDtypeStruct((R, C), F32)
    return _pcall(body, name="adamw", grid=(R // tr,), in_specs=[sp] * len(args), out_specs=[sp] * 4,
                  out_shape=[sd] * 4)(*args)


def _rope_tables(S):
    def inv_freq(n_dims, theta):
        return theta ** (-(jnp.arange(0, n_dims, 2, dtype=jnp.float32) / n_dims))

    pos = lax.broadcasted_iota(jnp.int32, (S, LANES), 0)
    d = lax.broadcasted_iota(jnp.int32, (S, LANES), 1) % HD
    d1 = lax.iota(jnp.int32, LANES) % HD
    ang = pos.astype(F32) * inv_freq(HD // 4, ROPE_THETA)[d1 % 8][None, :]
    sin = jnp.sin(ang)
    partial = (jnp.where(d < 16, jnp.cos(ang), 1.0), jnp.where((d >= 8) & (d < 16), sin, 0.0),
               jnp.where(d < 8, -sin, 0.0))
    grid_pos = jnp.where(d < 32, pos // GRID_W, pos % GRID_W)
    ang = grid_pos.astype(F32) * inv_freq(HD // 2, AXIAL_THETA)[d1 % 16][None, :]
    sin = jnp.sin(ang)
    axial = (jnp.cos(ang), jnp.where(d % 32 >= 16, sin, 0.0), jnp.where(d % 32 < 16, -sin, 0.0))
    return partial, axial


def _pad_w_in(w):
    cols = [w[:, :768]]
    for base in (768, 960):
        for g in range(3):
            kg = w[:, base + g * HD:base + (g + 1) * HD]
            cols += [kg, kg]
    cols.append(w[:, 1152:])
    return jnp.concatenate(cols, axis=1)


def _unpad_dw_in(dw):
    cols = [dw[:, :768]]
    for t0 in (K_T0, V_T0):
        for g in range(3):
            b = (t0 + g) * LANES
            cols.append(dw[:, b:b + HD] + dw[:, b + HD:b + LANES])
    cols.append(dw[:, M_T0 * LANES:])
    return jnp.concatenate(cols, axis=1)


def _grad_slices(dW_in, dW_mkv, dW_o, dW_gu, dW_d):
    return [None if dW_in is None else jnp.transpose(dW_in.reshape(D, 4, IN_W // 4), (1, 0, 2)),
            dW_mkv.reshape(4, D // 4, 512), dW_o.reshape(4, D // 4, D),
            jnp.transpose(dW_gu.reshape(D, 4, 2 * DFF // 4), (1, 0, 2)), dW_d.reshape(4, DFF // 4, D)]


def _fold(t):
    return t[..., :HD] + t[..., HD:]


def kernel(x, mem, mem_norm_g, w_in, w_mem_kv, w_o, g_mix_pre, g_mix_post, attn_sink, qk_norm_g, w_gate_up, w_down, g_ffn_pre, g_ffn_post, loss_target, m_mem_norm_g, m_w_in, m_w_mem_kv, m_w_o, m_g_mix_pre, m_g_mix_post, m_attn_sink, m_qk_norm_g, m_w_gate_up, m_w_down, m_g_ffn_pre, m_g_ffn_post, v_mem_norm_g, v_w_in, v_w_mem_kv, v_w_o, v_g_mix_pre, v_g_mix_post, v_attn_sink, v_qk_norm_g, v_w_gate_up, v_w_down, v_g_ffn_pre, v_g_ffn_post):
    S = x.shape[1]
    depth = w_in.shape[0]
    xs, memx, tgt = x[0], mem[0], loss_target[0]
    tab_p, tab_a = _rope_tables(S)
    row = lambda a: a.reshape(1, -1)

    shards_bf = [w.astype(BF) for w in (w_in, w_mem_kv, w_o, w_gate_up, w_down)]
    layer_shards = lambda i: [s[i] for s in shards_bf]
    W_in, W_mkv, W_o, W_g, W_u, W_gu, W_d = ([None] * depth for _ in range(7))

    def set_weights(i, gathered):
        if len(gathered) == 5:
            W_in[i] = jnp.concatenate([gathered[0][s] for s in range(4)], axis=1)
        gm, go, gg, gd = gathered[-4:]
        W_mkv[i] = jnp.concatenate([gm[s] for s in range(4)], axis=0)
        W_o[i] = jnp.concatenate([go[s] for s in range(4)], axis=0)
        W_g[i] = jnp.concatenate([gg[0], gg[1]], axis=1)
        W_u[i] = jnp.concatenate([gg[2], gg[3]], axis=1)
        W_gu[i] = jnp.concatenate([gg[0], gg[2], gg[1], gg[3]], axis=1)
        W_d[i] = jnp.concatenate([gd[s] for s in range(4)], axis=0)

    (g_in0,) = gather_shards(layer_shards(0)[:1])
    W_in[0] = jnp.concatenate([g_in0[s] for s in range(4)], axis=1)
    mem_g = row(mem_norm_g)
    zero_sink = jnp.zeros((12,), F32)
    qkg = jnp.pad(jnp.concatenate([qk_norm_g[0], qk_norm_g[0]], axis=1), ((0, 6), (0, 0)))
    no_qkg = jnp.zeros((8, LANES), F32)

    saved = []
    cur = xs
    for i in range(depth):
        kind = i % 3
        wp = _pad_w_in(W_in[i])
        sv = dict(x=cur, wp=wp)
        if kind == 1:
            h1, proj, raw, nrm = inproj_fwd(cur, row(g_mix_pre[i]), wp, tab_a, qkg, axial=True)
            sv["raw"] = raw
        else:
            h1, proj = inproj_fwd(cur, row(g_mix_pre[i]), wp, tab_p, no_qkg, axial=False)
        if kind == 0:
            res = banded_fwd(proj, attn_sink[i // 3], d=1, R=A_RADIUS, TQ=_band_tile(A_RADIUS, S), pair0=0, npairs=6,
                             use_sink=True, o_dtype=BF, gather=layer_shards(0)[1:] if i == 0 else ())
            tok, lse = res[:2]
            if i == 0:
                set_weights(0, res[2])
        elif kind == 1:
            bound = jnp.sqrt(jnp.max(nrm[0]) * jnp.max(nrm[1])) * LN2
            tok, lse = flash_fwd(proj, bound)
        else:
            os_, lses = [], []
            for g, (window, dil) in enumerate(C_GROUPS):
                rad = window // (2 * dil)
                o_g, l_g = banded_fwd(proj, zero_sink, d=dil, R=rad, TQ=_band_tile(rad, S // dil), pair0=2 * g, npairs=2,
                                      use_sink=False, o_dtype=F32)
                os_.append(o_g)
                lses.append(l_g)
            tok = combine_fwd(os_, lses)
            sv["os"], lse = os_, lses
        mem_n, mkv = norm_mm(memx, mem_g, W_mkv[i], tm=N_MEM)
        mo, mlse = mem_fwd(proj, mkv)
        o, x2 = mm_norm_res([tok, mo], W_o[i], row(g_mix_post[i]), cur, tm=1024)
        (h2, gate, up, act), gathered = ffn_up_fwd(x2, row(g_ffn_pre[i]), W_gu[i],
                                                   layer_shards(i + 1) if i + 1 < depth else ())
        if i + 1 < depth:
            set_weights(i + 1, gathered)
        f, x3 = mm_norm_res([act], W_d[i], row(g_ffn_post[i]), x2)
        sv.update(h1=h1, proj=proj, lse=lse, mem_n=mem_n, mkv=mkv, mlse=mlse, tok=tok, mo=mo, o=o, x2=x2, h2=h2, gate=gate,
                  up=up, act=act, f=f)
        saved.append(sv)
        cur = x3

    dcur, loss_vec = loss_bwd(cur, tgt)

    grad_parts, grad_recv = [None] * depth, [None] * depth
    dg_pre, dg_post, dg_fpre, dg_fpost = [None] * depth, [None] * depth, [None] * depth, [None] * depth
    dg_mem = jnp.zeros((1, D), F32)
    dsinks, dqk = {}, None
    for i in reversed(range(depth)):
        sv = saved[i]
        kind = i % 3
        proj = sv["proj"]
        pending = [p.astype(BF) for p in grad_parts[i + 1]] if i + 1 < depth else ()
        (df, dgate, dup, dg_fpost[i]), got = normbwd_mm_swiglu(dcur, sv["f"], row(g_ffn_post[i]), W_d[i], sv["gate"],
                                                               sv["up"], pending)
        if i + 1 < depth:
            grad_recv[i + 1] = got
        dx2, dg_fpre[i] = mm_nt_normbwd_res([(dgate, W_g[i]), (dup, W_u[i])], sv["x2"], row(g_ffn_pre[i]), dcur)
        dW_d = mm_acc(sv["act"], df, tk=1408, tn=D, ts=2048)
        dW_gu = jnp.concatenate([mm_acc(sv["h2"], dgate, tk=D, tn=1408, ts=2048),
                                 mm_acc(sv["h2"], dup, tk=D, tn=1408, ts=2048)], axis=1)
        do, dcat, delta, dg_post[i], *delta_t = normbwd_mm_cat(dx2, sv["o"], row(g_mix_post[i]), W_o[i], sv["tok"],
                                                               sv["mo"], stats_t=kind == 1)
        dW_o = jnp.concatenate([mm_acc(sv["tok"], do, tk=768, tn=D, ts=2048), mm_acc(sv["mo"], do, tk=256, tn=D, ts=2048)],
                               axis=0)
        dqm, dmkv = mem_bwd(proj, sv["mkv"], dcat, sv["mlse"], delta)
        dmkv = dmkv.astype(BF)
        (dgm,) = mm_nt_normbwd_res([(dmkv, W_mkv[i])], memx, mem_g, None, tm=N_MEM)
        dg_mem = dg_mem + dgm
        dW_mkv = mm_acc(sv["mem_n"], dmkv, tk=D, tn=512, ts=N_MEM)
        if kind == 0:
            sink = attn_sink[i // 3]
            args = dict(d=1, R=A_RADIUS, pair0=0, npairs=6)
            tile = _band_tile(A_RADIUS, S)
            early = [p.astype(BF) for p in _grad_slices(None, dW_mkv, dW_o, dW_gu, dW_d)[1:]] if i == 0 else ()
            res = banded_bwd_dq(proj, dcat, sv["lse"], delta, sink, TQ=tile, use_sink=True, scatter=early, **args)
            dq, dsk = res[:2]
            if i == 0:
                early_recv = res[2]
            dkp, dvp = banded_bwd_dkv(proj, dcat, sv["lse"], delta, TK=tile, **args)
            dsinks[i // 3] = dsk.reshape(3, 8, 2, 2, HD)[:, 0, :, :, 0].reshape(12)
        elif kind == 1:
            dq, dkp, dvp = flash_bwd(proj, dcat, sv["lse"], delta_t[0])
        else:
            dos, des = combine_bwd(dcat, sv["os"], sv["lse"])
            dqs, dks, dvs = [], [], []
            for g, (window, dil) in enumerate(C_GROUPS):
                rad = window // (2 * dil)
                args = dict(d=dil, R=rad, pair0=2 * g, npairs=2)
                tile = _band_tile(rad, S // dil)
                dq_g, _ = banded_bwd_dq(proj, dos[g], sv["lse"][g], des[g], zero_sink, TQ=tile, use_sink=False, **args)
                dk_g, dv_g = banded_bwd_dkv(proj, dos[g], sv["lse"][g], des[g], TK=tile, **args)
                dqs.append(dq_g)
                dks.append(dk_g)
                dvs.append(dv_g)
            dq, dkp, dvp = (jnp.concatenate(t, axis=1) for t in (dqs, dks, dvs))
        if kind == 1:
            dcur, dproj, dg_pre[i], dqk_t = inproj_bwd(dq, dkp, dvp, dqm, tab_a, sv["raw"], qkg, sv["wp"], sv["x"],
                                                       row(g_mix_pre[i]), dx2, axial=True)
            dqk = _fold(dqk_t[:2]).reshape(1, 2, HD)
        else:
            dcur, dproj, dg_pre[i], _ = inproj_bwd(dq, dkp, dvp, dqm, tab_p, proj, no_qkg, sv["wp"],
                                                   sv["x"], row(g_mix_pre[i]), dx2, axial=False)
        dW_in = _unpad_dw_in(mm_acc(sv["h1"], dproj, tk=D, tn=PW, ts=2048))
        grad_parts[i] = _grad_slices(dW_in, dW_mkv, dW_o, dW_gu, dW_d)
    grad_recv[0] = list(scatter_grads([grad_parts[0][0].astype(BF)])) + list(early_recv)

    x_i, y_i, _ = _coords()
    me = 2 * x_i + y_i
    big = [(w_in, m_w_in, v_w_in), (w_mem_kv, m_w_mem_kv, v_w_mem_kv), (w_o, m_w_o, v_w_o),
           (w_gate_up, m_w_gate_up, v_w_gate_up), (w_down, m_w_down, v_w_down)]
    parts = []
    for a, (w, _, _) in enumerate(big):
        C = w.shape[-1]
        own = jnp.stack([lax.dynamic_index_in_dim(grad_parts[l][a], me, 0, keepdims=False) for l in range(depth)])
        rc = jnp.stack([grad_recv[l][a] for l in range(depth)], axis=1)
        parts.append(sum_parts(own.reshape(-1, C), rc.reshape(3, -1, C)))
    sibs = sibling_swap(parts)
    big_out = []
    for (w, m, v), pa, pb in zip(big, parts, sibs):
        C = w.shape[-1]
        outs = adamw(w.reshape(-1, C), pa, pb, m.reshape(-1, C), v.reshape(-1, C))
        big_out.append([o.reshape(w.shape) for o in outs])

    small_w = [mem_norm_g, g_mix_pre, g_mix_post, attn_sink, qk_norm_g, g_ffn_pre, g_ffn_post]
    small_m = [m_mem_norm_g, m_g_mix_pre, m_g_mix_post, m_attn_sink, m_qk_norm_g, m_g_ffn_pre, m_g_ffn_post]
    small_v = [v_mem_norm_g, v_g_mix_pre, v_g_mix_post, v_attn_sink, v_qk_norm_g, v_g_ffn_pre, v_g_ffn_post]
    small_g = [dg_mem.reshape(D), jnp.concatenate(dg_pre, axis=0), jnp.concatenate(dg_post, axis=0),
               jnp.stack([dsinks[k] for k in sorted(dsinks)]), dqk, jnp.concatenate(dg_fpre, axis=0),
               jnp.concatenate(dg_fpost, axis=0)]
    sizes = [a.size for a in small_w]
    total = sum(sizes)
    rows_s = -(-(total + LANES) // (8 * LANES)) * 8

    def pack(arrs, extra=None):
        flat = jnp.concatenate([a.reshape(-1).astype(F32) for a in arrs])
        flat = jnp.pad(flat, (0, rows_s * LANES - LANES - total))
        tail = jnp.zeros((LANES,), F32) if extra is None else extra.reshape(LANES)
        return jnp.concatenate([flat, tail]).reshape(rows_s, LANES)

    tot = allsum_small(pack(small_g, loss_vec))
    loss = jnp.sum(tot[rows_s - 1])
    s_out = adamw(pack(small_w), tot, None, pack(small_m), pack(small_v))

    def unpack(buf):
        flat = buf.reshape(-1)
        out, off = [], 0
        for a, n in zip(small_w, sizes):
            out.append(flat[off:off + n].reshape(a.shape))
            off += n
        return out

    sg, sd_, sm, sv_ = (unpack(b) for b in s_out)

    def ordered(k):
        sm_ = (sg, sd_, sm, sv_)[k]
        b = [bo[k] for bo in big_out]
        return [sm_[0], b[0], b[1], b[2], sm_[1], sm_[2], sm_[3], sm_[4], b[3], b[4], sm_[5], sm_[6]]

    dx_out = dcur.reshape(1, S, D)
    return (loss, dx_out, *ordered(0), *ordered(1), *ordered(2), *ordered(3))
```
